```python
import jax, jax.numpy as jnp
from jax import lax
import numpy as np

D_MODEL = 4096
BATCH = 4
SEQ = 2048
DEPTH = 2
DEC_BATCH = 8
DEC_SEQ = 8
PAST_LEN = 16384
PAGE_SIZE = 128

HEAD_DIM = 128
MIX_W = D_MODEL
N_EVEN = (DEPTH + 1) // 2
N_ODD = DEPTH // 2
CONV_W = MIX_W // 2
CONV_K = 3
H_B = MIX_W // 2 // HEAD_DIM
KV_B = 4
H_IDX = 16
D_IDX = 128
TOPK_MAX = 256
H_C = MIX_W // 2 // HEAD_DIM
KV_C = 1
CMP_BLOCK = 32
SEL_BLOCK = 64
N_SEL = 16
WINDOW = 512
FORCE_BONUS = 1.0e6
H_D = MIX_W // 2 // HEAD_DIM
N_MEM = 256
H_MEM = 4
D_FF = -(-8 * D_MODEL // 768) * 256
QBLK = 128
ROPE_THETA = 10000.0
EPS = 1e-6

EVEN_SIZES = [CONV_W, CONV_W, CONV_W, H_B * HEAD_DIM, KV_B * HEAD_DIM, KV_B * HEAD_DIM, H_IDX * D_IDX, D_IDX, H_IDX]
ODD_SIZES = [H_C * HEAD_DIM] + [KV_C * HEAD_DIM] * 6 + [H_C * 3] + [H_D * HEAD_DIM] * 3

kernel_name = 'hybrid_conv_dsa_nsa_stickbreak_decode_step'


def rmsnorm(x, g):
    xf = x.astype(jnp.float32)
    y = xf * lax.rsqrt(jnp.mean(xf * xf, axis=-1, keepdims=True) + EPS)
    return (y * g.astype(jnp.float32)).astype(x.dtype)


def rope(x, pos):
    half = x.shape[-1] // 2
    inv = ROPE_THETA ** (-jnp.arange(half, dtype=jnp.float32) / half)
    ang = pos.astype(jnp.float32)[:, None] * inv[None, :]
    cos, sin = jnp.cos(ang)[:, None, :], jnp.sin(ang)[:, None, :]
    xf = x.astype(jnp.float32)
    x1, x2 = xf[..., :half], xf[..., half:]
    return jnp.concatenate([x1 * cos - x2 * sin, x2 * cos + x1 * sin], axis=-1).astype(x.dtype)


def masked_softmax(s, mask):
    s = jnp.where(mask, s, -jnp.inf)
    m = jnp.max(s, axis=-1, keepdims=True)
    e = jnp.where(mask, jnp.exp(s - jnp.where(jnp.isfinite(m), m, 0.0)), 0.0)
    return e / jnp.maximum(jnp.sum(e, axis=-1, keepdims=True), 1e-30)


def split_cols(z, sizes):
    return jnp.split(z, np.cumsum(sizes)[:-1].tolist(), axis=-1)


def short_conv(u_ext, w):
    t = u_ext.shape[1] - (CONV_K - 1)
    return sum(w[j] * u_ext[:, j:j + t] for j in range(CONV_K))


def gather_pages(pool, li, page_table):
    g = pool[li, page_table]
    return g.reshape((page_table.shape[0], -1) + pool.shape[3:])


def gather_rows(pool, li, page_table, new, pos, grp=None):
    nb = page_table.shape[0]
    past = page_table.shape[1] * PAGE_SIZE
    bidx = jnp.arange(nb).reshape((nb,) + (1,) * (pos.ndim - 1))
    pp = jnp.clip(pos, 0, past - 1)
    phys = page_table[bidx, pp // PAGE_SIZE]
    npos = jnp.clip(pos - past, 0, new.shape[1] - 1)
    if grp is None:
        a, b = pool[li, phys, pp % PAGE_SIZE], new[bidx, npos]
    else:
        a, b = pool[li, phys, pp % PAGE_SIZE, grp], new[bidx, npos, grp]
    is_past = (pos < past).reshape(pos.shape + (1,) * (a.ndim - pos.ndim))
    return jnp.where(is_past, a, b)


def dense_attn(q, k, v, mask):
    b, tq, h, d = q.shape
    hkv = k.shape[2]
    qg = q.reshape(b, tq, hkv, h // hkv, d)
    s = jnp.einsum('bqgrd,bsgd->bqgrs', qg, k).astype(jnp.float32) * d ** -0.5
    p = masked_softmax(s, mask[None, :, None, None, :])
    o = jnp.einsum('bqgrs,bsgd->bqgrd', p.astype(v.dtype), v)
    return o.reshape(b, tq, h, d)


def sel_attn(q, k, v, valid):
    b, tq, h, d = q.shape
    hkv = k.shape[2]
    qg = q.reshape(b, tq, hkv, h // hkv, d)
    s = jnp.einsum('bqgrd,bqgnd->bqgrn', qg, k).astype(jnp.float32) * d ** -0.5
    p = masked_softmax(s, valid[:, :, :, None, :])
    o = jnp.einsum('bqgrn,bqgnd->bqgrd', p.astype(v.dtype), v)
    return o.reshape(b, tq, h, d)


def dsa_select(qi, wi, kidx, qpos, kpos, topk):
    s = jnp.einsum('bqhd,bsd->bqhs', qi, kidx).astype(jnp.float32) * D_IDX ** -0.5
    score = jnp.einsum('bqhs,bqh->bqs', jax.nn.relu(s), wi.astype(jnp.float32)) * H_IDX ** -0.5
    score = jnp.where((kpos[None, :] <= qpos[:, None])[None], score, -jnp.inf)
    top, idx = lax.top_k(score, topk)
    return idx, jnp.isfinite(top)


def compress(x, w):
    b, s, hkv, d = x.shape
    nb = s // CMP_BLOCK
    xb = x[:, :nb * CMP_BLOCK].reshape(b, nb, CMP_BLOCK, hkv, d)
    return jnp.einsum('bnphd,pd->bnhd', xb, w)


def nsa_cmp(q, kcmp, vcmp, qpos):
    b, tq, h, d = q.shape
    nb, hkv = kcmp.shape[1], kcmp.shape[2]
    qg = q.reshape(b, tq, hkv, h // hkv, d)
    s = jnp.einsum('bqgrd,bngd->bqgrn', qg, kcmp).astype(jnp.float32) * d ** -0.5
    mask = ((jnp.arange(nb) + 1) * CMP_BLOCK - 1)[None, :] <= qpos[:, None]
    p = masked_softmax(s, mask[None, :, None, None, :])
    o = jnp.einsum('bqgrn,bngd->bqgrd', p.astype(vcmp.dtype), vcmp)
    return o.reshape(b, tq, h, d), p.sum(axis=3)


def nsa_select(imp, qpos, n_selblk):
    b, tq, hkv, nb = imp.shape
    ratio = SEL_BLOCK // CMP_BLOCK
    imp = jnp.pad(imp, ((0, 0), (0, 0), (0, 0), (0, n_selblk * ratio - nb)))
    imp = imp.reshape(b, tq, hkv, n_selblk, ratio).sum(-1)
    blk = jnp.arange(n_selblk)[None, :]
    cur = (qpos // SEL_BLOCK)[:, None]
    forced = (blk == 0) | (blk == cur) | (blk == cur - 1)
    admiss = blk * SEL_BLOCK <= qpos[:, None]
    score = jnp.where(forced[None, :, None, :], imp + FORCE_BONUS, imp)
    score = jnp.where(admiss[None, :, None, :], score, -jnp.inf)
    top, sel = lax.top_k(score, min(N_SEL, n_selblk))
    tok = (sel[..., None] * SEL_BLOCK + jnp.arange(SEL_BLOCK)).reshape(b, tq, hkv, -1)
    valid = jnp.repeat(jnp.isfinite(top), SEL_BLOCK, axis=-1) & (tok <= qpos[None, :, None, None])
    return tok, valid


def sb_block(q, k, v, mask, carry):
    z = jnp.einsum('bqhd,bshd->bhqs', q, k).astype(jnp.float32) * HEAD_DIM ** -0.5
    m = mask[None, None]
    lneg = jnp.where(m, jax.nn.log_sigmoid(-z), 0.0)
    log_a = jax.nn.log_sigmoid(z) + lax.cumsum(lneg, axis=3, reverse=True) - lneg + carry[..., None]
    a = jnp.where(m, jnp.exp(log_a), 0.0)
    o = jnp.einsum('bhqs,bshd->bqhd', a, v.astype(jnp.float32))
    return o, carry + lneg.sum(axis=3)


def swiglu(h, wg, wu, wd):
    return (jax.nn.silu(h @ wg) * (h @ wu)) @ wd


def mem_kv(mem, g, wk, wv):
    m = rmsnorm(mem, g)
    b, n, _ = m.shape
    return (m @ wk).reshape(b, n, H_MEM, HEAD_DIM), (m @ wv).reshape(b, n, H_MEM, HEAD_DIM)


def mem_attn(h, mk, mv, wq, wo):
    b, t, _ = h.shape
    q = (h @ wq).reshape(b, t, H_MEM, HEAD_DIM)
    o = dense_attn(q, mk, mv, jnp.ones((t, mk.shape[1]), bool))
    return o.reshape(b, t, -1) @ wo


def even_project(h, pos, w_in):
    b, t, _ = h.shape
    bg, cg, xa, q, k, v, qi, ki, wi = split_cols(h @ w_in, EVEN_SIZES)
    q = rope(q.reshape(b, t, H_B, HEAD_DIM), pos)
    k = rope(k.reshape(b, t, KV_B, HEAD_DIM), pos)
    v = v.reshape(b, t, KV_B, HEAD_DIM)
    qi = rope(qi.reshape(b, t, H_IDX, D_IDX), pos)
    ki = rope(ki.reshape(b, t, 1, D_IDX), pos)[:, :, 0]
    return bg, cg * xa, q, k, v, qi, ki, wi


def even_mixer_prompt(h, w_in, conv_w, w_out):
    b, t, _ = h.shape
    pos = jnp.arange(t)
    bg, u, q, k, v, qi, ki, wi = even_project(h, pos, w_in)
    ya = bg * short_conv(jnp.pad(u, ((0, 0), (CONV_K - 1, 0), (0, 0))), conv_w)
    topk = min(TOPK_MAX, t // 4)
    bi = jnp.arange(b)[:, None, None]

    def block(i):
        q0 = i * QBLK
        qpos = q0 + jnp.arange(QBLK)
        sl = lambda a: lax.dynamic_slice_in_dim(a, q0, QBLK, axis=1)
        idx, valid = dsa_select(sl(qi), sl(wi), ki, qpos, pos, topk)
        ob = sel_attn(sl(q), k[bi, idx].swapaxes(2, 3), v[bi, idx].swapaxes(2, 3), valid[:, :, None, :])
        return ob.reshape(b, QBLK, H_B * HEAD_DIM)

    ob = lax.map(block, jnp.arange(t // QBLK))
    ob = ob.transpose(1, 0, 2, 3).reshape(b, t, H_B * HEAD_DIM)
    y = jnp.concatenate([ya, ob], axis=-1) @ w_out
    return y, (u[:, t - (CONV_K - 1):], k, v, ki)


def even_mixer_sample(h, e, page_table, state_conv, c_k, c_v, c_kidx, w_in, conv_w, w_out):
    b, t, _ = h.shape
    past = page_table.shape[1] * PAGE_SIZE
    pos = past + jnp.arange(t)
    bg, u, q, k, v, qi, ki, wi = even_project(h, pos, w_in)
    u_ext = jnp.concatenate([state_conv[e], u], axis=1)
    ya = bg * short_conv(u_ext, conv_w)
    n_keys = past + t
    ki_all = jnp.concatenate([gather_pages(c_kidx, e, page_table), ki], axis=1)
    idx, valid = dsa_select(qi, wi, ki_all, pos, jnp.arange(n_keys), min(TOPK_MAX, n_keys // 4))
    ksel = gather_rows(c_k, e, page_table, k, idx).swapaxes(2, 3)
    vsel = gather_rows(c_v, e, page_table, v, idx).swapaxes(2, 3)
    ob = sel_attn(q, ksel, vsel, valid[:, :, None, :]).reshape(b, t, H_B * HEAD_DIM)
    y = jnp.concatenate([ya, ob], axis=-1) @ w_out
    return y, (u_ext[:, t:], k, v, ki)


def odd_project(h, pos, w_in):
    b, t, _ = h.shape
    qc, kc, vc, ks, vs, kw, vw, gc, qd, kd, vd = split_cols(h @ w_in, ODD_SIZES)
    heads = lambda a, n: a.reshape(b, t, n, HEAD_DIM)
    qc = rope(heads(qc, H_C), pos)
    kc, ks, kw = (rope(heads(a, KV_C), pos) for a in (kc, ks, kw))
    vc, vs, vw = (heads(a, KV_C) for a in (vc, vs, vw))
    gate = jax.nn.sigmoid(gc.reshape(b, t, H_C, 3).astype(jnp.float32)).astype(h.dtype)
    return qc, kc, vc, ks, vs, kw, vw, gate, heads(qd, H_D), heads(kd, H_D), heads(vd, H_D)


def gate_merge(g, o_cmp, o_sel, o_win):
    return g[..., 0:1] * o_cmp + g[..., 1:2] * o_sel + g[..., 2:3] * o_win


def odd_mixer_prompt(h, w_in, cmp_wk, cmp_wv, w_out):
    b, t, _ = h.shape
    pos = jnp.arange(t)
    qc, kc, vc, ks, vs, kw, vw, gate, qd, kd, vd = odd_project(h, pos, w_in)
    kcmp, vcmp = compress(kc, cmp_wk), compress(vc, cmp_wv)
    n_selblk = -(-t // SEL_BLOCK)
    pad = ((0, 0), (WINDOW, 0), (0, 0), (0, 0))
    kw_pad, vw_pad = jnp.pad(kw, pad), jnp.pad(vw, pad)
    bi = jnp.arange(b)[:, None, None, None]
    gi = jnp.arange(KV_C)[None, None, :, None]

    def block(i):
        q0 = i * QBLK
        qpos = q0 + jnp.arange(QBLK)
        sl = lambda a: lax.dynamic_slice_in_dim(a, q0, QBLK, axis=1)
        qcb = sl(qc)
        o_cmp, imp = nsa_cmp(qcb, kcmp, vcmp, qpos)
        tok, valid = nsa_select(imp, qpos, n_selblk)
        o_sel = sel_attn(qcb, ks[bi, tok, gi], vs[bi, tok, gi], valid)
        kwb = lax.dynamic_slice_in_dim(kw_pad, q0, QBLK + WINDOW, axis=1)
        vwb = lax.dynamic_slice_in_dim(vw_pad, q0, QBLK + WINDOW, axis=1)
        kwpos = q0 - WINDOW + jnp.arange(QBLK + WINDOW)
        wmask = (kwpos[None] <= qpos[:, None]) & (kwpos[None] > qpos[:, None] - WINDOW) & (kwpos[None] >= 0)
        o_win = dense_attn(qcb, kwb, vwb, wmask)
        o_c = gate_merge(sl(gate), o_cmp, o_sel, o_win)
        o_d, _ = sb_block(sl(qd), kd, vd, pos[None, :] < qpos[:, None], jnp.zeros((b, H_D, QBLK), jnp.float32))
        return jnp.concatenate([o_c.reshape(b, QBLK, -1), o_d.astype(h.dtype).reshape(b, QBLK, -1)], axis=-1)

    y = lax.map(block, jnp.arange(t // QBLK))
    y = y.transpose(1, 0, 2, 3).reshape(b, t, MIX_W) @ w_out
    nw = min(WINDOW, t)
    return y, (kc, vc, ks, vs, kw[:, t - nw:], vw[:, t - nw:], kd, vd)


def odd_mixer_sample(h, o, page_table, c_kc, c_vc, c_ks, c_vs, c_kw, c_vw, c_kd, c_vd, w_in, cmp_wk, cmp_wv, w_out):
    b, t, _ = h.shape
    past = page_table.shape[1] * PAGE_SIZE
    pos = past + jnp.arange(t)
    qc, kc, vc, ks, vs, kw, vw, gate, qd, kd, vd = odd_project(h, pos, w_in)
    kcmp = compress(jnp.concatenate([gather_pages(c_kc, o, page_table), kc], axis=1), cmp_wk)
    vcmp = compress(jnp.concatenate([gather_pages(c_vc, o, page_table), vc], axis=1), cmp_wv)
    o_cmp, imp = nsa_cmp(qc, kcmp, vcmp, pos)
    tok, valid = nsa_select(imp, pos, -(-(past + t) // SEL_BLOCK))
    gi = jnp.arange(KV_C)[None, None, :, None]
    ksel = gather_rows(c_ks, o, page_table, ks, tok, gi)
    vsel = gather_rows(c_vs, o, page_table, vs, tok, gi)
    o_sel = sel_attn(qc, ksel, vsel, valid)
    kw_all = jnp.concatenate([c_kw[o], kw], axis=1)
    vw_all = jnp.concatenate([c_vw[o], vw], axis=1)
    wb = c_kw.shape[2]
    kwpos = past - wb + jnp.arange(wb + t)
    wmask = (kwpos[None] <= pos[:, None]) & (kwpos[None] > pos[:, None] - WINDOW)
    o_win = dense_attn(qc, kw_all, vw_all, wmask)
    o_c = gate_merge(gate, o_cmp, o_sel, o_win)
    tt = jnp.arange(t)
    o_d, carry = sb_block(qd, kd, vd, tt[None, :] < tt[:, None], jnp.zeros((b, H_D, t), jnp.float32))

    def page_step(state, p):
        cr, acc = state
        phys = page_table[:, p]
        o_p, cr = sb_block(qd, c_kd[o, phys], c_vd[o, phys], jnp.ones((t, PAGE_SIZE), bool), cr)
        return (cr, acc + o_p), None

    (_, o_d), _ = lax.scan(page_step, (carry, o_d), jnp.arange(page_table.shape[1]), reverse=True)
    y = jnp.concatenate([o_c.reshape(b, t, -1), o_d.astype(h.dtype).reshape(b, t, -1)], axis=-1) @ w_out
    return y, (kc, vc, ks, vs, kw_all[:, t:], vw_all[:, t:], kd, vd)


def _stack(lst, j):
    return jnp.stack([s[j] for s in lst])


def setup_inputs(seed: int = 0) -> dict:
    key = jax.random.key(seed)
    ks = jax.random.split(key, 48)
    cnt = [0]

    def nrm(shape, scale=1.0):
        cnt[0] += 1
        return jax.random.normal(ks[cnt[0]], shape, jnp.float32) * scale

    def gain(shape):
        return 1.0 + 0.05 * nrm(shape)

    n_pages = PAST_LEN // PAGE_SIZE
    n_used = DEC_BATCH * n_pages
    n_pool = n_used + n_used // 4
    page_table = jax.random.permutation(ks[0], n_pool)[:n_used].reshape(DEC_BATCH, n_pages).astype(jnp.int32)
    wbuf = min(WINDOW, PAST_LEN)
    d_in = D_MODEL ** -0.5
    return {
        'x_prompt': nrm((BATCH, SEQ, D_MODEL)),
        'x_sample': nrm((DEC_BATCH, DEC_SEQ, D_MODEL)),
        'state_conv': nrm((N_EVEN, DEC_BATCH, CONV_K - 1, CONV_W)),
        'cache_dsa_k': nrm((N_EVEN, n_pool, PAGE_SIZE, KV_B, HEAD_DIM)),
        'cache_dsa_v': nrm((N_EVEN, n_pool, PAGE_SIZE, KV_B, HEAD_DIM)),
        'cache_dsa_kidx': nrm((N_EVEN, n_pool, PAGE_SIZE, D_IDX)),
        'cache_nsa_kc': nrm((N_ODD, n_pool, PAGE_SIZE, KV_C, HEAD_DIM)),
        'cache_nsa_vc': nrm((N_ODD, n_pool, PAGE_SIZE, KV_C, HEAD_DIM)),
        'cache_nsa_ks': nrm((N_ODD, n_pool, PAGE_SIZE, KV_C, HEAD_DIM)),
        'cache_nsa_vs': nrm((N_ODD, n_pool, PAGE_SIZE, KV_C, HEAD_DIM)),
        'cache_nsa_kw': nrm((N_ODD, DEC_BATCH, wbuf, KV_C, HEAD_DIM)),
        'cache_nsa_vw': nrm((N_ODD, DEC_BATCH, wbuf, KV_C, HEAD_DIM)),
        'cache_sb_k': nrm((N_ODD, n_pool, PAGE_SIZE, H_D, HEAD_DIM)),
        'cache_sb_v': nrm((N_ODD, n_pool, PAGE_SIZE, H_D, HEAD_DIM)),
        'cache_mem_k': nrm((DEPTH, DEC_BATCH, N_MEM, H_MEM, HEAD_DIM)),
        'cache_mem_v': nrm((DEPTH, DEC_BATCH, N_MEM, H_MEM, HEAD_DIM)),
        'page_table': page_table,
        'mem_prompt': nrm((BATCH, N_MEM, D_MODEL)),
        'norm_pre': gain((DEPTH, 3, D_MODEL)),
        'norm_post': gain((DEPTH, 3, D_MODEL)),
        'norm_mem': gain((DEPTH, D_MODEL)),
        'w_in_even': nrm((N_EVEN, D_MODEL, sum(EVEN_SIZES)), d_in),
        'conv_w': nrm((N_EVEN, CONV_K, CONV_W), CONV_K ** -0.5),
        'w_out_even': nrm((N_EVEN, MIX_W, D_MODEL), MIX_W ** -0.5),
        'w_in_odd': nrm((N_ODD, D_MODEL, sum(ODD_SIZES)), d_in),
        'cmp_wk': nrm((N_ODD, CMP_BLOCK, HEAD_DIM), CMP_BLOCK ** -0.5),
        'cmp_wv': nrm((N_ODD, CMP_BLOCK, HEAD_DIM), CMP_BLOCK ** -0.5),
        'w_out_odd': nrm((N_ODD, MIX_W, D_MODEL), MIX_W ** -0.5),
        'w_mq': nrm((DEPTH, D_MODEL, H_MEM * HEAD_DIM), d_in),
        'w_mk': nrm((DEPTH, D_MODEL, H_MEM * HEAD_DIM), d_in),
        'w_mv': nrm((DEPTH, D_MODEL, H_MEM * HEAD_DIM), d_in),
        'w_mo': nrm((DEPTH, H_MEM * HEAD_DIM, D_MODEL), (H_MEM * HEAD_DIM) ** -0.5),
        'w_gate': nrm((DEPTH, D_MODEL, D_FF), d_in),
        'w_up': nrm((DEPTH, D_MODEL, D_FF), d_in),
        'w_down': nrm((DEPTH, D_FF, D_MODEL), D_FF ** -0.5),
    }


def reference(x_prompt, x_sample, state_conv, cache_dsa_k, cache_dsa_v, cache_dsa_kidx,
              cache_nsa_kc, cache_nsa_vc, cache_nsa_ks, cache_nsa_vs, cache_nsa_kw, cache_nsa_vw,
              cache_sb_k, cache_sb_v, cache_mem_k, cache_mem_v, page_table, mem_prompt,
              norm_pre, norm_post, norm_mem, w_in_even, conv_w, w_out_even,
              w_in_odd, cmp_wk, cmp_wv, w_out_odd, w_mq, w_mk, w_mv, w_mo, w_gate, w_up, w_down):
    xp, xs = x_prompt, x_sample
    ev_p, ev_s, od_p, od_s, mem_p = [], [], [], [], []
    for li in range(DEPTH):
        g_pre, g_post = norm_pre[li], norm_post[li]
        if li % 2 == 0:
            e = li // 2
            wi, cw, wo = w_in_even[e], conv_w[e], w_out_even[e]
            mp, stp = even_mixer_prompt(rmsnorm(xp, g_pre[0]), wi, cw, wo)
            ms, sts = even_mixer_sample(rmsnorm(xs, g_pre[0]), e, page_table, state_conv,
                                        cache_dsa_k, cache_dsa_v, cache_dsa_kidx, wi, cw, wo)
            ev_p.append(stp)
            ev_s.append(sts)
        else:
            o = li // 2
            wi, wk, wv, wo = w_in_odd[o], cmp_wk[o], cmp_wv[o], w_out_odd[o]
            mp, stp = odd_mixer_prompt(rmsnorm(xp, g_pre[0]), wi, wk, wv, wo)
            ms, sts = odd_mixer_sample(rmsnorm(xs, g_pre[0]), o, page_table, cache_nsa_kc, cache_nsa_vc,
                                       cache_nsa_ks, cache_nsa_vs, cache_nsa_kw, cache_nsa_vw,
                                       cache_sb_k, cache_sb_v, wi, wk, wv, wo)
            od_p.append(stp)
            od_s.append(sts)
        xp = xp + rmsnorm(mp, g_post[0])
        xs = xs + rmsnorm(ms, g_post[0])
        wq, wmo = w_mq[li], w_mo[li]
        mkp, mvp = mem_kv(mem_prompt, norm_mem[li], w_mk[li], w_mv[li])
        mem_p.append((mkp, mvp))
        xp = xp + rmsnorm(mem_attn(rmsnorm(xp, g_pre[1]), mkp, mvp, wq, wmo), g_post[1])
        xs = xs + rmsnorm(mem_attn(rmsnorm(xs, g_pre[1]), cache_mem_k[li], cache_mem_v[li], wq, wmo), g_post[1])
        wg, wu, wd = w_gate[li], w_up[li], w_down[li]
        xp = xp + rmsnorm(swiglu(rmsnorm(xp, g_pre[2]), wg, wu, wd), g_post[2])
        xs = xs + rmsnorm(swiglu(rmsnorm(xs, g_pre[2]), wg, wu, wd), g_post[2])
    conv_p, dsa_k_p, dsa_v_p, dsa_kidx_p = (_stack(ev_p, j) for j in range(4))
    conv_s, dsa_k_s, dsa_v_s, dsa_kidx_s = (_stack(ev_s, j) for j in range(4))
    nsa_kc_p, nsa_vc_p, nsa_ks_p, nsa_vs_p, nsa_kw_p, nsa_vw_p, sb_k_p, sb_v_p = (_stack(od_p, j) for j in range(8))
    nsa_kc_s, nsa_vc_s, nsa_ks_s, nsa_vs_s, nsa_kw_s, nsa_vw_s, sb_k_s, sb_v_s = (_stack(od_s, j) for j in range(8))
    mem_k_p, mem_v_p = _stack(mem_p, 0), _stack(mem_p, 1)
    return (xp, xs,
            conv_p, dsa_k_p, dsa_v_p, dsa_kidx_p, nsa_kc_p, nsa_vc_p, nsa_ks_p, nsa_vs_p, nsa_kw_p, nsa_vw_p,
            sb_k_p, sb_v_p, mem_k_p, mem_v_p,
            conv_s, dsa_k_s, dsa_v_s, dsa_kidx_s, nsa_kc_s, nsa_vc_s, nsa_ks_s, nsa_vs_s, nsa_kw_s, nsa_vw_s,
            sb_k_s, sb_v_s)
```

```python
import functools

import numpy as np
import jax
import jax.numpy as jnp
from jax import lax
from jax.experimental import pallas as pl
from jax.experimental.pallas import tpu as pltpu

F32 = jnp.float32
BF16 = jnp.bfloat16

HEAD_DIM = 128
PAGE_SIZE = 128
CONV_K = 3
KV_B = 4
H_IDX = 16
D_IDX = 128
TOPK_MAX = 256
CMP_BLOCK = 32
SEL_BLOCK = 64
N_SEL = 16
WINDOW = 512
FORCE_BONUS = 1.0e6
H_MEM = 4
ROPE_THETA = 10000.0
EPS = 1e-6
NEG = -1e30
LANES = 128
VMEM_LIMIT = 56 * 1024 * 1024


def _cp(*sem):
    return pltpu.CompilerParams(dimension_semantics=sem, vmem_limit_bytes=VMEM_LIMIT)


def _tile(n, pref, mult):
    t = (min(pref, n) // mult) * mult
    while t >= mult:
        if n % t == 0:
            return t
        t -= mult
    return n


def _dot(a, b):
    return jnp.dot(a, b, preferred_element_type=F32)


def _dot_nt(a, b):
    return lax.dot_general(a, b, (((1,), (1,)), ((), ())), preferred_element_type=F32)


def _rms(x, g):
    return x * lax.rsqrt(jnp.mean(x * x, axis=-1, keepdims=True) + EPS) * g


def _rope(y, cos, sin):
    return y * cos + pltpu.roll(y, HEAD_DIM // 2, 1) * sin


def _norm_cast_kernel(x_ref, g_ref, o_ref):
    o_ref[...] = _rms(x_ref[...], g_ref[...]).astype(o_ref.dtype)


def norm_cast(x, g):
    m, d = x.shape
    tm = _tile(m, 256, 16)
    row = pl.BlockSpec((tm, d), lambda i: (i, 0))
    return pl.pallas_call(
        _norm_cast_kernel, grid=(m // tm,),
        in_specs=[row, pl.BlockSpec((1, d), lambda i: (0, 0))],
        out_specs=row, out_shape=jax.ShapeDtypeStruct((m, d), BF16),
        compiler_params=_cp("arbitrary"), name="norm_cast")(x, g.reshape(1, d))


def _resid_norm_kernel(x_ref, y_ref, gp_ref, gn_ref, xo_ref, h_ref):
    xn = x_ref[...] + _rms(y_ref[...], gp_ref[...])
    xo_ref[...] = xn
    h_ref[...] = _rms(xn, gn_ref[...]).astype(h_ref.dtype)


def _resid_kernel(x_ref, y_ref, gp_ref, xo_ref):
    xo_ref[...] = x_ref[...] + _rms(y_ref[...], gp_ref[...])


def resid_norm(x, y, g_post, g_next):
    m, d = x.shape
    tm = _tile(m, 256, 16)
    row = pl.BlockSpec((tm, d), lambda i: (i, 0))
    vec = pl.BlockSpec((1, d), lambda i: (0, 0))
    if g_next is None:
        return pl.pallas_call(
            _resid_kernel, grid=(m // tm,), in_specs=[row, row, vec], out_specs=row,
            out_shape=jax.ShapeDtypeStruct((m, d), F32),
            compiler_params=_cp("arbitrary"), name="resid")(x, y, g_post.reshape(1, d)), None
    return pl.pallas_call(
        _resid_norm_kernel, grid=(m // tm,), in_specs=[row, row, vec, vec],
        out_specs=[row, row],
        out_shape=[jax.ShapeDtypeStruct((m, d), F32), jax.ShapeDtypeStruct((m, d), BF16)],
        compiler_params=_cp("arbitrary"), name="resid_norm")(
            x, y, g_post.reshape(1, d), g_next.reshape(1, d))


def _mm_kernel(*refs, nx, nw, ne, groups, epilogue):
    xr, wr = refs[:nx], refs[nx:nx + nw]
    er, orf = refs[nx + nw:nx + nw + ne], refs[nx + nw + ne:]
    accs = []
    for grp in groups:
        acc = None
        for xi, wi in grp:
            d = _dot(xr[xi][...], wr[wi][...])
            acc = d if acc is None else acc + d
        accs.append(acc)
    epilogue(accs, er, orf)


def _ep_plain(accs, er, orf):
    orf[0][...] = accs[0].astype(orf[0].dtype)


def _ep_rope(accs, er, orf):
    cos, sin = er[0][...], er[1][...]
    y = accs[0]
    for c in range(y.shape[1] // HEAD_DIM):
        sl = slice(c * HEAD_DIM, (c + 1) * HEAD_DIM)
        orf[0][:, sl] = _rope(y[:, sl], cos, sin).astype(orf[0].dtype)


def _ep_split(accs, er, orf, *, rope_chunks):
    cos, sin = er[0][...], er[1][...]
    y = accs[0]
    for c in range(len(orf)):
        yc = y[:, c * HEAD_DIM:(c + 1) * HEAD_DIM]
        if c in rope_chunks:
            yc = _rope(yc, cos, sin)
        orf[c][...] = yc.astype(orf[c].dtype)


def matmul(xs, ws, groups, epilogue, out_dtypes, *, tables=None, n_tab_blocks=1,
           tn_pref=512, split_out=0, name="matmul"):
    m = xs[0].shape[0]
    n = ws[0].shape[1]
    if tables is not None:
        tm = tables[0].shape[0] // n_tab_blocks
    else:
        tm = _tile(m, 1024, 16)
    tn = n if (split_out or n % LANES) else _tile(n, tn_pref, LANES)
    in_specs = [pl.BlockSpec((tm, x.shape[1]), lambda i, j: (i, 0)) for x in xs]
    in_specs += [pl.BlockSpec((w.shape[0], tn), lambda i, j: (0, j)) for w in ws]
    extras = []
    if tables is not None:
        nb = n_tab_blocks
        in_specs += [pl.BlockSpec((tm, HEAD_DIM), lambda i, j: (i % nb, 0))] * 2
        extras = list(tables)
    if split_out:
        out_specs = [pl.BlockSpec((tm, HEAD_DIM), lambda i, j: (i, 0))] * split_out
        out_shape = [jax.ShapeDtypeStruct((m, HEAD_DIM), dt) for dt in out_dtypes]
    else:
        out_specs = [pl.BlockSpec((tm, tn), lambda i, j: (i, j))]
        out_shape = [jax.ShapeDtypeStruct((m, n), out_dtypes[0])]
    kern = functools.partial(_mm_kernel, nx=len(xs), nw=len(ws), ne=len(extras),
                             groups=groups, epilogue=epilogue)
    out = pl.pallas_call(
        kern, grid=(m // tm, n // tn), in_specs=in_specs, out_specs=out_specs,
        out_shape=out_shape, compiler_params=_cp("arbitrary", "arbitrary"), name=name)(
            *xs, *ws, *extras)
    return out if split_out else out[0]


def proj(h, w, dtype=F32, name="proj"):
    return matmul([h], [w], [[(0, 0)]], _ep_plain, [dtype], name=name)


def proj_rope(h, w, tabs, dtype, name="proj_rope"):
    cos, sin, nb = tabs
    return matmul([h], [w], [[(0, 0)]], _ep_rope, [dtype], tables=(cos, sin),
                  n_tab_blocks=nb, name=name)


def proj_out(xa, xb, wa, wb, name="proj_out"):
    return matmul([xa, xb], [wa, wb], [[(0, 0), (1, 1)]], _ep_plain, [F32], name=name)


def rope_tables(pos, reps, tm):
    half = HEAD_DIM // 2
    inv = ROPE_THETA ** (-jnp.arange(half, dtype=F32) / half)
    ang = pos.astype(F32)[:, None] * inv[None, :]
    cos, sin = jnp.cos(ang), jnp.sin(ang)
    cos = jnp.concatenate([cos, cos], axis=-1)
    sin = jnp.concatenate([-sin, sin], axis=-1)
    t = pos.shape[0]
    if tm > t:
        cos, sin = jnp.tile(cos, (tm // t, 1)), jnp.tile(sin, (tm // t, 1))
        return cos, sin, 1
    return cos, sin, t // tm


def _conv_kernel(x_ref, wb_ref, wc_ref, wx_ref, cw_ref, init_ref, ya_ref, st_ref, carry_ref):
    i = pl.program_id(2)

    @pl.when(i == 0)
    def _():
        carry_ref[...] = init_ref[0]

    x = x_ref[0]
    bg = _dot(x, wb_ref[...])
    u = _dot(x, wc_ref[...]) * _dot(x, wx_ref[...])
    tm = u.shape[0]
    c = carry_ref[...]
    rows = lax.broadcasted_iota(jnp.int32, u.shape, 0)
    u1 = jnp.where(rows == 0, c[1:2], pltpu.roll(u, 1, 0))
    u2 = jnp.where(rows == 0, c[0:1], jnp.where(rows == 1, c[1:2], pltpu.roll(u, 2, 0)))
    cw = cw_ref[...]
    conv = cw[0:1] * u2 + cw[1:2] * u1 + cw[2:3] * u
    ya_ref[0] = (bg * conv).astype(ya_ref.dtype)
    new = u[tm - (CONV_K - 1):tm]
    carry_ref[...] = new
    st_ref[0] = new


def conv_mixer(h3, wb, wc, wx, cw, init):
    b, t, d = h3.shape
    c = wb.shape[1]
    tm = _tile(t, 1024, 16)
    tn = _tile(c, 512, LANES)
    wspec = pl.BlockSpec((d, tn), lambda j, bi, i: (0, j))
    return pl.pallas_call(
        _conv_kernel, grid=(c // tn, b, t // tm),
        in_specs=[pl.BlockSpec((1, tm, d), lambda j, bi, i: (bi, i, 0)), wspec, wspec, wspec,
                  pl.BlockSpec((CONV_K, tn), lambda j, bi, i: (0, j)),
                  pl.BlockSpec((1, CONV_K - 1, tn), lambda j, bi, i: (bi, 0, j))],
        out_specs=[pl.BlockSpec((1, tm, tn), lambda j, bi, i: (bi, i, j)),
                   pl.BlockSpec((1, CONV_K - 1, tn), lambda j, bi, i: (bi, 0, j))],
        out_shape=[jax.ShapeDtypeStruct((b, t, c), BF16),
                   jax.ShapeDtypeStruct((b, CONV_K - 1, c), F32)],
        scratch_shapes=[pltpu.VMEM((CONV_K - 1, tn), F32)],
        compiler_params=_cp("arbitrary", "arbitrary", "arbitrary"), name="conv_mixer")(
            h3, wb, wc, wx, cw, init)


def _stack_heads(q_ref, heads):
    return jnp.concatenate([q_ref[:, h * HEAD_DIM:(h + 1) * HEAD_DIM] for h in heads], axis=0)


def _softmax_av(s, bias, v):
    s = s + bias[None]
    e = jnp.exp(s - jnp.max(s, axis=-1, keepdims=True))
    l = jnp.sum(e, axis=-1, keepdims=True)
    r, tq, n = e.shape
    o = _dot(e.reshape(r * tq, n).astype(BF16), v)
    return o / l.reshape(r * tq, 1)


def _sort_key(x):
    bits = lax.bitcast_convert_type(x + 0.0, jnp.int32)
    return jnp.where(bits < 0, bits ^ jnp.int32(0x7FFFFFFF), bits)


INT_MIN = -2 ** 31


def _kth_largest_key(key, k):
    def body(it, othr):
        bit = lax.shift_left(jnp.int32(1), jnp.int32(31) - it)
        cand = othr | bit
        cnt = jnp.sum(jnp.where(key >= (cand ^ jnp.int32(INT_MIN)), 1.0, 0.0), axis=1, keepdims=True)
        return jnp.where(cnt >= k, cand, othr)
    othr = lax.fori_loop(0, 32, body, jnp.zeros((key.shape[0], 1), jnp.int32))
    return othr ^ jnp.int32(INT_MIN)


def _topk_mask(key, k, scratch_ref):
    thr = _kth_largest_key(key, k)
    live = key > jnp.int32(INT_MIN)
    ge = (key >= thr) & live
    scratch_ref[...] = jnp.where(ge, 1.0, 0.0)
    n_ge = jnp.sum(jnp.where(ge, 1.0, 0.0), axis=1, keepdims=True)

    @pl.when(jnp.max(n_ge) > k)
    def _():
        gt = key > thr
        eq = (key == thr) & live
        need = k - jnp.sum(jnp.where(gt, 1.0, 0.0), axis=1, keepdims=True)
        r = lax.broadcasted_iota(jnp.int32, (LANES, LANES), 0)
        c = lax.broadcasted_iota(jnp.int32, (LANES, LANES), 1)
        before = jnp.where(r < c, 1.0, 0.0).astype(BF16)
        run = jnp.zeros_like(need)
        for ch in range(key.shape[1] // LANES):
            sl = slice(ch * LANES, (ch + 1) * LANES)
            e = jnp.where(eq[:, sl], 1.0, 0.0)
            pre = _dot(e.astype(BF16), before) + run
            scratch_ref[:, sl] = jnp.where(gt[:, sl], 1.0, e * jnp.where(pre < need, 1.0, 0.0))
            run = run + jnp.sum(e, axis=1, keepdims=True)


def _dsa_prompt_kernel(qi_ref, wi_ref, ki_ref, q_ref, k_ref, v_ref, o_ref,
                       kib_ref, kb_ref, vb_ref, mask_ref, *, topk):
    i = pl.program_id(1)

    @pl.when(i == 0)
    def _():
        kib_ref[...] = ki_ref[...].astype(BF16)
        kb_ref[...] = k_ref[...].astype(BF16)
        vb_ref[...] = v_ref[...].astype(BF16)

    tq = q_ref.shape[0]
    t = ki_ref.shape[0]
    kib = kib_ref[...]
    wi = wi_ref[...]
    score = jnp.zeros((tq, t), F32)
    for h in range(H_IDX):
        s = _dot_nt(qi_ref[:, h * D_IDX:(h + 1) * D_IDX], kib) * D_IDX ** -0.5
        score = score + jnp.maximum(s, 0.0) * wi[:, h:h + 1]
    score = score * H_IDX ** -0.5
    qpos = i * tq + lax.broadcasted_iota(jnp.int32, (tq, t), 0)
    kpos = lax.broadcasted_iota(jnp.int32, (tq, t), 1)
    key = jnp.where(kpos <= qpos, _sort_key(score), jnp.int32(INT_MIN))
    _topk_mask(key, topk, mask_ref)
    bias = jnp.where(mask_ref[...] > 0.0, 0.0, NEG)
    rep = q_ref.shape[1] // HEAD_DIM // KV_B
    for g in range(KV_B):
        qs = _stack_heads(q_ref, range(g * rep, (g + 1) * rep))
        sl = slice(g * HEAD_DIM, (g + 1) * HEAD_DIM)
        s = _dot_nt(qs, kb_ref[:, sl]) * HEAD_DIM ** -0.5
        o = _softmax_av(s.reshape(rep, tq, t), bias, vb_ref[:, sl])
        for r in range(rep):
            h = g * rep + r
            o_ref[:, h * HEAD_DIM:(h + 1) * HEAD_DIM] = o[r * tq:(r + 1) * tq].astype(o_ref.dtype)


def dsa_prompt(qi, wi, ki, q, k, v, b, t):
    m = q.shape[0]
    tq = _tile(t, 128, 16)
    nq = t // tq
    topk = min(TOPK_MAX, t // 4)
    qrow = lambda w: pl.BlockSpec((tq, w), lambda bi, i: (bi * nq + i, 0))
    full = lambda w: pl.BlockSpec((t, w), lambda bi, i: (bi, 0))
    return pl.pallas_call(
        functools.partial(_dsa_prompt_kernel, topk=topk), grid=(b, nq),
        in_specs=[qrow(qi.shape[1]), qrow(wi.shape[1]), full(ki.shape[1]),
                  qrow(q.shape[1]), full(k.shape[1]), full(v.shape[1])],
        out_specs=qrow(q.shape[1]),
        out_shape=jax.ShapeDtypeStruct((m, q.shape[1]), BF16),
        scratch_shapes=[pltpu.VMEM((t, ki.shape[1]), BF16), pltpu.VMEM((t, k.shape[1]), BF16),
                        pltpu.VMEM((t, v.shape[1]), BF16), pltpu.VMEM((tq, t), F32)],
        compiler_params=_cp("arbitrary", "arbitrary"), name="dsa_prompt")(qi, wi, ki, q, k, v)


def _masked_softmax(s, mask):
    m = jnp.max(jnp.where(mask, s, NEG), axis=-1, keepdims=True)
    m = jnp.where(m > 0.5 * NEG, m, 0.0)
    e = jnp.where(mask, jnp.exp(s - m), 0.0)
    return e / jnp.maximum(jnp.sum(e, axis=-1, keepdims=True), 1e-30)


def _pair_sums(imp, col):
    n = imp.shape[1]
    return imp + jnp.where(col % 2 == 0, pltpu.roll(imp, n - 1, 1), pltpu.roll(imp, 1, 1))


def _select_blocks(bs, col, qpos, n_selblk):
    blk = col // 2
    cur = qpos // SEL_BLOCK
    forced = (blk == 0) | (blk == cur) | (blk == cur - 1)
    admiss = (blk * SEL_BLOCK <= qpos) & (blk < n_selblk)
    work = jnp.where(admiss, jnp.where(forced, bs + FORCE_BONUS, bs), NEG)
    sel = jnp.zeros(bs.shape, jnp.bool_)
    big = jnp.int32(2 ** 30)
    for _ in range(min(N_SEL, n_selblk)):
        mx = jnp.max(work, axis=1, keepdims=True)
        idx = jnp.min(jnp.where(work == mx, col, big), axis=1, keepdims=True)
        pick = blk == idx // 2
        sel = sel | pick
        work = jnp.where(pick, -3e38, work)
    return jnp.where(sel & admiss, 1.0, 0.0)


def _nsa_prompt_kernel(q_ref, gc_ref, kc_ref, vc_ref, ks_ref, vs_ref, kw_ref, vw_ref,
                       cwk_ref, cwv_ref, o_ref,
                       kcmp_ref, vcmp_ref, ksb_ref, vsb_ref, kwb_ref, vwb_ref, *, win):
    i = pl.program_id(1)
    tq = q_ref.shape[0]
    t = kc_ref.shape[0]
    nb = t // CMP_BLOCK
    ncp = kcmp_ref.shape[0]
    n_selblk = -(-t // SEL_BLOCK)
    nh = q_ref.shape[1] // HEAD_DIM
    scale = HEAD_DIM ** -0.5

    @pl.when(i == 0)
    def _():
        kcmp_ref[...] = jnp.zeros(kcmp_ref.shape, kcmp_ref.dtype)
        vcmp_ref[...] = jnp.zeros(vcmp_ref.shape, vcmp_ref.dtype)
        kc = kc_ref[...].reshape(nb, CMP_BLOCK, HEAD_DIM)
        vc = vc_ref[...].reshape(nb, CMP_BLOCK, HEAD_DIM)
        kcmp_ref[0:nb] = jnp.sum(kc * cwk_ref[...][None], axis=1).astype(BF16)
        vcmp_ref[0:nb] = jnp.sum(vc * cwv_ref[...][None], axis=1).astype(BF16)
        ksb_ref[...] = ks_ref[...].astype(BF16)
        vsb_ref[...] = vs_ref[...].astype(BF16)
        kwb_ref[...] = kw_ref[...].astype(BF16)
        vwb_ref[...] = vw_ref[...].astype(BF16)

    qs = _stack_heads(q_ref, range(nh))
    col = lax.broadcasted_iota(jnp.int32, (tq, ncp), 1)
    qpos_c = i * tq + lax.broadcasted_iota(jnp.int32, (tq, ncp), 0)
    cmask = ((col + 1) * CMP_BLOCK - 1 <= qpos_c) & (col < nb)
    s = (_dot_nt(qs, kcmp_ref[...]) * scale).reshape(nh, tq, ncp)
    p = _masked_softmax(s, cmask[None])
    o_cmp = _dot(p.reshape(nh * tq, ncp).astype(BF16), vcmp_ref[...])
    imp = jnp.sum(p, axis=0)
    sel = _select_blocks(_pair_sums(imp, col), col, qpos_c, n_selblk)
    er = lax.broadcasted_iota(jnp.int32, (ncp, t), 0)
    ec = lax.broadcasted_iota(jnp.int32, (ncp, t), 1)
    expand = jnp.where(er == 2 * (ec // SEL_BLOCK), 1.0, 0.0).astype(BF16)
    tok = _dot(sel.astype(BF16), expand)
    qpos = i * tq + lax.broadcasted_iota(jnp.int32, (tq, t), 0)
    kpos = lax.broadcasted_iota(jnp.int32, (tq, t), 1)
    bias_sel = jnp.where((tok > 0.5) & (kpos <= qpos), 0.0, NEG)
    start = pl.multiple_of(jnp.clip(i * tq - WINDOW, 0, t - win), 16)
    qpos_w = i * tq + lax.broadcasted_iota(jnp.int32, (tq, win), 0)
    kpos_w = start + lax.broadcasted_iota(jnp.int32, (tq, win), 1)
    bias_win = jnp.where((kpos_w <= qpos_w) & (kpos_w > qpos_w - WINDOW), 0.0, NEG)
    kwin = kwb_ref[pl.ds(start, win), :]
    vwin = vwb_ref[pl.ds(start, win), :]
    gate = jax.nn.sigmoid(gc_ref[...])
    grp = 4
    for hg in range(nh // grp):
        q4 = qs[hg * grp * tq:(hg + 1) * grp * tq]
        s_sel = (_dot_nt(q4, ksb_ref[...]) * scale).reshape(grp, tq, t)
        o_sel = _softmax_av(s_sel, bias_sel, vsb_ref[...])
        s_win = (_dot_nt(q4, kwin) * scale).reshape(grp, tq, win)
        o_win = _softmax_av(s_win, bias_win, vwin)
        for r in range(grp):
            h = hg * grp + r
            oc = o_cmp[h * tq:(h + 1) * tq]
            o = (gate[:, 3 * h:3 * h + 1] * oc + gate[:, 3 * h + 1:3 * h + 2] * o_sel[r * tq:(r + 1) * tq]
                 + gate[:, 3 * h + 2:3 * h + 3] * o_win[r * tq:(r + 1) * tq])
            o_ref[:, h * HEAD_DIM:(h + 1) * HEAD_DIM] = o.astype(o_ref.dtype)


def nsa_prompt(q, gc, kc, vc, ks, vs, kw, vw, cwk, cwv, b, t):
    m, hq = q.shape
    tq = _tile(t, 128, 16)
    nq = t // tq
    win = min(WINDOW + tq, t)
    ncp = LANES
    assert t // CMP_BLOCK <= ncp and t % SEL_BLOCK == 0
    qrow = lambda w: pl.BlockSpec((tq, w), lambda bi, i: (bi * nq + i, 0))
    full = pl.BlockSpec((t, HEAD_DIM), lambda bi, i: (bi, 0))
    cw = pl.BlockSpec((CMP_BLOCK, HEAD_DIM), lambda bi, i: (0, 0))
    kv = pltpu.VMEM((t, HEAD_DIM), BF16)
    return pl.pallas_call(
        functools.partial(_nsa_prompt_kernel, win=win), grid=(b, nq),
        in_specs=[qrow(hq), qrow(gc.shape[1])] + [full] * 6 + [cw, cw],
        out_specs=qrow(hq), out_shape=jax.ShapeDtypeStruct((m, hq), BF16),
        scratch_shapes=[pltpu.VMEM((ncp, HEAD_DIM), BF16), pltpu.VMEM((ncp, HEAD_DIM), BF16),
                        kv, kv, kv, kv],
        compiler_params=_cp("arbitrary", "arbitrary"), name="nsa_prompt")(
            q, gc, kc, vc, ks, vs, kw, vw, cwk, cwv)


def _log_sigmoids(z):
    sp = jnp.log1p(jnp.exp(-jnp.abs(z)))
    return jnp.minimum(z, 0.0) - sp, jnp.minimum(-z, 0.0) - sp


def _split_bf16(x):
    hi = x.astype(BF16)
    return hi, (x - hi.astype(F32)).astype(BF16)


def _sb_prompt_kernel(q_ref, k_ref, v_ref, o_ref, *, sk):
    i = pl.program_id(2)
    tq = q_ref.shape[0]
    q = q_ref[...]
    nkb = ((i + 1) * tq) // sk
    qpos = i * tq + lax.broadcasted_iota(jnp.int32, (tq, sk), 0)
    kcol = lax.broadcasted_iota(jnp.int32, (tq, sk), 1)
    r = lax.broadcasted_iota(jnp.int32, (sk, sk), 0)
    c = lax.broadcasted_iota(jnp.int32, (sk, sk), 1)
    later = jnp.where(r > c, 1.0, 0.0).astype(BF16)

    def body(jj, st):
        carry, acc = st
        off = pl.multiple_of((nkb - 1 - jj) * sk, sk)
        kb = k_ref[pl.ds(off, sk), :].astype(BF16)
        vb = v_ref[pl.ds(off, sk), :].astype(BF16)
        z = _dot_nt(q, kb) * HEAD_DIM ** -0.5
        m = off + kcol < qpos
        ls, lneg = _log_sigmoids(z)
        lneg = jnp.where(m, lneg, 0.0)
        hi, lo = _split_bf16(lneg)
        log_a = ls + (_dot(hi, later) + _dot(lo, later)) + carry
        a = jnp.where(m, jnp.exp(log_a), 0.0)
        acc = acc + _dot(a.astype(BF16), vb)
        return carry + jnp.sum(lneg, axis=1, keepdims=True), acc

    _, acc = lax.fori_loop(0, nkb, body, (jnp.zeros((tq, 1), F32), jnp.zeros((tq, HEAD_DIM), F32)))
    o_ref[...] = acc.astype(o_ref.dtype)


def sb_prompt(q, k, v, b, t):
    m, hq = q.shape
    nh = hq // HEAD_DIM
    tq = _tile(t, 256, 16)
    sk = _tile(tq, 128, 16)
    nq = t // tq
    qrow = pl.BlockSpec((tq, HEAD_DIM), lambda bi, h, i: (bi * nq + i, h))
    full = pl.BlockSpec((t, HEAD_DIM), lambda bi, h, i: (bi, h))
    return pl.pallas_call(
        functools.partial(_sb_prompt_kernel, sk=sk), grid=(b, nh, nq),
        in_specs=[qrow, full, full], out_specs=qrow,
        out_shape=jax.ShapeDtypeStruct((m, hq), BF16),
        compiler_params=_cp("arbitrary", "arbitrary", "arbitrary"), name="sb_prompt")(q, k, v)


def _mem_kernel(h_ref, wq_ref, mk_ref, mv_ref, wo_ref, o_ref):
    q = _dot(h_ref[0], wq_ref[...])
    outs = []
    for hh in range(H_MEM):
        sl = slice(hh * HEAD_DIM, (hh + 1) * HEAD_DIM)
        s = _dot_nt(q[:, sl].astype(BF16), mk_ref[0, :, sl].astype(BF16)) * HEAD_DIM ** -0.5
        e = jnp.exp(s - jnp.max(s, axis=-1, keepdims=True))
        o = _dot(e.astype(BF16), mv_ref[0, :, sl].astype(BF16))
        outs.append(o / jnp.sum(e, axis=-1, keepdims=True))
    o_ref[0] = _dot(jnp.concatenate(outs, axis=1).astype(BF16), wo_ref[...])


def mem_sublayer(h3, wq, mk, mv, wo):
    b, t, d = h3.shape
    tq = _tile(t, 256, 16)
    nm, hm = mk.shape[1], mk.shape[2]
    row = pl.BlockSpec((1, tq, d), lambda bi, i: (bi, i, 0))
    mem = pl.BlockSpec((1, nm, hm), lambda bi, i: (bi, 0, 0))
    return pl.pallas_call(
        _mem_kernel, grid=(b, t // tq),
        in_specs=[row, pl.BlockSpec((d, hm), lambda bi, i: (0, 0)), mem, mem,
                  pl.BlockSpec((hm, d), lambda bi, i: (0, 0))],
        out_specs=row, out_shape=jax.ShapeDtypeStruct((b, t, d), F32),
        compiler_params=_cp("arbitrary", "arbitrary"), name="mem_sublayer")(h3, wq, mk, mv, wo)


def _ffn_kernel(h_ref, wg_ref, wu_ref, wd_ref, o_ref):
    f = pl.program_id(1)
    h = h_ref[...]
    g = _dot(h, wg_ref[...])
    a = (g * jax.nn.sigmoid(g)) * _dot(h, wu_ref[...])
    part = _dot(a.astype(BF16), wd_ref[...])

    @pl.when(f == 0)
    def _():
        o_ref[...] = part

    @pl.when(f > 0)
    def _():
        o_ref[...] += part


def ffn(h, wg, wu, wd):
    m, d = h.shape
    dff = wg.shape[1]
    tm = _tile(m, 512, 16)
    tf = _tile(dff, 256, LANES)
    return pl.pallas_call(
        _ffn_kernel, grid=(m // tm, dff // tf),
        in_specs=[pl.BlockSpec((tm, d), lambda i, f: (i, 0)),
                  pl.BlockSpec((d, tf), lambda i, f: (0, f)),
                  pl.BlockSpec((d, tf), lambda i, f: (0, f)),
                  pl.BlockSpec((tf, d), lambda i, f: (f, 0))],
        out_specs=pl.BlockSpec((tm, d), lambda i, f: (i, 0)),
        out_shape=jax.ShapeDtypeStruct((m, d), F32),
        compiler_params=_cp("arbitrary", "arbitrary"), name="ffn")(h, wg, wu, wd)


def _odd_weights(w_in, w_out, nh_c, nh_d):
    sizes = [nh_c * HEAD_DIM] + [HEAD_DIM] * 6 + [nh_c * 3] + [nh_d * HEAD_DIM] * 3
    offs = np.cumsum([0] + sizes)
    cut = lambda a, b_: w_in[:, offs[a]:offs[b_]].astype(BF16)
    w = {"qc": cut(0, 1), "kv6": cut(1, 7), "gc": cut(7, 8), "qd": cut(8, 9), "kd": cut(9, 10),
         "vd": cut(10, 11)}
    w["out_c"] = w_out[:nh_c * HEAD_DIM].astype(BF16)
    w["out_d"] = w_out[nh_c * HEAD_DIM:].astype(BF16)
    return w


def _odd_project(h, w, tabs):
    cos, sin, nb = tabs
    qc = proj_rope(h, w["qc"], tabs, BF16, name="proj_qc")
    kv6 = matmul([h], [w["kv6"]], [[(0, 0)]], functools.partial(_ep_split, rope_chunks=(0, 2, 4)),
                 [F32] * 6, tables=(cos, sin), n_tab_blocks=nb, split_out=6, name="proj_kv6")
    gc = proj(h, w["gc"], name="proj_gc")
    qd = proj(h, w["qd"], BF16, name="proj_qd")
    kd = proj(h, w["kd"], name="proj_kd")
    vd = proj(h, w["vd"], name="proj_vd")
    return qc, kv6, gc, qd, kd, vd


def odd_mixer_prompt(h, b, t, w, cwk, cwv):
    m, d = h.shape
    tabs = rope_tables(jnp.arange(t), b, _tile(m, 1024, 16))
    qc, (kc, vc, ks, vs, kw, vw), gc, qd, kd, vd = _odd_project(h, w, tabs)
    o_c = nsa_prompt(qc, gc, kc, vc, ks, vs, kw, vw, cwk, cwv, b, t)
    o_d = sb_prompt(qd, kd, vd, b, t)
    y = proj_out(o_c, o_d, w["out_c"], w["out_d"])
    nw = min(WINDOW, t)
    st1 = lambda a: a.reshape(b, t, 1, HEAD_DIM)
    sth = lambda a: a.reshape(b, t, -1, HEAD_DIM)
    return y, (st1(kc), st1(vc), st1(ks), st1(vs), st1(kw)[:, t - nw:], st1(vw)[:, t - nw:],
               sth(kd), sth(vd))


def _even_weights(w_in, conv_w, w_out):
    c = conv_w.shape[1]
    hq = w_out.shape[0] - c
    sizes = [c, c, c, hq, KV_B * HEAD_DIM, KV_B * HEAD_DIM, H_IDX * D_IDX, D_IDX, H_IDX]
    offs = np.cumsum([0] + sizes)
    names = ["bg", "cg", "xa", "q", "k", "v", "qi", "ki", "wi"]
    w = {n: w_in[:, offs[j]:offs[j + 1]].astype(BF16) for j, n in enumerate(names)}
    w["conv_w"] = conv_w
    w["out_a"] = w_out[:c].astype(BF16)
    w["out_b"] = w_out[c:].astype(BF16)
    return w


def _even_project(h, w, tabs):
    q = proj_rope(h, w["q"], tabs, BF16, name="proj_q")
    k = proj_rope(h, w["k"], tabs, F32, name="proj_k")
    v = proj(h, w["v"], name="proj_v")
    qi = proj_rope(h, w["qi"], tabs, BF16, name="proj_qi")
    ki = proj_rope(h, w["ki"], tabs, F32, name="proj_ki")
    wi = proj(h, w["wi"], name="proj_wi")
    return q, k, v, qi, ki, wi


def even_mixer_prompt(h, b, t, w):
    m, d = h.shape
    tabs = rope_tables(jnp.arange(t), b, _tile(m, 1024, 16))
    init = jnp.zeros((b, CONV_K - 1, w["bg"].shape[1]), F32)
    ya, conv_state = conv_mixer(h.reshape(b, t, d), w["bg"], w["cg"], w["xa"], w["conv_w"], init)
    q, k, v, qi, ki, wi = _even_project(h, w, tabs)
    ob = dsa_prompt(qi, wi, ki, q, k, v, b, t)
    y = proj_out(ya.reshape(m, -1), ob, w["out_a"], w["out_b"])
    return y, (conv_state, k.reshape(b, t, KV_B, HEAD_DIM), v.reshape(b, t, KV_B, HEAD_DIM),
               ki.reshape(b, t, D_IDX))


def _paged_call(kern, grid, in_specs, out_specs, out_shape, scratch, name, page_table, args):
    gs = pltpu.PrefetchScalarGridSpec(num_scalar_prefetch=1, grid=grid, in_specs=in_specs,
                                      out_specs=out_specs, scratch_shapes=scratch)
    return pl.pallas_call(kern, grid_spec=gs, out_shape=out_shape,
                          compiler_params=_cp("arbitrary", "arbitrary"), name=name)(page_table, *args)


def _pad_rows(a, rows):
    return jnp.pad(a, ((0, 0), (0, rows - a.shape[1]), (0, 0)))


def _rows_ht(a, b, t, nh):
    w = a.shape[1] // nh
    return a.reshape(b, t, nh, w).transpose(0, 2, 1, 3).reshape(b, nh * t, w)


def _rows_th(a, b, t, nh):
    w = a.shape[2]
    return a.reshape(b, nh, t, w).transpose(0, 2, 1, 3).reshape(b * t, nh * w)


def _dsa_scores_kernel(pt_ref, qi_ref, wi_ref, pool_ref, new_ref, o_ref, *, n_pages, t):
    p = pl.program_id(1)
    kp = jnp.where(p == n_pages, new_ref[0], pool_ref[0, 0]).astype(BF16)
    s = _dot_nt(qi_ref[0], kp) * D_IDX ** -0.5
    s = jnp.maximum(s, 0.0) * wi_ref[0]
    o_ref[0] = jnp.sum(s.reshape(H_IDX, t, PAGE_SIZE), axis=0) * H_IDX ** -0.5


def dsa_sample_scores(qi_r, wi_r, ki_new, pool, e, page_table, t):
    b, n_pages = page_table.shape
    last = n_pages - 1
    kern = functools.partial(_dsa_scores_kernel, n_pages=n_pages, t=t)
    per_b = lambda w: pl.BlockSpec((1, H_IDX * t, w), lambda bi, p, pt: (bi, 0, 0))
    return _paged_call(
        kern, (b, n_pages + 1),
        [per_b(D_IDX), per_b(1),
         pl.BlockSpec((1, 1, PAGE_SIZE, D_IDX), lambda bi, p, pt: (e, pt[bi, jnp.minimum(p, last)], 0, 0)),
         pl.BlockSpec((1, PAGE_SIZE, D_IDX), lambda bi, p, pt: (bi, 0, 0))],
        pl.BlockSpec((1, t, PAGE_SIZE), lambda bi, p, pt: (bi, 0, p)),
        jax.ShapeDtypeStruct((b, t, (n_pages + 1) * PAGE_SIZE), F32), [], "dsa_sample_scores",
        page_table, (qi_r, wi_r, pool, _pad_rows(ki_new, PAGE_SIZE)))


def _dsa_topk_kernel(s_ref, o_ref, mask_ref, *, past, topk):
    score = s_ref[0]
    kpos = lax.broadcasted_iota(jnp.int32, score.shape, 1)
    qpos = past + lax.broadcasted_iota(jnp.int32, score.shape, 0)
    key = jnp.where(kpos <= qpos, _sort_key(score), jnp.int32(INT_MIN))
    _topk_mask(key, topk, mask_ref)
    o_ref[0] = mask_ref[...]


def dsa_sample_topk(score, past, topk):
    b, t, nk = score.shape
    blk = pl.BlockSpec((1, t, nk), lambda bi: (bi, 0, 0))
    return pl.pallas_call(
        functools.partial(_dsa_topk_kernel, past=past, topk=topk), grid=(b,),
        in_specs=[blk], out_specs=blk, out_shape=jax.ShapeDtypeStruct((b, t, nk), F32),
        scratch_shapes=[pltpu.VMEM((t, nk), F32)],
        compiler_params=_cp("arbitrary"), name="dsa_sample_topk")(score)


def _online_softmax_step(s, valid, v, m_ref, l_ref, acc_ref):
    m_old = m_ref[...]
    m_new = jnp.maximum(m_old, jnp.max(jnp.where(valid, s, NEG), axis=-1, keepdims=True))
    alpha = jnp.exp(m_old - m_new)
    e = jnp.where(valid, jnp.exp(s - m_new), 0.0)
    l_ref[...] = alpha * l_ref[...] + jnp.sum(e, axis=-1, keepdims=True)
    acc_ref[...] = alpha * acc_ref[...] + _dot(e.astype(BF16), v)
    m_ref[...] = m_new


def _dsa_sample_attn_kernel(pt_ref, q_ref, mask_ref, kpool_ref, vpool_ref, knew_ref, vnew_ref, o_ref,
                            m_ref, l_ref, acc_ref, *, n_pages, rep):
    p = pl.program_id(1)

    @pl.when(p == 0)
    def _():
        m_ref[...] = jnp.full(m_ref.shape, NEG, F32)
        l_ref[...] = jnp.zeros(l_ref.shape, F32)
        acc_ref[...] = jnp.zeros(acc_ref.shape, F32)

    is_new = p == n_pages
    kp = jnp.where(is_new, knew_ref[0], kpool_ref[0, 0]).astype(BF16)
    vp = jnp.where(is_new, vnew_ref[0], vpool_ref[0, 0]).astype(BF16)
    valid = jnp.concatenate([mask_ref[0] > 0.5] * rep, axis=0)
    for g in range(KV_B):
        sl = slice(g * HEAD_DIM, (g + 1) * HEAD_DIM)
        s = _dot_nt(q_ref[0, g], kp[:, sl]) * HEAD_DIM ** -0.5
        _online_softmax_step(s, valid, vp[:, sl], m_ref.at[g], l_ref.at[g], acc_ref.at[g])

    @pl.when(is_new)
    def _():
        o_ref[0] = acc_ref[...] / l_ref[...]


def dsa_sample_attn(q_g, mask, kpool, vpool, k_new, v_new, e, page_table):
    b, n_pages = page_table.shape
    t = mask.shape[1]
    rows = q_g.shape[2]
    last = n_pages - 1
    wkv = KV_B * HEAD_DIM
    kern = functools.partial(_dsa_sample_attn_kernel, n_pages=n_pages, rep=rows // t)
    qspec = pl.BlockSpec((1, KV_B, rows, HEAD_DIM), lambda bi, p, pt: (bi, 0, 0, 0))
    pool = pl.BlockSpec((1, 1, PAGE_SIZE, wkv), lambda bi, p, pt: (e, pt[bi, jnp.minimum(p, last)], 0, 0))
    new = pl.BlockSpec((1, PAGE_SIZE, wkv), lambda bi, p, pt: (bi, 0, 0))
    return _paged_call(
        kern, (b, n_pages + 1),
        [qspec, pl.BlockSpec((1, t, PAGE_SIZE), lambda bi, p, pt: (bi, 0, p)), pool, pool, new, new],
        qspec, jax.ShapeDtypeStruct((b, KV_B, rows, HEAD_DIM), F32),
        [pltpu.VMEM((KV_B, rows, 1), F32), pltpu.VMEM((KV_B, rows, 1), F32),
         pltpu.VMEM((KV_B, rows, HEAD_DIM), F32)],
        "dsa_sample_attn", page_table,
        (q_g, mask, kpool, vpool, _pad_rows(k_new, PAGE_SIZE), _pad_rows(v_new, PAGE_SIZE)))


def even_mixer_sample(h, b, t, w, e, page_table, state_conv, c_k, c_v, c_kidx):
    m, d = h.shape
    n_pages = page_table.shape[1]
    past = n_pages * PAGE_SIZE
    tabs = rope_tables(past + jnp.arange(t), b, _tile(m, 1024, 16))
    ya, conv_state = conv_mixer(h.reshape(b, t, d), w["bg"], w["cg"], w["xa"], w["conv_w"], state_conv[e])
    q, k, v, qi, ki, wi = _even_project(h, w, tabs)
    score = dsa_sample_scores(_rows_ht(qi, b, t, H_IDX), _rows_ht(wi, b, t, H_IDX),
                              ki.reshape(b, t, D_IDX), c_kidx, e, page_table, t)
    mask = dsa_sample_topk(score, past, min(TOPK_MAX, (past + t) // 4))
    nh = q.shape[1] // HEAD_DIM
    rep = nh // KV_B
    q_g = _rows_ht(q, b, t, nh).reshape(b, KV_B, rep * t, HEAD_DIM)
    wkv = KV_B * HEAD_DIM
    n_pool = c_k.shape[1]
    o = dsa_sample_attn(q_g, mask, c_k.reshape(-1, n_pool, PAGE_SIZE, wkv),
                        c_v.reshape(-1, n_pool, PAGE_SIZE, wkv),
                        k.reshape(b, t, wkv), v.reshape(b, t, wkv), e, page_table)
    ob = _rows_th(o.reshape(b, nh * t, HEAD_DIM), b, t, nh).astype(BF16)
    y = proj_out(ya.reshape(m, -1), ob, w["out_a"], w["out_b"])
    return y, (conv_state, k.reshape(b, t, KV_B, HEAD_DIM), v.reshape(b, t, KV_B, HEAD_DIM),
               ki.reshape(b, t, D_IDX))


def _compress_pages_kernel(pt_ref, kc_ref, vc_ref, cwk_ref, cwv_ref, ko_ref, vo_ref):
    nb = PAGE_SIZE // CMP_BLOCK
    kc = kc_ref[0, 0].reshape(nb, CMP_BLOCK, HEAD_DIM)
    vc = vc_ref[0, 0].reshape(nb, CMP_BLOCK, HEAD_DIM)
    ko_ref[0, 0] = jnp.sum(kc * cwk_ref[...][None], axis=1)
    vo_ref[0, 0] = jnp.sum(vc * cwv_ref[...][None], axis=1)


def nsa_compress_pages(kpool, vpool, cwk, cwv, o, page_table):
    b, n_pages = page_table.shape
    nb = PAGE_SIZE // CMP_BLOCK
    pool = pl.BlockSpec((1, 1, PAGE_SIZE, HEAD_DIM), lambda bi, p, pt: (o, pt[bi, p], 0, 0))
    cw = pl.BlockSpec((CMP_BLOCK, HEAD_DIM), lambda bi, p, pt: (0, 0))
    out = pl.BlockSpec((1, 1, nb, HEAD_DIM), lambda bi, p, pt: (bi, p, 0, 0))
    shp = jax.ShapeDtypeStruct((b, n_pages, nb, HEAD_DIM), F32)
    kc, vc = _paged_call(_compress_pages_kernel, (b, n_pages), [pool, pool, cw, cw], [out, out],
                         [shp, shp], [], "nsa_compress_pages", page_table, (kpool, vpool, cwk, cwv))
    return kc.reshape(b, n_pages * nb, HEAD_DIM), vc.reshape(b, n_pages * nb, HEAD_DIM)


def _nsa_sample_a_kernel(q_ref, kcmp_ref, vcmp_ref, kw_ref, vw_ref, ocmp_ref, owin_ref, sel_ref,
                         *, past, t, nwin, n_selblk):
    q = q_ref[0]
    rows = q.shape[0]
    nh = rows // t
    scale = HEAD_DIM ** -0.5
    nb = kcmp_ref.shape[1]
    col = lax.broadcasted_iota(jnp.int32, (rows, nb), 1)
    pos = past + lax.broadcasted_iota(jnp.int32, (rows, nb), 0) % t
    s = _dot_nt(q, kcmp_ref[0].astype(BF16)) * scale
    p = _masked_softmax(s, (col + 1) * CMP_BLOCK - 1 <= pos)
    ocmp_ref[0] = _dot(p.astype(BF16), vcmp_ref[0].astype(BF16))
    imp = jnp.sum(p.reshape(nh, t, nb), axis=0)
    ncol = sel_ref.shape[2]
    imp = jnp.concatenate([imp, jnp.zeros((t, ncol - nb), F32)], axis=1)
    col_s = lax.broadcasted_iota(jnp.int32, (t, ncol), 1)
    pos_s = past + lax.broadcasted_iota(jnp.int32, (t, ncol), 0)
    sel_ref[0] = _select_blocks(_pair_sums(imp, col_s), col_s, pos_s, n_selblk)
    nwp = kw_ref.shape[1]
    colw = lax.broadcasted_iota(jnp.int32, (rows, nwp), 1)
    posw = past + lax.broadcasted_iota(jnp.int32, (rows, nwp), 0) % t
    kwpos = past + t - nwin + colw
    valid = (kwpos <= posw) & (kwpos > posw - WINDOW) & (colw < nwin)
    sw = _dot_nt(q, kw_ref[0].astype(BF16)) * scale
    pw = _masked_softmax(sw, valid)
    owin_ref[0] = _dot(pw.astype(BF16), vw_ref[0].astype(BF16))


def nsa_sample_a(q_r, kcmp, vcmp, kw_pad, vw_pad, past, t, nwin):
    b, rows, _ = q_r.shape
    nb = kcmp.shape[1]
    n_selblk = -(-(past + t) // SEL_BLOCK)
    ncol = -(-2 * n_selblk // LANES) * LANES
    assert ncol > nb >= 2 * n_selblk - 2 and nb % LANES == 0
    per_b = lambda r, w: pl.BlockSpec((1, r, w), lambda bi: (bi, 0, 0))
    kern = functools.partial(_nsa_sample_a_kernel, past=past, t=t, nwin=nwin, n_selblk=n_selblk)
    return pl.pallas_call(
        kern, grid=(b,),
        in_specs=[per_b(rows, HEAD_DIM), per_b(nb, HEAD_DIM), per_b(nb, HEAD_DIM),
                  per_b(kw_pad.shape[1], HEAD_DIM), per_b(kw_pad.shape[1], HEAD_DIM)],
        out_specs=[per_b(rows, HEAD_DIM), per_b(rows, HEAD_DIM), per_b(t, ncol)],
        out_shape=[jax.ShapeDtypeStruct((b, rows, HEAD_DIM), F32)] * 2
        + [jax.ShapeDtypeStruct((b, t, ncol), F32)],
        compiler_params=_cp("arbitrary"), name="nsa_sample_a")(q_r, kcmp, vcmp, kw_pad, vw_pad)


def _nsa_sample_b_kernel(pt_ref, q_ref, tok_ref, kpool_ref, vpool_ref, knew_ref, vnew_ref,
                         ocmp_ref, owin_ref, gate_ref, o_ref, m_ref, l_ref, acc_ref, *, n_pages, t):
    p = pl.program_id(1)

    @pl.when(p == 0)
    def _():
        m_ref[...] = jnp.full(m_ref.shape, NEG, F32)
        l_ref[...] = jnp.zeros(l_ref.shape, F32)
        acc_ref[...] = jnp.zeros(acc_ref.shape, F32)

    is_new = p == n_pages
    kp = jnp.where(is_new, knew_ref[0], kpool_ref[0, 0]).astype(BF16)
    vp = jnp.where(is_new, vnew_ref[0], vpool_ref[0, 0]).astype(BF16)
    q = q_ref[0]
    rows = q.shape[0]
    kpos = p * PAGE_SIZE + lax.broadcasted_iota(jnp.int32, (rows, PAGE_SIZE), 1)
    qpos = n_pages * PAGE_SIZE + lax.broadcasted_iota(jnp.int32, (rows, PAGE_SIZE), 0) % t
    valid = jnp.concatenate([tok_ref[0] > 0.5] * (rows // t), axis=0) & (kpos <= qpos)
    s = _dot_nt(q, kp) * HEAD_DIM ** -0.5
    _online_softmax_step(s, valid, vp, m_ref, l_ref, acc_ref)

    @pl.when(is_new)
    def _():
        g = jax.nn.sigmoid(gate_ref[0])
        o_ref[0] = (g[:, 0:1] * ocmp_ref[0] + g[:, 1:2] * (acc_ref[...] / l_ref[...])
                    + g[:, 2:3] * owin_ref[0])


def nsa_sample_b(q_r, tok, kpool, vpool, ks_new, vs_new, o_cmp, o_win, gate_r, o, page_table):
    b, n_pages = page_table.shape
    rows = q_r.shape[1]
    t = tok.shape[1]
    last = n_pages - 1
    per_b = lambda r, w: pl.BlockSpec((1, r, w), lambda bi, p, pt: (bi, 0, 0))
    pool = pl.BlockSpec((1, 1, PAGE_SIZE, HEAD_DIM), lambda bi, p, pt: (o, pt[bi, jnp.minimum(p, last)], 0, 0))
    kern = functools.partial(_nsa_sample_b_kernel, n_pages=n_pages, t=t)
    return _paged_call(
        kern, (b, n_pages + 1),
        [per_b(rows, HEAD_DIM), pl.BlockSpec((1, t, PAGE_SIZE), lambda bi, p, pt: (bi, 0, p)),
         pool, pool, per_b(PAGE_SIZE, HEAD_DIM), per_b(PAGE_SIZE, HEAD_DIM),
         per_b(rows, HEAD_DIM), per_b(rows, HEAD_DIM), per_b(rows, 3)],
        per_b(rows, HEAD_DIM), jax.ShapeDtypeStruct((b, rows, HEAD_DIM), F32),
        [pltpu.VMEM((rows, 1), F32), pltpu.VMEM((rows, 1), F32), pltpu.VMEM((rows, HEAD_DIM), F32)],
        "nsa_sample_b", page_table,
        (q_r, tok, kpool, vpool, _pad_rows(ks_new, PAGE_SIZE), _pad_rows(vs_new, PAGE_SIZE),
         o_cmp, o_win, gate_r))


def _sb_sample_kernel(pt_ref, qbd_ref, kpool_ref, vpool_ref, knew_ref, vnew_ref, o_ref,
                      carry_ref, acc_ref, *, n_pages, t):
    p = pl.program_id(1)

    @pl.when(p == 0)
    def _():
        carry_ref[...] = jnp.zeros(carry_ref.shape, F32)
        acc_ref[...] = jnp.zeros(acc_ref.shape, F32)

    is_new = p == 0
    kp = jnp.where(is_new, knew_ref[0], kpool_ref[0, 0]).astype(BF16)
    vp = jnp.where(is_new, vnew_ref[0], vpool_ref[0, 0]).astype(BF16)
    z = _dot(kp, qbd_ref[0]) * HEAD_DIM ** -0.5
    n, cols = z.shape
    srow = lax.broadcasted_iota(jnp.int32, (n, cols), 0)
    tcol = lax.broadcasted_iota(jnp.int32, (n, cols), 1) % t
    m = jnp.logical_not(is_new) | (srow < tcol)
    ls, lneg = _log_sigmoids(z)
    lneg = jnp.where(m, lneg, 0.0)
    r = lax.broadcasted_iota(jnp.int32, (n, n), 0)
    c = lax.broadcasted_iota(jnp.int32, (n, n), 1)
    later = jnp.where(c > r, 1.0, 0.0).astype(BF16)
    hi, lo = _split_bf16(lneg)
    log_a = ls + (_dot(later, hi) + _dot(later, lo)) + carry_ref[...]
    a = jnp.where(m, jnp.exp(log_a), 0.0)
    acc_ref[...] += _dot(a.T.astype(BF16), vp)
    carry_ref[...] += jnp.sum(lneg, axis=0, keepdims=True)

    @pl.when(p == n_pages)
    def _():
        for h in range(cols // t):
            o_ref[0, h * t:(h + 1) * t, :] = acc_ref[h * t:(h + 1) * t, h * HEAD_DIM:(h + 1) * HEAD_DIM]


def sb_sample(qd, kd_new, vd_new, kpool, vpool, o, page_table, b, t):
    n_pages = page_table.shape[1]
    w = qd.shape[1]
    nh = w // HEAD_DIM
    eye = jnp.eye(nh, dtype=qd.dtype)
    qbd = jnp.einsum('bthd,hg->bhdgt', qd.reshape(b, t, nh, HEAD_DIM), eye).reshape(b, w, nh * t)
    per_b = lambda r, c: pl.BlockSpec((1, r, c), lambda bi, p, pt: (bi, 0, 0))
    pool = pl.BlockSpec((1, 1, PAGE_SIZE, w),
                        lambda bi, p, pt: (o, pt[bi, jnp.clip(n_pages - p, 0, n_pages - 1)], 0, 0))
    kern = functools.partial(_sb_sample_kernel, n_pages=n_pages, t=t)
    return _paged_call(
        kern, (b, n_pages + 1),
        [per_b(w, nh * t), pool, pool, per_b(PAGE_SIZE, w), per_b(PAGE_SIZE, w)],
        per_b(nh * t, HEAD_DIM), jax.ShapeDtypeStruct((b, nh * t, HEAD_DIM), F32),
        [pltpu.VMEM((1, nh * t), F32), pltpu.VMEM((nh * t, w), F32)],
        "sb_sample", page_table,
        (qbd, kpool, vpool, _pad_rows(kd_new, PAGE_SIZE), _pad_rows(vd_new, PAGE_SIZE)))


def odd_mixer_sample(h, b, t, w, cwk, cwv, o, page_table, c_kc, c_vc, c_ks, c_vs, c_kw, c_vw,
                     c_kd, c_vd):
    m, d = h.shape
    n_pages = page_table.shape[1]
    past = n_pages * PAGE_SIZE
    assert past % CMP_BLOCK == 0 and t < CMP_BLOCK
    tabs = rope_tables(past + jnp.arange(t), b, _tile(m, 1024, 16))
    qc, (kc, vc, ks, vs, kw, vw), gc, qd, kd, vd = _odd_project(h, w, tabs)
    nh = qc.shape[1] // HEAD_DIM
    n_pool = c_kc.shape[1]
    pool1 = lambda a: a.reshape(-1, n_pool, PAGE_SIZE, HEAD_DIM)
    seq = lambda a: a.reshape(b, t, -1)
    kcmp, vcmp = nsa_compress_pages(pool1(c_kc), pool1(c_vc), cwk, cwv, o, page_table)
    wb = c_kw.shape[2]
    kw_all = jnp.concatenate([c_kw[o].reshape(b, wb, HEAD_DIM), seq(kw)], axis=1)
    vw_all = jnp.concatenate([c_vw[o].reshape(b, wb, HEAD_DIM), seq(vw)], axis=1)
    nwp = -(-(wb + t) // LANES) * LANES
    q_r = _rows_ht(qc, b, t, nh)
    o_cmp, o_win, sel = nsa_sample_a(q_r, kcmp, vcmp, _pad_rows(kw_all, nwp), _pad_rows(vw_all, nwp),
                                     past, t, wb + t)
    n_selblk = -(-(past + t) // SEL_BLOCK)
    tok = jnp.repeat(sel[:, :, 0:2 * n_selblk:2], SEL_BLOCK, axis=-1)
    tok = jnp.pad(tok, ((0, 0), (0, 0), (0, (n_pages + 1) * PAGE_SIZE - tok.shape[-1])))
    gate_r = _rows_ht(gc, b, t, nh)
    o_c = nsa_sample_b(q_r, tok, pool1(c_ks), pool1(c_vs), seq(ks), seq(vs), o_cmp, o_win, gate_r,
                       o, page_table)
    wd = kd.shape[1]
    poolh = lambda a: a.reshape(-1, n_pool, PAGE_SIZE, wd)
    o_d = sb_sample(qd, seq(kd), seq(vd), poolh(c_kd), poolh(c_vd), o, page_table, b, t)
    y = proj_out(_rows_th(o_c, b, t, nh).astype(BF16), _rows_th(o_d, b, t, wd // HEAD_DIM).astype(BF16),
                 w["out_c"], w["out_d"])
    st1 = lambda a: a.reshape(b, t, 1, HEAD_DIM)
    return y, (st1(kc), st1(vc), st1(ks), st1(vs), kw_all[:, t:].reshape(b, wb, 1, HEAD_DIM),
               vw_all[:, t:].reshape(b, wb, 1, HEAD_DIM), kd.reshape(b, t, -1, HEAD_DIM),
               vd.reshape(b, t, -1, HEAD_DIM))


def kernel(x_prompt, x_sample, state_conv, cache_dsa_k, cache_dsa_v, cache_dsa_kidx, cache_nsa_kc, cache_nsa_vc, cache_nsa_ks, cache_nsa_vs, cache_nsa_kw, cache_nsa_vw, cache_sb_k, cache_sb_v, cache_mem_k, cache_mem_v, page_table, mem_prompt, norm_pre, norm_post, norm_mem, w_in_even, conv_w, w_out_even, w_in_odd, cmp_wk, cmp_wv, w_out_odd, w_mq, w_mk, w_mv, w_mo, w_gate, w_up, w_down):
    bp, tp, d = x_prompt.shape
    bs, ts, _ = x_sample.shape
    depth = norm_pre.shape[0]
    n_mem = mem_prompt.shape[1]
    hm = w_mq.shape[2]
    nh_c = nh_d = w_out_odd.shape[1] // 2 // HEAD_DIM
    xp = x_prompt.reshape(bp * tp, d)
    xs = x_sample.reshape(bs * ts, d)
    hp = norm_cast(xp, norm_pre[0, 0])
    hs = norm_cast(xs, norm_pre[0, 0])
    ev_p, ev_s, od_p, od_s, mem_p = [], [], [], [], []
    for li in range(depth):
        g_pre, g_post = norm_pre[li], norm_post[li]
        if li % 2 == 0:
            e = li // 2
            w = _even_weights(w_in_even[e], conv_w[e], w_out_even[e])
            mp, stp = even_mixer_prompt(hp, bp, tp, w)
            ms, sts = even_mixer_sample(hs, bs, ts, w, e, page_table, state_conv,
                                        cache_dsa_k, cache_dsa_v, cache_dsa_kidx)
            ev_p.append(stp)
            ev_s.append(sts)
        else:
            o = li // 2
            w = _odd_weights(w_in_odd[o], w_out_odd[o], nh_c, nh_d)
            mp, stp = odd_mixer_prompt(hp, bp, tp, w, cmp_wk[o], cmp_wv[o])
            ms, sts = odd_mixer_sample(hs, bs, ts, w, cmp_wk[o], cmp_wv[o], o, page_table,
                                       cache_nsa_kc, cache_nsa_vc, cache_nsa_ks, cache_nsa_vs,
                                       cache_nsa_kw, cache_nsa_vw, cache_sb_k, cache_sb_v)
            od_p.append(stp)
            od_s.append(sts)
        xp, hp = resid_norm(xp, mp, g_post[0], g_pre[1])
        xs, hs = resid_norm(xs, ms, g_post[0], g_pre[1])
        wq, wo = w_mq[li].astype(BF16), w_mo[li].astype(BF16)
        hmem = norm_cast(mem_prompt.reshape(bp * n_mem, d), norm_mem[li])
        mkp = proj(hmem, w_mk[li].astype(BF16), name="proj_mk")
        mvp = proj(hmem, w_mv[li].astype(BF16), name="proj_mv")
        mem_p.append((mkp.reshape(bp, n_mem, H_MEM, HEAD_DIM), mvp.reshape(bp, n_mem, H_MEM, HEAD_DIM)))
        yp = mem_sublayer(hp.reshape(bp, tp, d), wq, mkp.reshape(bp, n_mem, hm),
                          mvp.reshape(bp, n_mem, hm), wo).reshape(bp * tp, d)
        ys = mem_sublayer(hs.reshape(bs, ts, d), wq, cache_mem_k[li].reshape(bs, n_mem, hm),
                          cache_mem_v[li].reshape(bs, n_mem, hm), wo).reshape(bs * ts, d)
        xp, hp = resid_norm(xp, yp, g_post[1], g_pre[2])
        xs, hs = resid_norm(xs, ys, g_post[1], g_pre[2])
        wg, wu, wd = w_gate[li].astype(BF16), w_up[li].astype(BF16), w_down[li].astype(BF16)
        g_next = norm_pre[li + 1, 0] if li + 1 < depth else None
        xp, hp = resid_norm(xp, ffn(hp, wg, wu, wd), g_post[2], g_next)
        xs, hs = resid_norm(xs, ffn(hs, wg, wu, wd), g_post[2], g_next)
    stack = lambda lst, j: jnp.stack([s[j] for s in lst])
    return ((xp.reshape(bp, tp, d), xs.reshape(bs, ts, d))
            + tuple(stack(ev_p, j) for j in range(4)) + tuple(stack(od_p, j) for j in range(8))
            + (stack(mem_p, 0), stack(mem_p, 1))
            + tuple(stack(ev_s, j) for j in range(4)) + tuple(stack(od_s, j) for j in range(8)))
```

```python
import functools

import numpy as np
import jax
import jax.numpy as jnp
from jax import lax
from jax.experimental import pallas as pl
from jax.experimental.pallas import tpu as pltpu

F32 = jnp.float32
BF16 = jnp.bfloat16

HEAD_DIM = 128
PAGE_SIZE = 128
CONV_K = 3
KV_B = 4
H_IDX = 16
D_IDX = 128
TOPK_MAX = 256
CMP_BLOCK = 32
SEL_BLOCK = 64
N_SEL = 16
WINDOW = 512
FORCE_BONUS = 1.0e6
H_MEM = 4
ROPE_THETA = 10000.0
EPS = 1e-6
NEG = -1e30
LANES = 128
VMEM_LIMIT = 56 * 1024 * 1024


def _cp(*sem):
    return pltpu.CompilerParams(dimension_semantics=sem, vmem_limit_bytes=VMEM_LIMIT)


def _tile(n, pref, mult):
    t = (min(pref, n) // mult) * mult
    while t >= mult:
        if n % t == 0:
            return t
        t -= mult
    return n


def _dot(a, b):
    return jnp.dot(a, b, preferred_element_type=F32)


def _dot_nt(a, b):
    return lax.dot_general(a, b, (((1,), (1,)), ((), ())), preferred_element_type=F32)


def _rms(x, g):
    return x * lax.rsqrt(jnp.mean(x * x, axis=-1, keepdims=True) + EPS) * g


def _rope(y, cos, sin):
    return y * cos + pltpu.roll(y, HEAD_DIM // 2, 1) * sin


def _norm_cast_kernel(x_ref, g_ref, o_ref):
    o_ref[...] = _rms(x_ref[...], g_ref[...]).astype(o_ref.dtype)


def norm_cast(x, g):
    m, d = x.shape
    tm = _tile(m, 256, 16)
    row = pl.BlockSpec((tm, d), lambda i: (i, 0))
    return pl.pallas_call(
        _norm_cast_kernel, grid=(m // tm,),
        in_specs=[row, pl.BlockSpec((1, d), lambda i: (0, 0))],
        out_specs=row, out_shape=jax.ShapeDtypeStruct((m, d), BF16),
        compiler_params=_cp("arbitrary"), name="norm_cast")(x, g.reshape(1, d))


def _resid_norm_kernel(x_ref, y_ref, gp_ref, gn_ref, xo_ref, h_ref):
    xn = x_ref[...] + _rms(y_ref[...], gp_ref[...])
    xo_ref[...] = xn
    h_ref[...] = _rms(xn, gn_ref[...]).astype(h_ref.dtype)


def _resid_kernel(x_ref, y_ref, gp_ref, xo_ref):
    xo_ref[...] = x_ref[...] + _rms(y_ref[...], gp_ref[...])


def resid_norm(x, y, g_post, g_next):
    m, d = x.shape
    tm = _tile(m, 256, 16)
    row = pl.BlockSpec((tm, d), lambda i: (i, 0))
    vec = pl.BlockSpec((1, d), lambda i: (0, 0))
    if g_next is None:
        return pl.pallas_call(
            _resid_kernel, grid=(m // tm,), in_specs=[row, row, vec], out_specs=row,
            out_shape=jax.ShapeDtypeStruct((m, d), F32),
            compiler_params=_cp("arbitrary"), name="resid")(x, y, g_post.reshape(1, d)), None
    return pl.pallas_call(
        _resid_norm_kernel, grid=(m // tm,), in_specs=[row, row, vec, vec],
        out_specs=[row, row],
        out_shape=[jax.ShapeDtypeStruct((m, d), F32), jax.ShapeDtypeStruct((m, d), BF16)],
        compiler_params=_cp("arbitrary"), name="resid_norm")(
            x, y, g_post.reshape(1, d), g_next.reshape(1, d))


def _mm_kernel(*refs, nx, nw, ne, groups, epilogue):
    xr, wr = refs[:nx], refs[nx:nx + nw]
    er, orf = refs[nx + nw:nx + nw + ne], refs[nx + nw + ne:]
    accs = []
    for grp in groups:
        acc = None
        for xi, wi in grp:
            d = _dot(xr[xi][...], wr[wi][...])
            acc = d if acc is None else acc + d
        accs.append(acc)
    epilogue(accs, er, orf)


def _ep_plain(accs, er, orf):
    orf[0][...] = accs[0].astype(orf[0].dtype)


def _ep_rope(accs, er, orf):
    cos, sin = er[0][...], er[1][...]
    y = accs[0]
    for c in range(y.shape[1] // HEAD_DIM):
        sl = slice(c * HEAD_DIM, (c + 1) * HEAD_DIM)
        orf[0][:, sl] = _rope(y[:, sl], cos, sin).astype(orf[0].dtype)


def _ep_split(accs, er, orf, *, rope_chunks):
    cos, sin = er[0][...], er[1][...]
    y = accs[0]
    for c in range(len(orf)):
        yc = y[:, c * HEAD_DIM:(c + 1) * HEAD_DIM]
        if c in rope_chunks:
            yc = _rope(yc, cos, sin)
        orf[c][...] = yc.astype(orf[c].dtype)


def matmul(xs, ws, groups, epilogue, out_dtypes, *, tables=None, n_tab_blocks=1,
           tn_pref=512, split_out=0, name="matmul"):
    m = xs[0].shape[0]
    n = ws[0].shape[1]
    if tables is not None:
        tm = tables[0].shape[0] // n_tab_blocks
    else:
        tm = _tile(m, 1024, 16)
    tn = n if (split_out or n % LANES) else _tile(n, tn_pref, LANES)
    in_specs = [pl.BlockSpec((tm, x.shape[1]), lambda i, j: (i, 0)) for x in xs]
    in_specs += [pl.BlockSpec((w.shape[0], tn), lambda i, j: (0, j)) for w in ws]
    extras = []
    if tables is not None:
        nb = n_tab_blocks
        in_specs += [pl.BlockSpec((tm, HEAD_DIM), lambda i, j: (i % nb, 0))] * 2
        extras = list(tables)
    if split_out:
        out_specs = [pl.BlockSpec((tm, HEAD_DIM), lambda i, j: (i, 0))] * split_out
        out_shape = [jax.ShapeDtypeStruct((m, HEAD_DIM), dt) for dt in out_dtypes]
    else:
        out_specs = [pl.BlockSpec((tm, tn), lambda i, j: (i, j))]
        out_shape = [jax.ShapeDtypeStruct((m, n), out_dtypes[0])]
    kern = functools.partial(_mm_kernel, nx=len(xs), nw=len(ws), ne=len(extras),
                             groups=groups, epilogue=epilogue)
    out = pl.pallas_call(
        kern, grid=(m // tm, n // tn), in_specs=in_specs, out_specs=out_specs,
        out_shape=out_shape, compiler_params=_cp("arbitrary", "arbitrary"), name=name)(
            *xs, *ws, *extras)
    return out if split_out else out[0]


def proj(h, w, dtype=F32, name="proj"):
    return matmul([h], [w], [[(0, 0)]], _ep_plain, [dtype], name=name)


def proj_rope(h, w, tabs, dtype, name="proj_rope"):
    cos, sin, nb = tabs
    return matmul([h], [w], [[(0, 0)]], _ep_rope, [dtype], tables=(cos, sin),
                  n_tab_blocks=nb, name=name)


def proj_out(xa, xb, wa, wb, name="proj_out"):
    return matmul([xa, xb], [wa, wb], [[(0, 0), (1, 1)]], _ep_plain, [F32], name=name)


def rope_tables(pos, reps, tm):
    half = HEAD_DIM // 2
    inv = ROPE_THETA ** (-jnp.arange(half, dtype=F32) / half)
    ang = pos.astype(F32)[:, None] * inv[None, :]
    cos, sin = jnp.cos(ang), jnp.sin(ang)
    cos = jnp.concatenate([cos, cos], axis=-1)
    sin = jnp.concatenate([-sin, sin], axis=-1)
    t = pos.shape[0]
    if tm > t:
        cos, sin = jnp.tile(cos, (tm // t, 1)), jnp.tile(sin, (tm // t, 1))
        return cos, sin, 1
    return cos, sin, t // tm


def _conv_kernel(x_ref, wb_ref, wc_ref, wx_ref, cw_ref, init_ref, ya_ref, st_ref, carry_ref):
    i = pl.program_id(2)

    @pl.when(i == 0)
    def _():
        carry_ref[...] = init_ref[0]

    x = x_ref[0]
    bg = _dot(x, wb_ref[...])
    u = _dot(x, wc_ref[...]) * _dot(x, wx_ref[...])
    tm = u.shape[0]
    c = carry_ref[...]
    rows = lax.broadcasted_iota(jnp.int32, u.shape, 0)
    u1 = jnp.where(rows == 0, c[1:2], pltpu.roll(u, 1, 0))
    u2 = jnp.where(rows == 0, c[0:1], jnp.where(rows == 1, c[1:2], pltpu.roll(u, 2, 0)))
    cw = cw_ref[...]
    conv = cw[0:1] * u2 + cw[1:2] * u1 + cw[2:3] * u
    ya_ref[0] = (bg * conv).astype(ya_ref.dtype)
    new = u[tm - (CONV_K - 1):tm]
    carry_ref[...] = new
    st_ref[0] = new


def conv_mixer(h3, wb, wc, wx, cw, init):
    b, t, d = h3.shape
    c = wb.shape[1]
    tm = _tile(t, 1024, 16)
    tn = _tile(c, 512, LANES)
    wspec = pl.BlockSpec((d, tn), lambda j, bi, i: (0, j))
    return pl.pallas_call(
        _conv_kernel, grid=(c // tn, b, t // tm),
        in_specs=[pl.BlockSpec((1, tm, d), lambda j, bi, i: (bi, i, 0)), wspec, wspec, wspec,
                  pl.BlockSpec((CONV_K, tn), lambda j, bi, i: (0, j)),
                  pl.BlockSpec((1, CONV_K - 1, tn), lambda j, bi, i: (bi, 0, j))],
        out_specs=[pl.BlockSpec((1, tm, tn), lambda j, bi, i: (bi, i, j)),
                   pl.BlockSpec((1, CONV_K - 1, tn), lambda j, bi, i: (bi, 0, j))],
        out_shape=[jax.ShapeDtypeStruct((b, t, c), BF16),
                   jax.ShapeDtypeStruct((b, CONV_K - 1, c), F32)],
        scratch_shapes=[pltpu.VMEM((CONV_K - 1, tn), F32)],
        compiler_params=_cp("arbitrary", "arbitrary", "arbitrary"), name="conv_mixer")(
            h3, wb, wc, wx, cw, init)


def _stack_heads(q_ref, heads):
    return jnp.concatenate([q_ref[:, h * HEAD_DIM:(h + 1) * HEAD_DIM] for h in heads], axis=0)


def _softmax_av(s, bias, v):
    s = s + bias[None]
    e = jnp.exp(s - jnp.max(s, axis=-1, keepdims=True))
    l = jnp.sum(e, axis=-1, keepdims=True)
    r, tq, n = e.shape
    o = _dot(e.reshape(r * tq, n).astype(BF16), v)
    return o / l.reshape(r * tq, 1)


def _sort_key(x):
    bits = lax.bitcast_convert_type(x + 0.0, jnp.int32)
    return jnp.where(bits < 0, bits ^ jnp.int32(0x7FFFFFFF), bits)


INT_MIN = -2 ** 31


def _kth_largest_key(key, k):
    def body(it, othr):
        bit = lax.shift_left(jnp.int32(1), jnp.int32(31) - it)
        cand = othr | bit
        cnt = jnp.sum(jnp.where(key >= (cand ^ jnp.int32(INT_MIN)), 1.0, 0.0), axis=1, keepdims=True)
        return jnp.where(cnt >= k, cand, othr)
    othr = lax.fori_loop(0, 32, body, jnp.zeros((key.shape[0], 1), jnp.int32))
    return othr ^ jnp.int32(INT_MIN)


def _topk_mask(key, k, scratch_ref):
    thr = _kth_largest_key(key, k)
    live = key > jnp.int32(INT_MIN)
    ge = (key >= thr) & live
    scratch_ref[...] = jnp.where(ge, 1.0, 0.0)
    n_ge = jnp.sum(jnp.where(ge, 1.0, 0.0), axis=1, keepdims=True)

    @pl.when(jnp.max(n_ge) > k)
    def _():
        gt = key > thr
        eq = (key == thr) & live
        need = k - jnp.sum(jnp.where(gt, 1.0, 0.0), axis=1, keepdims=True)
        r = lax.broadcasted_iota(jnp.int32, (LANES, LANES), 0)
        c = lax.broadcasted_iota(jnp.int32, (LANES, LANES), 1)
        before = jnp.where(r < c, 1.0, 0.0).astype(BF16)
        run = jnp.zeros_like(need)
        for ch in range(key.shape[1] // LANES):
            sl = slice(ch * LANES, (ch + 1) * LANES)
            e = jnp.where(eq[:, sl], 1.0, 0.0)
            pre = _dot(e.astype(BF16), before) + run
            scratch_ref[:, sl] = jnp.where(gt[:, sl], 1.0, e * jnp.where(pre < need, 1.0, 0.0))
            run = run + jnp.sum(e, axis=1, keepdims=True)


CAUSAL_BUCKETS = 4


def _by_key_extent(i, tq, t, fn):
    nb = CAUSAL_BUCKETS if t % (CAUSAL_BUCKETS * tq) == 0 else 1
    size = t // nb
    for bkt in range(nb):
        @pl.when((i * tq) // size == bkt)
        def _():
            fn((bkt + 1) * size)


def _dsa_prompt_kernel(qi_ref, wi_ref, ki_ref, q_ref, k_ref, v_ref, o_ref,
                       kib_ref, kb_ref, vb_ref, mask_ref, *, topk):
    i = pl.program_id(1)

    @pl.when(i == 0)
    def _():
        kib_ref[...] = ki_ref[...].astype(BF16)
        kb_ref[...] = k_ref[...].astype(BF16)
        vb_ref[...] = v_ref[...].astype(BF16)

    tq = q_ref.shape[0]
    t = ki_ref.shape[0]
    wi = wi_ref[...]
    rep = q_ref.shape[1] // HEAD_DIM // KV_B

    def attend(n):
        kib = kib_ref[0:n]
        score = jnp.zeros((tq, n), F32)
        for h in range(H_IDX):
            s = _dot_nt(qi_ref[:, h * D_IDX:(h + 1) * D_IDX], kib) * D_IDX ** -0.5
            score = score + jnp.maximum(s, 0.0) * wi[:, h:h + 1]
        score = score * H_IDX ** -0.5
        qpos = i * tq + lax.broadcasted_iota(jnp.int32, (tq, n), 0)
        kpos = lax.broadcasted_iota(jnp.int32, (tq, n), 1)
        key = jnp.where(kpos <= qpos, _sort_key(score), jnp.int32(INT_MIN))
        sel_ref = mask_ref.at[:, 0:n]
        _topk_mask(key, topk, sel_ref)
        bias = jnp.where(sel_ref[...] > 0.0, 0.0, NEG)
        for g in range(KV_B):
            qs = _stack_heads(q_ref, range(g * rep, (g + 1) * rep))
            sl = slice(g * HEAD_DIM, (g + 1) * HEAD_DIM)
            s = _dot_nt(qs, kb_ref[0:n, sl]) * HEAD_DIM ** -0.5
            o = _softmax_av(s.reshape(rep, tq, n), bias, vb_ref[0:n, sl])
            for r in range(rep):
                h = g * rep + r
                o_ref[:, h * HEAD_DIM:(h + 1) * HEAD_DIM] = o[r * tq:(r + 1) * tq].astype(o_ref.dtype)

    _by_key_extent(i, tq, t, attend)


def dsa_prompt(qi, wi, ki, q, k, v, b, t):
    m = q.shape[0]
    tq = _tile(t, 128, 16)
    nq = t // tq
    topk = min(TOPK_MAX, t // 4)
    qrow = lambda w: pl.BlockSpec((tq, w), lambda bi, i: (bi * nq + i, 0))
    full = lambda w: pl.BlockSpec((t, w), lambda bi, i: (bi, 0))
    return pl.pallas_call(
        functools.partial(_dsa_prompt_kernel, topk=topk), grid=(b, nq),
        in_specs=[qrow(qi.shape[1]), qrow(wi.shape[1]), full(ki.shape[1]),
                  qrow(q.shape[1]), full(k.shape[1]), full(v.shape[1])],
        out_specs=qrow(q.shape[1]),
        out_shape=jax.ShapeDtypeStruct((m, q.shape[1]), BF16),
        scratch_shapes=[pltpu.VMEM((t, ki.shape[1]), BF16), pltpu.VMEM((t, k.shape[1]), BF16),
                        pltpu.VMEM((t, v.shape[1]), BF16), pltpu.VMEM((tq, t), F32)],
        compiler_params=_cp("arbitrary", "arbitrary"), name="dsa_prompt")(qi, wi, ki, q, k, v)


def _masked_softmax(s, mask):
    m = jnp.max(jnp.where(mask, s, NEG), axis=-1, keepdims=True)
    m = jnp.where(m > 0.5 * NEG, m, 0.0)
    e = jnp.where(mask, jnp.exp(s - m), 0.0)
    return e / jnp.maximum(jnp.sum(e, axis=-1, keepdims=True), 1e-30)


def _pair_sums(imp, col):
    n = imp.shape[1]
    return imp + jnp.where(col % 2 == 0, pltpu.roll(imp, n - 1, 1), pltpu.roll(imp, 1, 1))


def _select_blocks(bs, col, qpos, n_selblk):
    blk = col // 2
    cur = qpos // SEL_BLOCK
    forced = (blk == 0) | (blk == cur) | (blk == cur - 1)
    admiss = (blk * SEL_BLOCK <= qpos) & (blk < n_selblk)
    work = jnp.where(admiss, jnp.where(forced, bs + FORCE_BONUS, bs), NEG)
    sel = jnp.zeros(bs.shape, jnp.bool_)
    big = jnp.int32(2 ** 30)
    for _ in range(min(N_SEL, n_selblk)):
        mx = jnp.max(work, axis=1, keepdims=True)
        idx = jnp.min(jnp.where(work == mx, col, big), axis=1, keepdims=True)
        pick = blk == idx // 2
        sel = sel | pick
        work = jnp.where(pick, -3e38, work)
    return jnp.where(sel & admiss, 1.0, 0.0)


def _nsa_prompt_kernel(q_ref, gc_ref, kc_ref, vc_ref, ks_ref, vs_ref, kw_ref, vw_ref,
                       cwk_ref, cwv_ref, o_ref,
                       kcmp_ref, vcmp_ref, ksb_ref, vsb_ref, kwb_ref, vwb_ref, *, win):
    i = pl.program_id(1)
    tq = q_ref.shape[0]
    t = kc_ref.shape[0]
    nb = t // CMP_BLOCK
    ncp = kcmp_ref.shape[0]
    n_selblk = -(-t // SEL_BLOCK)
    nh = q_ref.shape[1] // HEAD_DIM
    scale = HEAD_DIM ** -0.5

    @pl.when(i == 0)
    def _():
        kcmp_ref[...] = jnp.zeros(kcmp_ref.shape, kcmp_ref.dtype)
        vcmp_ref[...] = jnp.zeros(vcmp_ref.shape, vcmp_ref.dtype)
        kc = kc_ref[...].reshape(nb, CMP_BLOCK, HEAD_DIM)
        vc = vc_ref[...].reshape(nb, CMP_BLOCK, HEAD_DIM)
        kcmp_ref[0:nb] = jnp.sum(kc * cwk_ref[...][None], axis=1).astype(BF16)
        vcmp_ref[0:nb] = jnp.sum(vc * cwv_ref[...][None], axis=1).astype(BF16)
        ksb_ref[...] = ks_ref[...].astype(BF16)
        vsb_ref[...] = vs_ref[...].astype(BF16)
        kwb_ref[...] = kw_ref[...].astype(BF16)
        vwb_ref[...] = vw_ref[...].astype(BF16)

    qs = _stack_heads(q_ref, range(nh))
    col = lax.broadcasted_iota(jnp.int32, (tq, ncp), 1)
    qpos_c = i * tq + lax.broadcasted_iota(jnp.int32, (tq, ncp), 0)
    cmask = ((col + 1) * CMP_BLOCK - 1 <= qpos_c) & (col < nb)
    s = (_dot_nt(qs, kcmp_ref[...]) * scale).reshape(nh, tq, ncp)
    p = _masked_softmax(s, cmask[None])
    o_cmp = _dot(p.reshape(nh * tq, ncp).astype(BF16), vcmp_ref[...])
    imp = jnp.sum(p, axis=0)
    sel = _select_blocks(_pair_sums(imp, col), col, qpos_c, n_selblk).astype(BF16)
    start = pl.multiple_of(jnp.clip(i * tq - WINDOW, 0, t - win), 16)
    qpos_w = i * tq + lax.broadcasted_iota(jnp.int32, (tq, win), 0)
    kpos_w = start + lax.broadcasted_iota(jnp.int32, (tq, win), 1)
    bias_win = jnp.where((kpos_w <= qpos_w) & (kpos_w > qpos_w - WINDOW), 0.0, NEG)
    kwin = kwb_ref[pl.ds(start, win), :]
    vwin = vwb_ref[pl.ds(start, win), :]
    gate = jax.nn.sigmoid(gc_ref[...])
    grp = 4
    o_wins = []
    for hg in range(nh // grp):
        q4 = qs[hg * grp * tq:(hg + 1) * grp * tq]
        s_win = (_dot_nt(q4, kwin) * scale).reshape(grp, tq, win)
        o_wins.append(_softmax_av(s_win, bias_win, vwin))

    def attend(n):
        er = lax.broadcasted_iota(jnp.int32, (ncp, n), 0)
        ec = lax.broadcasted_iota(jnp.int32, (ncp, n), 1)
        expand = jnp.where(er == 2 * (ec // SEL_BLOCK), 1.0, 0.0).astype(BF16)
        tok = _dot(sel, expand)
        qpos = i * tq + lax.broadcasted_iota(jnp.int32, (tq, n), 0)
        kpos = lax.broadcasted_iota(jnp.int32, (tq, n), 1)
        bias_sel = jnp.where((tok > 0.5) & (kpos <= qpos), 0.0, NEG)
        for hg in range(nh // grp):
            q4 = qs[hg * grp * tq:(hg + 1) * grp * tq]
            s_sel = (_dot_nt(q4, ksb_ref[0:n]) * scale).reshape(grp, tq, n)
            o_sel = _softmax_av(s_sel, bias_sel, vsb_ref[0:n])
            o_win = o_wins[hg]
            for r in range(grp):
                h = hg * grp + r
                rows = slice(r * tq, (r + 1) * tq)
                o = (gate[:, 3 * h:3 * h + 1] * o_cmp[h * tq:(h + 1) * tq]
                     + gate[:, 3 * h + 1:3 * h + 2] * o_sel[rows] + gate[:, 3 * h + 2:3 * h + 3] * o_win[rows])
                o_ref[:, h * HEAD_DIM:(h + 1) * HEAD_DIM] = o.astype(o_ref.dtype)

    _by_key_extent(i, tq, t, attend)


def nsa_prompt(q, gc, kc, vc, ks, vs, kw, vw, cwk, cwv, b, t):
    m, hq = q.shape
    tq = _tile(t, 128, 16)
    nq = t // tq
    win = min(WINDOW + tq, t)
    ncp = LANES
    assert t // CMP_BLOCK <= ncp and t % SEL_BLOCK == 0
    qrow = lambda w: pl.BlockSpec((tq, w), lambda bi, i: (bi * nq + i, 0))
    full = pl.BlockSpec((t, HEAD_DIM), lambda bi, i: (bi, 0))
    cw = pl.BlockSpec((CMP_BLOCK, HEAD_DIM), lambda bi, i: (0, 0))
    kv = pltpu.VMEM((t, HEAD_DIM), BF16)
    return pl.pallas_call(
        functools.partial(_nsa_prompt_kernel, win=win), grid=(b, nq),
        in_specs=[qrow(hq), qrow(gc.shape[1])] + [full] * 6 + [cw, cw],
        out_specs=qrow(hq), out_shape=jax.ShapeDtypeStruct((m, hq), BF16),
        scratch_shapes=[pltpu.VMEM((ncp, HEAD_DIM), BF16), pltpu.VMEM((ncp, HEAD_DIM), BF16),
                        kv, kv, kv, kv],
        compiler_params=_cp("arbitrary", "arbitrary"), name="nsa_prompt")(
            q, gc, kc, vc, ks, vs, kw, vw, cwk, cwv)


def _log_sigmoids(z):
    sp = jnp.log1p(jnp.exp(-jnp.abs(z)))
    return jnp.minimum(z, 0.0) - sp, jnp.minimum(-z, 0.0) - sp


def _split_bf16(x):
    hi = x.astype(BF16)
    return hi, (x - hi.astype(F32)).astype(BF16)


def _sb_prompt_kernel(q_ref, k_ref, v_ref, o_ref):
    i = pl.program_id(2)
    tq = q_ref.shape[0]
    nhs = q_ref.shape[1] // HEAD_DIM
    r = lax.broadcasted_iota(jnp.int32, (tq, tq), 0)
    c = lax.broadcasted_iota(jnp.int32, (tq, tq), 1)
    later = jnp.where(r > c, 1.0, 0.0).astype(BF16)
    before = c < r
    qs = [q_ref[:, h * HEAD_DIM:(h + 1) * HEAD_DIM] for h in range(nhs)]

    def block(h, off, carry, acc, m):
        sl = slice(h * HEAD_DIM, (h + 1) * HEAD_DIM)
        kb = k_ref[pl.ds(off, tq), sl].astype(BF16)
        vb = v_ref[pl.ds(off, tq), sl].astype(BF16)
        ls, lneg = _log_sigmoids(_dot_nt(qs[h], kb) * HEAD_DIM ** -0.5)
        if m is not None:
            lneg = jnp.where(m, lneg, 0.0)
        hi, lo = _split_bf16(lneg)
        a = jnp.exp(ls + (_dot(hi, later) + _dot(lo, later)) + carry)
        if m is not None:
            a = jnp.where(m, a, 0.0)
        return carry + jnp.sum(lneg, axis=1, keepdims=True), acc + _dot(a.astype(BF16), vb)

    zero = (jnp.zeros((tq, 1), F32), jnp.zeros((tq, HEAD_DIM), F32))
    diag = pl.multiple_of(i * tq, tq)
    st = tuple(block(h, diag, *zero, before) for h in range(nhs))

    def body(jj, st):
        off = pl.multiple_of((i - 1 - jj) * tq, tq)
        return tuple(block(h, off, *st[h], None) for h in range(nhs))

    st = lax.fori_loop(0, i, body, st)
    for h in range(nhs):
        o_ref[:, h * HEAD_DIM:(h + 1) * HEAD_DIM] = st[h][1].astype(o_ref.dtype)


def sb_prompt(q, k, v, b, t):
    m, hq = q.shape
    nhs = 2
    tq = _tile(t, 256, 16)
    nq = t // tq
    w = nhs * HEAD_DIM
    qrow = pl.BlockSpec((tq, w), lambda bi, h, i: (bi * nq + i, h))
    full = pl.BlockSpec((t, w), lambda bi, h, i: (bi, h))
    return pl.pallas_call(
        _sb_prompt_kernel, grid=(b, hq // w, nq),
        in_specs=[qrow, full, full], out_specs=qrow,
        out_shape=jax.ShapeDtypeStruct((m, hq), BF16),
        compiler_params=_cp("arbitrary", "arbitrary", "arbitrary"), name="sb_prompt")(q, k, v)


def _mem_kernel(h_ref, wq_ref, mk_ref, mv_ref, wo_ref, o_ref):
    q = _dot(h_ref[0], wq_ref[...])
    outs = []
    for hh in range(H_MEM):
        sl = slice(hh * HEAD_DIM, (hh + 1) * HEAD_DIM)
        s = _dot_nt(q[:, sl].astype(BF16), mk_ref[0, :, sl].astype(BF16)) * HEAD_DIM ** -0.5
        e = jnp.exp(s - jnp.max(s, axis=-1, keepdims=True))
        o = _dot(e.astype(BF16), mv_ref[0, :, sl].astype(BF16))
        outs.append(o / jnp.sum(e, axis=-1, keepdims=True))
    o_ref[0] = _dot(jnp.concatenate(outs, axis=1).astype(BF16), wo_ref[...])


def mem_sublayer(h3, wq, mk, mv, wo):
    b, t, d = h3.shape
    tq = _tile(t, 256, 16)
    nm, hm = mk.shape[1], mk.shape[2]
    row = pl.BlockSpec((1, tq, d), lambda bi, i: (bi, i, 0))
    mem = pl.BlockSpec((1, nm, hm), lambda bi, i: (bi, 0, 0))
    return pl.pallas_call(
        _mem_kernel, grid=(b, t // tq),
        in_specs=[row, pl.BlockSpec((d, hm), lambda bi, i: (0, 0)), mem, mem,
                  pl.BlockSpec((hm, d), lambda bi, i: (0, 0))],
        out_specs=row, out_shape=jax.ShapeDtypeStruct((b, t, d), F32),
        compiler_params=_cp("arbitrary", "arbitrary"), name="mem_sublayer")(h3, wq, mk, mv, wo)


def _ffn_kernel(h_ref, wg_ref, wu_ref, wd_ref, o_ref, *, dff):
    f = pl.program_id(1)
    tf = wd_ref.shape[0]
    h = h_ref[...]
    g = _dot(h, wg_ref[...])
    a = (g * jax.nn.sigmoid(g)) * _dot(h, wu_ref[...])
    wd = wd_ref[...]
    if dff % tf:
        valid = dff - f * tf
        a = jnp.where(lax.broadcasted_iota(jnp.int32, a.shape, 1) < valid, a, 0.0)
        wd = jnp.where(lax.broadcasted_iota(jnp.int32, wd.shape, 0) < valid, wd, jnp.zeros_like(wd))
    part = _dot(a.astype(BF16), wd)

    @pl.when(f == 0)
    def _():
        o_ref[...] = part

    @pl.when(f > 0)
    def _():
        o_ref[...] += part


def ffn(h, wg, wu, wd):
    m, d = h.shape
    dff = wg.shape[1]
    tm = _tile(m, 512, 16)
    tf = min(512, dff)
    return pl.pallas_call(
        functools.partial(_ffn_kernel, dff=dff), grid=(m // tm, pl.cdiv(dff, tf)),
        in_specs=[pl.BlockSpec((tm, d), lambda i, f: (i, 0)),
                  pl.BlockSpec((d, tf), lambda i, f: (0, f)),
                  pl.BlockSpec((d, tf), lambda i, f: (0, f)),
                  pl.BlockSpec((tf, d), lambda i, f: (f, 0))],
        out_specs=pl.BlockSpec((tm, d), lambda i, f: (i, 0)),
        out_shape=jax.ShapeDtypeStruct((m, d), F32),
        compiler_params=_cp("arbitrary", "arbitrary"), name="ffn")(h, wg, wu, wd)


def _odd_weights(w_in, w_out, nh_c, nh_d):
    sizes = [nh_c * HEAD_DIM] + [HEAD_DIM] * 6 + [nh_c * 3] + [nh_d * HEAD_DIM] * 3
    offs = np.cumsum([0] + sizes)
    cut = lambda a, b_: w_in[:, offs[a]:offs[b_]].astype(BF16)
    w = {"qc": cut(0, 1), "kv6": cut(1, 7), "gc": cut(7, 8), "qd": cut(8, 9), "kd": cut(9, 10),
         "vd": cut(10, 11)}
    w["out_c"] = w_out[:nh_c * HEAD_DIM].astype(BF16)
    w["out_d"] = w_out[nh_c * HEAD_DIM:].astype(BF16)
    return w


def _odd_project(h, w, tabs):
    cos, sin, nb = tabs
    qc = proj_rope(h, w["qc"], tabs, BF16, name="proj_qc")
    kv6 = matmul([h], [w["kv6"]], [[(0, 0)]], functools.partial(_ep_split, rope_chunks=(0, 2, 4)),
                 [F32] * 6, tables=(cos, sin), n_tab_blocks=nb, split_out=6, name="proj_kv6")
    gc = proj(h, w["gc"], name="proj_gc")
    qd = proj(h, w["qd"], BF16, name="proj_qd")
    kd = proj(h, w["kd"], name="proj_kd")
    vd = proj(h, w["vd"], name="proj_vd")
    return qc, kv6, gc, qd, kd, vd


def odd_mixer_prompt(h, b, t, w, cwk, cwv):
    m, d = h.shape
    tabs = rope_tables(jnp.arange(t), b, _tile(m, 1024, 16))
    qc, (kc, vc, ks, vs, kw, vw), gc, qd, kd, vd = _odd_project(h, w, tabs)
    o_c = nsa_prompt(qc, gc, kc, vc, ks, vs, kw, vw, cwk, cwv, b, t)
    o_d = sb_prompt(qd, kd, vd, b, t)
    y = proj_out(o_c, o_d, w["out_c"], w["out_d"])
    nw = min(WINDOW, t)
    st1 = lambda a: a.reshape(b, t, 1, HEAD_DIM)
    sth = lambda a: a.reshape(b, t, -1, HEAD_DIM)
    return y, (st1(kc), st1(vc), st1(ks), st1(vs), st1(kw)[:, t - nw:], st1(vw)[:, t - nw:],
               sth(kd), sth(vd))


def _even_weights(w_in, conv_w, w_out):
    c = conv_w.shape[1]
    hq = w_out.shape[0] - c
    sizes = [c, c, c, hq, KV_B * HEAD_DIM, KV_B * HEAD_DIM, H_IDX * D_IDX, D_IDX, H_IDX]
    offs = np.cumsum([0] + sizes)
    names = ["bg", "cg", "xa", "q", "k", "v", "qi", "ki", "wi"]
    w = {n: w_in[:, offs[j]:offs[j + 1]].astype(BF16) for j, n in enumerate(names)}
    w["conv_w"] = conv_w
    w["out_a"] = w_out[:c].astype(BF16)
    w["out_b"] = w_out[c:].astype(BF16)
    return w


def _even_project(h, w, tabs):
    q = proj_rope(h, w["q"], tabs, BF16, name="proj_q")
    k = proj_rope(h, w["k"], tabs, F32, name="proj_k")
    v = proj(h, w["v"], name="proj_v")
    qi = proj_rope(h, w["qi"], tabs, BF16, name="proj_qi")
    ki = proj_rope(h, w["ki"], tabs, F32, name="proj_ki")
    wi = proj(h, w["wi"], name="proj_wi")
    return q, k, v, qi, ki, wi


def even_mixer_prompt(h, b, t, w):
    m, d = h.shape
    tabs = rope_tables(jnp.arange(t), b, _tile(m, 1024, 16))
    init = jnp.zeros((b, CONV_K - 1, w["bg"].shape[1]), F32)
    ya, conv_state = conv_mixer(h.reshape(b, t, d), w["bg"], w["cg"], w["xa"], w["conv_w"], init)
    q, k, v, qi, ki, wi = _even_project(h, w, tabs)
    ob = dsa_prompt(qi, wi, ki, q, k, v, b, t)
    y = proj_out(ya.reshape(m, -1), ob, w["out_a"], w["out_b"])
    return y, (conv_state, k.reshape(b, t, KV_B, HEAD_DIM), v.reshape(b, t, KV_B, HEAD_DIM),
               ki.reshape(b, t, D_IDX))


def _paged_call(kern, grid, in_specs, out_specs, out_shape, scratch, name, page_table, args):
    gs = pltpu.PrefetchScalarGridSpec(num_scalar_prefetch=1, grid=grid, in_specs=in_specs,
                                      out_specs=out_specs, scratch_shapes=scratch)
    return pl.pallas_call(kern, grid_spec=gs, out_shape=out_shape,
                          compiler_params=_cp("arbitrary", "arbitrary"), name=name)(page_table, *args)


def _pad_rows(a, rows):
    return jnp.pad(a, ((0, 0), (0, rows - a.shape[1])) + ((0, 0),) * (a.ndim - 2))


def _page_specs(block_tail, layer, g_pages, page_of):
    zeros = (0,) * len(block_tail)

    def spec(j):
        return pl.BlockSpec((1, 1) + block_tail,
                            lambda bi, p, pt: (layer, page_of(bi, p, pt, j)) + zeros)
    return [spec(j) for j in range(g_pages)]


def _forward_pages(g_pages):
    return lambda bi, p, pt, j: pt[bi, p * g_pages + j]


def _rows_ht(a, b, t, nh):
    w = a.shape[1] // nh
    return a.reshape(b, t, nh, w).transpose(0, 2, 1, 3).reshape(b, nh * t, w)


def _rows_th(a, b, t, nh):
    w = a.shape[2]
    return a.reshape(b, nh, t, w).transpose(0, 2, 1, 3).reshape(b * t, nh * w)


def _dsa_scores_kernel(pt_ref, qi_ref, wi_ref, new_ref, *rest, g_pages, t):
    pools, (o_ref, onew_ref) = rest[:g_pages], rest[g_pages:]
    qi, wi = qi_ref[0], wi_ref[0]

    def scores(kb):
        s = _dot_nt(qi, kb) * D_IDX ** -0.5
        s = jnp.maximum(s, 0.0) * wi
        return jnp.sum(s.reshape(H_IDX, t, kb.shape[0]), axis=0) * H_IDX ** -0.5

    o_ref[0] = scores(jnp.concatenate([r[0, 0] for r in pools], axis=0).astype(BF16))

    @pl.when(pl.program_id(1) == 0)
    def _():
        onew_ref[0] = scores(new_ref[0].astype(BF16))


def dsa_sample_scores(qi_r, wi_r, ki_new, pool, e, page_table, t):
    b, n_pages = page_table.shape
    g = _tile(n_pages, 8, 1)
    kern = functools.partial(_dsa_scores_kernel, g_pages=g, t=t)
    per_b = lambda r, w: pl.BlockSpec((1, r, w), lambda bi, p, pt: (bi, 0, 0))
    past, new = _paged_call(
        kern, (b, n_pages // g),
        [per_b(H_IDX * t, D_IDX), per_b(H_IDX * t, 1), per_b(PAGE_SIZE, D_IDX)]
        + _page_specs((PAGE_SIZE, D_IDX), e, g, _forward_pages(g)),
        [pl.BlockSpec((1, t, g * PAGE_SIZE), lambda bi, p, pt: (bi, 0, p)), per_b(t, PAGE_SIZE)],
        [jax.ShapeDtypeStruct((b, t, n_pages * PAGE_SIZE), F32),
         jax.ShapeDtypeStruct((b, t, PAGE_SIZE), F32)], [], "dsa_sample_scores",
        page_table, (qi_r, wi_r, _pad_rows(ki_new, PAGE_SIZE)) + (pool,) * g)
    return jnp.concatenate([past, new], axis=-1)


def _dsa_topk_kernel(s_ref, o_ref, mask_ref, *, past, topk):
    score = s_ref[0]
    kpos = lax.broadcasted_iota(jnp.int32, score.shape, 1)
    qpos = past + lax.broadcasted_iota(jnp.int32, score.shape, 0)
    key = jnp.where(kpos <= qpos, _sort_key(score), jnp.int32(INT_MIN))
    _topk_mask(key, topk, mask_ref)
    o_ref[0] = mask_ref[...]


def dsa_sample_topk(score, past, topk):
    b, t, nk = score.shape
    blk = pl.BlockSpec((1, t, nk), lambda bi: (bi, 0, 0))
    return pl.pallas_call(
        functools.partial(_dsa_topk_kernel, past=past, topk=topk), grid=(b,),
        in_specs=[blk], out_specs=blk, out_shape=jax.ShapeDtypeStruct((b, t, nk), F32),
        scratch_shapes=[pltpu.VMEM((t, nk), F32)],
        compiler_params=_cp("arbitrary"), name="dsa_sample_topk")(score)


def _online_softmax_step(s, valid, v, m_ref, l_ref, acc_ref):
    m_old = m_ref[...]
    m_new = jnp.maximum(m_old, jnp.max(jnp.where(valid, s, NEG), axis=-1, keepdims=True))
    alpha = jnp.exp(m_old - m_new)
    e = jnp.where(valid, jnp.exp(s - m_new), 0.0)
    l_ref[...] = alpha * l_ref[...] + jnp.sum(e, axis=-1, keepdims=True)
    acc_ref[...] = alpha * acc_ref[...] + _dot(e.astype(BF16), v)
    m_ref[...] = m_new


def _init_softmax_state(m_ref, l_ref, acc_ref):
    m_ref[...] = jnp.full(m_ref.shape, NEG, F32)
    l_ref[...] = jnp.zeros(l_ref.shape, F32)
    acc_ref[...] = jnp.zeros(acc_ref.shape, F32)


def _dsa_sample_attn_kernel(pt_ref, q_ref, mask_ref, masknew_ref, knew_ref, vnew_ref, *rest,
                            g_pages, rep):
    kps, vps = rest[:g_pages], rest[g_pages:2 * g_pages]
    o_ref, m_ref, l_ref, acc_ref = rest[2 * g_pages:]
    p = pl.program_id(1)

    @pl.when(p == 0)
    def _():
        _init_softmax_state(m_ref, l_ref, acc_ref)

    def update(keys, vals, valid_t):
        valid = jnp.concatenate([valid_t] * rep, axis=0)
        for g in range(KV_B):
            s = _dot_nt(q_ref[0, g], keys(g).astype(BF16)) * HEAD_DIM ** -0.5
            _online_softmax_step(s, valid, vals(g).astype(BF16), m_ref.at[g], l_ref.at[g], acc_ref.at[g])

    update(lambda g: jnp.concatenate([r[0, 0, :, g, :] for r in kps], axis=0),
           lambda g: jnp.concatenate([r[0, 0, :, g, :] for r in vps], axis=0), mask_ref[0] > 0.5)

    @pl.when(p == pl.num_programs(1) - 1)
    def _():
        update(lambda g: knew_ref[0, :, g, :], lambda g: vnew_ref[0, :, g, :], masknew_ref[0] > 0.5)
        o_ref[0] = acc_ref[...] / l_ref[...]


def dsa_sample_attn(q_g, mask, kpool, vpool, k_new, v_new, e, page_table):
    b, n_pages = page_table.shape
    t = mask.shape[1]
    rows = q_g.shape[2]
    g = _tile(n_pages, 8, 1)
    kern = functools.partial(_dsa_sample_attn_kernel, g_pages=g, rep=rows // t)
    qspec = pl.BlockSpec((1, KV_B, rows, HEAD_DIM), lambda bi, p, pt: (bi, 0, 0, 0))
    new = pl.BlockSpec((1, PAGE_SIZE, KV_B, HEAD_DIM), lambda bi, p, pt: (bi, 0, 0, 0))
    pools = _page_specs((PAGE_SIZE, KV_B, HEAD_DIM), e, g, _forward_pages(g))
    return _paged_call(
        kern, (b, n_pages // g),
        [qspec, pl.BlockSpec((1, t, g * PAGE_SIZE), lambda bi, p, pt: (bi, 0, p)),
         pl.BlockSpec((1, t, PAGE_SIZE), lambda bi, p, pt: (bi, 0, n_pages)), new, new] + pools + pools,
        qspec, jax.ShapeDtypeStruct((b, KV_B, rows, HEAD_DIM), F32),
        [pltpu.VMEM((KV_B, rows, 1), F32), pltpu.VMEM((KV_B, rows, 1), F32),
         pltpu.VMEM((KV_B, rows, HEAD_DIM), F32)],
        "dsa_sample_attn", page_table,
        (q_g, mask, mask, _pad_rows(k_new, PAGE_SIZE), _pad_rows(v_new, PAGE_SIZE))
        + (kpool,) * g + (vpool,) * g)


def even_mixer_sample(h, b, t, w, e, page_table, state_conv, c_k, c_v, c_kidx):
    m, d = h.shape
    n_pages = page_table.shape[1]
    past = n_pages * PAGE_SIZE
    tabs = rope_tables(past + jnp.arange(t), b, _tile(m, 1024, 16))
    ya, conv_state = conv_mixer(h.reshape(b, t, d), w["bg"], w["cg"], w["xa"], w["conv_w"], state_conv[e])
    q, k, v, qi, ki, wi = _even_project(h, w, tabs)
    score = dsa_sample_scores(_rows_ht(qi, b, t, H_IDX), _rows_ht(wi, b, t, H_IDX),
                              ki.reshape(b, t, D_IDX), c_kidx, e, page_table, t)
    mask = dsa_sample_topk(score, past, min(TOPK_MAX, (past + t) // 4))
    nh = q.shape[1] // HEAD_DIM
    rep = nh // KV_B
    q_g = _rows_ht(q, b, t, nh).reshape(b, KV_B, rep * t, HEAD_DIM)
    o = dsa_sample_attn(q_g, mask, c_k, c_v, k.reshape(b, t, KV_B, HEAD_DIM),
                        v.reshape(b, t, KV_B, HEAD_DIM), e, page_table)
    ob = _rows_th(o.reshape(b, nh * t, HEAD_DIM), b, t, nh).astype(BF16)
    y = proj_out(ya.reshape(m, -1), ob, w["out_a"], w["out_b"])
    return y, (conv_state, k.reshape(b, t, KV_B, HEAD_DIM), v.reshape(b, t, KV_B, HEAD_DIM),
               ki.reshape(b, t, D_IDX))


def _compress_pages_kernel(pt_ref, cwk_ref, cwv_ref, *rest, g_pages):
    kcs, vcs = rest[:g_pages], rest[g_pages:2 * g_pages]
    ko_ref, vo_ref = rest[2 * g_pages:]
    nb = PAGE_SIZE // CMP_BLOCK
    cwk, cwv = cwk_ref[...][None], cwv_ref[...][None]
    for j in range(g_pages):
        ko_ref[0, j] = jnp.sum(kcs[j][0, 0].reshape(nb, CMP_BLOCK, HEAD_DIM) * cwk, axis=1)
        vo_ref[0, j] = jnp.sum(vcs[j][0, 0].reshape(nb, CMP_BLOCK, HEAD_DIM) * cwv, axis=1)


def nsa_compress_pages(kpool, vpool, cwk, cwv, o, page_table):
    b, n_pages = page_table.shape
    nb = PAGE_SIZE // CMP_BLOCK
    g = _tile(n_pages, 8, 1)
    pools = _page_specs((PAGE_SIZE, HEAD_DIM), o, g, _forward_pages(g))
    cw = pl.BlockSpec((CMP_BLOCK, HEAD_DIM), lambda bi, p, pt: (0, 0))
    out = pl.BlockSpec((1, g, nb, HEAD_DIM), lambda bi, p, pt: (bi, p, 0, 0))
    shp = jax.ShapeDtypeStruct((b, n_pages, nb, HEAD_DIM), F32)
    kc, vc = _paged_call(functools.partial(_compress_pages_kernel, g_pages=g), (b, n_pages // g),
                         [cw, cw] + pools + pools, [out, out], [shp, shp], [], "nsa_compress_pages",
                         page_table, (cwk, cwv) + (kpool,) * g + (vpool,) * g)
    return kc.reshape(b, n_pages * nb, HEAD_DIM), vc.reshape(b, n_pages * nb, HEAD_DIM)


def _nsa_sample_a_kernel(q_ref, kcmp_ref, vcmp_ref, kw_ref, vw_ref, ocmp_ref, owin_ref, sel_ref,
                         *, past, t, nwin, n_selblk):
    q = q_ref[0]
    rows = q.shape[0]
    nh = rows // t
    scale = HEAD_DIM ** -0.5
    nb = kcmp_ref.shape[1]
    col = lax.broadcasted_iota(jnp.int32, (rows, nb), 1)
    pos = past + lax.broadcasted_iota(jnp.int32, (rows, nb), 0) % t
    s = _dot_nt(q, kcmp_ref[0].astype(BF16)) * scale
    p = _masked_softmax(s, (col + 1) * CMP_BLOCK - 1 <= pos)
    ocmp_ref[0] = _dot(p.astype(BF16), vcmp_ref[0].astype(BF16))
    imp = jnp.sum(p.reshape(nh, t, nb), axis=0)
    ncol = sel_ref.shape[2]
    imp = jnp.concatenate([imp, jnp.zeros((t, ncol - nb), F32)], axis=1)
    col_s = lax.broadcasted_iota(jnp.int32, (t, ncol), 1)
    pos_s = past + lax.broadcasted_iota(jnp.int32, (t, ncol), 0)
    sel_ref[0] = _select_blocks(_pair_sums(imp, col_s), col_s, pos_s, n_selblk)
    nwp = kw_ref.shape[1]
    colw = lax.broadcasted_iota(jnp.int32, (rows, nwp), 1)
    posw = past + lax.broadcasted_iota(jnp.int32, (rows, nwp), 0) % t
    kwpos = past + t - nwin + colw
    valid = (kwpos <= posw) & (kwpos > posw - WINDOW) & (colw < nwin)
    sw = _dot_nt(q, kw_ref[0].astype(BF16)) * scale
    pw = _masked_softmax(sw, valid)
    owin_ref[0] = _dot(pw.astype(BF16), vw_ref[0].astype(BF16))


def nsa_sample_a(q_r, kcmp, vcmp, kw_pad, vw_pad, past, t, nwin):
    b, rows, _ = q_r.shape
    nb = kcmp.shape[1]
    n_selblk = -(-(past + t) // SEL_BLOCK)
    ncol = -(-2 * n_selblk // LANES) * LANES
    assert ncol > nb >= 2 * n_selblk - 2 and nb % LANES == 0
    per_b = lambda r, w: pl.BlockSpec((1, r, w), lambda bi: (bi, 0, 0))
    kern = functools.partial(_nsa_sample_a_kernel, past=past, t=t, nwin=nwin, n_selblk=n_selblk)
    return pl.pallas_call(
        kern, grid=(b,),
        in_specs=[per_b(rows, HEAD_DIM), per_b(nb, HEAD_DIM), per_b(nb, HEAD_DIM),
                  per_b(kw_pad.shape[1], HEAD_DIM), per_b(kw_pad.shape[1], HEAD_DIM)],
        out_specs=[per_b(rows, HEAD_DIM), per_b(rows, HEAD_DIM), per_b(t, ncol)],
        out_shape=[jax.ShapeDtypeStruct((b, rows, HEAD_DIM), F32)] * 2
        + [jax.ShapeDtypeStruct((b, t, ncol), F32)],
        compiler_params=_cp("arbitrary"), name="nsa_sample_a")(q_r, kcmp, vcmp, kw_pad, vw_pad)


def _nsa_sample_b_kernel(pt_ref, q_ref, tok_ref, toknew_ref, knew_ref, vnew_ref,
                         ocmp_ref, owin_ref, gate_ref, *rest, g_pages, n_pages, t):
    kps, vps = rest[:g_pages], rest[g_pages:2 * g_pages]
    o_ref, m_ref, l_ref, acc_ref = rest[2 * g_pages:]
    p = pl.program_id(1)

    @pl.when(p == 0)
    def _():
        _init_softmax_state(m_ref, l_ref, acc_ref)

    q = q_ref[0]
    rows = q.shape[0]

    def update(keys, vals, tok, first_key):
        n = keys.shape[0]
        kpos = first_key + lax.broadcasted_iota(jnp.int32, (rows, n), 1)
        qpos = n_pages * PAGE_SIZE + lax.broadcasted_iota(jnp.int32, (rows, n), 0) % t
        valid = jnp.concatenate([tok > 0.5] * (rows // t), axis=0) & (kpos <= qpos)
        s = _dot_nt(q, keys.astype(BF16)) * HEAD_DIM ** -0.5
        _online_softmax_step(s, valid, vals.astype(BF16), m_ref, l_ref, acc_ref)

    update(jnp.concatenate([r[0, 0] for r in kps], axis=0),
           jnp.concatenate([r[0, 0] for r in vps], axis=0), tok_ref[0], p * (g_pages * PAGE_SIZE))

    @pl.when(p == pl.num_programs(1) - 1)
    def _():
        update(knew_ref[0], vnew_ref[0], toknew_ref[0], n_pages * PAGE_SIZE)
        g = jax.nn.sigmoid(gate_ref[0])
        o_ref[0] = (g[:, 0:1] * ocmp_ref[0] + g[:, 1:2] * (acc_ref[...] / l_ref[...])
                    + g[:, 2:3] * owin_ref[0])


def nsa_sample_b(q_r, tok, kpool, vpool, ks_new, vs_new, o_cmp, o_win, gate_r, o, page_table):
    b, n_pages = page_table.shape
    rows = q_r.shape[1]
    t = tok.shape[1]
    g = _tile(n_pages, 8, 1)
    per_b = lambda r, w: pl.BlockSpec((1, r, w), lambda bi, p, pt: (bi, 0, 0))
    pools = _page_specs((PAGE_SIZE, HEAD_DIM), o, g, _forward_pages(g))
    kern = functools.partial(_nsa_sample_b_kernel, g_pages=g, n_pages=n_pages, t=t)
    return _paged_call(
        kern, (b, n_pages // g),
        [per_b(rows, HEAD_DIM), pl.BlockSpec((1, t, g * PAGE_SIZE), lambda bi, p, pt: (bi, 0, p)),
         pl.BlockSpec((1, t, PAGE_SIZE), lambda bi, p, pt: (bi, 0, n_pages)),
         per_b(PAGE_SIZE, HEAD_DIM), per_b(PAGE_SIZE, HEAD_DIM),
         per_b(rows, HEAD_DIM), per_b(rows, HEAD_DIM), per_b(rows, 3)] + pools + pools,
        per_b(rows, HEAD_DIM), jax.ShapeDtypeStruct((b, rows, HEAD_DIM), F32),
        [pltpu.VMEM((rows, 1), F32), pltpu.VMEM((rows, 1), F32), pltpu.VMEM((rows, HEAD_DIM), F32)],
        "nsa_sample_b", page_table,
        (q_r, tok, tok, _pad_rows(ks_new, PAGE_SIZE), _pad_rows(vs_new, PAGE_SIZE),
         o_cmp, o_win, gate_r) + (kpool,) * g + (vpool,) * g)


def _sb_sample_kernel(pt_ref, qt_ref, knew_ref, vnew_ref, *rest, g_pages, t):
    kps, vps = rest[:g_pages], rest[g_pages:2 * g_pages]
    o_ref, carry_ref, acc_ref, z_ref = rest[2 * g_pages:]
    p = pl.program_id(1)
    n, nh = knew_ref.shape[1], knew_ref.shape[2]
    cols = nh * t

    @pl.when(p == 0)
    def _():
        carry_ref[...] = jnp.zeros(carry_ref.shape, F32)
        acc_ref[...] = jnp.zeros(acc_ref.shape, F32)

    iota = lambda shape, d: lax.broadcasted_iota(jnp.int32, shape, d)
    later = jnp.where(iota((n, n), 1) > iota((n, n), 0), 1.0, 0.0).astype(BF16)
    own_head = iota((nh, cols), 1) // t == iota((nh, cols), 0)
    spread = jnp.where(iota((n, n * nh), 1) // nh == iota((n, n * nh), 0), 1.0, 0.0).astype(BF16)
    own_rows = iota((cols, n * nh), 0) // t == iota((cols, n * nh), 1) % nh

    def block(k3, v3, m):
        k2 = k3.reshape(n * nh, HEAD_DIM).astype(BF16)
        v2 = v3.reshape(n * nh, HEAD_DIM).astype(BF16)
        z_all = _dot(k2, qt_ref[0]).reshape(n, nh, cols)
        z_ref[...] = jnp.sum(jnp.where(own_head[None], z_all, 0.0), axis=1)
        ls, lneg = _log_sigmoids(z_ref[...] * HEAD_DIM ** -0.5)
        if m is not None:
            lneg = jnp.where(m, lneg, 0.0)
        hi, lo = _split_bf16(lneg)
        a = jnp.exp(ls + (_dot(later, hi) + _dot(later, lo)) + carry_ref[...])
        if m is not None:
            a = jnp.where(m, a, 0.0)
        a2 = jnp.where(own_rows, _dot(a.T.astype(BF16), spread), 0.0)
        acc_ref[...] += _dot(a2.astype(BF16), v2)
        carry_ref[...] += jnp.sum(lneg, axis=0, keepdims=True)

    @pl.when(p == 0)
    def _():
        block(knew_ref[0], vnew_ref[0], iota((n, cols), 0) < iota((n, cols), 1) % t)

    for j in range(g_pages):
        block(kps[j][0, 0], vps[j][0, 0], None)

    @pl.when(p == pl.num_programs(1) - 1)
    def _():
        o_ref[0] = acc_ref[...]


def sb_sample(qd, kd_new, vd_new, kpool, vpool, o, page_table, b, t):
    n_pages = page_table.shape[1]
    nh = qd.shape[1] // HEAD_DIM
    g = _tile(n_pages, 4, 1)
    qt = qd.reshape(b, t, nh, HEAD_DIM).transpose(0, 3, 2, 1).reshape(b, HEAD_DIM, nh * t)
    new = pl.BlockSpec((1, PAGE_SIZE, nh, HEAD_DIM), lambda bi, p, pt: (bi, 0, 0, 0))
    pools = _page_specs((PAGE_SIZE, nh, HEAD_DIM), o, g,
                        lambda bi, p, pt, j: pt[bi, n_pages - 1 - (p * g + j)])
    out = pl.BlockSpec((1, nh * t, HEAD_DIM), lambda bi, p, pt: (bi, 0, 0))
    return _paged_call(
        functools.partial(_sb_sample_kernel, g_pages=g, t=t), (b, n_pages // g),
        [pl.BlockSpec((1, HEAD_DIM, nh * t), lambda bi, p, pt: (bi, 0, 0)), new, new]
        + pools + pools,
        out, jax.ShapeDtypeStruct((b, nh * t, HEAD_DIM), F32),
        [pltpu.VMEM((1, nh * t), F32), pltpu.VMEM((nh * t, HEAD_DIM), F32),
         pltpu.VMEM((PAGE_SIZE, nh * t), F32)],
        "sb_sample", page_table,
        (qt, _pad_rows(kd_new, PAGE_SIZE), _pad_rows(vd_new, PAGE_SIZE)) + (kpool,) * g + (vpool,) * g)


def odd_mixer_sample(h, b, t, w, cwk, cwv, o, page_table, c_kc, c_vc, c_ks, c_vs, c_kw, c_vw,
                     c_kd, c_vd):
    m, d = h.shape
    n_pages = page_table.shape[1]
    past = n_pages * PAGE_SIZE
    assert past % CMP_BLOCK == 0 and t < CMP_BLOCK
    tabs = rope_tables(past + jnp.arange(t), b, _tile(m, 1024, 16))
    qc, (kc, vc, ks, vs, kw, vw), gc, qd, kd, vd = _odd_project(h, w, tabs)
    nh = qc.shape[1] // HEAD_DIM
    n_pool = c_kc.shape[1]
    pool1 = lambda a: a.reshape(-1, n_pool, PAGE_SIZE, HEAD_DIM)
    seq = lambda a: a.reshape(b, t, -1)
    kcmp, vcmp = nsa_compress_pages(pool1(c_kc), pool1(c_vc), cwk, cwv, o, page_table)
    wb = c_kw.shape[2]
    kw_all = jnp.concatenate([c_kw[o].reshape(b, wb, HEAD_DIM), seq(kw)], axis=1)
    vw_all = jnp.concatenate([c_vw[o].reshape(b, wb, HEAD_DIM), seq(vw)], axis=1)
    nwp = -(-(wb + t) // LANES) * LANES
    q_r = _rows_ht(qc, b, t, nh)
    o_cmp, o_win, sel = nsa_sample_a(q_r, kcmp, vcmp, _pad_rows(kw_all, nwp), _pad_rows(vw_all, nwp),
                                     past, t, wb + t)
    n_selblk = -(-(past + t) // SEL_BLOCK)
    tok = jnp.repeat(sel[:, :, 0:2 * n_selblk:2], SEL_BLOCK, axis=-1)
    tok = jnp.pad(tok, ((0, 0), (0, 0), (0, (n_pages + 1) * PAGE_SIZE - tok.shape[-1])))
    gate_r = _rows_ht(gc, b, t, nh)
    o_c = nsa_sample_b(q_r, tok, pool1(c_ks), pool1(c_vs), seq(ks), seq(vs), o_cmp, o_win, gate_r,
                       o, page_table)
    nh_d = kd.shape[1] // HEAD_DIM
    heads = lambda a: a.reshape(b, t, nh_d, HEAD_DIM)
    o_d = sb_sample(qd, heads(kd), heads(vd), c_kd, c_vd, o, page_table, b, t)
    y = proj_out(_rows_th(o_c, b, t, nh).astype(BF16), _rows_th(o_d, b, t, nh_d).astype(BF16),
                 w["out_c"], w["out_d"])
    st1 = lambda a: a.reshape(b, t, 1, HEAD_DIM)
    return y, (st1(kc), st1(vc), st1(ks), st1(vs), kw_all[:, t:].reshape(b, wb, 1, HEAD_DIM),
               vw_all[:, t:].reshape(b, wb, 1, HEAD_DIM), kd.reshape(b, t, -1, HEAD_DIM),
               vd.reshape(b, t, -1, HEAD_DIM))


def kernel(x_prompt, x_sample, state_conv, cache_dsa_k, cache_dsa_v, cache_dsa_kidx, cache_nsa_kc, cache_nsa_vc, cache_nsa_ks, cache_nsa_vs, cache_nsa_kw, cache_nsa_vw, cache_sb_k, cache_sb_v, cache_mem_k, cache_mem_v, page_table, mem_prompt, norm_pre, norm_post, norm_mem, w_in_even, conv_w, w_out_even, w_in_odd, cmp_wk, cmp_wv, w_out_odd, w_mq, w_mk, w_mv, w_mo, w_gate, w_up, w_down):
    bp, tp, d = x_prompt.shape
    bs, ts, _ = x_sample.shape
    depth = norm_pre.shape[0]
    n_mem = mem_prompt.shape[1]
    hm = w_mq.shape[2]
    nh_c = nh_d = w_out_odd.shape[1] // 2 // HEAD_DIM
    xp = x_prompt.reshape(bp * tp, d)
    xs = x_sample.reshape(bs * ts, d)
    hp = norm_cast(xp, norm_pre[0, 0])
    hs = norm_cast(xs, norm_pre[0, 0])
    ev_p, ev_s, od_p, od_s, mem_p = [], [], [], [], []
    for li in range(depth):
        g_pre, g_post = norm_pre[li], norm_post[li]
        if li % 2 == 0:
            e = li // 2
            w = _even_weights(w_in_even[e], conv_w[e], w_out_even[e])
            mp, stp = even_mixer_prompt(hp, bp, tp, w)
            ms, sts = even_mixer_sample(hs, bs, ts, w, e, page_table, state_conv,
                                        cache_dsa_k, cache_dsa_v, cache_dsa_kidx)
            ev_p.append(stp)
            ev_s.append(sts)
        else:
            o = li // 2
            w = _odd_weights(w_in_odd[o], w_out_odd[o], nh_c, nh_d)
            mp, stp = odd_mixer_prompt(hp, bp, tp, w, cmp_wk[o], cmp_wv[o])
            ms, sts = odd_mixer_sample(hs, bs, ts, w, cmp_wk[o], cmp_wv[o], o, page_table,
                                       cache_nsa_kc, cache_nsa_vc, cache_nsa_ks, cache_nsa_vs,
                                       cache_nsa_kw, cache_nsa_vw, cache_sb_k, cache_sb_v)
            od_p.append(stp)
            od_s.append(sts)
        xp, hp = resid_norm(xp, mp, g_post[0], g_pre[1])
        xs, hs = resid_norm(xs, ms, g_post[0], g_pre[1])
        wq, wo = w_mq[li].astype(BF16), w_mo[li].astype(BF16)
        hmem = norm_cast(mem_prompt.reshape(bp * n_mem, d), norm_mem[li])
        mkp = proj(hmem, w_mk[li].astype(BF16), name="proj_mk")
        mvp = proj(hmem, w_mv[li].astype(BF16), name="proj_mv")
        mem_p.append((mkp.reshape(bp, n_mem, H_MEM, HEAD_DIM), mvp.reshape(bp, n_mem, H_MEM, HEAD_DIM)))
        yp = mem_sublayer(hp.reshape(bp, tp, d), wq, mkp.reshape(bp, n_mem, hm),
                          mvp.reshape(bp, n_mem, hm), wo).reshape(bp * tp, d)
        ys = mem_sublayer(hs.reshape(bs, ts, d), wq, cache_mem_k[li].reshape(bs, n_mem, hm),
                          cache_mem_v[li].reshape(bs, n_mem, hm), wo).reshape(bs * ts, d)
        xp, hp = resid_norm(xp, yp, g_post[1], g_pre[2])
        xs, hs = resid_norm(xs, ys, g_post[1], g_pre[2])
        wg, wu, wd = w_gate[li].astype(BF16), w_up[li].astype(BF16), w_down[li].astype(BF16)
        g_next = norm_pre[li + 1, 0] if li + 1 < depth else None
        xp, hp = resid_norm(xp, ffn(hp, wg, wu, wd), g_post[2], g_next)
        xs, hs = resid_norm(xs, ffn(hs, wg, wu, wd), g_post[2], g_next)
    stack = lambda lst, j: jnp.stack([s[j] for s in lst])
    return ((xp.reshape(bp, tp, d), xs.reshape(bs, ts, d))
            + tuple(stack(ev_p, j) for j in range(4)) + tuple(stack(od_p, j) for j in range(8))
            + (stack(mem_p, 0), stack(mem_p, 1))
            + tuple(stack(ev_s, j) for j in range(4)) + tuple(stack(od_s, j) for j in range(8)))
```

```python
import functools

import numpy as np
import jax
import jax.numpy as jnp
from jax import lax
from jax.experimental import pallas as pl
from jax.experimental.pallas import tpu as pltpu

F32 = jnp.float32
BF16 = jnp.bfloat16

HEAD_DIM = 128
PAGE_SIZE = 128
CONV_K = 3
KV_B = 4
H_IDX = 16
D_IDX = 128
TOPK_MAX = 256
CMP_BLOCK = 32
SEL_BLOCK = 64
N_SEL = 16
WINDOW = 512
FORCE_BONUS = 1.0e6
H_MEM = 4
ROPE_THETA = 10000.0
EPS = 1e-6
NEG = -1e30
LANES = 128
VMEM_LIMIT = 56 * 1024 * 1024


def _cp(*sem):
    return pltpu.CompilerParams(dimension_semantics=sem, vmem_limit_bytes=VMEM_LIMIT)


def _tile(n, pref, mult):
    t = (min(pref, n) // mult) * mult
    while t >= mult:
        if n % t == 0:
            return t
        t -= mult
    return n


def _dot(a, b):
    return jnp.dot(a, b, preferred_element_type=F32)


def _dot_nt(a, b):
    return lax.dot_general(a, b, (((1,), (1,)), ((), ())), preferred_element_type=F32)


def _rms(x, g):
    return x * lax.rsqrt(jnp.mean(x * x, axis=-1, keepdims=True) + EPS) * g


def _rope(y, cos, sin):
    return y * cos + pltpu.roll(y, HEAD_DIM // 2, 1) * sin


def _norm_cast_kernel(x_ref, g_ref, o_ref):
    o_ref[...] = _rms(x_ref[...], g_ref[...]).astype(o_ref.dtype)


def norm_cast(x, g):
    m, d = x.shape
    tm = _tile(m, 256, 16)
    row = pl.BlockSpec((tm, d), lambda i: (i, 0))
    return pl.pallas_call(
        _norm_cast_kernel, grid=(m // tm,),
        in_specs=[row, pl.BlockSpec((1, d), lambda i: (0, 0))],
        out_specs=row, out_shape=jax.ShapeDtypeStruct((m, d), BF16),
        compiler_params=_cp("arbitrary"), name="norm_cast")(x, g.reshape(1, d))


def _resid_norm_kernel(x_ref, y_ref, gp_ref, gn_ref, xo_ref, h_ref):
    xn = x_ref[...] + _rms(y_ref[...], gp_ref[...])
    xo_ref[...] = xn
    h_ref[...] = _rms(xn, gn_ref[...]).astype(h_ref.dtype)


def _resid_kernel(x_ref, y_ref, gp_ref, xo_ref):
    xo_ref[...] = x_ref[...] + _rms(y_ref[...], gp_ref[...])


def resid_norm(x, y, g_post, g_next):
    m, d = x.shape
    tm = _tile(m, 256, 16)
    row = pl.BlockSpec((tm, d), lambda i: (i, 0))
    vec = pl.BlockSpec((1, d), lambda i: (0, 0))
    if g_next is None:
        return pl.pallas_call(
            _resid_kernel, grid=(m // tm,), in_specs=[row, row, vec], out_specs=row,
            out_shape=jax.ShapeDtypeStruct((m, d), F32),
            compiler_params=_cp("arbitrary"), name="resid")(x, y, g_post.reshape(1, d)), None
    return pl.pallas_call(
        _resid_norm_kernel, grid=(m // tm,), in_specs=[row, row, vec, vec],
        out_specs=[row, row],
        out_shape=[jax.ShapeDtypeStruct((m, d), F32), jax.ShapeDtypeStruct((m, d), BF16)],
        compiler_params=_cp("arbitrary"), name="resid_norm")(
            x, y, g_post.reshape(1, d), g_next.reshape(1, d))


def _mm_kernel(*refs, nx, nw, ne, groups, epilogue):
    xr, wr = refs[:nx], refs[nx:nx + nw]
    er, orf = refs[nx + nw:nx + nw + ne], refs[nx + nw + ne:]
    accs = []
    for grp in groups:
        acc = None
        for xi, wi in grp:
            d = _dot(xr[xi][...], wr[wi][...])
            acc = d if acc is None else acc + d
        accs.append(acc)
    epilogue(accs, er, orf)


def _ep_plain(accs, er, orf):
    orf[0][...] = accs[0].astype(orf[0].dtype)


def _ep_rope(accs, er, orf):
    cos, sin = er[0][...], er[1][...]
    y = accs[0]
    for c in range(y.shape[1] // HEAD_DIM):
        sl = slice(c * HEAD_DIM, (c + 1) * HEAD_DIM)
        orf[0][:, sl] = _rope(y[:, sl], cos, sin).astype(orf[0].dtype)


def _ep_split(accs, er, orf, *, rope_chunks):
    cos, sin = er[0][...], er[1][...]
    y = accs[0]
    for c in range(len(orf)):
        yc = y[:, c * HEAD_DIM:(c + 1) * HEAD_DIM]
        if c in rope_chunks:
            yc = _rope(yc, cos, sin)
        orf[c][...] = yc.astype(orf[c].dtype)


def matmul(xs, ws, groups, epilogue, out_dtypes, *, tables=None, n_tab_blocks=1,
           tn_pref=512, split_out=0, name="matmul"):
    m = xs[0].shape[0]
    n = ws[0].shape[1]
    if tables is not None:
        tm = tables[0].shape[0] // n_tab_blocks
    else:
        tm = _tile(m, 1024, 16)
    tn = n if (split_out or n % LANES) else _tile(n, tn_pref, LANES)
    in_specs = [pl.BlockSpec((tm, x.shape[1]), lambda i, j: (i, 0)) for x in xs]
    in_specs += [pl.BlockSpec((w.shape[0], tn), lambda i, j: (0, j)) for w in ws]
    extras = []
    if tables is not None:
        nb = n_tab_blocks
        in_specs += [pl.BlockSpec((tm, HEAD_DIM), lambda i, j: (i % nb, 0))] * 2
        extras = list(tables)
    if split_out:
        out_specs = [pl.BlockSpec((tm, HEAD_DIM), lambda i, j: (i, 0))] * split_out
        out_shape = [jax.ShapeDtypeStruct((m, HEAD_DIM), dt) for dt in out_dtypes]
    else:
        out_specs = [pl.BlockSpec((tm, tn), lambda i, j: (i, j))]
        out_shape = [jax.ShapeDtypeStruct((m, n), out_dtypes[0])]
    kern = functools.partial(_mm_kernel, nx=len(xs), nw=len(ws), ne=len(extras),
                             groups=groups, epilogue=epilogue)
    out = pl.pallas_call(
        kern, grid=(m // tm, n // tn), in_specs=in_specs, out_specs=out_specs,
        out_shape=out_shape, compiler_params=_cp("arbitrary", "arbitrary"), name=name)(
            *xs, *ws, *extras)
    return out if split_out else out[0]


def proj(h, w, dtype=F32, name="proj"):
    return matmul([h], [w], [[(0, 0)]], _ep_plain, [dtype], name=name)


def proj_rope(h, w, tabs, dtype, name="proj_rope"):
    cos, sin, nb = tabs
    return matmul([h], [w], [[(0, 0)]], _ep_rope, [dtype], tables=(cos, sin),
                  n_tab_blocks=nb, name=name)


def proj_out(xa, xb, wa, wb, name="proj_out"):
    return matmul([xa, xb], [wa, wb], [[(0, 0), (1, 1)]], _ep_plain, [F32], name=name)


def rope_tables(pos, reps, tm):
    half = HEAD_DIM // 2
    inv = ROPE_THETA ** (-jnp.arange(half, dtype=F32) / half)
    ang = pos.astype(F32)[:, None] * inv[None, :]
    cos, sin = jnp.cos(ang), jnp.sin(ang)
    cos = jnp.concatenate([cos, cos], axis=-1)
    sin = jnp.concatenate([-sin, sin], axis=-1)
    t = pos.shape[0]
    if tm > t:
        cos, sin = jnp.tile(cos, (tm // t, 1)), jnp.tile(sin, (tm // t, 1))
        return cos, sin, 1
    return cos, sin, t // tm


def _conv_kernel(x_ref, wb_ref, wc_ref, wx_ref, cw_ref, init_ref, ya_ref, st_ref, carry_ref):
    i = pl.program_id(2)

    @pl.when(i == 0)
    def _():
        carry_ref[...] = init_ref[0]

    x = x_ref[0]
    bg = _dot(x, wb_ref[...])
    u = _dot(x, wc_ref[...]) * _dot(x, wx_ref[...])
    tm = u.shape[0]
    c = carry_ref[...]
    rows = lax.broadcasted_iota(jnp.int32, u.shape, 0)
    u1 = jnp.where(rows == 0, c[1:2], pltpu.roll(u, 1, 0))
    u2 = jnp.where(rows == 0, c[0:1], jnp.where(rows == 1, c[1:2], pltpu.roll(u, 2, 0)))
    cw = cw_ref[...]
    conv = cw[0:1] * u2 + cw[1:2] * u1 + cw[2:3] * u
    ya_ref[0] = (bg * conv).astype(ya_ref.dtype)
    new = u[tm - (CONV_K - 1):tm]
    carry_ref[...] = new
    st_ref[0] = new


def conv_mixer(h3, wb, wc, wx, cw, init):
    b, t, d = h3.shape
    c = wb.shape[1]
    tm = _tile(t, 1024, 16)
    tn = _tile(c, 512, LANES)
    wspec = pl.BlockSpec((d, tn), lambda j, bi, i: (0, j))
    return pl.pallas_call(
        _conv_kernel, grid=(c // tn, b, t // tm),
        in_specs=[pl.BlockSpec((1, tm, d), lambda j, bi, i: (bi, i, 0)), wspec, wspec, wspec,
                  pl.BlockSpec((CONV_K, tn), lambda j, bi, i: (0, j)),
                  pl.BlockSpec((1, CONV_K - 1, tn), lambda j, bi, i: (bi, 0, j))],
        out_specs=[pl.BlockSpec((1, tm, tn), lambda j, bi, i: (bi, i, j)),
                   pl.BlockSpec((1, CONV_K - 1, tn), lambda j, bi, i: (bi, 0, j))],
        out_shape=[jax.ShapeDtypeStruct((b, t, c), BF16),
                   jax.ShapeDtypeStruct((b, CONV_K - 1, c), F32)],
        scratch_shapes=[pltpu.VMEM((CONV_K - 1, tn), F32)],
        compiler_params=_cp("arbitrary", "arbitrary", "arbitrary"), name="conv_mixer")(
            h3, wb, wc, wx, cw, init)


def _stack_heads(q_ref, heads):
    return jnp.concatenate([q_ref[:, h * HEAD_DIM:(h + 1) * HEAD_DIM] for h in heads], axis=0)


LOG2E = 1.4426950408889634


def _with_ones_column(v):
    ones = jnp.where(lax.broadcasted_iota(jnp.int32, v.shape, 1) == 0, 1.0, 0.0).astype(v.dtype)
    return jnp.concatenate([v, ones], axis=1)


def _softmax_av(qk, scale, bias, v_ones):
    s = qk * (scale * LOG2E) + bias[None]
    e = jnp.exp2(s - jnp.max(s, axis=-1, keepdims=True))
    r, tq, n = e.shape
    o = _dot(e.reshape(r * tq, n).astype(BF16), v_ones)
    return o[:, :HEAD_DIM] / o[:, HEAD_DIM:HEAD_DIM + 1]


def _sort_key(x):
    bits = lax.bitcast_convert_type(x + 0.0, jnp.int32)
    return jnp.where(bits < 0, bits ^ jnp.int32(0x7FFFFFFF), bits)


INT_MIN = -2 ** 31


def _kth_largest_key(key, k):
    def body(it, othr):
        bit = lax.shift_left(jnp.int32(1), jnp.int32(31) - it)
        cand = othr | bit
        cnt = jnp.sum(jnp.where(key >= (cand ^ jnp.int32(INT_MIN)), 1.0, 0.0), axis=1, keepdims=True)
        return jnp.where(cnt >= k, cand, othr)
    othr = lax.fori_loop(0, 32, body, jnp.zeros((key.shape[0], 1), jnp.int32))
    return othr ^ jnp.int32(INT_MIN)


def _topk_mask(key, k, scratch_ref):
    thr = _kth_largest_key(key, k)
    live = key > jnp.int32(INT_MIN)
    ge = (key >= thr) & live
    scratch_ref[...] = jnp.where(ge, 1.0, 0.0)
    n_ge = jnp.sum(jnp.where(ge, 1.0, 0.0), axis=1, keepdims=True)

    @pl.when(jnp.max(n_ge) > k)
    def _():
        gt = key > thr
        eq = (key == thr) & live
        need = k - jnp.sum(jnp.where(gt, 1.0, 0.0), axis=1, keepdims=True)
        r = lax.broadcasted_iota(jnp.int32, (LANES, LANES), 0)
        c = lax.broadcasted_iota(jnp.int32, (LANES, LANES), 1)
        before = jnp.where(r < c, 1.0, 0.0).astype(BF16)
        run = jnp.zeros_like(need)
        for ch in range(key.shape[1] // LANES):
            sl = slice(ch * LANES, (ch + 1) * LANES)
            e = jnp.where(eq[:, sl], 1.0, 0.0)
            pre = _dot(e.astype(BF16), before) + run
            scratch_ref[:, sl] = jnp.where(gt[:, sl], 1.0, e * jnp.where(pre < need, 1.0, 0.0))
            run = run + jnp.sum(e, axis=1, keepdims=True)


CAUSAL_BUCKETS = 4


def _by_key_extent(i, tq, t, fn):
    nb = CAUSAL_BUCKETS if t % (CAUSAL_BUCKETS * tq) == 0 else 1
    size = t // nb
    for bkt in range(nb):
        @pl.when((i * tq) // size == bkt)
        def _():
            fn((bkt + 1) * size)


def _dsa_prompt_kernel(qi_ref, wi_ref, ki_ref, q_ref, k_ref, v_ref, o_ref,
                       kib_ref, kb_ref, vb_ref, mask_ref, *, topk):
    i = pl.program_id(1)

    @pl.when(i == 0)
    def _():
        kib_ref[...] = ki_ref[...].astype(BF16)
        kb_ref[...] = k_ref[...].astype(BF16)
        for g in range(KV_B):
            vb_ref[g] = _with_ones_column(v_ref[:, g * HEAD_DIM:(g + 1) * HEAD_DIM].astype(BF16))

    tq = q_ref.shape[0]
    t = ki_ref.shape[0]
    wi = wi_ref[...]
    rep = q_ref.shape[1] // HEAD_DIM // KV_B

    def attend(n):
        kib = kib_ref[0:n]
        score = jnp.zeros((tq, n), F32)
        for h in range(H_IDX):
            s = _dot_nt(qi_ref[:, h * D_IDX:(h + 1) * D_IDX], kib) * D_IDX ** -0.5
            score = score + jnp.maximum(s, 0.0) * wi[:, h:h + 1]
        score = score * H_IDX ** -0.5
        qpos = i * tq + lax.broadcasted_iota(jnp.int32, (tq, n), 0)
        kpos = lax.broadcasted_iota(jnp.int32, (tq, n), 1)
        key = jnp.where(kpos <= qpos, _sort_key(score), jnp.int32(INT_MIN))
        sel_ref = mask_ref.at[:, 0:n]
        _topk_mask(key, topk, sel_ref)
        bias = jnp.where(sel_ref[...] > 0.0, 0.0, NEG)
        for g in range(KV_B):
            qs = _stack_heads(q_ref, range(g * rep, (g + 1) * rep))
            sl = slice(g * HEAD_DIM, (g + 1) * HEAD_DIM)
            qk = _dot_nt(qs, kb_ref[0:n, sl]).reshape(rep, tq, n)
            o = _softmax_av(qk, HEAD_DIM ** -0.5, bias, vb_ref[g, 0:n])
            for r in range(rep):
                h = g * rep + r
                o_ref[:, h * HEAD_DIM:(h + 1) * HEAD_DIM] = o[r * tq:(r + 1) * tq].astype(o_ref.dtype)

    _by_key_extent(i, tq, t, attend)


def dsa_prompt(qi, wi, ki, q, k, v, b, t):
    m = q.shape[0]
    tq = _tile(t, 128, 16)
    nq = t // tq
    topk = min(TOPK_MAX, t // 4)
    qrow = lambda w: pl.BlockSpec((tq, w), lambda bi, i: (bi * nq + i, 0))
    full = lambda w: pl.BlockSpec((t, w), lambda bi, i: (bi, 0))
    return pl.pallas_call(
        functools.partial(_dsa_prompt_kernel, topk=topk), grid=(b, nq),
        in_specs=[qrow(qi.shape[1]), qrow(wi.shape[1]), full(ki.shape[1]),
                  qrow(q.shape[1]), full(k.shape[1]), full(v.shape[1])],
        out_specs=qrow(q.shape[1]),
        out_shape=jax.ShapeDtypeStruct((m, q.shape[1]), BF16),
        scratch_shapes=[pltpu.VMEM((t, ki.shape[1]), BF16), pltpu.VMEM((t, k.shape[1]), BF16),
                        pltpu.VMEM((KV_B, t, 2 * HEAD_DIM), BF16), pltpu.VMEM((tq, t), F32)],
        compiler_params=_cp("arbitrary", "arbitrary"), name="dsa_prompt")(qi, wi, ki, q, k, v)


def _masked_softmax(s, mask):
    m = jnp.max(jnp.where(mask, s, NEG), axis=-1, keepdims=True)
    m = jnp.where(m > 0.5 * NEG, m, 0.0)
    e = jnp.where(mask, jnp.exp(s - m), 0.0)
    return e / jnp.maximum(jnp.sum(e, axis=-1, keepdims=True), 1e-30)


def _pair_sums(imp, col):
    n = imp.shape[1]
    return imp + jnp.where(col % 2 == 0, pltpu.roll(imp, n - 1, 1), pltpu.roll(imp, 1, 1))


def _select_blocks(bs, col, qpos, n_selblk):
    blk = col // 2
    cur = qpos // SEL_BLOCK
    forced = (blk == 0) | (blk == cur) | (blk == cur - 1)
    admiss = (blk * SEL_BLOCK <= qpos) & (blk < n_selblk)
    work = jnp.where(admiss, jnp.where(forced, bs + FORCE_BONUS, bs), NEG)
    sel = jnp.zeros(bs.shape, jnp.bool_)
    big = jnp.int32(2 ** 30)
    for _ in range(min(N_SEL, n_selblk)):
        mx = jnp.max(work, axis=1, keepdims=True)
        idx = jnp.min(jnp.where(work == mx, col, big), axis=1, keepdims=True)
        pick = blk == idx // 2
        sel = sel | pick
        work = jnp.where(pick, -3e38, work)
    return jnp.where(sel & admiss, 1.0, 0.0)


def _nsa_prompt_kernel(q_ref, gc_ref, kc_ref, vc_ref, ks_ref, vs_ref, kw_ref, vw_ref,
                       cwk_ref, cwv_ref, o_ref,
                       kcmp_ref, vcmp_ref, ksb_ref, vsb_ref, kwb_ref, vwb_ref, *, win):
    i = pl.program_id(1)
    tq = q_ref.shape[0]
    t = kc_ref.shape[0]
    nb = t // CMP_BLOCK
    ncp = kcmp_ref.shape[0]
    n_selblk = -(-t // SEL_BLOCK)
    nh = q_ref.shape[1] // HEAD_DIM
    scale = HEAD_DIM ** -0.5

    @pl.when(i == 0)
    def _():
        kcmp_ref[...] = jnp.zeros(kcmp_ref.shape, kcmp_ref.dtype)
        vcmp_ref[...] = jnp.zeros(vcmp_ref.shape, vcmp_ref.dtype)
        kc = kc_ref[...].reshape(nb, CMP_BLOCK, HEAD_DIM)
        vc = vc_ref[...].reshape(nb, CMP_BLOCK, HEAD_DIM)
        kcmp_ref[0:nb] = jnp.sum(kc * cwk_ref[...][None], axis=1).astype(BF16)
        vcmp_ref[0:nb] = jnp.sum(vc * cwv_ref[...][None], axis=1).astype(BF16)
        ksb_ref[...] = ks_ref[...].astype(BF16)
        vsb_ref[...] = _with_ones_column(vs_ref[...].astype(BF16))
        kwb_ref[...] = kw_ref[...].astype(BF16)
        vwb_ref[...] = _with_ones_column(vw_ref[...].astype(BF16))

    qs = _stack_heads(q_ref, range(nh))
    col = lax.broadcasted_iota(jnp.int32, (tq, ncp), 1)
    qpos_c = i * tq + lax.broadcasted_iota(jnp.int32, (tq, ncp), 0)
    cmask = ((col + 1) * CMP_BLOCK - 1 <= qpos_c) & (col < nb)
    s = (_dot_nt(qs, kcmp_ref[...]) * scale).reshape(nh, tq, ncp)
    p = _masked_softmax(s, cmask[None])
    o_cmp = _dot(p.reshape(nh * tq, ncp).astype(BF16), vcmp_ref[...])
    imp = jnp.sum(p, axis=0)
    sel = _select_blocks(_pair_sums(imp, col), col, qpos_c, n_selblk).astype(BF16)
    start = pl.multiple_of(jnp.clip(i * tq - WINDOW, 0, t - win), 16)
    qpos_w = i * tq + lax.broadcasted_iota(jnp.int32, (tq, win), 0)
    kpos_w = start + lax.broadcasted_iota(jnp.int32, (tq, win), 1)
    bias_win = jnp.where((kpos_w <= qpos_w) & (kpos_w > qpos_w - WINDOW), 0.0, NEG)
    kwin = kwb_ref[pl.ds(start, win), :]
    vwin = vwb_ref[pl.ds(start, win), :]
    gate = jax.nn.sigmoid(gc_ref[...])
    grp = 4
    o_wins = []
    for hg in range(nh // grp):
        q4 = qs[hg * grp * tq:(hg + 1) * grp * tq]
        qk_win = _dot_nt(q4, kwin).reshape(grp, tq, win)
        o_wins.append(_softmax_av(qk_win, scale, bias_win, vwin))

    def attend(n):
        er = lax.broadcasted_iota(jnp.int32, (ncp, n), 0)
        ec = lax.broadcasted_iota(jnp.int32, (ncp, n), 1)
        expand = jnp.where(er == 2 * (ec // SEL_BLOCK), 1.0, 0.0).astype(BF16)
        tok = _dot(sel, expand)
        qpos = i * tq + lax.broadcasted_iota(jnp.int32, (tq, n), 0)
        kpos = lax.broadcasted_iota(jnp.int32, (tq, n), 1)
        bias_sel = jnp.where((tok > 0.5) & (kpos <= qpos), 0.0, NEG)
        for hg in range(nh // grp):
            q4 = qs[hg * grp * tq:(hg + 1) * grp * tq]
            qk_sel = _dot_nt(q4, ksb_ref[0:n]).reshape(grp, tq, n)
            o_sel = _softmax_av(qk_sel, scale, bias_sel, vsb_ref[0:n])
            o_win = o_wins[hg]
            for r in range(grp):
                h = hg * grp + r
                rows = slice(r * tq, (r + 1) * tq)
                o = (gate[:, 3 * h:3 * h + 1] * o_cmp[h * tq:(h + 1) * tq]
                     + gate[:, 3 * h + 1:3 * h + 2] * o_sel[rows] + gate[:, 3 * h + 2:3 * h + 3] * o_win[rows])
                o_ref[:, h * HEAD_DIM:(h + 1) * HEAD_DIM] = o.astype(o_ref.dtype)

    _by_key_extent(i, tq, t, attend)


def nsa_prompt(q, gc, kc, vc, ks, vs, kw, vw, cwk, cwv, b, t):
    m, hq = q.shape
    tq = _tile(t, 128, 16)
    nq = t // tq
    win = min(WINDOW + tq, t)
    ncp = LANES
    assert t // CMP_BLOCK <= ncp and t % SEL_BLOCK == 0
    qrow = lambda w: pl.BlockSpec((tq, w), lambda bi, i: (bi * nq + i, 0))
    full = pl.BlockSpec((t, HEAD_DIM), lambda bi, i: (bi, 0))
    cw = pl.BlockSpec((CMP_BLOCK, HEAD_DIM), lambda bi, i: (0, 0))
    kv = pltpu.VMEM((t, HEAD_DIM), BF16)
    kv_ones = pltpu.VMEM((t, 2 * HEAD_DIM), BF16)
    return pl.pallas_call(
        functools.partial(_nsa_prompt_kernel, win=win), grid=(b, nq),
        in_specs=[qrow(hq), qrow(gc.shape[1])] + [full] * 6 + [cw, cw],
        out_specs=qrow(hq), out_shape=jax.ShapeDtypeStruct((m, hq), BF16),
        scratch_shapes=[pltpu.VMEM((ncp, HEAD_DIM), BF16), pltpu.VMEM((ncp, HEAD_DIM), BF16),
                        kv, kv_ones, kv, kv_ones],
        compiler_params=_cp("arbitrary", "arbitrary"), name="nsa_prompt")(
            q, gc, kc, vc, ks, vs, kw, vw, cwk, cwv)


def _log_sigmoids(z):
    ls = jnp.minimum(z, 0.0) - jnp.log(1.0 + jnp.exp(-jnp.abs(z)))
    return ls, ls - z


def _split_bf16(x):
    hi = x.astype(BF16)
    return hi, (x - hi.astype(F32)).astype(BF16)


SB_HEADS_PER_STEP = 2


def _sb_prompt_kernel(q_ref, k_ref, v_ref, o_ref):
    i = pl.program_id(2)
    tq = q_ref.shape[0]
    nhs = q_ref.shape[1] // HEAD_DIM
    r = lax.broadcasted_iota(jnp.int32, (tq, tq), 0)
    c = lax.broadcasted_iota(jnp.int32, (tq, tq), 1)
    later = jnp.where(r > c, 1.0, 0.0).astype(BF16)
    before = c < r
    qs = [q_ref[:, h * HEAD_DIM:(h + 1) * HEAD_DIM] for h in range(nhs)]
    cols = lambda h: slice(h * HEAD_DIM, (h + 1) * HEAD_DIM)
    block_off = lambda j: pl.multiple_of(jnp.maximum(j, 0) * tq, tq)

    def scores(h, off):
        return _dot_nt(qs[h], k_ref[pl.ds(off, tq), cols(h)].astype(BF16)) * HEAD_DIM ** -0.5

    def weights(z, carry, m):
        ls, lneg = _log_sigmoids(z)
        if m is not None:
            lneg = jnp.where(m, lneg, 0.0)
        hi, lo = _split_bf16(lneg)
        after = _dot(hi, later) + _dot(lo, later)
        a = jnp.exp(ls + after + carry)
        if m is not None:
            a = jnp.where(m, a, 0.0)
        return a.astype(BF16), carry + after[:, 0:1] + lneg[:, 0:1]

    def weighted_values(h, a, off, acc):
        return acc + _dot(a, v_ref[pl.ds(off, tq), cols(h)].astype(BF16))

    st = []
    for h in range(nhs):
        a, carry = weights(scores(h, block_off(i)), jnp.zeros((tq, 1), F32), before)
        st.append((scores(h, block_off(i - 1)), a, carry, jnp.zeros((tq, HEAD_DIM), F32)))

    def body(jj, st):
        out = []
        for h in range(nhs):
            z, a_prev, carry, acc = st[h]
            z_next = scores(h, block_off(i - 2 - jj))
            acc = weighted_values(h, a_prev, block_off(i - jj), acc)
            a, carry = weights(z, carry, None)
            out.append((z_next, a, carry, acc))
        return tuple(out)

    st = lax.fori_loop(0, i, body, tuple(st))
    for h in range(nhs):
        acc = weighted_values(h, st[h][1], 0, st[h][3])
        o_ref[:, h * HEAD_DIM:(h + 1) * HEAD_DIM] = acc.astype(o_ref.dtype)


def sb_prompt(q, k, v, b, t):
    m, hq = q.shape
    nhs = SB_HEADS_PER_STEP
    tq = _tile(t, 256, 16)
    nq = t // tq
    w = nhs * HEAD_DIM
    qrow = pl.BlockSpec((tq, w), lambda bi, h, i: (bi * nq + i, h))
    full = pl.BlockSpec((t, w), lambda bi, h, i: (bi, h))
    return pl.pallas_call(
        _sb_prompt_kernel, grid=(b, hq // w, nq),
        in_specs=[qrow, full, full], out_specs=qrow,
        out_shape=jax.ShapeDtypeStruct((m, hq), BF16),
        compiler_params=_cp("arbitrary", "arbitrary", "arbitrary"), name="sb_prompt")(q, k, v)


def _mem_kernel(h_ref, wq_ref, mk_ref, mv_ref, wo_ref, o_ref):
    q = _dot(h_ref[0], wq_ref[...])
    outs = []
    for hh in range(H_MEM):
        sl = slice(hh * HEAD_DIM, (hh + 1) * HEAD_DIM)
        s = _dot_nt(q[:, sl].astype(BF16), mk_ref[0, :, sl].astype(BF16)) * HEAD_DIM ** -0.5
        e = jnp.exp(s - jnp.max(s, axis=-1, keepdims=True))
        o = _dot(e.astype(BF16), mv_ref[0, :, sl].astype(BF16))
        outs.append(o / jnp.sum(e, axis=-1, keepdims=True))
    o_ref[0] = _dot(jnp.concatenate(outs, axis=1).astype(BF16), wo_ref[...])


def mem_sublayer(h3, wq, mk, mv, wo):
    b, t, d = h3.shape
    tq = _tile(t, 256, 16)
    nm, hm = mk.shape[1], mk.shape[2]
    row = pl.BlockSpec((1, tq, d), lambda bi, i: (bi, i, 0))
    mem = pl.BlockSpec((1, nm, hm), lambda bi, i: (bi, 0, 0))
    return pl.pallas_call(
        _mem_kernel, grid=(b, t // tq),
        in_specs=[row, pl.BlockSpec((d, hm), lambda bi, i: (0, 0)), mem, mem,
                  pl.BlockSpec((hm, d), lambda bi, i: (0, 0))],
        out_specs=row, out_shape=jax.ShapeDtypeStruct((b, t, d), F32),
        compiler_params=_cp("arbitrary", "arbitrary"), name="mem_sublayer")(h3, wq, mk, mv, wo)


def _ffn_kernel(h_ref, wg_ref, wu_ref, wd_ref, o_ref, *, dff):
    f = pl.program_id(1)
    tf = wd_ref.shape[0]
    h = h_ref[...]
    g = _dot(h, wg_ref[...])
    a = (g * jax.nn.sigmoid(g)) * _dot(h, wu_ref[...])
    wd = wd_ref[...]
    if dff % tf:
        valid = dff - f * tf
        a = jnp.where(lax.broadcasted_iota(jnp.int32, a.shape, 1) < valid, a, 0.0)
        wd = jnp.where(lax.broadcasted_iota(jnp.int32, wd.shape, 0) < valid, wd, jnp.zeros_like(wd))
    part = _dot(a.astype(BF16), wd)

    @pl.when(f == 0)
    def _():
        o_ref[...] = part

    @pl.when(f > 0)
    def _():
        o_ref[...] += part


def ffn(h, wg, wu, wd, li):
    m, d = h.shape
    dff = wg.shape[2]
    tm = _tile(m, 512, 16)
    tf = min(512, dff)
    return pl.pallas_call(
        functools.partial(_ffn_kernel, dff=dff), grid=(m // tm, pl.cdiv(dff, tf)),
        in_specs=[pl.BlockSpec((tm, d), lambda i, f: (i, 0)),
                  pl.BlockSpec((None, d, tf), lambda i, f: (li, 0, f)),
                  pl.BlockSpec((None, d, tf), lambda i, f: (li, 0, f)),
                  pl.BlockSpec((None, tf, d), lambda i, f: (li, f, 0))],
        out_specs=pl.BlockSpec((tm, d), lambda i, f: (i, 0)),
        out_shape=jax.ShapeDtypeStruct((m, d), F32),
        compiler_params=_cp("arbitrary", "arbitrary"), name="ffn")(h, wg, wu, wd)


def _odd_weights(w_in, w_out, nh_c, nh_d):
    sizes = [nh_c * HEAD_DIM] + [HEAD_DIM] * 6 + [nh_c * 3] + [nh_d * HEAD_DIM] * 3
    offs = np.cumsum([0] + sizes)
    cut = lambda a, b_: w_in[:, offs[a]:offs[b_]].astype(BF16)
    w = {"qc": cut(0, 1), "kv6": cut(1, 7), "gc": cut(7, 8), "qd": cut(8, 9), "kd": cut(9, 10),
         "vd": cut(10, 11)}
    w["out_c"] = w_out[:nh_c * HEAD_DIM].astype(BF16)
    w["out_d"] = w_out[nh_c * HEAD_DIM:].astype(BF16)
    return w


def _odd_project(h, w, tabs):
    cos, sin, nb = tabs
    qc = proj_rope(h, w["qc"], tabs, BF16, name="proj_qc")
    kv6 = matmul([h], [w["kv6"]], [[(0, 0)]], functools.partial(_ep_split, rope_chunks=(0, 2, 4)),
                 [F32] * 6, tables=(cos, sin), n_tab_blocks=nb, split_out=6, name="proj_kv6")
    gc = proj(h, w["gc"], name="proj_gc")
    qd = proj(h, w["qd"], BF16, name="proj_qd")
    kd = proj(h, w["kd"], name="proj_kd")
    vd = proj(h, w["vd"], name="proj_vd")
    return qc, kv6, gc, qd, kd, vd


def odd_mixer_prompt(h, b, t, w, cwk, cwv):
    m, d = h.shape
    tabs = rope_tables(jnp.arange(t), b, _tile(m, 1024, 16))
    qc, (kc, vc, ks, vs, kw, vw), gc, qd, kd, vd = _odd_project(h, w, tabs)
    o_c = nsa_prompt(qc, gc, kc, vc, ks, vs, kw, vw, cwk, cwv, b, t)
    o_d = sb_prompt(qd, kd, vd, b, t)
    y = proj_out(o_c, o_d, w["out_c"], w["out_d"])
    nw = min(WINDOW, t)
    st1 = lambda a: a.reshape(b, t, 1, HEAD_DIM)
    sth = lambda a: a.reshape(b, t, -1, HEAD_DIM)
    return y, (st1(kc), st1(vc), st1(ks), st1(vs), st1(kw)[:, t - nw:], st1(vw)[:, t - nw:],
               sth(kd), sth(vd))


def _even_weights(w_in, conv_w, w_out):
    c = conv_w.shape[1]
    hq = w_out.shape[0] - c
    sizes = [c, c, c, hq, KV_B * HEAD_DIM, KV_B * HEAD_DIM, H_IDX * D_IDX, D_IDX, H_IDX]
    offs = np.cumsum([0] + sizes)
    names = ["bg", "cg", "xa", "q", "k", "v", "qi", "ki", "wi"]
    w = {n: w_in[:, offs[j]:offs[j + 1]].astype(BF16) for j, n in enumerate(names)}
    w["conv_w"] = conv_w
    w["out_a"] = w_out[:c].astype(BF16)
    w["out_b"] = w_out[c:].astype(BF16)
    return w


def _even_project(h, w, tabs):
    q = proj_rope(h, w["q"], tabs, BF16, name="proj_q")
    k = proj_rope(h, w["k"], tabs, F32, name="proj_k")
    v = proj(h, w["v"], name="proj_v")
    qi = proj_rope(h, w["qi"], tabs, BF16, name="proj_qi")
    ki = proj_rope(h, w["ki"], tabs, F32, name="proj_ki")
    wi = proj(h, w["wi"], name="proj_wi")
    return q, k, v, qi, ki, wi


def even_mixer_prompt(h, b, t, w):
    m, d = h.shape
    tabs = rope_tables(jnp.arange(t), b, _tile(m, 1024, 16))
    init = jnp.zeros((b, CONV_K - 1, w["bg"].shape[1]), F32)
    ya, conv_state = conv_mixer(h.reshape(b, t, d), w["bg"], w["cg"], w["xa"], w["conv_w"], init)
    q, k, v, qi, ki, wi = _even_project(h, w, tabs)
    ob = dsa_prompt(qi, wi, ki, q, k, v, b, t)
    y = proj_out(ya.reshape(m, -1), ob, w["out_a"], w["out_b"])
    return y, (conv_state, k.reshape(b, t, KV_B, HEAD_DIM), v.reshape(b, t, KV_B, HEAD_DIM),
               ki.reshape(b, t, D_IDX))


def _paged_call(kern, grid, in_specs, out_specs, out_shape, scratch, name, page_table, args):
    gs = pltpu.PrefetchScalarGridSpec(num_scalar_prefetch=1, grid=grid, in_specs=in_specs,
                                      out_specs=out_specs, scratch_shapes=scratch)
    return pl.pallas_call(kern, grid_spec=gs, out_shape=out_shape,
                          compiler_params=_cp("arbitrary", "arbitrary"), name=name)(page_table, *args)


def _pad_rows(a, rows):
    return jnp.pad(a, ((0, 0), (0, rows - a.shape[1])) + ((0, 0),) * (a.ndim - 2))


def _page_specs(block_tail, layer, g_pages, page_of):
    zeros = (0,) * len(block_tail)

    def spec(j):
        return pl.BlockSpec((1, 1) + block_tail,
                            lambda bi, p, pt: (layer, page_of(bi, p, pt, j)) + zeros)
    return [spec(j) for j in range(g_pages)]


def _forward_pages(g_pages):
    return lambda bi, p, pt, j: pt[bi, p * g_pages + j]


def _rows_ht(a, b, t, nh):
    w = a.shape[1] // nh
    return a.reshape(b, t, nh, w).transpose(0, 2, 1, 3).reshape(b, nh * t, w)


def _rows_th(a, b, t, nh):
    w = a.shape[2]
    return a.reshape(b, nh, t, w).transpose(0, 2, 1, 3).reshape(b * t, nh * w)


def _dsa_scores_kernel(pt_ref, qi_ref, wi_ref, new_ref, *rest, g_pages, t):
    pools, (o_ref, onew_ref) = rest[:g_pages], rest[g_pages:]
    qi, wi = qi_ref[0], wi_ref[0]

    def scores(kb):
        s = _dot_nt(qi, kb) * D_IDX ** -0.5
        s = jnp.maximum(s, 0.0) * wi
        return jnp.sum(s.reshape(H_IDX, t, kb.shape[0]), axis=0) * H_IDX ** -0.5

    o_ref[0] = scores(jnp.concatenate([r[0, 0] for r in pools], axis=0).astype(BF16))

    @pl.when(pl.program_id(1) == 0)
    def _():
        onew_ref[0] = scores(new_ref[0].astype(BF16))


def dsa_sample_scores(qi_r, wi_r, ki_new, pool, e, page_table, t):
    b, n_pages = page_table.shape
    g = _tile(n_pages, 8, 1)
    kern = functools.partial(_dsa_scores_kernel, g_pages=g, t=t)
    per_b = lambda r, w: pl.BlockSpec((1, r, w), lambda bi, p, pt: (bi, 0, 0))
    past, new = _paged_call(
        kern, (b, n_pages // g),
        [per_b(H_IDX * t, D_IDX), per_b(H_IDX * t, 1), per_b(PAGE_SIZE, D_IDX)]
        + _page_specs((PAGE_SIZE, D_IDX), e, g, _forward_pages(g)),
        [pl.BlockSpec((1, t, g * PAGE_SIZE), lambda bi, p, pt: (bi, 0, p)), per_b(t, PAGE_SIZE)],
        [jax.ShapeDtypeStruct((b, t, n_pages * PAGE_SIZE), F32),
         jax.ShapeDtypeStruct((b, t, PAGE_SIZE), F32)], [], "dsa_sample_scores",
        page_table, (qi_r, wi_r, _pad_rows(ki_new, PAGE_SIZE)) + (pool,) * g)
    return jnp.concatenate([past, new], axis=-1)


def _dsa_topk_kernel(s_ref, o_ref, mask_ref, *, past, topk):
    score = s_ref[0]
    kpos = lax.broadcasted_iota(jnp.int32, score.shape, 1)
    qpos = past + lax.broadcasted_iota(jnp.int32, score.shape, 0)
    key = jnp.where(kpos <= qpos, _sort_key(score), jnp.int32(INT_MIN))
    _topk_mask(key, topk, mask_ref)
    o_ref[0] = mask_ref[...]


def dsa_sample_topk(score, past, topk):
    b, t, nk = score.shape
    blk = pl.BlockSpec((1, t, nk), lambda bi: (bi, 0, 0))
    return pl.pallas_call(
        functools.partial(_dsa_topk_kernel, past=past, topk=topk), grid=(b,),
        in_specs=[blk], out_specs=blk, out_shape=jax.ShapeDtypeStruct((b, t, nk), F32),
        scratch_shapes=[pltpu.VMEM((t, nk), F32)],
        compiler_params=_cp("arbitrary"), name="dsa_sample_topk")(score)


def _online_softmax_step(s, valid, v, m_ref, l_ref, acc_ref):
    m_old = m_ref[...]
    m_new = jnp.maximum(m_old, jnp.max(jnp.where(valid, s, NEG), axis=-1, keepdims=True))
    alpha = jnp.exp(m_old - m_new)
    e = jnp.where(valid, jnp.exp(s - m_new), 0.0)
    l_ref[...] = alpha * l_ref[...] + jnp.sum(e, axis=-1, keepdims=True)
    acc_ref[...] = alpha * acc_ref[...] + _dot(e.astype(BF16), v)
    m_ref[...] = m_new


def _init_softmax_state(m_ref, l_ref, acc_ref):
    m_ref[...] = jnp.full(m_ref.shape, NEG, F32)
    l_ref[...] = jnp.zeros(l_ref.shape, F32)
    acc_ref[...] = jnp.zeros(acc_ref.shape, F32)


def _dsa_sample_attn_kernel(pt_ref, q_ref, mask_ref, masknew_ref, knew_ref, vnew_ref, *rest,
                            g_pages, rep):
    kps, vps = rest[:g_pages], rest[g_pages:2 * g_pages]
    o_ref, m_ref, l_ref, acc_ref = rest[2 * g_pages:]
    p = pl.program_id(1)

    @pl.when(p == 0)
    def _():
        _init_softmax_state(m_ref, l_ref, acc_ref)

    def update(keys, vals, valid_t):
        valid = jnp.concatenate([valid_t] * rep, axis=0)
        for g in range(KV_B):
            s = _dot_nt(q_ref[0, g], keys(g).astype(BF16)) * HEAD_DIM ** -0.5
            _online_softmax_step(s, valid, vals(g).astype(BF16), m_ref.at[g], l_ref.at[g], acc_ref.at[g])

    update(lambda g: jnp.concatenate([r[0, 0, :, g, :] for r in kps], axis=0),
           lambda g: jnp.concatenate([r[0, 0, :, g, :] for r in vps], axis=0), mask_ref[0] > 0.5)

    @pl.when(p == pl.num_programs(1) - 1)
    def _():
        update(lambda g: knew_ref[0, :, g, :], lambda g: vnew_ref[0, :, g, :], masknew_ref[0] > 0.5)
        o_ref[0] = acc_ref[...] / l_ref[...]


def dsa_sample_attn(q_g, mask, kpool, vpool, k_new, v_new, e, page_table):
    b, n_pages = page_table.shape
    t = mask.shape[1]
    rows = q_g.shape[2]
    g = _tile(n_pages, 8, 1)
    kern = functools.partial(_dsa_sample_attn_kernel, g_pages=g, rep=rows // t)
    qspec = pl.BlockSpec((1, KV_B, rows, HEAD_DIM), lambda bi, p, pt: (bi, 0, 0, 0))
    new = pl.BlockSpec((1, PAGE_SIZE, KV_B, HEAD_DIM), lambda bi, p, pt: (bi, 0, 0, 0))
    pools = _page_specs((PAGE_SIZE, KV_B, HEAD_DIM), e, g, _forward_pages(g))
    return _paged_call(
        kern, (b, n_pages // g),
        [qspec, pl.BlockSpec((1, t, g * PAGE_SIZE), lambda bi, p, pt: (bi, 0, p)),
         pl.BlockSpec((1, t, PAGE_SIZE), lambda bi, p, pt: (bi, 0, n_pages)), new, new] + pools + pools,
        qspec, jax.ShapeDtypeStruct((b, KV_B, rows, HEAD_DIM), F32),
        [pltpu.VMEM((KV_B, rows, 1), F32), pltpu.VMEM((KV_B, rows, 1), F32),
         pltpu.VMEM((KV_B, rows, HEAD_DIM), F32)],
        "dsa_sample_attn", page_table,
        (q_g, mask, mask, _pad_rows(k_new, PAGE_SIZE), _pad_rows(v_new, PAGE_SIZE))
        + (kpool,) * g + (vpool,) * g)


def even_mixer_sample(h, b, t, w, e, page_table, state_conv, c_k, c_v, c_kidx):
    m, d = h.shape
    n_pages = page_table.shape[1]
    past = n_pages * PAGE_SIZE
    tabs = rope_tables(past + jnp.arange(t), b, _tile(m, 1024, 16))
    ya, conv_state = conv_mixer(h.reshape(b, t, d), w["bg"], w["cg"], w["xa"], w["conv_w"], state_conv[e])
    q, k, v, qi, ki, wi = _even_project(h, w, tabs)
    score = dsa_sample_scores(_rows_ht(qi, b, t, H_IDX), _rows_ht(wi, b, t, H_IDX),
                              ki.reshape(b, t, D_IDX), c_kidx, e, page_table, t)
    mask = dsa_sample_topk(score, past, min(TOPK_MAX, (past + t) // 4))
    nh = q.shape[1] // HEAD_DIM
    rep = nh // KV_B
    q_g = _rows_ht(q, b, t, nh).reshape(b, KV_B, rep * t, HEAD_DIM)
    o = dsa_sample_attn(q_g, mask, c_k, c_v, k.reshape(b, t, KV_B, HEAD_DIM),
                        v.reshape(b, t, KV_B, HEAD_DIM), e, page_table)
    ob = _rows_th(o.reshape(b, nh * t, HEAD_DIM), b, t, nh).astype(BF16)
    y = proj_out(ya.reshape(m, -1), ob, w["out_a"], w["out_b"])
    return y, (conv_state, k.reshape(b, t, KV_B, HEAD_DIM), v.reshape(b, t, KV_B, HEAD_DIM),
               ki.reshape(b, t, D_IDX))


def _compress_pages_kernel(pt_ref, cwk_ref, cwv_ref, *rest, g_pages):
    kcs, vcs = rest[:g_pages], rest[g_pages:2 * g_pages]
    ko_ref, vo_ref = rest[2 * g_pages:]
    nb = PAGE_SIZE // CMP_BLOCK
    cwk, cwv = cwk_ref[...][None], cwv_ref[...][None]
    for j in range(g_pages):
        ko_ref[0, j] = jnp.sum(kcs[j][0, 0].reshape(nb, CMP_BLOCK, HEAD_DIM) * cwk, axis=1)
        vo_ref[0, j] = jnp.sum(vcs[j][0, 0].reshape(nb, CMP_BLOCK, HEAD_DIM) * cwv, axis=1)


def nsa_compress_pages(kpool, vpool, cwk, cwv, o, page_table):
    b, n_pages = page_table.shape
    nb = PAGE_SIZE // CMP_BLOCK
    g = _tile(n_pages, 8, 1)
    pools = _page_specs((PAGE_SIZE, HEAD_DIM), o, g, _forward_pages(g))
    cw = pl.BlockSpec((CMP_BLOCK, HEAD_DIM), lambda bi, p, pt: (0, 0))
    out = pl.BlockSpec((1, g, nb, HEAD_DIM), lambda bi, p, pt: (bi, p, 0, 0))
    shp = jax.ShapeDtypeStruct((b, n_pages, nb, HEAD_DIM), F32)
    kc, vc = _paged_call(functools.partial(_compress_pages_kernel, g_pages=g), (b, n_pages // g),
                         [cw, cw] + pools + pools, [out, out], [shp, shp], [], "nsa_compress_pages",
                         page_table, (cwk, cwv) + (kpool,) * g + (vpool,) * g)
    return kc.reshape(b, n_pages * nb, HEAD_DIM), vc.reshape(b, n_pages * nb, HEAD_DIM)


def _nsa_sample_a_kernel(q_ref, kcmp_ref, vcmp_ref, kw_ref, vw_ref, ocmp_ref, owin_ref, sel_ref,
                         *, past, t, nwin, n_selblk):
    q = q_ref[0]
    rows = q.shape[0]
    nh = rows // t
    scale = HEAD_DIM ** -0.5
    nb = kcmp_ref.shape[1]
    col = lax.broadcasted_iota(jnp.int32, (rows, nb), 1)
    pos = past + lax.broadcasted_iota(jnp.int32, (rows, nb), 0) % t
    s = _dot_nt(q, kcmp_ref[0].astype(BF16)) * scale
    p = _masked_softmax(s, (col + 1) * CMP_BLOCK - 1 <= pos)
    ocmp_ref[0] = _dot(p.astype(BF16), vcmp_ref[0].astype(BF16))
    imp = jnp.sum(p.reshape(nh, t, nb), axis=0)
    ncol = sel_ref.shape[2]
    imp = jnp.concatenate([imp, jnp.zeros((t, ncol - nb), F32)], axis=1)
    col_s = lax.broadcasted_iota(jnp.int32, (t, ncol), 1)
    pos_s = past + lax.broadcasted_iota(jnp.int32, (t, ncol), 0)
    sel_ref[0] = _select_blocks(_pair_sums(imp, col_s), col_s, pos_s, n_selblk)
    nwp = kw_ref.shape[1]
    colw = lax.broadcasted_iota(jnp.int32, (rows, nwp), 1)
    posw = past + lax.broadcasted_iota(jnp.int32, (rows, nwp), 0) % t
    kwpos = past + t - nwin + colw
    valid = (kwpos <= posw) & (kwpos > posw - WINDOW) & (colw < nwin)
    sw = _dot_nt(q, kw_ref[0].astype(BF16)) * scale
    pw = _masked_softmax(sw, valid)
    owin_ref[0] = _dot(pw.astype(BF16), vw_ref[0].astype(BF16))


def nsa_sample_a(q_r, kcmp, vcmp, kw_pad, vw_pad, past, t, nwin):
    b, rows, _ = q_r.shape
    nb = kcmp.shape[1]
    n_selblk = -(-(past + t) // SEL_BLOCK)
    ncol = -(-2 * n_selblk // LANES) * LANES
    assert ncol > nb >= 2 * n_selblk - 2 and nb % LANES == 0
    per_b = lambda r, w: pl.BlockSpec((1, r, w), lambda bi: (bi, 0, 0))
    kern = functools.partial(_nsa_sample_a_kernel, past=past, t=t, nwin=nwin, n_selblk=n_selblk)
    return pl.pallas_call(
        kern, grid=(b,),
        in_specs=[per_b(rows, HEAD_DIM), per_b(nb, HEAD_DIM), per_b(nb, HEAD_DIM),
                  per_b(kw_pad.shape[1], HEAD_DIM), per_b(kw_pad.shape[1], HEAD_DIM)],
        out_specs=[per_b(rows, HEAD_DIM), per_b(rows, HEAD_DIM), per_b(t, ncol)],
        out_shape=[jax.ShapeDtypeStruct((b, rows, HEAD_DIM), F32)] * 2
        + [jax.ShapeDtypeStruct((b, t, ncol), F32)],
        compiler_params=_cp("arbitrary"), name="nsa_sample_a")(q_r, kcmp, vcmp, kw_pad, vw_pad)


def _nsa_sample_b_kernel(pt_ref, q_ref, tok_ref, toknew_ref, knew_ref, vnew_ref,
                         ocmp_ref, owin_ref, gate_ref, *rest, g_pages, n_pages, t):
    kps, vps = rest[:g_pages], rest[g_pages:2 * g_pages]
    o_ref, m_ref, l_ref, acc_ref = rest[2 * g_pages:]
    p = pl.program_id(1)

    @pl.when(p == 0)
    def _():
        _init_softmax_state(m_ref, l_ref, acc_ref)

    q = q_ref[0]
    rows = q.shape[0]

    def update(keys, vals, tok, first_key):
        n = keys.shape[0]
        kpos = first_key + lax.broadcasted_iota(jnp.int32, (rows, n), 1)
        qpos = n_pages * PAGE_SIZE + lax.broadcasted_iota(jnp.int32, (rows, n), 0) % t
        valid = jnp.concatenate([tok > 0.5] * (rows // t), axis=0) & (kpos <= qpos)
        s = _dot_nt(q, keys.astype(BF16)) * HEAD_DIM ** -0.5
        _online_softmax_step(s, valid, vals.astype(BF16), m_ref, l_ref, acc_ref)

    update(jnp.concatenate([r[0, 0] for r in kps], axis=0),
           jnp.concatenate([r[0, 0] for r in vps], axis=0), tok_ref[0], p * (g_pages * PAGE_SIZE))

    @pl.when(p == pl.num_programs(1) - 1)
    def _():
        update(knew_ref[0], vnew_ref[0], toknew_ref[0], n_pages * PAGE_SIZE)
        g = jax.nn.sigmoid(gate_ref[0])
        o_ref[0] = (g[:, 0:1] * ocmp_ref[0] + g[:, 1:2] * (acc_ref[...] / l_ref[...])
                    + g[:, 2:3] * owin_ref[0])


def nsa_sample_b(q_r, tok, kpool, vpool, ks_new, vs_new, o_cmp, o_win, gate_r, o, page_table):
    b, n_pages = page_table.shape
    rows = q_r.shape[1]
    t = tok.shape[1]
    g = _tile(n_pages, 8, 1)
    per_b = lambda r, w: pl.BlockSpec((1, r, w), lambda bi, p, pt: (bi, 0, 0))
    pools = _page_specs((PAGE_SIZE, HEAD_DIM), o, g, _forward_pages(g))
    kern = functools.partial(_nsa_sample_b_kernel, g_pages=g, n_pages=n_pages, t=t)
    return _paged_call(
        kern, (b, n_pages // g),
        [per_b(rows, HEAD_DIM), pl.BlockSpec((1, t, g * PAGE_SIZE), lambda bi, p, pt: (bi, 0, p)),
         pl.BlockSpec((1, t, PAGE_SIZE), lambda bi, p, pt: (bi, 0, n_pages)),
         per_b(PAGE_SIZE, HEAD_DIM), per_b(PAGE_SIZE, HEAD_DIM),
         per_b(rows, HEAD_DIM), per_b(rows, HEAD_DIM), per_b(rows, 3)] + pools + pools,
        per_b(rows, HEAD_DIM), jax.ShapeDtypeStruct((b, rows, HEAD_DIM), F32),
        [pltpu.VMEM((rows, 1), F32), pltpu.VMEM((rows, 1), F32), pltpu.VMEM((rows, HEAD_DIM), F32)],
        "nsa_sample_b", page_table,
        (q_r, tok, tok, _pad_rows(ks_new, PAGE_SIZE), _pad_rows(vs_new, PAGE_SIZE),
         o_cmp, o_win, gate_r) + (kpool,) * g + (vpool,) * g)


def _sb_sample_kernel(pt_ref, qt_ref, knew_ref, vnew_ref, *rest, g_pages, t):
    kps, vps = rest[:g_pages], rest[g_pages:2 * g_pages]
    o_ref, carry_ref, acc_ref, z_ref = rest[2 * g_pages:]
    p = pl.program_id(1)
    n, nh = knew_ref.shape[1], knew_ref.shape[2]
    cols = nh * t

    @pl.when(p == 0)
    def _():
        carry_ref[...] = jnp.zeros(carry_ref.shape, F32)
        acc_ref[...] = jnp.zeros(acc_ref.shape, F32)

    iota = lambda shape, d: lax.broadcasted_iota(jnp.int32, shape, d)
    later = jnp.where(iota((n, n), 1) > iota((n, n), 0), 1.0, 0.0).astype(BF16)
    own_head = iota((nh, cols), 1) // t == iota((nh, cols), 0)
    spread = jnp.where(iota((n, n * nh), 1) // nh == iota((n, n * nh), 0), 1.0, 0.0).astype(BF16)
    own_rows = iota((cols, n * nh), 0) // t == iota((cols, n * nh), 1) % nh

    def block(k3, v3, m):
        k2 = k3.reshape(n * nh, HEAD_DIM).astype(BF16)
        v2 = v3.reshape(n * nh, HEAD_DIM).astype(BF16)
        z_all = _dot(k2, qt_ref[0]).reshape(n, nh, cols)
        z_ref[...] = jnp.sum(jnp.where(own_head[None], z_all, 0.0), axis=1)
        ls, lneg = _log_sigmoids(z_ref[...] * HEAD_DIM ** -0.5)
        if m is not None:
            lneg = jnp.where(m, lneg, 0.0)
        hi, lo = _split_bf16(lneg)
        after = _dot(later, hi) + _dot(later, lo)
        a = jnp.exp(ls + after + carry_ref[...])
        if m is not None:
            a = jnp.where(m, a, 0.0)
        a2 = jnp.where(own_rows, _dot(a.T.astype(BF16), spread), 0.0)
        acc_ref[...] += _dot(a2.astype(BF16), v2)
        carry_ref[...] += after[0:1] + lneg[0:1]

    @pl.when(p == 0)
    def _():
        block(knew_ref[0], vnew_ref[0], iota((n, cols), 0) < iota((n, cols), 1) % t)

    for j in range(g_pages):
        block(kps[j][0, 0], vps[j][0, 0], None)

    @pl.when(p == pl.num_programs(1) - 1)
    def _():
        o_ref[0] = acc_ref[...]


def sb_sample(qd, kd_new, vd_new, kpool, vpool, o, page_table, b, t):
    n_pages = page_table.shape[1]
    nh = qd.shape[1] // HEAD_DIM
    g = _tile(n_pages, 4, 1)
    qt = qd.reshape(b, t, nh, HEAD_DIM).transpose(0, 3, 2, 1).reshape(b, HEAD_DIM, nh * t)
    new = pl.BlockSpec((1, PAGE_SIZE, nh, HEAD_DIM), lambda bi, p, pt: (bi, 0, 0, 0))
    pools = _page_specs((PAGE_SIZE, nh, HEAD_DIM), o, g,
                        lambda bi, p, pt, j: pt[bi, n_pages - 1 - (p * g + j)])
    out = pl.BlockSpec((1, nh * t, HEAD_DIM), lambda bi, p, pt: (bi, 0, 0))
    return _paged_call(
        functools.partial(_sb_sample_kernel, g_pages=g, t=t), (b, n_pages // g),
        [pl.BlockSpec((1, HEAD_DIM, nh * t), lambda bi, p, pt: (bi, 0, 0)), new, new]
        + pools + pools,
        out, jax.ShapeDtypeStruct((b, nh * t, HEAD_DIM), F32),
        [pltpu.VMEM((1, nh * t), F32), pltpu.VMEM((nh * t, HEAD_DIM), F32),
         pltpu.VMEM((PAGE_SIZE, nh * t), F32)],
        "sb_sample", page_table,
        (qt, _pad_rows(kd_new, PAGE_SIZE), _pad_rows(vd_new, PAGE_SIZE)) + (kpool,) * g + (vpool,) * g)


def odd_mixer_sample(h, b, t, w, cwk, cwv, o, page_table, c_kc, c_vc, c_ks, c_vs, c_kw, c_vw,
                     c_kd, c_vd):
    m, d = h.shape
    n_pages = page_table.shape[1]
    past = n_pages * PAGE_SIZE
    assert past % CMP_BLOCK == 0 and t < CMP_BLOCK
    tabs = rope_tables(past + jnp.arange(t), b, _tile(m, 1024, 16))
    qc, (kc, vc, ks, vs, kw, vw), gc, qd, kd, vd = _odd_project(h, w, tabs)
    nh = qc.shape[1] // HEAD_DIM
    n_pool = c_kc.shape[1]
    pool1 = lambda a: a.reshape(-1, n_pool, PAGE_SIZE, HEAD_DIM)
    seq = lambda a: a.reshape(b, t, -1)
    kcmp, vcmp = nsa_compress_pages(pool1(c_kc), pool1(c_vc), cwk, cwv, o, page_table)
    wb = c_kw.shape[2]
    kw_all = jnp.concatenate([c_kw[o].reshape(b, wb, HEAD_DIM), seq(kw)], axis=1)
    vw_all = jnp.concatenate([c_vw[o].reshape(b, wb, HEAD_DIM), seq(vw)], axis=1)
    nwp = -(-(wb + t) // LANES) * LANES
    q_r = _rows_ht(qc, b, t, nh)
    o_cmp, o_win, sel = nsa_sample_a(q_r, kcmp, vcmp, _pad_rows(kw_all, nwp), _pad_rows(vw_all, nwp),
                                     past, t, wb + t)
    n_selblk = -(-(past + t) // SEL_BLOCK)
    tok = jnp.repeat(sel[:, :, 0:2 * n_selblk:2], SEL_BLOCK, axis=-1)
    tok = jnp.pad(tok, ((0, 0), (0, 0), (0, (n_pages + 1) * PAGE_SIZE - tok.shape[-1])))
    gate_r = _rows_ht(gc, b, t, nh)
    o_c = nsa_sample_b(q_r, tok, pool1(c_ks), pool1(c_vs), seq(ks), seq(vs), o_cmp, o_win, gate_r,
                       o, page_table)
    nh_d = kd.shape[1] // HEAD_DIM
    heads = lambda a: a.reshape(b, t, nh_d, HEAD_DIM)
    o_d = sb_sample(qd, heads(kd), heads(vd), c_kd, c_vd, o, page_table, b, t)
    y = proj_out(_rows_th(o_c, b, t, nh).astype(BF16), _rows_th(o_d, b, t, nh_d).astype(BF16),
                 w["out_c"], w["out_d"])
    st1 = lambda a: a.reshape(b, t, 1, HEAD_DIM)
    return y, (st1(kc), st1(vc), st1(ks), st1(vs), kw_all[:, t:].reshape(b, wb, 1, HEAD_DIM),
               vw_all[:, t:].reshape(b, wb, 1, HEAD_DIM), kd.reshape(b, t, -1, HEAD_DIM),
               vd.reshape(b, t, -1, HEAD_DIM))


def kernel(x_prompt, x_sample, state_conv, cache_dsa_k, cache_dsa_v, cache_dsa_kidx, cache_nsa_kc, cache_nsa_vc, cache_nsa_ks, cache_nsa_vs, cache_nsa_kw, cache_nsa_vw, cache_sb_k, cache_sb_v, cache_mem_k, cache_mem_v, page_table, mem_prompt, norm_pre, norm_post, norm_mem, w_in_even, conv_w, w_out_even, w_in_odd, cmp_wk, cmp_wv, w_out_odd, w_mq, w_mk, w_mv, w_mo, w_gate, w_up, w_down):
    bp, tp, d = x_prompt.shape
    bs, ts, _ = x_sample.shape
    depth = norm_pre.shape[0]
    n_mem = mem_prompt.shape[1]
    hm = w_mq.shape[2]
    nh_c = nh_d = w_out_odd.shape[1] // 2 // HEAD_DIM
    xp = x_prompt.reshape(bp * tp, d)
    xs = x_sample.reshape(bs * ts, d)
    hp = norm_cast(xp, norm_pre[0, 0])
    hs = norm_cast(xs, norm_pre[0, 0])
    ev_p, ev_s, od_p, od_s, mem_p = [], [], [], [], []
    wg, wu, wd = w_gate.astype(BF16), w_up.astype(BF16), w_down.astype(BF16)
    for li in range(depth):
        g_pre, g_post = norm_pre[li], norm_post[li]
        if li % 2 == 0:
            e = li // 2
            w = _even_weights(w_in_even[e], conv_w[e], w_out_even[e])
            mp, stp = even_mixer_prompt(hp, bp, tp, w)
            ms, sts = even_mixer_sample(hs, bs, ts, w, e, page_table, state_conv,
                                        cache_dsa_k, cache_dsa_v, cache_dsa_kidx)
            ev_p.append(stp)
            ev_s.append(sts)
        else:
            o = li // 2
            w = _odd_weights(w_in_odd[o], w_out_odd[o], nh_c, nh_d)
            mp, stp = odd_mixer_prompt(hp, bp, tp, w, cmp_wk[o], cmp_wv[o])
            ms, sts = odd_mixer_sample(hs, bs, ts, w, cmp_wk[o], cmp_wv[o], o, page_table,
                                       cache_nsa_kc, cache_nsa_vc, cache_nsa_ks, cache_nsa_vs,
                                       cache_nsa_kw, cache_nsa_vw, cache_sb_k, cache_sb_v)
            od_p.append(stp)
            od_s.append(sts)
        xp, hp = resid_norm(xp, mp, g_post[0], g_pre[1])
        xs, hs = resid_norm(xs, ms, g_post[0], g_pre[1])
        wq, wo = w_mq[li].astype(BF16), w_mo[li].astype(BF16)
        hmem = norm_cast(mem_prompt.reshape(bp * n_mem, d), norm_mem[li])
        mkp = proj(hmem, w_mk[li].astype(BF16), name="proj_mk")
        mvp = proj(hmem, w_mv[li].astype(BF16), name="proj_mv")
        mem_p.append((mkp.reshape(bp, n_mem, H_MEM, HEAD_DIM), mvp.reshape(bp, n_mem, H_MEM, HEAD_DIM)))
        yp = mem_sublayer(hp.reshape(bp, tp, d), wq, mkp.reshape(bp, n_mem, hm),
                          mvp.reshape(bp, n_mem, hm), wo).reshape(bp * tp, d)
        ys = mem_sublayer(hs.reshape(bs, ts, d), wq, cache_mem_k[li].reshape(bs, n_mem, hm),
                          cache_mem_v[li].reshape(bs, n_mem, hm), wo).reshape(bs * ts, d)
        xp, hp = resid_norm(xp, yp, g_post[1], g_pre[2])
        xs, hs = resid_norm(xs, ys, g_post[1], g_pre[2])
        g_next = norm_pre[li + 1, 0] if li + 1 < depth else None
        xp, hp = resid_norm(xp, ffn(hp, wg, wu, wd, li), g_post[2], g_next)
        xs, hs = resid_norm(xs, ffn(hs, wg, wu, wd, li), g_post[2], g_next)
    stack = lambda lst, j: jnp.stack([s[j] for s in lst])
    return ((xp.reshape(bp, tp, d), xs.reshape(bs, ts, d))
            + tuple(stack(ev_p, j) for j in range(4)) + tuple(stack(od_p, j) for j in range(8))
            + (stack(mem_p, 0), stack(mem_p, 1))
            + tuple(stack(ev_s, j) for j in range(4)) + tuple(stack(od_s, j) for j in range(8)))
```

```python
import functools

import numpy as np
import jax
import jax.numpy as jnp
from jax import lax
from jax.experimental import pallas as pl
from jax.experimental.pallas import tpu as pltpu

F32 = jnp.float32
BF16 = jnp.bfloat16

HEAD_DIM = 128
PAGE_SIZE = 128
CONV_K = 3
KV_B = 4
H_IDX = 16
D_IDX = 128
TOPK_MAX = 256
CMP_BLOCK = 32
SEL_BLOCK = 64
N_SEL = 16
WINDOW = 512
FORCE_BONUS = 1.0e6
H_MEM = 4
ROPE_THETA = 10000.0
EPS = 1e-6
NEG = -1e30
LANES = 128
VMEM_LIMIT = 56 * 1024 * 1024


def _cp(*sem):
    return pltpu.CompilerParams(dimension_semantics=sem, vmem_limit_bytes=VMEM_LIMIT)


def _tile(n, pref, mult):
    t = (min(pref, n) // mult) * mult
    while t >= mult:
        if n % t == 0:
            return t
        t -= mult
    return n


def _dot(a, b):
    return jnp.dot(a, b, preferred_element_type=F32)


def _dot_nt(a, b):
    return lax.dot_general(a, b, (((1,), (1,)), ((), ())), preferred_element_type=F32)


def _rms(x, g):
    return x * lax.rsqrt(jnp.mean(x * x, axis=-1, keepdims=True) + EPS) * g


def _rope(y, cos, sin):
    return y * cos + pltpu.roll(y, HEAD_DIM // 2, 1) * sin


def _norm_cast_kernel(x_ref, g_ref, o_ref):
    o_ref[...] = _rms(x_ref[...], g_ref[...]).astype(o_ref.dtype)


def norm_cast(x, g):
    m, d = x.shape
    tm = _tile(m, 256, 16)
    row = pl.BlockSpec((tm, d), lambda i: (i, 0))
    return pl.pallas_call(
        _norm_cast_kernel, grid=(m // tm,),
        in_specs=[row, pl.BlockSpec((1, d), lambda i: (0, 0))],
        out_specs=row, out_shape=jax.ShapeDtypeStruct((m, d), BF16),
        compiler_params=_cp("arbitrary"), name="norm_cast")(x, g.reshape(1, d))


def _resid_norm_kernel(x_ref, y_ref, gp_ref, gn_ref, xo_ref, h_ref):
    xn = x_ref[...] + _rms(y_ref[...], gp_ref[...])
    xo_ref[...] = xn
    h_ref[...] = _rms(xn, gn_ref[...]).astype(h_ref.dtype)


def _resid_kernel(x_ref, y_ref, gp_ref, xo_ref):
    xo_ref[...] = x_ref[...] + _rms(y_ref[...], gp_ref[...])


def resid_norm(x, y, g_post, g_next):
    m, d = x.shape
    tm = _tile(m, 256, 16)
    row = pl.BlockSpec((tm, d), lambda i: (i, 0))
    vec = pl.BlockSpec((1, d), lambda i: (0, 0))
    if g_next is None:
        return pl.pallas_call(
            _resid_kernel, grid=(m // tm,), in_specs=[row, row, vec], out_specs=row,
            out_shape=jax.ShapeDtypeStruct((m, d), F32),
            compiler_params=_cp("arbitrary"), name="resid")(x, y, g_post.reshape(1, d)), None
    return pl.pallas_call(
        _resid_norm_kernel, grid=(m // tm,), in_specs=[row, row, vec, vec],
        out_specs=[row, row],
        out_shape=[jax.ShapeDtypeStruct((m, d), F32), jax.ShapeDtypeStruct((m, d), BF16)],
        compiler_params=_cp("arbitrary"), name="resid_norm")(
            x, y, g_post.reshape(1, d), g_next.reshape(1, d))


def _mm_kernel(*refs, nx, nw, ne, groups, epilogue):
    xr, wr = refs[:nx], refs[nx:nx + nw]
    er, orf = refs[nx + nw:nx + nw + ne], refs[nx + nw + ne:]
    accs = []
    for grp in groups:
        acc = None
        for xi, wi in grp:
            d = _dot(xr[xi][...], wr[wi][...])
            acc = d if acc is None else acc + d
        accs.append(acc)
    epilogue(accs, er, orf)


def _ep_plain(accs, er, orf):
    orf[0][...] = accs[0].astype(orf[0].dtype)


def _ep_rope(accs, er, orf):
    cos, sin = er[0][...], er[1][...]
    y = accs[0]
    for c in range(y.shape[1] // HEAD_DIM):
        sl = slice(c * HEAD_DIM, (c + 1) * HEAD_DIM)
        orf[0][:, sl] = _rope(y[:, sl], cos, sin).astype(orf[0].dtype)


def _ep_split(accs, er, orf, *, rope_chunks):
    cos, sin = er[0][...], er[1][...]
    y = accs[0]
    for c in range(len(orf)):
        yc = y[:, c * HEAD_DIM:(c + 1) * HEAD_DIM]
        if c in rope_chunks:
            yc = _rope(yc, cos, sin)
        orf[c][...] = yc.astype(orf[c].dtype)


def matmul(xs, ws, groups, epilogue, out_dtypes, *, tables=None, n_tab_blocks=1,
           tm_pref=1024, tn_pref=512, split_out=0, layer=None, name="matmul"):
    m = xs[0].shape[0]
    n = ws[0].shape[-1]
    if tables is not None:
        tm = tables[0].shape[0] // n_tab_blocks
    else:
        tm = _tile(m, tm_pref, 16)
    tn = n if (split_out or n % LANES) else _tile(n, tn_pref, LANES)
    in_specs = [pl.BlockSpec((tm, x.shape[1]), lambda i, j: (i, 0)) for x in xs]
    if layer is None:
        in_specs += [pl.BlockSpec((w.shape[0], tn), lambda i, j: (0, j)) for w in ws]
    else:
        in_specs += [pl.BlockSpec((None, w.shape[1], tn), lambda i, j: (layer, 0, j)) for w in ws]
    extras = []
    if tables is not None:
        nb = n_tab_blocks
        in_specs += [pl.BlockSpec((tm, HEAD_DIM), lambda i, j: (i % nb, 0))] * 2
        extras = list(tables)
    if split_out:
        out_specs = [pl.BlockSpec((tm, HEAD_DIM), lambda i, j: (i, 0))] * split_out
        out_shape = [jax.ShapeDtypeStruct((m, HEAD_DIM), dt) for dt in out_dtypes]
    else:
        out_specs = [pl.BlockSpec((tm, tn), lambda i, j: (i, j))]
        out_shape = [jax.ShapeDtypeStruct((m, n), out_dtypes[0])]
    kern = functools.partial(_mm_kernel, nx=len(xs), nw=len(ws), ne=len(extras),
                             groups=groups, epilogue=epilogue)
    out = pl.pallas_call(
        kern, grid=(m // tm, n // tn), in_specs=in_specs, out_specs=out_specs,
        out_shape=out_shape, compiler_params=_cp("arbitrary", "arbitrary"), name=name)(
            *xs, *ws, *extras)
    return out if split_out else out[0]


def proj(h, w, dtype=F32, name="proj"):
    return matmul([h], [w], [[(0, 0)]], _ep_plain, [dtype], name=name)


def proj_rope(h, w, tabs, dtype, name="proj_rope"):
    cos, sin, nb = tabs
    return matmul([h], [w], [[(0, 0)]], _ep_rope, [dtype], tables=(cos, sin),
                  n_tab_blocks=nb, name=name)


def proj_out(xa, xb, wa, wb, name="proj_out"):
    return matmul([xa, xb], [wa, wb], [[(0, 0), (1, 1)]], _ep_plain, [F32], name=name)


def rope_tables(pos, reps, tm):
    half = HEAD_DIM // 2
    inv = ROPE_THETA ** (-jnp.arange(half, dtype=F32) / half)
    ang = pos.astype(F32)[:, None] * inv[None, :]
    cos, sin = jnp.cos(ang), jnp.sin(ang)
    cos = jnp.concatenate([cos, cos], axis=-1)
    sin = jnp.concatenate([-sin, sin], axis=-1)
    t = pos.shape[0]
    if tm > t:
        cos, sin = jnp.tile(cos, (tm // t, 1)), jnp.tile(sin, (tm // t, 1))
        return cos, sin, 1
    return cos, sin, t // tm


def _conv_kernel(x_ref, wb_ref, wc_ref, wx_ref, cw_ref, init_ref, ya_ref, st_ref, carry_ref):
    i = pl.program_id(2)

    @pl.when(i == 0)
    def _():
        carry_ref[...] = init_ref[0]

    x = x_ref[0]
    bg = _dot(x, wb_ref[...])
    u = _dot(x, wc_ref[...]) * _dot(x, wx_ref[...])
    tm = u.shape[0]
    c = carry_ref[...]
    rows = lax.broadcasted_iota(jnp.int32, u.shape, 0)
    u1 = jnp.where(rows == 0, c[1:2], pltpu.roll(u, 1, 0))
    u2 = jnp.where(rows == 0, c[0:1], jnp.where(rows == 1, c[1:2], pltpu.roll(u, 2, 0)))
    cw = cw_ref[...]
    conv = cw[0:1] * u2 + cw[1:2] * u1 + cw[2:3] * u
    ya_ref[0] = (bg * conv).astype(ya_ref.dtype)
    new = u[tm - (CONV_K - 1):tm]
    carry_ref[...] = new
    st_ref[0] = new


def conv_mixer(h3, wb, wc, wx, cw, init):
    b, t, d = h3.shape
    c = wb.shape[1]
    tm = _tile(t, 1024, 16)
    tn = _tile(c, 512, LANES)
    wspec = pl.BlockSpec((d, tn), lambda j, bi, i: (0, j))
    return pl.pallas_call(
        _conv_kernel, grid=(c // tn, b, t // tm),
        in_specs=[pl.BlockSpec((1, tm, d), lambda j, bi, i: (bi, i, 0)), wspec, wspec, wspec,
                  pl.BlockSpec((CONV_K, tn), lambda j, bi, i: (0, j)),
                  pl.BlockSpec((1, CONV_K - 1, tn), lambda j, bi, i: (bi, 0, j))],
        out_specs=[pl.BlockSpec((1, tm, tn), lambda j, bi, i: (bi, i, j)),
                   pl.BlockSpec((1, CONV_K - 1, tn), lambda j, bi, i: (bi, 0, j))],
        out_shape=[jax.ShapeDtypeStruct((b, t, c), BF16),
                   jax.ShapeDtypeStruct((b, CONV_K - 1, c), F32)],
        scratch_shapes=[pltpu.VMEM((CONV_K - 1, tn), F32)],
        compiler_params=_cp("arbitrary", "arbitrary", "arbitrary"), name="conv_mixer")(
            h3, wb, wc, wx, cw, init)


def _stack_heads(q_ref, heads):
    return jnp.concatenate([q_ref[:, h * HEAD_DIM:(h + 1) * HEAD_DIM] for h in heads], axis=0)


LOG2E = 1.4426950408889634


def _with_ones_column(v):
    ones = jnp.where(lax.broadcasted_iota(jnp.int32, v.shape, 1) == 0, 1.0, 0.0).astype(v.dtype)
    return jnp.concatenate([v, ones], axis=1)


def _softmax_av(qk, scale, bias, v_ones):
    s = qk * (scale * LOG2E) + bias[None]
    e = jnp.exp2(s - jnp.max(s, axis=-1, keepdims=True))
    r, tq, n = e.shape
    o = _dot(e.reshape(r * tq, n).astype(BF16), v_ones)
    return o[:, :HEAD_DIM] / o[:, HEAD_DIM:HEAD_DIM + 1]


def _sort_key(x):
    bits = lax.bitcast_convert_type(x + 0.0, jnp.int32)
    return jnp.where(bits < 0, bits ^ jnp.int32(0x7FFFFFFF), bits)


INT_MIN = -2 ** 31


def _kth_largest_key(key, k):
    def body(it, othr):
        bit = lax.shift_left(jnp.int32(1), jnp.int32(31) - it)
        cand = othr | bit
        cnt = jnp.sum(jnp.where(key >= (cand ^ jnp.int32(INT_MIN)), 1.0, 0.0), axis=1, keepdims=True)
        return jnp.where(cnt >= k, cand, othr)
    othr = lax.fori_loop(0, 32, body, jnp.zeros((key.shape[0], 1), jnp.int32))
    return othr ^ jnp.int32(INT_MIN)


def _topk_mask(key, k, scratch_ref):
    thr = _kth_largest_key(key, k)
    live = key > jnp.int32(INT_MIN)
    ge = (key >= thr) & live
    scratch_ref[...] = jnp.where(ge, 1.0, 0.0)
    n_ge = jnp.sum(jnp.where(ge, 1.0, 0.0), axis=1, keepdims=True)

    @pl.when(jnp.max(n_ge) > k)
    def _():
        gt = key > thr
        eq = (key == thr) & live
        need = k - jnp.sum(jnp.where(gt, 1.0, 0.0), axis=1, keepdims=True)
        r = lax.broadcasted_iota(jnp.int32, (LANES, LANES), 0)
        c = lax.broadcasted_iota(jnp.int32, (LANES, LANES), 1)
        before = jnp.where(r < c, 1.0, 0.0).astype(BF16)
        run = jnp.zeros_like(need)
        for ch in range(key.shape[1] // LANES):
            sl = slice(ch * LANES, (ch + 1) * LANES)
            e = jnp.where(eq[:, sl], 1.0, 0.0)
            pre = _dot(e.astype(BF16), before) + run
            scratch_ref[:, sl] = jnp.where(gt[:, sl], 1.0, e * jnp.where(pre < need, 1.0, 0.0))
            run = run + jnp.sum(e, axis=1, keepdims=True)


CAUSAL_BUCKETS = 4


def _by_key_extent(i, tq, t, fn):
    nb = CAUSAL_BUCKETS if t % (CAUSAL_BUCKETS * tq) == 0 else 1
    size = t // nb
    for bkt in range(nb):
        @pl.when((i * tq) // size == bkt)
        def _():
            fn((bkt + 1) * size)


def _dsa_prompt_kernel(qi_ref, wi_ref, ki_ref, q_ref, k_ref, v_ref, o_ref,
                       kib_ref, kb_ref, vb_ref, mask_ref, *, topk):
    i = pl.program_id(1)

    @pl.when(i == 0)
    def _():
        kib_ref[...] = ki_ref[...].astype(BF16)
        kb_ref[...] = k_ref[...].astype(BF16)
        for g in range(KV_B):
            vb_ref[g] = _with_ones_column(v_ref[:, g * HEAD_DIM:(g + 1) * HEAD_DIM].astype(BF16))

    tq = q_ref.shape[0]
    t = ki_ref.shape[0]
    wi = wi_ref[...]
    rep = q_ref.shape[1] // HEAD_DIM // KV_B

    def attend(n):
        kib = kib_ref[0:n]
        score = jnp.zeros((tq, n), F32)
        for h in range(H_IDX):
            s = _dot_nt(qi_ref[:, h * D_IDX:(h + 1) * D_IDX], kib) * D_IDX ** -0.5
            score = score + jnp.maximum(s, 0.0) * wi[:, h:h + 1]
        score = score * H_IDX ** -0.5
        qpos = i * tq + lax.broadcasted_iota(jnp.int32, (tq, n), 0)
        kpos = lax.broadcasted_iota(jnp.int32, (tq, n), 1)
        key = jnp.where(kpos <= qpos, _sort_key(score), jnp.int32(INT_MIN))
        sel_ref = mask_ref.at[:, 0:n]
        _topk_mask(key, topk, sel_ref)
        bias = jnp.where(sel_ref[...] > 0.0, 0.0, NEG)
        for g in range(KV_B):
            qs = _stack_heads(q_ref, range(g * rep, (g + 1) * rep))
            sl = slice(g * HEAD_DIM, (g + 1) * HEAD_DIM)
            qk = _dot_nt(qs, kb_ref[0:n, sl]).reshape(rep, tq, n)
            o = _softmax_av(qk, HEAD_DIM ** -0.5, bias, vb_ref[g, 0:n])
            for r in range(rep):
                h = g * rep + r
                o_ref[:, h * HEAD_DIM:(h + 1) * HEAD_DIM] = o[r * tq:(r + 1) * tq].astype(o_ref.dtype)

    _by_key_extent(i, tq, t, attend)


def dsa_prompt(qi, wi, ki, q, k, v, b, t):
    m = q.shape[0]
    tq = _tile(t, 128, 16)
    nq = t // tq
    topk = min(TOPK_MAX, t // 4)
    qrow = lambda w: pl.BlockSpec((tq, w), lambda bi, i: (bi * nq + i, 0))
    full = lambda w: pl.BlockSpec((t, w), lambda bi, i: (bi, 0))
    return pl.pallas_call(
        functools.partial(_dsa_prompt_kernel, topk=topk), grid=(b, nq),
        in_specs=[qrow(qi.shape[1]), qrow(wi.shape[1]), full(ki.shape[1]),
                  qrow(q.shape[1]), full(k.shape[1]), full(v.shape[1])],
        out_specs=qrow(q.shape[1]),
        out_shape=jax.ShapeDtypeStruct((m, q.shape[1]), BF16),
        scratch_shapes=[pltpu.VMEM((t, ki.shape[1]), BF16), pltpu.VMEM((t, k.shape[1]), BF16),
                        pltpu.VMEM((KV_B, t, 2 * HEAD_DIM), BF16), pltpu.VMEM((tq, t), F32)],
        compiler_params=_cp("arbitrary", "arbitrary"), name="dsa_prompt")(qi, wi, ki, q, k, v)


def _masked_softmax(s, mask):
    m = jnp.max(jnp.where(mask, s, NEG), axis=-1, keepdims=True)
    m = jnp.where(m > 0.5 * NEG, m, 0.0)
    e = jnp.where(mask, jnp.exp(s - m), 0.0)
    return e / jnp.maximum(jnp.sum(e, axis=-1, keepdims=True), 1e-30)


def _pair_sums(imp, col):
    n = imp.shape[1]
    return imp + jnp.where(col % 2 == 0, pltpu.roll(imp, n - 1, 1), pltpu.roll(imp, 1, 1))


def _select_blocks(bs, col, qpos, n_selblk):
    blk = col // 2
    cur = qpos // SEL_BLOCK
    forced = (blk == 0) | (blk == cur) | (blk == cur - 1)
    admiss = (blk * SEL_BLOCK <= qpos) & (blk < n_selblk)
    work = jnp.where(admiss, jnp.where(forced, bs + FORCE_BONUS, bs), NEG)
    sel = jnp.zeros(bs.shape, jnp.bool_)
    big = jnp.int32(2 ** 30)
    for _ in range(min(N_SEL, n_selblk)):
        mx = jnp.max(work, axis=1, keepdims=True)
        idx = jnp.min(jnp.where(work == mx, col, big), axis=1, keepdims=True)
        pick = blk == idx // 2
        sel = sel | pick
        work = jnp.where(pick, -3e38, work)
    return jnp.where(sel & admiss, 1.0, 0.0)


def _nsa_prompt_kernel(q_ref, gc_ref, kc_ref, vc_ref, ks_ref, vs_ref, kw_ref, vw_ref,
                       cwk_ref, cwv_ref, o_ref,
                       kcmp_ref, vcmp_ref, ksb_ref, vsb_ref, kwb_ref, vwb_ref, *, win):
    i = pl.program_id(1)
    tq = q_ref.shape[0]
    t = kc_ref.shape[0]
    nb = t // CMP_BLOCK
    ncp = kcmp_ref.shape[0]
    n_selblk = -(-t // SEL_BLOCK)
    nh = q_ref.shape[1] // HEAD_DIM
    scale = HEAD_DIM ** -0.5

    @pl.when(i == 0)
    def _():
        kcmp_ref[...] = jnp.zeros(kcmp_ref.shape, kcmp_ref.dtype)
        vcmp_ref[...] = jnp.zeros(vcmp_ref.shape, vcmp_ref.dtype)
        kc = kc_ref[...].reshape(nb, CMP_BLOCK, HEAD_DIM)
        vc = vc_ref[...].reshape(nb, CMP_BLOCK, HEAD_DIM)
        kcmp_ref[0:nb] = jnp.sum(kc * cwk_ref[...][None], axis=1).astype(BF16)
        vcmp_ref[0:nb] = jnp.sum(vc * cwv_ref[...][None], axis=1).astype(BF16)
        ksb_ref[...] = ks_ref[...].astype(BF16)
        vsb_ref[...] = _with_ones_column(vs_ref[...].astype(BF16))
        kwb_ref[...] = kw_ref[...].astype(BF16)
        vwb_ref[...] = _with_ones_column(vw_ref[...].astype(BF16))

    qs = _stack_heads(q_ref, range(nh))
    col = lax.broadcasted_iota(jnp.int32, (tq, ncp), 1)
    qpos_c = i * tq + lax.broadcasted_iota(jnp.int32, (tq, ncp), 0)
    cmask = ((col + 1) * CMP_BLOCK - 1 <= qpos_c) & (col < nb)
    s = (_dot_nt(qs, kcmp_ref[...]) * scale).reshape(nh, tq, ncp)
    p = _masked_softmax(s, cmask[None])
    o_cmp = _dot(p.reshape(nh * tq, ncp).astype(BF16), vcmp_ref[...])
    imp = jnp.sum(p, axis=0)
    sel = _select_blocks(_pair_sums(imp, col), col, qpos_c, n_selblk).astype(BF16)
    start = pl.multiple_of(jnp.clip(i * tq - WINDOW, 0, t - win), 16)
    qpos_w = i * tq + lax.broadcasted_iota(jnp.int32, (tq, win), 0)
    kpos_w = start + lax.broadcasted_iota(jnp.int32, (tq, win), 1)
    bias_win = jnp.where((kpos_w <= qpos_w) & (kpos_w > qpos_w - WINDOW), 0.0, NEG)
    kwin = kwb_ref[pl.ds(start, win), :]
    vwin = vwb_ref[pl.ds(start, win), :]
    gate = jax.nn.sigmoid(gc_ref[...])
    grp = 4
    o_wins = []
    for hg in range(nh // grp):
        q4 = qs[hg * grp * tq:(hg + 1) * grp * tq]
        qk_win = _dot_nt(q4, kwin).reshape(grp, tq, win)
        o_wins.append(_softmax_av(qk_win, scale, bias_win, vwin))

    def attend(n):
        er = lax.broadcasted_iota(jnp.int32, (ncp, n), 0)
        ec = lax.broadcasted_iota(jnp.int32, (ncp, n), 1)
        expand = jnp.where(er == 2 * (ec // SEL_BLOCK), 1.0, 0.0).astype(BF16)
        tok = _dot(sel, expand)
        qpos = i * tq + lax.broadcasted_iota(jnp.int32, (tq, n), 0)
        kpos = lax.broadcasted_iota(jnp.int32, (tq, n), 1)
        bias_sel = jnp.where((tok > 0.5) & (kpos <= qpos), 0.0, NEG)
        for hg in range(nh // grp):
            q4 = qs[hg * grp * tq:(hg + 1) * grp * tq]
            qk_sel = _dot_nt(q4, ksb_ref[0:n]).reshape(grp, tq, n)
            o_sel = _softmax_av(qk_sel, scale, bias_sel, vsb_ref[0:n])
            o_win = o_wins[hg]
            for r in range(grp):
                h = hg * grp + r
                rows = slice(r * tq, (r + 1) * tq)
                o = (gate[:, 3 * h:3 * h + 1] * o_cmp[h * tq:(h + 1) * tq]
                     + gate[:, 3 * h + 1:3 * h + 2] * o_sel[rows] + gate[:, 3 * h + 2:3 * h + 3] * o_win[rows])
                o_ref[:, h * HEAD_DIM:(h + 1) * HEAD_DIM] = o.astype(o_ref.dtype)

    _by_key_extent(i, tq, t, attend)


def nsa_prompt(q, gc, kc, vc, ks, vs, kw, vw, cwk, cwv, b, t):
    m, hq = q.shape
    tq = _tile(t, 128, 16)
    nq = t // tq
    win = min(WINDOW + tq, t)
    ncp = LANES
    assert t // CMP_BLOCK <= ncp and t % SEL_BLOCK == 0
    qrow = lambda w: pl.BlockSpec((tq, w), lambda bi, i: (bi * nq + i, 0))
    full = pl.BlockSpec((t, HEAD_DIM), lambda bi, i: (bi, 0))
    cw = pl.BlockSpec((CMP_BLOCK, HEAD_DIM), lambda bi, i: (0, 0))
    kv = pltpu.VMEM((t, HEAD_DIM), BF16)
    kv_ones = pltpu.VMEM((t, 2 * HEAD_DIM), BF16)
    return pl.pallas_call(
        functools.partial(_nsa_prompt_kernel, win=win), grid=(b, nq),
        in_specs=[qrow(hq), qrow(gc.shape[1])] + [full] * 6 + [cw, cw],
        out_specs=qrow(hq), out_shape=jax.ShapeDtypeStruct((m, hq), BF16),
        scratch_shapes=[pltpu.VMEM((ncp, HEAD_DIM), BF16), pltpu.VMEM((ncp, HEAD_DIM), BF16),
                        kv, kv_ones, kv, kv_ones],
        compiler_params=_cp("arbitrary", "arbitrary"), name="nsa_prompt")(
            q, gc, kc, vc, ks, vs, kw, vw, cwk, cwv)


def _log_sigmoids(z):
    ls = jnp.minimum(z, 0.0) - jnp.log(1.0 + jnp.exp(-jnp.abs(z)))
    return ls, ls - z


def _split_bf16(x):
    hi = x.astype(BF16)
    return hi, (x - hi.astype(F32)).astype(BF16)


SB_HEADS_PER_STEP = 2


def _sb_prompt_kernel(q_ref, k_ref, v_ref, o_ref):
    i = pl.program_id(2)
    tq = q_ref.shape[0]
    nhs = q_ref.shape[1] // HEAD_DIM
    r = lax.broadcasted_iota(jnp.int32, (tq, tq), 0)
    c = lax.broadcasted_iota(jnp.int32, (tq, tq), 1)
    later = jnp.where(r > c, 1.0, 0.0).astype(BF16)
    before = c < r
    qs = [q_ref[:, h * HEAD_DIM:(h + 1) * HEAD_DIM] for h in range(nhs)]
    cols = lambda h: slice(h * HEAD_DIM, (h + 1) * HEAD_DIM)
    block_off = lambda j: pl.multiple_of(jnp.maximum(j, 0) * tq, tq)

    def scores(h, off):
        return _dot_nt(qs[h], k_ref[pl.ds(off, tq), cols(h)].astype(BF16)) * HEAD_DIM ** -0.5

    def weights(z, carry, m):
        ls, lneg = _log_sigmoids(z)
        if m is not None:
            lneg = jnp.where(m, lneg, 0.0)
        hi, lo = _split_bf16(lneg)
        after = _dot(hi, later) + _dot(lo, later)
        a = jnp.exp(ls + after + carry)
        if m is not None:
            a = jnp.where(m, a, 0.0)
        return a.astype(BF16), carry + after[:, 0:1] + lneg[:, 0:1]

    def weighted_values(h, a, off, acc):
        return acc + _dot(a, v_ref[pl.ds(off, tq), cols(h)].astype(BF16))

    st = []
    for h in range(nhs):
        a, carry = weights(scores(h, block_off(i)), jnp.zeros((tq, 1), F32), before)
        st.append((scores(h, block_off(i - 1)), a, carry, jnp.zeros((tq, HEAD_DIM), F32)))

    def body(jj, st):
        out = []
        for h in range(nhs):
            z, a_prev, carry, acc = st[h]
            z_next = scores(h, block_off(i - 2 - jj))
            acc = weighted_values(h, a_prev, block_off(i - jj), acc)
            a, carry = weights(z, carry, None)
            out.append((z_next, a, carry, acc))
        return tuple(out)

    st = lax.fori_loop(0, i, body, tuple(st))
    for h in range(nhs):
        acc = weighted_values(h, st[h][1], 0, st[h][3])
        o_ref[:, h * HEAD_DIM:(h + 1) * HEAD_DIM] = acc.astype(o_ref.dtype)


def sb_prompt(q, k, v, b, t):
    m, hq = q.shape
    nhs = SB_HEADS_PER_STEP
    tq = _tile(t, 256, 16)
    nq = t // tq
    w = nhs * HEAD_DIM
    qrow = pl.BlockSpec((tq, w), lambda bi, h, i: (bi * nq + i, h))
    full = pl.BlockSpec((t, w), lambda bi, h, i: (bi, h))
    return pl.pallas_call(
        _sb_prompt_kernel, grid=(b, hq // w, nq),
        in_specs=[qrow, full, full], out_specs=qrow,
        out_shape=jax.ShapeDtypeStruct((m, hq), BF16),
        compiler_params=_cp("arbitrary", "arbitrary", "arbitrary"), name="sb_prompt")(q, k, v)


def _mem_kernel(h_ref, x_ref, wq_ref, mk_ref, mv_ref, wo_ref, gp_ref, gn_ref, xo_ref, ho_ref):
    q = _dot(h_ref[0], wq_ref[...])
    outs = []
    for hh in range(H_MEM):
        sl = slice(hh * HEAD_DIM, (hh + 1) * HEAD_DIM)
        s = _dot_nt(q[:, sl].astype(BF16), mk_ref[0, :, sl].astype(BF16)) * HEAD_DIM ** -0.5
        e = jnp.exp(s - jnp.max(s, axis=-1, keepdims=True))
        o = _dot(e.astype(BF16), mv_ref[0, :, sl].astype(BF16))
        outs.append(o / jnp.sum(e, axis=-1, keepdims=True))
    y = _dot(jnp.concatenate(outs, axis=1).astype(BF16), wo_ref[...])
    xn = x_ref[0] + _rms(y, gp_ref[...])
    xo_ref[0] = xn
    ho_ref[0] = _rms(xn, gn_ref[...]).astype(ho_ref.dtype)


def mem_sublayer(h3, x3, wq, mk, mv, wo, g_post, g_next):
    b, t, d = h3.shape
    tq = _tile(t, 256, 16)
    nm, hm = mk.shape[1], mk.shape[2]
    row = pl.BlockSpec((1, tq, d), lambda bi, i: (bi, i, 0))
    mem = pl.BlockSpec((1, nm, hm), lambda bi, i: (bi, 0, 0))
    vec = pl.BlockSpec((1, d), lambda bi, i: (0, 0))
    return pl.pallas_call(
        _mem_kernel, grid=(b, t // tq),
        in_specs=[row, row, pl.BlockSpec((d, hm), lambda bi, i: (0, 0)), mem, mem,
                  pl.BlockSpec((hm, d), lambda bi, i: (0, 0)), vec, vec],
        out_specs=[row, row],
        out_shape=[jax.ShapeDtypeStruct((b, t, d), F32), jax.ShapeDtypeStruct((b, t, d), BF16)],
        compiler_params=_cp("arbitrary", "arbitrary"), name="mem_sublayer")(
            h3, x3, wq, mk, mv, wo, g_post.reshape(1, d), g_next.reshape(1, d))


def _ep_swiglu(accs, er, orf):
    g, u = accs
    orf[0][...] = ((g * jax.nn.sigmoid(g)) * u).astype(orf[0].dtype)


def ffn(h, wg, wu, wd, li):
    act = matmul([h], [wg, wu], [[(0, 0)], [(0, 1)]], _ep_swiglu, [BF16], layer=li, name="ffn_gate_up")
    return matmul([act], [wd], [[(0, 0)]], _ep_plain, [F32], tm_pref=512, layer=li, name="ffn_down")


def _odd_weights(w_in, w_out, nh_c, nh_d):
    sizes = [nh_c * HEAD_DIM] + [HEAD_DIM] * 6 + [nh_c * 3] + [nh_d * HEAD_DIM] * 3
    offs = np.cumsum([0] + sizes)
    cut = lambda a, b_: w_in[:, offs[a]:offs[b_]].astype(BF16)
    w = {"qc": cut(0, 1), "kv6": cut(1, 7), "gc": cut(7, 8), "qd": cut(8, 9), "kd": cut(9, 10),
         "vd": cut(10, 11)}
    w["out_c"] = w_out[:nh_c * HEAD_DIM].astype(BF16)
    w["out_d"] = w_out[nh_c * HEAD_DIM:].astype(BF16)
    return w


def _odd_project(h, w, tabs):
    cos, sin, nb = tabs
    qc = proj_rope(h, w["qc"], tabs, BF16, name="proj_qc")
    kv6 = matmul([h], [w["kv6"]], [[(0, 0)]], functools.partial(_ep_split, rope_chunks=(0, 2, 4)),
                 [F32] * 6, tables=(cos, sin), n_tab_blocks=nb, split_out=6, name="proj_kv6")
    gc = proj(h, w["gc"], name="proj_gc")
    qd = proj(h, w["qd"], BF16, name="proj_qd")
    kd = proj(h, w["kd"], name="proj_kd")
    vd = proj(h, w["vd"], name="proj_vd")
    return qc, kv6, gc, qd, kd, vd


def odd_mixer_prompt(h, b, t, w, cwk, cwv):
    m, d = h.shape
    tabs = rope_tables(jnp.arange(t), b, _tile(m, 1024, 16))
    qc, (kc, vc, ks, vs, kw, vw), gc, qd, kd, vd = _odd_project(h, w, tabs)
    o_c = nsa_prompt(qc, gc, kc, vc, ks, vs, kw, vw, cwk, cwv, b, t)
    o_d = sb_prompt(qd, kd, vd, b, t)
    y = proj_out(o_c, o_d, w["out_c"], w["out_d"])
    nw = min(WINDOW, t)
    st1 = lambda a: a.reshape(b, t, 1, HEAD_DIM)
    sth = lambda a: a.reshape(b, t, -1, HEAD_DIM)
    return y, (st1(kc), st1(vc), st1(ks), st1(vs), st1(kw)[:, t - nw:], st1(vw)[:, t - nw:],
               sth(kd), sth(vd))


def _even_weights(w_in, conv_w, w_out):
    c = conv_w.shape[1]
    hq = w_out.shape[0] - c
    sizes = [c, c, c, hq, KV_B * HEAD_DIM, KV_B * HEAD_DIM, H_IDX * D_IDX, D_IDX, H_IDX]
    offs = np.cumsum([0] + sizes)
    names = ["bg", "cg", "xa", "q", "k", "v", "qi", "ki", "wi"]
    w = {n: w_in[:, offs[j]:offs[j + 1]].astype(BF16) for j, n in enumerate(names)}
    w["conv_w"] = conv_w
    w["out_a"] = w_out[:c].astype(BF16)
    w["out_b"] = w_out[c:].astype(BF16)
    return w


def _even_project(h, w, tabs):
    q = proj_rope(h, w["q"], tabs, BF16, name="proj_q")
    k = proj_rope(h, w["k"], tabs, F32, name="proj_k")
    v = proj(h, w["v"], name="proj_v")
    qi = proj_rope(h, w["qi"], tabs, BF16, name="proj_qi")
    ki = proj_rope(h, w["ki"], tabs, F32, name="proj_ki")
    wi = proj(h, w["wi"], name="proj_wi")
    return q, k, v, qi, ki, wi


def even_mixer_prompt(h, b, t, w):
    m, d = h.shape
    tabs = rope_tables(jnp.arange(t), b, _tile(m, 1024, 16))
    init = jnp.zeros((b, CONV_K - 1, w["bg"].shape[1]), F32)
    ya, conv_state = conv_mixer(h.reshape(b, t, d), w["bg"], w["cg"], w["xa"], w["conv_w"], init)
    q, k, v, qi, ki, wi = _even_project(h, w, tabs)
    ob = dsa_prompt(qi, wi, ki, q, k, v, b, t)
    y = proj_out(ya.reshape(m, -1), ob, w["out_a"], w["out_b"])
    return y, (conv_state, k.reshape(b, t, KV_B, HEAD_DIM), v.reshape(b, t, KV_B, HEAD_DIM),
               ki.reshape(b, t, D_IDX))


def _paged_call(kern, grid, in_specs, out_specs, out_shape, scratch, name, page_table, args):
    gs = pltpu.PrefetchScalarGridSpec(num_scalar_prefetch=1, grid=grid, in_specs=in_specs,
                                      out_specs=out_specs, scratch_shapes=scratch)
    return pl.pallas_call(kern, grid_spec=gs, out_shape=out_shape,
                          compiler_params=_cp("arbitrary", "arbitrary"), name=name)(page_table, *args)


def _pad_rows(a, rows):
    return jnp.pad(a, ((0, 0), (0, rows - a.shape[1])) + ((0, 0),) * (a.ndim - 2))


def _page_specs(block_tail, layer, g_pages, page_of):
    zeros = (0,) * len(block_tail)

    def spec(j):
        return pl.BlockSpec((1, 1) + block_tail,
                            lambda bi, p, pt: (layer, page_of(bi, p, pt, j)) + zeros)
    return [spec(j) for j in range(g_pages)]


def _forward_pages(g_pages):
    return lambda bi, p, pt, j: pt[bi, p * g_pages + j]


def _rows_ht(a, b, t, nh):
    w = a.shape[1] // nh
    return a.reshape(b, t, nh, w).transpose(0, 2, 1, 3).reshape(b, nh * t, w)


def _rows_th(a, b, t, nh):
    w = a.shape[2]
    return a.reshape(b, nh, t, w).transpose(0, 2, 1, 3).reshape(b * t, nh * w)


def _dsa_scores_kernel(pt_ref, qi_ref, wi_ref, new_ref, *rest, g_pages, t):
    pools, (o_ref, onew_ref) = rest[:g_pages], rest[g_pages:]
    qi, wi = qi_ref[0], wi_ref[0]

    def scores(kb):
        s = _dot_nt(qi, kb) * D_IDX ** -0.5
        s = jnp.maximum(s, 0.0) * wi
        return jnp.sum(s.reshape(H_IDX, t, kb.shape[0]), axis=0) * H_IDX ** -0.5

    o_ref[0] = scores(jnp.concatenate([r[0, 0] for r in pools], axis=0).astype(BF16))

    @pl.when(pl.program_id(1) == 0)
    def _():
        onew_ref[0] = scores(new_ref[0].astype(BF16))


def dsa_sample_scores(qi_r, wi_r, ki_new, pool, e, page_table, t):
    b, n_pages = page_table.shape
    g = _tile(n_pages, 8, 1)
    kern = functools.partial(_dsa_scores_kernel, g_pages=g, t=t)
    per_b = lambda r, w: pl.BlockSpec((1, r, w), lambda bi, p, pt: (bi, 0, 0))
    past, new = _paged_call(
        kern, (b, n_pages // g),
        [per_b(H_IDX * t, D_IDX), per_b(H_IDX * t, 1), per_b(PAGE_SIZE, D_IDX)]
        + _page_specs((PAGE_SIZE, D_IDX), e, g, _forward_pages(g)),
        [pl.BlockSpec((1, t, g * PAGE_SIZE), lambda bi, p, pt: (bi, 0, p)), per_b(t, PAGE_SIZE)],
        [jax.ShapeDtypeStruct((b, t, n_pages * PAGE_SIZE), F32),
         jax.ShapeDtypeStruct((b, t, PAGE_SIZE), F32)], [], "dsa_sample_scores",
        page_table, (qi_r, wi_r, _pad_rows(ki_new, PAGE_SIZE)) + (pool,) * g)
    return jnp.concatenate([past, new], axis=-1)


def _dsa_topk_kernel(s_ref, o_ref, mask_ref, *, past, topk):
    score = s_ref[0]
    kpos = lax.broadcasted_iota(jnp.int32, score.shape, 1)
    qpos = past + lax.broadcasted_iota(jnp.int32, score.shape, 0)
    key = jnp.where(kpos <= qpos, _sort_key(score), jnp.int32(INT_MIN))
    _topk_mask(key, topk, mask_ref)
    o_ref[0] = mask_ref[...]


def dsa_sample_topk(score, past, topk):
    b, t, nk = score.shape
    blk = pl.BlockSpec((1, t, nk), lambda bi: (bi, 0, 0))
    return pl.pallas_call(
        functools.partial(_dsa_topk_kernel, past=past, topk=topk), grid=(b,),
        in_specs=[blk], out_specs=blk, out_shape=jax.ShapeDtypeStruct((b, t, nk), F32),
        scratch_shapes=[pltpu.VMEM((t, nk), F32)],
        compiler_params=_cp("arbitrary"), name="dsa_sample_topk")(score)


def _online_softmax_step(s, valid, v, m_ref, l_ref, acc_ref):
    m_old = m_ref[...]
    m_new = jnp.maximum(m_old, jnp.max(jnp.where(valid, s, NEG), axis=-1, keepdims=True))
    alpha = jnp.exp(m_old - m_new)
    e = jnp.where(valid, jnp.exp(s - m_new), 0.0)
    l_ref[...] = alpha * l_ref[...] + jnp.sum(e, axis=-1, keepdims=True)
    acc_ref[...] = alpha * acc_ref[...] + _dot(e.astype(BF16), v)
    m_ref[...] = m_new


def _init_softmax_state(m_ref, l_ref, acc_ref):
    m_ref[...] = jnp.full(m_ref.shape, NEG, F32)
    l_ref[...] = jnp.zeros(l_ref.shape, F32)
    acc_ref[...] = jnp.zeros(acc_ref.shape, F32)


def _dsa_sample_attn_kernel(pt_ref, q_ref, mask_ref, masknew_ref, knew_ref, vnew_ref, *rest,
                            g_pages, rep):
    kps, vps = rest[:g_pages], rest[g_pages:2 * g_pages]
    o_ref, m_ref, l_ref, acc_ref = rest[2 * g_pages:]
    p = pl.program_id(1)

    @pl.when(p == 0)
    def _():
        _init_softmax_state(m_ref, l_ref, acc_ref)

    def update(keys, vals, valid_t):
        valid = jnp.concatenate([valid_t] * rep, axis=0)
        for g in range(KV_B):
            s = _dot_nt(q_ref[0, g], keys(g).astype(BF16)) * HEAD_DIM ** -0.5
            _online_softmax_step(s, valid, vals(g).astype(BF16), m_ref.at[g], l_ref.at[g], acc_ref.at[g])

    update(lambda g: jnp.concatenate([r[0, 0, :, g, :] for r in kps], axis=0),
           lambda g: jnp.concatenate([r[0, 0, :, g, :] for r in vps], axis=0), mask_ref[0] > 0.5)

    @pl.when(p == pl.num_programs(1) - 1)
    def _():
        update(lambda g: knew_ref[0, :, g, :], lambda g: vnew_ref[0, :, g, :], masknew_ref[0] > 0.5)
        o_ref[0] = acc_ref[...] / l_ref[...]


def dsa_sample_attn(q_g, mask, kpool, vpool, k_new, v_new, e, page_table):
    b, n_pages = page_table.shape
    t = mask.shape[1]
    rows = q_g.shape[2]
    g = _tile(n_pages, 8, 1)
    kern = functools.partial(_dsa_sample_attn_kernel, g_pages=g, rep=rows // t)
    qspec = pl.BlockSpec((1, KV_B, rows, HEAD_DIM), lambda bi, p, pt: (bi, 0, 0, 0))
    new = pl.BlockSpec((1, PAGE_SIZE, KV_B, HEAD_DIM), lambda bi, p, pt: (bi, 0, 0, 0))
    pools = _page_specs((PAGE_SIZE, KV_B, HEAD_DIM), e, g, _forward_pages(g))
    return _paged_call(
        kern, (b, n_pages // g),
        [qspec, pl.BlockSpec((1, t, g * PAGE_SIZE), lambda bi, p, pt: (bi, 0, p)),
         pl.BlockSpec((1, t, PAGE_SIZE), lambda bi, p, pt: (bi, 0, n_pages)), new, new] + pools + pools,
        qspec, jax.ShapeDtypeStruct((b, KV_B, rows, HEAD_DIM), F32),
        [pltpu.VMEM((KV_B, rows, 1), F32), pltpu.VMEM((KV_B, rows, 1), F32),
         pltpu.VMEM((KV_B, rows, HEAD_DIM), F32)],
        "dsa_sample_attn", page_table,
        (q_g, mask, mask, _pad_rows(k_new, PAGE_SIZE), _pad_rows(v_new, PAGE_SIZE))
        + (kpool,) * g + (vpool,) * g)


def even_mixer_sample(h, b, t, w, e, page_table, state_conv, c_k, c_v, c_kidx):
    m, d = h.shape
    n_pages = page_table.shape[1]
    past = n_pages * PAGE_SIZE
    tabs = rope_tables(past + jnp.arange(t), b, _tile(m, 1024, 16))
    ya, conv_state = conv_mixer(h.reshape(b, t, d), w["bg"], w["cg"], w["xa"], w["conv_w"], state_conv[e])
    q, k, v, qi, ki, wi = _even_project(h, w, tabs)
    score = dsa_sample_scores(_rows_ht(qi, b, t, H_IDX), _rows_ht(wi, b, t, H_IDX),
                              ki.reshape(b, t, D_IDX), c_kidx, e, page_table, t)
    mask = dsa_sample_topk(score, past, min(TOPK_MAX, (past + t) // 4))
    nh = q.shape[1] // HEAD_DIM
    rep = nh // KV_B
    q_g = _rows_ht(q, b, t, nh).reshape(b, KV_B, rep * t, HEAD_DIM)
    o = dsa_sample_attn(q_g, mask, c_k, c_v, k.reshape(b, t, KV_B, HEAD_DIM),
                        v.reshape(b, t, KV_B, HEAD_DIM), e, page_table)
    ob = _rows_th(o.reshape(b, nh * t, HEAD_DIM), b, t, nh).astype(BF16)
    y = proj_out(ya.reshape(m, -1), ob, w["out_a"], w["out_b"])
    return y, (conv_state, k.reshape(b, t, KV_B, HEAD_DIM), v.reshape(b, t, KV_B, HEAD_DIM),
               ki.reshape(b, t, D_IDX))


def _compress_pages_kernel(pt_ref, cwk_ref, cwv_ref, *rest, g_pages):
    kcs, vcs = rest[:g_pages], rest[g_pages:2 * g_pages]
    ko_ref, vo_ref = rest[2 * g_pages:]
    nb = PAGE_SIZE // CMP_BLOCK
    cwk, cwv = cwk_ref[...][None], cwv_ref[...][None]
    for j in range(g_pages):
        ko_ref[0, j] = jnp.sum(kcs[j][0, 0].reshape(nb, CMP_BLOCK, HEAD_DIM) * cwk, axis=1)
        vo_ref[0, j] = jnp.sum(vcs[j][0, 0].reshape(nb, CMP_BLOCK, HEAD_DIM) * cwv, axis=1)


def nsa_compress_pages(kpool, vpool, cwk, cwv, o, page_table):
    b, n_pages = page_table.shape
    nb = PAGE_SIZE // CMP_BLOCK
    g = _tile(n_pages, 8, 1)
    pools = _page_specs((PAGE_SIZE, HEAD_DIM), o, g, _forward_pages(g))
    cw = pl.BlockSpec((CMP_BLOCK, HEAD_DIM), lambda bi, p, pt: (0, 0))
    out = pl.BlockSpec((1, g, nb, HEAD_DIM), lambda bi, p, pt: (bi, p, 0, 0))
    shp = jax.ShapeDtypeStruct((b, n_pages, nb, HEAD_DIM), F32)
    kc, vc = _paged_call(functools.partial(_compress_pages_kernel, g_pages=g), (b, n_pages // g),
                         [cw, cw] + pools + pools, [out, out], [shp, shp], [], "nsa_compress_pages",
                         page_table, (cwk, cwv) + (kpool,) * g + (vpool,) * g)
    return kc.reshape(b, n_pages * nb, HEAD_DIM), vc.reshape(b, n_pages * nb, HEAD_DIM)


def _nsa_sample_a_kernel(q_ref, kcmp_ref, vcmp_ref, kw_ref, vw_ref, ocmp_ref, owin_ref, sel_ref,
                         *, past, t, nwin, n_selblk):
    q = q_ref[0]
    rows = q.shape[0]
    nh = rows // t
    scale = HEAD_DIM ** -0.5
    nb = kcmp_ref.shape[1]
    col = lax.broadcasted_iota(jnp.int32, (rows, nb), 1)
    pos = past + lax.broadcasted_iota(jnp.int32, (rows, nb), 0) % t
    s = _dot_nt(q, kcmp_ref[0].astype(BF16)) * scale
    p = _masked_softmax(s, (col + 1) * CMP_BLOCK - 1 <= pos)
    ocmp_ref[0] = _dot(p.astype(BF16), vcmp_ref[0].astype(BF16))
    imp = jnp.sum(p.reshape(nh, t, nb), axis=0)
    ncol = sel_ref.shape[2]
    imp = jnp.concatenate([imp, jnp.zeros((t, ncol - nb), F32)], axis=1)
    col_s = lax.broadcasted_iota(jnp.int32, (t, ncol), 1)
    pos_s = past + lax.broadcasted_iota(jnp.int32, (t, ncol), 0)
    sel_ref[0] = _select_blocks(_pair_sums(imp, col_s), col_s, pos_s, n_selblk)
    nwp = kw_ref.shape[1]
    colw = lax.broadcasted_iota(jnp.int32, (rows, nwp), 1)
    posw = past + lax.broadcasted_iota(jnp.int32, (rows, nwp), 0) % t
    kwpos = past + t - nwin + colw
    valid = (kwpos <= posw) & (kwpos > posw - WINDOW) & (colw < nwin)
    sw = _dot_nt(q, kw_ref[0].astype(BF16)) * scale
    pw = _masked_softmax(sw, valid)
    owin_ref[0] = _dot(pw.astype(BF16), vw_ref[0].astype(BF16))


def nsa_sample_a(q_r, kcmp, vcmp, kw_pad, vw_pad, past, t, nwin):
    b, rows, _ = q_r.shape
    nb = kcmp.shape[1]
    n_selblk = -(-(past + t) // SEL_BLOCK)
    ncol = -(-2 * n_selblk // LANES) * LANES
    assert ncol > nb >= 2 * n_selblk - 2 and nb % LANES == 0
    per_b = lambda r, w: pl.BlockSpec((1, r, w), lambda bi: (bi, 0, 0))
    kern = functools.partial(_nsa_sample_a_kernel, past=past, t=t, nwin=nwin, n_selblk=n_selblk)
    return pl.pallas_call(
        kern, grid=(b,),
        in_specs=[per_b(rows, HEAD_DIM), per_b(nb, HEAD_DIM), per_b(nb, HEAD_DIM),
                  per_b(kw_pad.shape[1], HEAD_DIM), per_b(kw_pad.shape[1], HEAD_DIM)],
        out_specs=[per_b(rows, HEAD_DIM), per_b(rows, HEAD_DIM), per_b(t, ncol)],
        out_shape=[jax.ShapeDtypeStruct((b, rows, HEAD_DIM), F32)] * 2
        + [jax.ShapeDtypeStruct((b, t, ncol), F32)],
        compiler_params=_cp("arbitrary"), name="nsa_sample_a")(q_r, kcmp, vcmp, kw_pad, vw_pad)


def _nsa_sample_b_kernel(pt_ref, q_ref, tok_ref, toknew_ref, knew_ref, vnew_ref,
                         ocmp_ref, owin_ref, gate_ref, *rest, g_pages, n_pages, t):
    kps, vps = rest[:g_pages], rest[g_pages:2 * g_pages]
    o_ref, m_ref, l_ref, acc_ref = rest[2 * g_pages:]
    p = pl.program_id(1)

    @pl.when(p == 0)
    def _():
        _init_softmax_state(m_ref, l_ref, acc_ref)

    q = q_ref[0]
    rows = q.shape[0]

    def update(keys, vals, tok, first_key):
        n = keys.shape[0]
        kpos = first_key + lax.broadcasted_iota(jnp.int32, (rows, n), 1)
        qpos = n_pages * PAGE_SIZE + lax.broadcasted_iota(jnp.int32, (rows, n), 0) % t
        valid = jnp.concatenate([tok > 0.5] * (rows // t), axis=0) & (kpos <= qpos)
        s = _dot_nt(q, keys.astype(BF16)) * HEAD_DIM ** -0.5
        _online_softmax_step(s, valid, vals.astype(BF16), m_ref, l_ref, acc_ref)

    update(jnp.concatenate([r[0, 0] for r in kps], axis=0),
           jnp.concatenate([r[0, 0] for r in vps], axis=0), tok_ref[0], p * (g_pages * PAGE_SIZE))

    @pl.when(p == pl.num_programs(1) - 1)
    def _():
        update(knew_ref[0], vnew_ref[0], toknew_ref[0], n_pages * PAGE_SIZE)
        g = jax.nn.sigmoid(gate_ref[0])
        o_ref[0] = (g[:, 0:1] * ocmp_ref[0] + g[:, 1:2] * (acc_ref[...] / l_ref[...])
                    + g[:, 2:3] * owin_ref[0])


def nsa_sample_b(q_r, tok, kpool, vpool, ks_new, vs_new, o_cmp, o_win, gate_r, o, page_table):
    b, n_pages = page_table.shape
    rows = q_r.shape[1]
    t = tok.shape[1]
    g = _tile(n_pages, 8, 1)
    per_b = lambda r, w: pl.BlockSpec((1, r, w), lambda bi, p, pt: (bi, 0, 0))
    pools = _page_specs((PAGE_SIZE, HEAD_DIM), o, g, _forward_pages(g))
    kern = functools.partial(_nsa_sample_b_kernel, g_pages=g, n_pages=n_pages, t=t)
    return _paged_call(
        kern, (b, n_pages // g),
        [per_b(rows, HEAD_DIM), pl.BlockSpec((1, t, g * PAGE_SIZE), lambda bi, p, pt: (bi, 0, p)),
         pl.BlockSpec((1, t, PAGE_SIZE), lambda bi, p, pt: (bi, 0, n_pages)),
         per_b(PAGE_SIZE, HEAD_DIM), per_b(PAGE_SIZE, HEAD_DIM),
         per_b(rows, HEAD_DIM), per_b(rows, HEAD_DIM), per_b(rows, 3)] + pools + pools,
        per_b(rows, HEAD_DIM), jax.ShapeDtypeStruct((b, rows, HEAD_DIM), F32),
        [pltpu.VMEM((rows, 1), F32), pltpu.VMEM((rows, 1), F32), pltpu.VMEM((rows, HEAD_DIM), F32)],
        "nsa_sample_b", page_table,
        (q_r, tok, tok, _pad_rows(ks_new, PAGE_SIZE), _pad_rows(vs_new, PAGE_SIZE),
         o_cmp, o_win, gate_r) + (kpool,) * g + (vpool,) * g)


def _sb_sample_kernel(pt_ref, qt_ref, knew_ref, vnew_ref, *rest, g_pages, t):
    kps, vps = rest[:g_pages], rest[g_pages:2 * g_pages]
    o_ref, carry_ref, acc_ref, z_ref = rest[2 * g_pages:]
    p = pl.program_id(1)
    n, nh = knew_ref.shape[1], knew_ref.shape[2]
    cols = nh * t

    @pl.when(p == 0)
    def _():
        carry_ref[...] = jnp.zeros(carry_ref.shape, F32)
        acc_ref[...] = jnp.zeros(acc_ref.shape, F32)

    iota = lambda shape, d: lax.broadcasted_iota(jnp.int32, shape, d)
    later = jnp.where(iota((n, n), 1) > iota((n, n), 0), 1.0, 0.0).astype(BF16)
    own_head = iota((nh, cols), 1) // t == iota((nh, cols), 0)
    spread = jnp.where(iota((n, n * nh), 1) // nh == iota((n, n * nh), 0), 1.0, 0.0).astype(BF16)
    own_rows = iota((cols, n * nh), 0) // t == iota((cols, n * nh), 1) % nh

    def log_weights(k3, slot, m):
        k2 = k3.reshape(n * nh, HEAD_DIM).astype(BF16)
        z_all = _dot(k2, qt_ref[0]).reshape(n, nh, cols)
        z_ref[slot] = jnp.sum(jnp.where(own_head[None], z_all, 0.0), axis=1)
        ls, lneg = _log_sigmoids(z_ref[slot] * HEAD_DIM ** -0.5)
        if m is not None:
            lneg = jnp.where(m, lneg, 0.0)
        hi, lo = _split_bf16(lneg)
        after = _dot(later, hi) + _dot(later, lo)
        return ls + after, after[0:1] + lneg[0:1]

    def weighted_values(v3, log_a, m):
        a = jnp.exp(log_a)
        if m is not None:
            a = jnp.where(m, a, 0.0)
        a2 = jnp.where(own_rows, _dot(a.T.astype(BF16), spread), 0.0)
        return _dot(a2.astype(BF16), v3.reshape(n * nh, HEAD_DIM).astype(BF16))

    @pl.when(p == 0)
    def _():
        m = iota((n, cols), 0) < iota((n, cols), 1) % t
        log_a, total = log_weights(knew_ref[0], g_pages, m)
        acc_ref[...] += weighted_values(vnew_ref[0], log_a, m)
        carry_ref[...] += total

    parts = [log_weights(kps[j][0, 0], j, None) for j in range(g_pages)]
    carry = carry_ref[...]
    acc = acc_ref[...]
    for j in range(g_pages):
        acc = acc + weighted_values(vps[j][0, 0], parts[j][0] + carry, None)
        carry = carry + parts[j][1]
    carry_ref[...] = carry
    acc_ref[...] = acc

    @pl.when(p == pl.num_programs(1) - 1)
    def _():
        o_ref[0] = acc_ref[...]


def sb_sample(qd, kd_new, vd_new, kpool, vpool, o, page_table, b, t):
    n_pages = page_table.shape[1]
    nh = qd.shape[1] // HEAD_DIM
    g = _tile(n_pages, 4, 1)
    qt = qd.reshape(b, t, nh, HEAD_DIM).transpose(0, 3, 2, 1).reshape(b, HEAD_DIM, nh * t)
    new = pl.BlockSpec((1, PAGE_SIZE, nh, HEAD_DIM), lambda bi, p, pt: (bi, 0, 0, 0))
    pools = _page_specs((PAGE_SIZE, nh, HEAD_DIM), o, g,
                        lambda bi, p, pt, j: pt[bi, n_pages - 1 - (p * g + j)])
    out = pl.BlockSpec((1, nh * t, HEAD_DIM), lambda bi, p, pt: (bi, 0, 0))
    return _paged_call(
        functools.partial(_sb_sample_kernel, g_pages=g, t=t), (b, n_pages // g),
        [pl.BlockSpec((1, HEAD_DIM, nh * t), lambda bi, p, pt: (bi, 0, 0)), new, new]
        + pools + pools,
        out, jax.ShapeDtypeStruct((b, nh * t, HEAD_DIM), F32),
        [pltpu.VMEM((1, nh * t), F32), pltpu.VMEM((nh * t, HEAD_DIM), F32),
         pltpu.VMEM((g + 1, PAGE_SIZE, nh * t), F32)],
        "sb_sample", page_table,
        (qt, _pad_rows(kd_new, PAGE_SIZE), _pad_rows(vd_new, PAGE_SIZE)) + (kpool,) * g + (vpool,) * g)


def odd_mixer_sample(h, b, t, w, cwk, cwv, o, page_table, c_kc, c_vc, c_ks, c_vs, c_kw, c_vw,
                     c_kd, c_vd):
    m, d = h.shape
    n_pages = page_table.shape[1]
    past = n_pages * PAGE_SIZE
    assert past % CMP_BLOCK == 0 and t < CMP_BLOCK
    tabs = rope_tables(past + jnp.arange(t), b, _tile(m, 1024, 16))
    qc, (kc, vc, ks, vs, kw, vw), gc, qd, kd, vd = _odd_project(h, w, tabs)
    nh = qc.shape[1] // HEAD_DIM
    n_pool = c_kc.shape[1]
    pool1 = lambda a: a.reshape(-1, n_pool, PAGE_SIZE, HEAD_DIM)
    seq = lambda a: a.reshape(b, t, -1)
    kcmp, vcmp = nsa_compress_pages(pool1(c_kc), pool1(c_vc), cwk, cwv, o, page_table)
    wb = c_kw.shape[2]
    kw_all = jnp.concatenate([c_kw[o].reshape(b, wb, HEAD_DIM), seq(kw)], axis=1)
    vw_all = jnp.concatenate([c_vw[o].reshape(b, wb, HEAD_DIM), seq(vw)], axis=1)
    nwp = -(-(wb + t) // LANES) * LANES
    q_r = _rows_ht(qc, b, t, nh)
    o_cmp, o_win, sel = nsa_sample_a(q_r, kcmp, vcmp, _pad_rows(kw_all, nwp), _pad_rows(vw_all, nwp),
                                     past, t, wb + t)
    n_selblk = -(-(past + t) // SEL_BLOCK)
    tok = jnp.repeat(sel[:, :, 0:2 * n_selblk:2], SEL_BLOCK, axis=-1)
    tok = jnp.pad(tok, ((0, 0), (0, 0), (0, (n_pages + 1) * PAGE_SIZE - tok.shape[-1])))
    gate_r = _rows_ht(gc, b, t, nh)
    o_c = nsa_sample_b(q_r, tok, pool1(c_ks), pool1(c_vs), seq(ks), seq(vs), o_cmp, o_win, gate_r,
                       o, page_table)
    nh_d = kd.shape[1] // HEAD_DIM
    heads = lambda a: a.reshape(b, t, nh_d, HEAD_DIM)
    o_d = sb_sample(qd, heads(kd), heads(vd), c_kd, c_vd, o, page_table, b, t)
    y = proj_out(_rows_th(o_c, b, t, nh).astype(BF16), _rows_th(o_d, b, t, nh_d).astype(BF16),
                 w["out_c"], w["out_d"])
    st1 = lambda a: a.reshape(b, t, 1, HEAD_DIM)
    return y, (st1(kc), st1(vc), st1(ks), st1(vs), kw_all[:, t:].reshape(b, wb, 1, HEAD_DIM),
               vw_all[:, t:].reshape(b, wb, 1, HEAD_DIM), kd.reshape(b, t, -1, HEAD_DIM),
               vd.reshape(b, t, -1, HEAD_DIM))


def kernel(x_prompt, x_sample, state_conv, cache_dsa_k, cache_dsa_v, cache_dsa_kidx, cache_nsa_kc, cache_nsa_vc, cache_nsa_ks, cache_nsa_vs, cache_nsa_kw, cache_nsa_vw, cache_sb_k, cache_sb_v, cache_mem_k, cache_mem_v, page_table, mem_prompt, norm_pre, norm_post, norm_mem, w_in_even, conv_w, w_out_even, w_in_odd, cmp_wk, cmp_wv, w_out_odd, w_mq, w_mk, w_mv, w_mo, w_gate, w_up, w_down):
    bp, tp, d = x_prompt.shape
    bs, ts, _ = x_sample.shape
    depth = norm_pre.shape[0]
    n_mem = mem_prompt.shape[1]
    hm = w_mq.shape[2]
    nh_c = nh_d = w_out_odd.shape[1] // 2 // HEAD_DIM
    xp = x_prompt.reshape(bp * tp, d)
    xs = x_sample.reshape(bs * ts, d)
    hp = norm_cast(xp, norm_pre[0, 0])
    hs = norm_cast(xs, norm_pre[0, 0])
    ev_p, ev_s, od_p, od_s, mem_p = [], [], [], [], []
    wg, wu, wd = w_gate.astype(BF16), w_up.astype(BF16), w_down.astype(BF16)
    for li in range(depth):
        g_pre, g_post = norm_pre[li], norm_post[li]
        if li % 2 == 0:
            e = li // 2
            w = _even_weights(w_in_even[e], conv_w[e], w_out_even[e])
            mp, stp = even_mixer_prompt(hp, bp, tp, w)
            ms, sts = even_mixer_sample(hs, bs, ts, w, e, page_table, state_conv,
                                        cache_dsa_k, cache_dsa_v, cache_dsa_kidx)
            ev_p.append(stp)
            ev_s.append(sts)
        else:
            o = li // 2
            w = _odd_weights(w_in_odd[o], w_out_odd[o], nh_c, nh_d)
            mp, stp = odd_mixer_prompt(hp, bp, tp, w, cmp_wk[o], cmp_wv[o])
            ms, sts = odd_mixer_sample(hs, bs, ts, w, cmp_wk[o], cmp_wv[o], o, page_table,
                                       cache_nsa_kc, cache_nsa_vc, cache_nsa_ks, cache_nsa_vs,
                                       cache_nsa_kw, cache_nsa_vw, cache_sb_k, cache_sb_v)
            od_p.append(stp)
            od_s.append(sts)
        xp, hp = resid_norm(xp, mp, g_post[0], g_pre[1])
        xs, hs = resid_norm(xs, ms, g_post[0], g_pre[1])
        wq, wo = w_mq[li].astype(BF16), w_mo[li].astype(BF16)
        hmem = norm_cast(mem_prompt.reshape(bp * n_mem, d), norm_mem[li])
        mkp = proj(hmem, w_mk[li].astype(BF16), name="proj_mk")
        mvp = proj(hmem, w_mv[li].astype(BF16), name="proj_mv")
        mem_p.append((mkp.reshape(bp, n_mem, H_MEM, HEAD_DIM), mvp.reshape(bp, n_mem, H_MEM, HEAD_DIM)))
        xp, hp = mem_sublayer(hp.reshape(bp, tp, d), xp.reshape(bp, tp, d), wq, mkp.reshape(bp, n_mem, hm),
                              mvp.reshape(bp, n_mem, hm), wo, g_post[1], g_pre[2])
        xs, hs = mem_sublayer(hs.reshape(bs, ts, d), xs.reshape(bs, ts, d), wq,
                              cache_mem_k[li].reshape(bs, n_mem, hm),
                              cache_mem_v[li].reshape(bs, n_mem, hm), wo, g_post[1], g_pre[2])
        xp, hp = xp.reshape(bp * tp, d), hp.reshape(bp * tp, d)
        xs, hs = xs.reshape(bs * ts, d), hs.reshape(bs * ts, d)
        g_next = norm_pre[li + 1, 0] if li + 1 < depth else None
        xp, hp = resid_norm(xp, ffn(hp, wg, wu, wd, li), g_post[2], g_next)
        xs, hs = resid_norm(xs, ffn(hs, wg, wu, wd, li), g_post[2], g_next)
    stack = lambda lst, j: jnp.stack([s[j] for s in lst])
    return ((xp.reshape(bp, tp, d), xs.reshape(bs, ts, d))
            + tuple(stack(ev_p, j) for j in range(4)) + tuple(stack(od_p, j) for j in range(8))
            + (stack(mem_p, 0), stack(mem_p, 1))
            + tuple(stack(ev_s, j) for j in range(4)) + tuple(stack(od_s, j) for j in range(8)))
```

```python
import functools

import numpy as np
import jax
import jax.numpy as jnp
from jax import lax
from jax.experimental import pallas as pl
from jax.experimental.pallas import tpu as pltpu

F32 = jnp.float32
BF16 = jnp.bfloat16

HEAD_DIM = 128
PAGE_SIZE = 128
CONV_K = 3
KV_B = 4
H_IDX = 16
D_IDX = 128
TOPK_MAX = 256
CMP_BLOCK = 32
SEL_BLOCK = 64
N_SEL = 16
WINDOW = 512
FORCE_BONUS = 1.0e6
H_MEM = 4
ROPE_THETA = 10000.0
EPS = 1e-6
NEG = -1e30
LANES = 128
VMEM_LIMIT = 56 * 1024 * 1024


def _cp(*sem):
    return pltpu.CompilerParams(dimension_semantics=sem, vmem_limit_bytes=VMEM_LIMIT)


def _tile(n, pref, mult):
    t = (min(pref, n) // mult) * mult
    while t >= mult:
        if n % t == 0:
            return t
        t -= mult
    return n


def _dot(a, b):
    return jnp.dot(a, b, preferred_element_type=F32)


def _dot_nt(a, b):
    return lax.dot_general(a, b, (((1,), (1,)), ((), ())), preferred_element_type=F32)


def _rms(x, g):
    return x * lax.rsqrt(jnp.mean(x * x, axis=-1, keepdims=True) + EPS) * g


def _rope(y, cos, sin):
    return y * cos + pltpu.roll(y, HEAD_DIM // 2, 1) * sin


def _norm_cast_kernel(x_ref, g_ref, o_ref):
    o_ref[...] = _rms(x_ref[...], g_ref[...]).astype(o_ref.dtype)


def norm_cast(x, g):
    m, d = x.shape
    tm = _tile(m, 256, 16)
    row = pl.BlockSpec((tm, d), lambda i: (i, 0))
    return pl.pallas_call(
        _norm_cast_kernel, grid=(m // tm,),
        in_specs=[row, pl.BlockSpec((1, d), lambda i: (0, 0))],
        out_specs=row, out_shape=jax.ShapeDtypeStruct((m, d), BF16),
        compiler_params=_cp("arbitrary"), name="norm_cast")(x, g.reshape(1, d))


def _resid_norm_kernel(x_ref, y_ref, gp_ref, gn_ref, xo_ref, h_ref):
    xn = x_ref[...] + _rms(y_ref[...], gp_ref[...])
    xo_ref[...] = xn
    h_ref[...] = _rms(xn, gn_ref[...]).astype(h_ref.dtype)


def _resid_kernel(x_ref, y_ref, gp_ref, xo_ref):
    xo_ref[...] = x_ref[...] + _rms(y_ref[...], gp_ref[...])


def resid_norm(x, y, g_post, g_next):
    m, d = x.shape
    tm = _tile(m, 256, 16)
    row = pl.BlockSpec((tm, d), lambda i: (i, 0))
    vec = pl.BlockSpec((1, d), lambda i: (0, 0))
    if g_next is None:
        return pl.pallas_call(
            _resid_kernel, grid=(m // tm,), in_specs=[row, row, vec], out_specs=row,
            out_shape=jax.ShapeDtypeStruct((m, d), F32),
            compiler_params=_cp("arbitrary"), name="resid")(x, y, g_post.reshape(1, d)), None
    return pl.pallas_call(
        _resid_norm_kernel, grid=(m // tm,), in_specs=[row, row, vec, vec],
        out_specs=[row, row],
        out_shape=[jax.ShapeDtypeStruct((m, d), F32), jax.ShapeDtypeStruct((m, d), BF16)],
        compiler_params=_cp("arbitrary"), name="resid_norm")(
            x, y, g_post.reshape(1, d), g_next.reshape(1, d))


def _mm_kernel(*refs, nx, nw, ne, groups, epilogue):
    xr, wr = refs[:nx], refs[nx:nx + nw]
    er, orf = refs[nx + nw:nx + nw + ne], refs[nx + nw + ne:]
    accs = []
    for grp in groups:
        acc = None
        for xi, wi in grp:
            d = _dot(xr[xi][...], wr[wi][...])
            acc = d if acc is None else acc + d
        accs.append(acc)
    epilogue(accs, er, orf)


def _ep_plain(accs, er, orf):
    orf[0][...] = accs[0].astype(orf[0].dtype)


def _ep_rope(accs, er, orf):
    cos, sin = er[0][...], er[1][...]
    y = accs[0]
    for c in range(y.shape[1] // HEAD_DIM):
        sl = slice(c * HEAD_DIM, (c + 1) * HEAD_DIM)
        orf[0][:, sl] = _rope(y[:, sl], cos, sin).astype(orf[0].dtype)


def _ep_split(accs, er, orf, *, rope_chunks):
    cos, sin = er[0][...], er[1][...]
    y = accs[0]
    for c in range(len(orf)):
        yc = y[:, c * HEAD_DIM:(c + 1) * HEAD_DIM]
        if c in rope_chunks:
            yc = _rope(yc, cos, sin)
        orf[c][...] = yc.astype(orf[c].dtype)


def matmul(xs, ws, groups, epilogue, out_dtypes, *, tables=None, n_tab_blocks=1,
           tm_pref=1024, tn_pref=512, split_out=0, layer=None, name="matmul"):
    m = xs[0].shape[0]
    n = ws[0].shape[-1]
    if tables is not None:
        tm = tables[0].shape[0] // n_tab_blocks
    else:
        tm = _tile(m, tm_pref, 16)
    tn = n if (split_out or n % LANES) else _tile(n, tn_pref, LANES)
    in_specs = [pl.BlockSpec((tm, x.shape[1]), lambda i, j: (i, 0)) for x in xs]
    if layer is None:
        in_specs += [pl.BlockSpec((w.shape[0], tn), lambda i, j: (0, j)) for w in ws]
    else:
        in_specs += [pl.BlockSpec((None, w.shape[1], tn), lambda i, j: (layer, 0, j)) for w in ws]
    extras = []
    if tables is not None:
        nb = n_tab_blocks
        in_specs += [pl.BlockSpec((tm, HEAD_DIM), lambda i, j: (i % nb, 0))] * 2
        extras = list(tables)
    if split_out:
        out_specs = [pl.BlockSpec((tm, HEAD_DIM), lambda i, j: (i, 0))] * split_out
        out_shape = [jax.ShapeDtypeStruct((m, HEAD_DIM), dt) for dt in out_dtypes]
    else:
        out_specs = [pl.BlockSpec((tm, tn), lambda i, j: (i, j))]
        out_shape = [jax.ShapeDtypeStruct((m, n), out_dtypes[0])]
    kern = functools.partial(_mm_kernel, nx=len(xs), nw=len(ws), ne=len(extras),
                             groups=groups, epilogue=epilogue)
    out = pl.pallas_call(
        kern, grid=(m // tm, n // tn), in_specs=in_specs, out_specs=out_specs,
        out_shape=out_shape, compiler_params=_cp("arbitrary", "arbitrary"), name=name)(
            *xs, *ws, *extras)
    return out if split_out else out[0]


def proj(h, w, dtype=F32, name="proj"):
    return matmul([h], [w], [[(0, 0)]], _ep_plain, [dtype], name=name)


def proj_rope(h, w, tabs, dtype, name="proj_rope"):
    cos, sin, nb = tabs
    return matmul([h], [w], [[(0, 0)]], _ep_rope, [dtype], tables=(cos, sin),
                  n_tab_blocks=nb, name=name)


def proj_out(xa, xb, wa, wb, name="proj_out"):
    return matmul([xa, xb], [wa, wb], [[(0, 0), (1, 1)]], _ep_plain, [F32], name=name)


def rope_tables(pos, reps, tm):
    half = HEAD_DIM // 2
    inv = ROPE_THETA ** (-jnp.arange(half, dtype=F32) / half)
    ang = pos.astype(F32)[:, None] * inv[None, :]
    cos, sin = jnp.cos(ang), jnp.sin(ang)
    cos = jnp.concatenate([cos, cos], axis=-1)
    sin = jnp.concatenate([-sin, sin], axis=-1)
    t = pos.shape[0]
    if tm > t:
        cos, sin = jnp.tile(cos, (tm // t, 1)), jnp.tile(sin, (tm // t, 1))
        return cos, sin, 1
    return cos, sin, t // tm


def _conv_kernel(x_ref, wb_ref, wc_ref, wx_ref, cw_ref, init_ref, ya_ref, st_ref, carry_ref):
    i = pl.program_id(2)

    @pl.when(i == 0)
    def _():
        carry_ref[...] = init_ref[0]

    x = x_ref[0]
    bg = _dot(x, wb_ref[...])
    u = _dot(x, wc_ref[...]) * _dot(x, wx_ref[...])
    tm = u.shape[0]
    c = carry_ref[...]
    rows = lax.broadcasted_iota(jnp.int32, u.shape, 0)
    u1 = jnp.where(rows == 0, c[1:2], pltpu.roll(u, 1, 0))
    u2 = jnp.where(rows == 0, c[0:1], jnp.where(rows == 1, c[1:2], pltpu.roll(u, 2, 0)))
    cw = cw_ref[...]
    conv = cw[0:1] * u2 + cw[1:2] * u1 + cw[2:3] * u
    ya_ref[0] = (bg * conv).astype(ya_ref.dtype)
    new = u[tm - (CONV_K - 1):tm]
    carry_ref[...] = new
    st_ref[0] = new


def conv_mixer(h3, wb, wc, wx, cw, init):
    b, t, d = h3.shape
    c = wb.shape[1]
    tm = _tile(t, 1024, 16)
    tn = _tile(c, 512, LANES)
    wspec = pl.BlockSpec((d, tn), lambda j, bi, i: (0, j))
    return pl.pallas_call(
        _conv_kernel, grid=(c // tn, b, t // tm),
        in_specs=[pl.BlockSpec((1, tm, d), lambda j, bi, i: (bi, i, 0)), wspec, wspec, wspec,
                  pl.BlockSpec((CONV_K, tn), lambda j, bi, i: (0, j)),
                  pl.BlockSpec((1, CONV_K - 1, tn), lambda j, bi, i: (bi, 0, j))],
        out_specs=[pl.BlockSpec((1, tm, tn), lambda j, bi, i: (bi, i, j)),
                   pl.BlockSpec((1, CONV_K - 1, tn), lambda j, bi, i: (bi, 0, j))],
        out_shape=[jax.ShapeDtypeStruct((b, t, c), BF16),
                   jax.ShapeDtypeStruct((b, CONV_K - 1, c), F32)],
        scratch_shapes=[pltpu.VMEM((CONV_K - 1, tn), F32)],
        compiler_params=_cp("arbitrary", "arbitrary", "arbitrary"), name="conv_mixer")(
            h3, wb, wc, wx, cw, init)


def _stack_heads(q_ref, heads):
    return jnp.concatenate([q_ref[:, h * HEAD_DIM:(h + 1) * HEAD_DIM] for h in heads], axis=0)


LOG2E = 1.4426950408889634


def _with_ones_column(v):
    ones = jnp.where(lax.broadcasted_iota(jnp.int32, v.shape, 1) == 0, 1.0, 0.0).astype(v.dtype)
    return jnp.concatenate([v, ones], axis=1)


def _softmax_av(qk, scale, bias, v_ones):
    s = qk * (scale * LOG2E) + bias[None]
    e = jnp.exp2(s - jnp.max(s, axis=-1, keepdims=True))
    r, tq, n = e.shape
    o = _dot(e.reshape(r * tq, n).astype(BF16), v_ones)
    return o[:, :HEAD_DIM] / o[:, HEAD_DIM:HEAD_DIM + 1]


def _sort_key(x):
    bits = lax.bitcast_convert_type(x + 0.0, jnp.int32)
    return jnp.where(bits < 0, bits ^ jnp.int32(0x7FFFFFFF), bits)


INT_MIN = -2 ** 31


def _kth_largest_key(key, k):
    def body(it, othr):
        bit = lax.shift_left(jnp.int32(1), jnp.int32(31) - it)
        cand = othr | bit
        cnt = jnp.sum(jnp.where(key >= (cand ^ jnp.int32(INT_MIN)), 1.0, 0.0), axis=1, keepdims=True)
        return jnp.where(cnt >= k, cand, othr)
    othr = lax.fori_loop(0, 32, body, jnp.zeros((key.shape[0], 1), jnp.int32))
    return othr ^ jnp.int32(INT_MIN)


def _topk_mask(key, k, scratch_ref):
    thr = _kth_largest_key(key, k)
    live = key > jnp.int32(INT_MIN)
    ge = (key >= thr) & live
    scratch_ref[...] = jnp.where(ge, 1.0, 0.0)
    n_ge = jnp.sum(jnp.where(ge, 1.0, 0.0), axis=1, keepdims=True)

    @pl.when(jnp.max(n_ge) > k)
    def _():
        gt = key > thr
        eq = (key == thr) & live
        need = k - jnp.sum(jnp.where(gt, 1.0, 0.0), axis=1, keepdims=True)
        r = lax.broadcasted_iota(jnp.int32, (LANES, LANES), 0)
        c = lax.broadcasted_iota(jnp.int32, (LANES, LANES), 1)
        before = jnp.where(r < c, 1.0, 0.0).astype(BF16)
        run = jnp.zeros_like(need)
        for ch in range(key.shape[1] // LANES):
            sl = slice(ch * LANES, (ch + 1) * LANES)
            e = jnp.where(eq[:, sl], 1.0, 0.0)
            pre = _dot(e.astype(BF16), before) + run
            scratch_ref[:, sl] = jnp.where(gt[:, sl], 1.0, e * jnp.where(pre < need, 1.0, 0.0))
            run = run + jnp.sum(e, axis=1, keepdims=True)


CAUSAL_BUCKETS = 8


def _by_key_extent(i, tq, t, fn):
    nb = CAUSAL_BUCKETS if t % (CAUSAL_BUCKETS * tq) == 0 else 1
    size = t // nb
    for bkt in range(nb):
        @pl.when((i * tq) // size == bkt)
        def _():
            fn((bkt + 1) * size)


def _dsa_prompt_kernel(qi_ref, wi_ref, ki_ref, q_ref, k_ref, v_ref, o_ref,
                       kib_ref, kb_ref, vb_ref, mask_ref, *, topk):
    i = pl.program_id(1)

    @pl.when(i == 0)
    def _():
        kib_ref[...] = ki_ref[...].astype(BF16)
        kb_ref[...] = k_ref[...].astype(BF16)
        for g in range(KV_B):
            vb_ref[g] = _with_ones_column(v_ref[:, g * HEAD_DIM:(g + 1) * HEAD_DIM].astype(BF16))

    tq = q_ref.shape[0]
    t = ki_ref.shape[0]
    wi = wi_ref[...]
    rep = q_ref.shape[1] // HEAD_DIM // KV_B

    def attend(n):
        kib = kib_ref[0:n]
        score = jnp.zeros((tq, n), F32)
        for h in range(H_IDX):
            s = _dot_nt(qi_ref[:, h * D_IDX:(h + 1) * D_IDX], kib) * D_IDX ** -0.5
            score = score + jnp.maximum(s, 0.0) * wi[:, h:h + 1]
        score = score * H_IDX ** -0.5
        qpos = i * tq + lax.broadcasted_iota(jnp.int32, (tq, n), 0)
        kpos = lax.broadcasted_iota(jnp.int32, (tq, n), 1)
        key = jnp.where(kpos <= qpos, _sort_key(score), jnp.int32(INT_MIN))
        sel_ref = mask_ref.at[:, 0:n]
        _topk_mask(key, topk, sel_ref)
        bias = jnp.where(sel_ref[...] > 0.0, 0.0, NEG)
        for g in range(KV_B):
            qs = _stack_heads(q_ref, range(g * rep, (g + 1) * rep))
            sl = slice(g * HEAD_DIM, (g + 1) * HEAD_DIM)
            qk = _dot_nt(qs, kb_ref[0:n, sl]).reshape(rep, tq, n)
            o = _softmax_av(qk, HEAD_DIM ** -0.5, bias, vb_ref[g, 0:n])
            for r in range(rep):
                h = g * rep + r
                o_ref[:, h * HEAD_DIM:(h + 1) * HEAD_DIM] = o[r * tq:(r + 1) * tq].astype(o_ref.dtype)

    _by_key_extent(i, tq, t, attend)


def dsa_prompt(qi, wi, ki, q, k, v, b, t):
    m = q.shape[0]
    tq = _tile(t, 128, 16)
    nq = t // tq
    topk = min(TOPK_MAX, t // 4)
    qrow = lambda w: pl.BlockSpec((tq, w), lambda bi, i: (bi * nq + i, 0))
    full = lambda w: pl.BlockSpec((t, w), lambda bi, i: (bi, 0))
    return pl.pallas_call(
        functools.partial(_dsa_prompt_kernel, topk=topk), grid=(b, nq),
        in_specs=[qrow(qi.shape[1]), qrow(wi.shape[1]), full(ki.shape[1]),
                  qrow(q.shape[1]), full(k.shape[1]), full(v.shape[1])],
        out_specs=qrow(q.shape[1]),
        out_shape=jax.ShapeDtypeStruct((m, q.shape[1]), BF16),
        scratch_shapes=[pltpu.VMEM((t, ki.shape[1]), BF16), pltpu.VMEM((t, k.shape[1]), BF16),
                        pltpu.VMEM((KV_B, t, 2 * HEAD_DIM), BF16), pltpu.VMEM((tq, t), F32)],
        compiler_params=_cp("arbitrary", "arbitrary"), name="dsa_prompt")(qi, wi, ki, q, k, v)


def _masked_softmax(s, mask):
    m = jnp.max(jnp.where(mask, s, NEG), axis=-1, keepdims=True)
    m = jnp.where(m > 0.5 * NEG, m, 0.0)
    e = jnp.where(mask, jnp.exp(s - m), 0.0)
    return e / jnp.maximum(jnp.sum(e, axis=-1, keepdims=True), 1e-30)


def _pair_sums(imp, col):
    n = imp.shape[1]
    return imp + jnp.where(col % 2 == 0, pltpu.roll(imp, n - 1, 1), pltpu.roll(imp, 1, 1))


def _select_blocks(bs, col, qpos, n_selblk):
    blk = col // 2
    cur = qpos // SEL_BLOCK
    forced = (blk == 0) | (blk == cur) | (blk == cur - 1)
    admiss = (blk * SEL_BLOCK <= qpos) & (blk < n_selblk)
    work = jnp.where(admiss, jnp.where(forced, bs + FORCE_BONUS, bs), NEG)
    sel = jnp.zeros(bs.shape, jnp.bool_)
    big = jnp.int32(2 ** 30)
    for _ in range(min(N_SEL, n_selblk)):
        mx = jnp.max(work, axis=1, keepdims=True)
        idx = jnp.min(jnp.where(work == mx, col, big), axis=1, keepdims=True)
        pick = blk == idx // 2
        sel = sel | pick
        work = jnp.where(pick, -3e38, work)
    return jnp.where(sel & admiss, 1.0, 0.0)


def _nsa_prompt_kernel(q_ref, gc_ref, kc_ref, vc_ref, ks_ref, vs_ref, kw_ref, vw_ref,
                       cwk_ref, cwv_ref, o_ref,
                       kcmp_ref, vcmp_ref, ksb_ref, vsb_ref, kwb_ref, vwb_ref, *, win):
    i = pl.program_id(1)
    tq = q_ref.shape[0]
    t = kc_ref.shape[0]
    nb = t // CMP_BLOCK
    ncp = kcmp_ref.shape[0]
    n_selblk = -(-t // SEL_BLOCK)
    nh = q_ref.shape[1] // HEAD_DIM
    scale = HEAD_DIM ** -0.5

    @pl.when(i == 0)
    def _():
        kcmp_ref[...] = jnp.zeros(kcmp_ref.shape, kcmp_ref.dtype)
        vcmp_ref[...] = jnp.zeros(vcmp_ref.shape, vcmp_ref.dtype)
        kc = kc_ref[...].reshape(nb, CMP_BLOCK, HEAD_DIM)
        vc = vc_ref[...].reshape(nb, CMP_BLOCK, HEAD_DIM)
        kcmp_ref[0:nb] = jnp.sum(kc * cwk_ref[...][None], axis=1).astype(BF16)
        vcmp_ref[0:nb] = jnp.sum(vc * cwv_ref[...][None], axis=1).astype(BF16)
        ksb_ref[...] = ks_ref[...].astype(BF16)
        vsb_ref[...] = _with_ones_column(vs_ref[...].astype(BF16))
        kwb_ref[...] = kw_ref[...].astype(BF16)
        vwb_ref[...] = _with_ones_column(vw_ref[...].astype(BF16))

    qs = _stack_heads(q_ref, range(nh))
    col = lax.broadcasted_iota(jnp.int32, (tq, ncp), 1)
    qpos_c = i * tq + lax.broadcasted_iota(jnp.int32, (tq, ncp), 0)
    cmask = ((col + 1) * CMP_BLOCK - 1 <= qpos_c) & (col < nb)
    s = (_dot_nt(qs, kcmp_ref[...]) * scale).reshape(nh, tq, ncp)
    p = _masked_softmax(s, cmask[None])
    o_cmp = _dot(p.reshape(nh * tq, ncp).astype(BF16), vcmp_ref[...])
    imp = jnp.sum(p, axis=0)
    sel = _select_blocks(_pair_sums(imp, col), col, qpos_c, n_selblk).astype(BF16)
    start = pl.multiple_of(jnp.clip(i * tq - WINDOW, 0, t - win), 16)
    qpos_w = i * tq + lax.broadcasted_iota(jnp.int32, (tq, win), 0)
    kpos_w = start + lax.broadcasted_iota(jnp.int32, (tq, win), 1)
    bias_win = jnp.where((kpos_w <= qpos_w) & (kpos_w > qpos_w - WINDOW), 0.0, NEG)
    kwin = kwb_ref[pl.ds(start, win), :]
    vwin = vwb_ref[pl.ds(start, win), :]
    gate = jax.nn.sigmoid(gc_ref[...])
    grp = 4
    o_wins = []
    for hg in range(nh // grp):
        q4 = qs[hg * grp * tq:(hg + 1) * grp * tq]
        qk_win = _dot_nt(q4, kwin).reshape(grp, tq, win)
        o_wins.append(_softmax_av(qk_win, scale, bias_win, vwin))

    def attend(n):
        er = lax.broadcasted_iota(jnp.int32, (ncp, n), 0)
        ec = lax.broadcasted_iota(jnp.int32, (ncp, n), 1)
        expand = jnp.where(er == 2 * (ec // SEL_BLOCK), 1.0, 0.0).astype(BF16)
        tok = _dot(sel, expand)
        qpos = i * tq + lax.broadcasted_iota(jnp.int32, (tq, n), 0)
        kpos = lax.broadcasted_iota(jnp.int32, (tq, n), 1)
        bias_sel = jnp.where((tok > 0.5) & (kpos <= qpos), 0.0, NEG)
        for hg in range(nh // grp):
            q4 = qs[hg * grp * tq:(hg + 1) * grp * tq]
            qk_sel = _dot_nt(q4, ksb_ref[0:n]).reshape(grp, tq, n)
            o_sel = _softmax_av(qk_sel, scale, bias_sel, vsb_ref[0:n])
            o_win = o_wins[hg]
            for r in range(grp):
                h = hg * grp + r
                rows = slice(r * tq, (r + 1) * tq)
                o = (gate[:, 3 * h:3 * h + 1] * o_cmp[h * tq:(h + 1) * tq]
                     + gate[:, 3 * h + 1:3 * h + 2] * o_sel[rows] + gate[:, 3 * h + 2:3 * h + 3] * o_win[rows])
                o_ref[:, h * HEAD_DIM:(h + 1) * HEAD_DIM] = o.astype(o_ref.dtype)

    _by_key_extent(i, tq, t, attend)


def nsa_prompt(q, gc, kc, vc, ks, vs, kw, vw, cwk, cwv, b, t):
    m, hq = q.shape
    tq = _tile(t, 128, 16)
    nq = t // tq
    win = min(WINDOW + tq, t)
    ncp = LANES
    assert t // CMP_BLOCK <= ncp and t % SEL_BLOCK == 0
    qrow = lambda w: pl.BlockSpec((tq, w), lambda bi, i: (bi * nq + i, 0))
    full = pl.BlockSpec((t, HEAD_DIM), lambda bi, i: (bi, 0))
    cw = pl.BlockSpec((CMP_BLOCK, HEAD_DIM), lambda bi, i: (0, 0))
    kv = pltpu.VMEM((t, HEAD_DIM), BF16)
    kv_ones = pltpu.VMEM((t, 2 * HEAD_DIM), BF16)
    return pl.pallas_call(
        functools.partial(_nsa_prompt_kernel, win=win), grid=(b, nq),
        in_specs=[qrow(hq), qrow(gc.shape[1])] + [full] * 6 + [cw, cw],
        out_specs=qrow(hq), out_shape=jax.ShapeDtypeStruct((m, hq), BF16),
        scratch_shapes=[pltpu.VMEM((ncp, HEAD_DIM), BF16), pltpu.VMEM((ncp, HEAD_DIM), BF16),
                        kv, kv_ones, kv, kv_ones],
        compiler_params=_cp("arbitrary", "arbitrary"), name="nsa_prompt")(
            q, gc, kc, vc, ks, vs, kw, vw, cwk, cwv)


def _log_sigmoids(z):
    ls = jnp.minimum(z, 0.0) - jnp.log(1.0 + jnp.exp(-jnp.abs(z)))
    return ls, ls - z


def _split_bf16(x):
    hi = x.astype(BF16)
    return hi, (x - hi.astype(F32)).astype(BF16)


SB_HEADS_PER_STEP = 2


def _sb_prompt_kernel(q_ref, k_ref, v_ref, o_ref):
    i = pl.program_id(2)
    tq = q_ref.shape[0]
    nhs = q_ref.shape[1] // HEAD_DIM
    r = lax.broadcasted_iota(jnp.int32, (tq, tq), 0)
    c = lax.broadcasted_iota(jnp.int32, (tq, tq), 1)
    later = jnp.where(r > c, 1.0, 0.0).astype(BF16)
    before = c < r
    qs = [q_ref[:, h * HEAD_DIM:(h + 1) * HEAD_DIM] for h in range(nhs)]
    cols = lambda h: slice(h * HEAD_DIM, (h + 1) * HEAD_DIM)
    block_off = lambda j: pl.multiple_of(jnp.maximum(j, 0) * tq, tq)

    def scores(h, off):
        return _dot_nt(qs[h], k_ref[pl.ds(off, tq), cols(h)].astype(BF16)) * HEAD_DIM ** -0.5

    def weights(z, carry, m):
        ls, lneg = _log_sigmoids(z)
        if m is not None:
            lneg = jnp.where(m, lneg, 0.0)
        hi, lo = _split_bf16(lneg)
        after = _dot(hi, later) + _dot(lo, later)
        a = jnp.exp(ls + after + carry)
        if m is not None:
            a = jnp.where(m, a, 0.0)
        return a.astype(BF16), carry + after[:, 0:1] + lneg[:, 0:1]

    def weighted_values(h, a, off, acc):
        return acc + _dot(a, v_ref[pl.ds(off, tq), cols(h)].astype(BF16))

    st = []
    for h in range(nhs):
        a, carry = weights(scores(h, block_off(i)), jnp.zeros((tq, 1), F32), before)
        st.append((scores(h, block_off(i - 1)), a, carry, jnp.zeros((tq, HEAD_DIM), F32)))

    def body(jj, st):
        out = []
        for h in range(nhs):
            z, a_prev, carry, acc = st[h]
            z_next = scores(h, block_off(i - 2 - jj))
            acc = weighted_values(h, a_prev, block_off(i - jj), acc)
            a, carry = weights(z, carry, None)
            out.append((z_next, a, carry, acc))
        return tuple(out)

    st = lax.fori_loop(0, i, body, tuple(st))
    for h in range(nhs):
        acc = weighted_values(h, st[h][1], 0, st[h][3])
        o_ref[:, h * HEAD_DIM:(h + 1) * HEAD_DIM] = acc.astype(o_ref.dtype)


def sb_prompt(q, k, v, b, t):
    m, hq = q.shape
    nhs = SB_HEADS_PER_STEP
    tq = _tile(t, 256, 16)
    nq = t // tq
    w = nhs * HEAD_DIM
    qrow = pl.BlockSpec((tq, w), lambda bi, h, i: (bi * nq + i, h))
    full = pl.BlockSpec((t, w), lambda bi, h, i: (bi, h))
    return pl.pallas_call(
        _sb_prompt_kernel, grid=(b, hq // w, nq),
        in_specs=[qrow, full, full], out_specs=qrow,
        out_shape=jax.ShapeDtypeStruct((m, hq), BF16),
        compiler_params=_cp("arbitrary", "arbitrary", "arbitrary"), name="sb_prompt")(q, k, v)


def _mem_kernel(h_ref, x_ref, wq_ref, mk_ref, mv_ref, wo_ref, gp_ref, gn_ref, xo_ref, ho_ref):
    q = _dot(h_ref[0], wq_ref[...])
    outs = []
    for hh in range(H_MEM):
        sl = slice(hh * HEAD_DIM, (hh + 1) * HEAD_DIM)
        s = _dot_nt(q[:, sl].astype(BF16), mk_ref[0, :, sl].astype(BF16)) * HEAD_DIM ** -0.5
        e = jnp.exp(s - jnp.max(s, axis=-1, keepdims=True))
        o = _dot(e.astype(BF16), mv_ref[0, :, sl].astype(BF16))
        outs.append(o / jnp.sum(e, axis=-1, keepdims=True))
    y = _dot(jnp.concatenate(outs, axis=1).astype(BF16), wo_ref[...])
    xn = x_ref[0] + _rms(y, gp_ref[...])
    xo_ref[0] = xn
    ho_ref[0] = _rms(xn, gn_ref[...]).astype(ho_ref.dtype)


def mem_sublayer(h3, x3, wq, mk, mv, wo, g_post, g_next):
    b, t, d = h3.shape
    tq = _tile(t, 256, 16)
    nm, hm = mk.shape[1], mk.shape[2]
    row = pl.BlockSpec((1, tq, d), lambda bi, i: (bi, i, 0))
    mem = pl.BlockSpec((1, nm, hm), lambda bi, i: (bi, 0, 0))
    vec = pl.BlockSpec((1, d), lambda bi, i: (0, 0))
    return pl.pallas_call(
        _mem_kernel, grid=(b, t // tq),
        in_specs=[row, row, pl.BlockSpec((d, hm), lambda bi, i: (0, 0)), mem, mem,
                  pl.BlockSpec((hm, d), lambda bi, i: (0, 0)), vec, vec],
        out_specs=[row, row],
        out_shape=[jax.ShapeDtypeStruct((b, t, d), F32), jax.ShapeDtypeStruct((b, t, d), BF16)],
        compiler_params=_cp("arbitrary", "arbitrary"), name="mem_sublayer")(
            h3, x3, wq, mk, mv, wo, g_post.reshape(1, d), g_next.reshape(1, d))


def _ep_swiglu(accs, er, orf):
    g, u = accs
    orf[0][...] = ((g * jax.nn.sigmoid(g)) * u).astype(orf[0].dtype)


def ffn(h, wg, wu, wd, li):
    act = matmul([h], [wg, wu], [[(0, 0)], [(0, 1)]], _ep_swiglu, [BF16], layer=li, name="ffn_gate_up")
    return matmul([act], [wd], [[(0, 0)]], _ep_plain, [F32], tm_pref=512, layer=li, name="ffn_down")


def _odd_weights(w_in, w_out, nh_c, nh_d):
    sizes = [nh_c * HEAD_DIM] + [HEAD_DIM] * 6 + [nh_c * 3] + [nh_d * HEAD_DIM] * 3
    offs = np.cumsum([0] + sizes)
    cut = lambda a, b_: w_in[:, offs[a]:offs[b_]].astype(BF16)
    w = {"qc": cut(0, 1), "kv6": cut(1, 7), "gc": cut(7, 8), "qd": cut(8, 9), "kd": cut(9, 10),
         "vd": cut(10, 11)}
    w["out_c"] = w_out[:nh_c * HEAD_DIM].astype(BF16)
    w["out_d"] = w_out[nh_c * HEAD_DIM:].astype(BF16)
    return w


def _odd_project(h, w, tabs):
    cos, sin, nb = tabs
    qc = proj_rope(h, w["qc"], tabs, BF16, name="proj_qc")
    kv6 = matmul([h], [w["kv6"]], [[(0, 0)]], functools.partial(_ep_split, rope_chunks=(0, 2, 4)),
                 [F32] * 6, tables=(cos, sin), n_tab_blocks=nb, split_out=6, name="proj_kv6")
    gc = proj(h, w["gc"], name="proj_gc")
    qd = proj(h, w["qd"], BF16, name="proj_qd")
    kd = proj(h, w["kd"], name="proj_kd")
    vd = proj(h, w["vd"], name="proj_vd")
    return qc, kv6, gc, qd, kd, vd


def odd_mixer_prompt(h, b, t, w, cwk, cwv):
    m, d = h.shape
    tabs = rope_tables(jnp.arange(t), b, _tile(m, 1024, 16))
    qc, (kc, vc, ks, vs, kw, vw), gc, qd, kd, vd = _odd_project(h, w, tabs)
    o_c = nsa_prompt(qc, gc, kc, vc, ks, vs, kw, vw, cwk, cwv, b, t)
    o_d = sb_prompt(qd, kd, vd, b, t)
    y = proj_out(o_c, o_d, w["out_c"], w["out_d"])
    nw = min(WINDOW, t)
    st1 = lambda a: a.reshape(b, t, 1, HEAD_DIM)
    sth = lambda a: a.reshape(b, t, -1, HEAD_DIM)
    return y, (st1(kc), st1(vc), st1(ks), st1(vs), st1(kw)[:, t - nw:], st1(vw)[:, t - nw:],
               sth(kd), sth(vd))


def _even_weights(w_in, conv_w, w_out):
    c = conv_w.shape[1]
    hq = w_out.shape[0] - c
    sizes = [c, c, c, hq, KV_B * HEAD_DIM, KV_B * HEAD_DIM, H_IDX * D_IDX, D_IDX, H_IDX]
    offs = np.cumsum([0] + sizes)
    names = ["bg", "cg", "xa", "q", "k", "v", "qi", "ki", "wi"]
    w = {n: w_in[:, offs[j]:offs[j + 1]].astype(BF16) for j, n in enumerate(names)}
    w["conv_w"] = conv_w
    w["out_a"] = w_out[:c].astype(BF16)
    w["out_b"] = w_out[c:].astype(BF16)
    return w


def _even_project(h, w, tabs):
    q = proj_rope(h, w["q"], tabs, BF16, name="proj_q")
    k = proj_rope(h, w["k"], tabs, F32, name="proj_k")
    v = proj(h, w["v"], name="proj_v")
    qi = proj_rope(h, w["qi"], tabs, BF16, name="proj_qi")
    ki = proj_rope(h, w["ki"], tabs, F32, name="proj_ki")
    wi = proj(h, w["wi"], name="proj_wi")
    return q, k, v, qi, ki, wi


def even_mixer_prompt(h, b, t, w):
    m, d = h.shape
    tabs = rope_tables(jnp.arange(t), b, _tile(m, 1024, 16))
    init = jnp.zeros((b, CONV_K - 1, w["bg"].shape[1]), F32)
    ya, conv_state = conv_mixer(h.reshape(b, t, d), w["bg"], w["cg"], w["xa"], w["conv_w"], init)
    q, k, v, qi, ki, wi = _even_project(h, w, tabs)
    ob = dsa_prompt(qi, wi, ki, q, k, v, b, t)
    y = proj_out(ya.reshape(m, -1), ob, w["out_a"], w["out_b"])
    return y, (conv_state, k.reshape(b, t, KV_B, HEAD_DIM), v.reshape(b, t, KV_B, HEAD_DIM),
               ki.reshape(b, t, D_IDX))


def _paged_call(kern, grid, in_specs, out_specs, out_shape, scratch, name, page_table, args):
    gs = pltpu.PrefetchScalarGridSpec(num_scalar_prefetch=1, grid=grid, in_specs=in_specs,
                                      out_specs=out_specs, scratch_shapes=scratch)
    return pl.pallas_call(kern, grid_spec=gs, out_shape=out_shape,
                          compiler_params=_cp("arbitrary", "arbitrary"), name=name)(page_table, *args)


def _pad_rows(a, rows):
    return jnp.pad(a, ((0, 0), (0, rows - a.shape[1])) + ((0, 0),) * (a.ndim - 2))


def _page_specs(block_tail, layer, g_pages, page_of):
    zeros = (0,) * len(block_tail)

    def spec(j):
        return pl.BlockSpec((1, 1) + block_tail,
                            lambda bi, p, pt: (layer, page_of(bi, p, pt, j)) + zeros)
    return [spec(j) for j in range(g_pages)]


def _forward_pages(g_pages):
    return lambda bi, p, pt, j: pt[bi, p * g_pages + j]


def _rows_ht(a, b, t, nh):
    w = a.shape[1] // nh
    return a.reshape(b, t, nh, w).transpose(0, 2, 1, 3).reshape(b, nh * t, w)


def _rows_th(a, b, t, nh):
    w = a.shape[2]
    return a.reshape(b, nh, t, w).transpose(0, 2, 1, 3).reshape(b * t, nh * w)


def _dsa_scores_kernel(pt_ref, qi_ref, wi_ref, new_ref, *rest, g_pages, t):
    pools, (o_ref, onew_ref) = rest[:g_pages], rest[g_pages:]
    qi, wi = qi_ref[0], wi_ref[0]

    def scores(kb):
        s = _dot_nt(qi, kb) * D_IDX ** -0.5
        s = jnp.maximum(s, 0.0) * wi
        return jnp.sum(s.reshape(H_IDX, t, kb.shape[0]), axis=0) * H_IDX ** -0.5

    o_ref[0] = scores(jnp.concatenate([r[0, 0] for r in pools], axis=0).astype(BF16))

    @pl.when(pl.program_id(1) == 0)
    def _():
        onew_ref[0] = scores(new_ref[0].astype(BF16))


def dsa_sample_scores(qi_r, wi_r, ki_new, pool, e, page_table, t):
    b, n_pages = page_table.shape
    g = _tile(n_pages, 8, 1)
    kern = functools.partial(_dsa_scores_kernel, g_pages=g, t=t)
    per_b = lambda r, w: pl.BlockSpec((1, r, w), lambda bi, p, pt: (bi, 0, 0))
    past, new = _paged_call(
        kern, (b, n_pages // g),
        [per_b(H_IDX * t, D_IDX), per_b(H_IDX * t, 1), per_b(PAGE_SIZE, D_IDX)]
        + _page_specs((PAGE_SIZE, D_IDX), e, g, _forward_pages(g)),
        [pl.BlockSpec((1, t, g * PAGE_SIZE), lambda bi, p, pt: (bi, 0, p)), per_b(t, PAGE_SIZE)],
        [jax.ShapeDtypeStruct((b, t, n_pages * PAGE_SIZE), F32),
         jax.ShapeDtypeStruct((b, t, PAGE_SIZE), F32)], [], "dsa_sample_scores",
        page_table, (qi_r, wi_r, _pad_rows(ki_new, PAGE_SIZE)) + (pool,) * g)
    return jnp.concatenate([past, new], axis=-1)


def _dsa_topk_kernel(s_ref, o_ref, mask_ref, *, past, topk):
    score = s_ref[0]
    kpos = lax.broadcasted_iota(jnp.int32, score.shape, 1)
    qpos = past + lax.broadcasted_iota(jnp.int32, score.shape, 0)
    key = jnp.where(kpos <= qpos, _sort_key(score), jnp.int32(INT_MIN))
    _topk_mask(key, topk, mask_ref)
    o_ref[0] = mask_ref[...]


def dsa_sample_topk(score, past, topk):
    b, t, nk = score.shape
    blk = pl.BlockSpec((1, t, nk), lambda bi: (bi, 0, 0))
    return pl.pallas_call(
        functools.partial(_dsa_topk_kernel, past=past, topk=topk), grid=(b,),
        in_specs=[blk], out_specs=blk, out_shape=jax.ShapeDtypeStruct((b, t, nk), F32),
        scratch_shapes=[pltpu.VMEM((t, nk), F32)],
        compiler_params=_cp("arbitrary"), name="dsa_sample_topk")(score)


def _online_softmax_step(s, valid, v, m_ref, l_ref, acc_ref):
    m_old = m_ref[...]
    m_new = jnp.maximum(m_old, jnp.max(jnp.where(valid, s, NEG), axis=-1, keepdims=True))
    alpha = jnp.exp(m_old - m_new)
    e = jnp.where(valid, jnp.exp(s - m_new), 0.0)
    l_ref[...] = alpha * l_ref[...] + jnp.sum(e, axis=-1, keepdims=True)
    acc_ref[...] = alpha * acc_ref[...] + _dot(e.astype(BF16), v)
    m_ref[...] = m_new


def _init_softmax_state(m_ref, l_ref, acc_ref):
    m_ref[...] = jnp.full(m_ref.shape, NEG, F32)
    l_ref[...] = jnp.zeros(l_ref.shape, F32)
    acc_ref[...] = jnp.zeros(acc_ref.shape, F32)


def _dsa_sample_attn_kernel(pt_ref, qt_ref, mask_ref, masknew_ref, knew_ref, vnew_ref, *rest,
                            g_pages, t):
    kps, vps = rest[:g_pages], rest[g_pages:2 * g_pages]
    o_ref, m_ref, l_ref, acc_ref, z_ref = rest[2 * g_pages:]
    p = pl.program_id(1)
    n, ng = PAGE_SIZE, KV_B
    cols = qt_ref.shape[2]
    per_group = cols // ng

    @pl.when(p == 0)
    def _():
        _init_softmax_state(m_ref, l_ref, acc_ref)

    iota = lambda shape, d: lax.broadcasted_iota(jnp.int32, shape, d)
    own_group = iota((n * ng, cols), 0) % ng == iota((n * ng, cols), 1) // per_group
    spread = jnp.where(iota((n, n * ng), 1) // ng == iota((n, n * ng), 0), 1.0, 0.0).astype(BF16)
    own_rows = iota((cols, n * ng), 0) // per_group == iota((cols, n * ng), 1) % ng

    def scores(k2, slot):
        z_ref[slot] = jnp.where(own_group, _dot(k2.astype(BF16), qt_ref[0]), 0.0)
        z = z_ref[slot, pl.ds(0, n, stride=ng), :]
        for g in range(1, ng):
            z = z + z_ref[slot, pl.ds(g, n, stride=ng), :]
        return z.T

    def weighted_values(e, v2):
        a2 = jnp.where(own_rows, _dot(e.astype(BF16), spread), 0.0)
        return _dot(a2.astype(BF16), v2.astype(BF16))

    def update(k2s, v2s, valid_t, first_slot):
        s = jnp.concatenate([scores(k2, first_slot + j) for j, k2 in enumerate(k2s)], axis=1)
        s = s * HEAD_DIM ** -0.5
        valid = jnp.concatenate([valid_t] * (cols // t), axis=0)
        m_old = m_ref[...]
        m_new = jnp.maximum(m_old, jnp.max(jnp.where(valid, s, NEG), axis=-1, keepdims=True))
        alpha = jnp.exp(m_old - m_new)
        e = jnp.where(valid, jnp.exp(s - m_new), 0.0)
        l_ref[...] = alpha * l_ref[...] + jnp.sum(e, axis=-1, keepdims=True)
        pv = None
        for j, v2 in enumerate(v2s):
            d = weighted_values(e[:, j * n:(j + 1) * n], v2)
            pv = d if pv is None else pv + d
        acc_ref[...] = alpha * acc_ref[...] + pv
        m_ref[...] = m_new

    update([r[...] for r in kps], [r[...] for r in vps], mask_ref[0] > 0.5, 0)

    @pl.when(p == pl.num_programs(1) - 1)
    def _():
        update([knew_ref[0]], [vnew_ref[0]], masknew_ref[0] > 0.5, g_pages)
        o_ref[0] = acc_ref[...] / l_ref[...]


def dsa_sample_attn(q_r, mask, kpool, vpool, k_new, v_new, e, page_table):
    b, n_pages = page_table.shape
    t = mask.shape[1]
    cols = q_r.shape[1]
    g = _tile(n_pages, 8, 1)
    rows = PAGE_SIZE * KV_B
    flat = lambda pool: pool.reshape(pool.shape[0], -1, HEAD_DIM)
    new_rows = lambda a: _pad_rows(a, PAGE_SIZE).reshape(b, rows, HEAD_DIM)
    per_b = lambda r, w: pl.BlockSpec((1, r, w), lambda bi, p, pt: (bi, 0, 0))
    pools = [pl.BlockSpec((None, rows, HEAD_DIM), functools.partial(
        lambda bi, p, pt, j: (e, pt[bi, p * g + j], 0), j=j)) for j in range(g)]
    return _paged_call(
        functools.partial(_dsa_sample_attn_kernel, g_pages=g, t=t), (b, n_pages // g),
        [per_b(HEAD_DIM, cols), pl.BlockSpec((1, t, g * PAGE_SIZE), lambda bi, p, pt: (bi, 0, p)),
         pl.BlockSpec((1, t, PAGE_SIZE), lambda bi, p, pt: (bi, 0, n_pages)),
         per_b(rows, HEAD_DIM), per_b(rows, HEAD_DIM)] + pools + pools,
        per_b(cols, HEAD_DIM), jax.ShapeDtypeStruct((b, cols, HEAD_DIM), F32),
        [pltpu.VMEM((cols, 1), F32), pltpu.VMEM((cols, 1), F32), pltpu.VMEM((cols, HEAD_DIM), F32),
         pltpu.VMEM((g + 1, rows, cols), F32)],
        "dsa_sample_attn", page_table,
        (q_r.transpose(0, 2, 1), mask, mask, new_rows(k_new), new_rows(v_new))
        + (flat(kpool),) * g + (flat(vpool),) * g)


def even_mixer_sample(h, b, t, w, e, page_table, state_conv, c_k, c_v, c_kidx):
    m, d = h.shape
    n_pages = page_table.shape[1]
    past = n_pages * PAGE_SIZE
    tabs = rope_tables(past + jnp.arange(t), b, _tile(m, 1024, 16))
    ya, conv_state = conv_mixer(h.reshape(b, t, d), w["bg"], w["cg"], w["xa"], w["conv_w"], state_conv[e])
    q, k, v, qi, ki, wi = _even_project(h, w, tabs)
    score = dsa_sample_scores(_rows_ht(qi, b, t, H_IDX), _rows_ht(wi, b, t, H_IDX),
                              ki.reshape(b, t, D_IDX), c_kidx, e, page_table, t)
    mask = dsa_sample_topk(score, past, min(TOPK_MAX, (past + t) // 4))
    nh = q.shape[1] // HEAD_DIM
    o = dsa_sample_attn(_rows_ht(q, b, t, nh), mask, c_k, c_v, k.reshape(b, t, KV_B, HEAD_DIM),
                        v.reshape(b, t, KV_B, HEAD_DIM), e, page_table)
    ob = _rows_th(o, b, t, nh).astype(BF16)
    y = proj_out(ya.reshape(m, -1), ob, w["out_a"], w["out_b"])
    return y, (conv_state, k.reshape(b, t, KV_B, HEAD_DIM), v.reshape(b, t, KV_B, HEAD_DIM),
               ki.reshape(b, t, D_IDX))


def _compress_pages_kernel(pt_ref, cwk_ref, cwv_ref, *rest, g_pages):
    kcs, vcs = rest[:g_pages], rest[g_pages:2 * g_pages]
    ko_ref, vo_ref = rest[2 * g_pages:]
    nb = PAGE_SIZE // CMP_BLOCK
    cwk, cwv = cwk_ref[...][None], cwv_ref[...][None]
    for j in range(g_pages):
        ko_ref[0, j] = jnp.sum(kcs[j][0, 0].reshape(nb, CMP_BLOCK, HEAD_DIM) * cwk, axis=1)
        vo_ref[0, j] = jnp.sum(vcs[j][0, 0].reshape(nb, CMP_BLOCK, HEAD_DIM) * cwv, axis=1)


def nsa_compress_pages(kpool, vpool, cwk, cwv, o, page_table):
    b, n_pages = page_table.shape
    nb = PAGE_SIZE // CMP_BLOCK
    g = _tile(n_pages, 8, 1)
    pools = _page_specs((PAGE_SIZE, HEAD_DIM), o, g, _forward_pages(g))
    cw = pl.BlockSpec((CMP_BLOCK, HEAD_DIM), lambda bi, p, pt: (0, 0))
    out = pl.BlockSpec((1, g, nb, HEAD_DIM), lambda bi, p, pt: (bi, p, 0, 0))
    shp = jax.ShapeDtypeStruct((b, n_pages, nb, HEAD_DIM), F32)
    kc, vc = _paged_call(functools.partial(_compress_pages_kernel, g_pages=g), (b, n_pages // g),
                         [cw, cw] + pools + pools, [out, out], [shp, shp], [], "nsa_compress_pages",
                         page_table, (cwk, cwv) + (kpool,) * g + (vpool,) * g)
    return kc.reshape(b, n_pages * nb, HEAD_DIM), vc.reshape(b, n_pages * nb, HEAD_DIM)


def _nsa_sample_a_kernel(q_ref, kcmp_ref, vcmp_ref, kw_ref, vw_ref, ocmp_ref, owin_ref, sel_ref,
                         *, past, t, nwin, n_selblk):
    q = q_ref[0]
    rows = q.shape[0]
    nh = rows // t
    scale = HEAD_DIM ** -0.5
    nb = kcmp_ref.shape[1]
    col = lax.broadcasted_iota(jnp.int32, (rows, nb), 1)
    pos = past + lax.broadcasted_iota(jnp.int32, (rows, nb), 0) % t
    s = _dot_nt(q, kcmp_ref[0].astype(BF16)) * scale
    p = _masked_softmax(s, (col + 1) * CMP_BLOCK - 1 <= pos)
    ocmp_ref[0] = _dot(p.astype(BF16), vcmp_ref[0].astype(BF16))
    imp = jnp.sum(p.reshape(nh, t, nb), axis=0)
    ncol = sel_ref.shape[2]
    imp = jnp.concatenate([imp, jnp.zeros((t, ncol - nb), F32)], axis=1)
    col_s = lax.broadcasted_iota(jnp.int32, (t, ncol), 1)
    pos_s = past + lax.broadcasted_iota(jnp.int32, (t, ncol), 0)
    sel_ref[0] = _select_blocks(_pair_sums(imp, col_s), col_s, pos_s, n_selblk)
    nwp = kw_ref.shape[1]
    colw = lax.broadcasted_iota(jnp.int32, (rows, nwp), 1)
    posw = past + lax.broadcasted_iota(jnp.int32, (rows, nwp), 0) % t
    kwpos = past + t - nwin + colw
    valid = (kwpos <= posw) & (kwpos > posw - WINDOW) & (colw < nwin)
    sw = _dot_nt(q, kw_ref[0].astype(BF16)) * scale
    pw = _masked_softmax(sw, valid)
    owin_ref[0] = _dot(pw.astype(BF16), vw_ref[0].astype(BF16))


def nsa_sample_a(q_r, kcmp, vcmp, kw_pad, vw_pad, past, t, nwin):
    b, rows, _ = q_r.shape
    nb = kcmp.shape[1]
    n_selblk = -(-(past + t) // SEL_BLOCK)
    ncol = -(-2 * n_selblk // LANES) * LANES
    assert ncol > nb >= 2 * n_selblk - 2 and nb % LANES == 0
    per_b = lambda r, w: pl.BlockSpec((1, r, w), lambda bi: (bi, 0, 0))
    kern = functools.partial(_nsa_sample_a_kernel, past=past, t=t, nwin=nwin, n_selblk=n_selblk)
    return pl.pallas_call(
        kern, grid=(b,),
        in_specs=[per_b(rows, HEAD_DIM), per_b(nb, HEAD_DIM), per_b(nb, HEAD_DIM),
                  per_b(kw_pad.shape[1], HEAD_DIM), per_b(kw_pad.shape[1], HEAD_DIM)],
        out_specs=[per_b(rows, HEAD_DIM), per_b(rows, HEAD_DIM), per_b(t, ncol)],
        out_shape=[jax.ShapeDtypeStruct((b, rows, HEAD_DIM), F32)] * 2
        + [jax.ShapeDtypeStruct((b, t, ncol), F32)],
        compiler_params=_cp("arbitrary"), name="nsa_sample_a")(q_r, kcmp, vcmp, kw_pad, vw_pad)


def _nsa_sample_b_kernel(pt_ref, q_ref, tok_ref, toknew_ref, knew_ref, vnew_ref,
                         ocmp_ref, owin_ref, gate_ref, *rest, g_pages, n_pages, t):
    kps, vps = rest[:g_pages], rest[g_pages:2 * g_pages]
    o_ref, m_ref, l_ref, acc_ref = rest[2 * g_pages:]
    p = pl.program_id(1)

    @pl.when(p == 0)
    def _():
        _init_softmax_state(m_ref, l_ref, acc_ref)

    q = q_ref[0]
    rows = q.shape[0]

    def update(keys, vals, tok, first_key):
        n = keys.shape[0]
        kpos = first_key + lax.broadcasted_iota(jnp.int32, (rows, n), 1)
        qpos = n_pages * PAGE_SIZE + lax.broadcasted_iota(jnp.int32, (rows, n), 0) % t
        valid = jnp.concatenate([tok > 0.5] * (rows // t), axis=0) & (kpos <= qpos)
        s = _dot_nt(q, keys.astype(BF16)) * HEAD_DIM ** -0.5
        _online_softmax_step(s, valid, vals.astype(BF16), m_ref, l_ref, acc_ref)

    update(jnp.concatenate([r[0, 0] for r in kps], axis=0),
           jnp.concatenate([r[0, 0] for r in vps], axis=0), tok_ref[0], p * (g_pages * PAGE_SIZE))

    @pl.when(p == pl.num_programs(1) - 1)
    def _():
        update(knew_ref[0], vnew_ref[0], toknew_ref[0], n_pages * PAGE_SIZE)
        g = jax.nn.sigmoid(gate_ref[0])
        o_ref[0] = (g[:, 0:1] * ocmp_ref[0] + g[:, 1:2] * (acc_ref[...] / l_ref[...])
                    + g[:, 2:3] * owin_ref[0])


def nsa_sample_b(q_r, tok, kpool, vpool, ks_new, vs_new, o_cmp, o_win, gate_r, o, page_table):
    b, n_pages = page_table.shape
    rows = q_r.shape[1]
    t = tok.shape[1]
    g = _tile(n_pages, 8, 1)
    per_b = lambda r, w: pl.BlockSpec((1, r, w), lambda bi, p, pt: (bi, 0, 0))
    pools = _page_specs((PAGE_SIZE, HEAD_DIM), o, g, _forward_pages(g))
    kern = functools.partial(_nsa_sample_b_kernel, g_pages=g, n_pages=n_pages, t=t)
    return _paged_call(
        kern, (b, n_pages // g),
        [per_b(rows, HEAD_DIM), pl.BlockSpec((1, t, g * PAGE_SIZE), lambda bi, p, pt: (bi, 0, p)),
         pl.BlockSpec((1, t, PAGE_SIZE), lambda bi, p, pt: (bi, 0, n_pages)),
         per_b(PAGE_SIZE, HEAD_DIM), per_b(PAGE_SIZE, HEAD_DIM),
         per_b(rows, HEAD_DIM), per_b(rows, HEAD_DIM), per_b(rows, 3)] + pools + pools,
        per_b(rows, HEAD_DIM), jax.ShapeDtypeStruct((b, rows, HEAD_DIM), F32),
        [pltpu.VMEM((rows, 1), F32), pltpu.VMEM((rows, 1), F32), pltpu.VMEM((rows, HEAD_DIM), F32)],
        "nsa_sample_b", page_table,
        (q_r, tok, tok, _pad_rows(ks_new, PAGE_SIZE), _pad_rows(vs_new, PAGE_SIZE),
         o_cmp, o_win, gate_r) + (kpool,) * g + (vpool,) * g)


def _sb_sample_kernel(pt_ref, qt_ref, knew_ref, vnew_ref, *rest, g_pages, t):
    kps, vps = rest[:g_pages], rest[g_pages:2 * g_pages]
    o_ref, carry_ref, acc_ref, z_ref = rest[2 * g_pages:]
    p = pl.program_id(1)
    n, nh = knew_ref.shape[1], knew_ref.shape[2]
    cols = nh * t

    @pl.when(p == 0)
    def _():
        carry_ref[...] = jnp.zeros(carry_ref.shape, F32)
        acc_ref[...] = jnp.zeros(acc_ref.shape, F32)

    iota = lambda shape, d: lax.broadcasted_iota(jnp.int32, shape, d)
    later = jnp.where(iota((n, n), 1) > iota((n, n), 0), 1.0, 0.0).astype(BF16)
    own_head = iota((nh, cols), 1) // t == iota((nh, cols), 0)
    spread = jnp.where(iota((n, n * nh), 1) // nh == iota((n, n * nh), 0), 1.0, 0.0).astype(BF16)
    own_rows = iota((cols, n * nh), 0) // t == iota((cols, n * nh), 1) % nh

    def log_weights(k3, slot, m):
        k2 = k3.reshape(n * nh, HEAD_DIM).astype(BF16)
        z_all = _dot(k2, qt_ref[0]).reshape(n, nh, cols)
        z_ref[slot] = jnp.sum(jnp.where(own_head[None], z_all, 0.0), axis=1)
        ls, lneg = _log_sigmoids(z_ref[slot] * HEAD_DIM ** -0.5)
        if m is not None:
            lneg = jnp.where(m, lneg, 0.0)
        hi, lo = _split_bf16(lneg)
        after = _dot(later, hi) + _dot(later, lo)
        return ls + after, after[0:1] + lneg[0:1]

    def weighted_values(v3, log_a, m):
        a = jnp.exp(log_a)
        if m is not None:
            a = jnp.where(m, a, 0.0)
        a2 = jnp.where(own_rows, _dot(a.T.astype(BF16), spread), 0.0)
        return _dot(a2.astype(BF16), v3.reshape(n * nh, HEAD_DIM).astype(BF16))

    @pl.when(p == 0)
    def _():
        m = iota((n, cols), 0) < iota((n, cols), 1) % t
        log_a, total = log_weights(knew_ref[0], g_pages, m)
        acc_ref[...] += weighted_values(vnew_ref[0], log_a, m)
        carry_ref[...] += total

    parts = [log_weights(kps[j][0, 0], j, None) for j in range(g_pages)]
    carry = carry_ref[...]
    acc = acc_ref[...]
    for j in range(g_pages):
        acc = acc + weighted_values(vps[j][0, 0], parts[j][0] + carry, None)
        carry = carry + parts[j][1]
    carry_ref[...] = carry
    acc_ref[...] = acc

    @pl.when(p == pl.num_programs(1) - 1)
    def _():
        o_ref[0] = acc_ref[...]


def sb_sample(qd, kd_new, vd_new, kpool, vpool, o, page_table, b, t):
    n_pages = page_table.shape[1]
    nh = qd.shape[1] // HEAD_DIM
    g = _tile(n_pages, 4, 1)
    qt = qd.reshape(b, t, nh, HEAD_DIM).transpose(0, 3, 2, 1).reshape(b, HEAD_DIM, nh * t)
    new = pl.BlockSpec((1, PAGE_SIZE, nh, HEAD_DIM), lambda bi, p, pt: (bi, 0, 0, 0))
    pools = _page_specs((PAGE_SIZE, nh, HEAD_DIM), o, g,
                        lambda bi, p, pt, j: pt[bi, n_pages - 1 - (p * g + j)])
    out = pl.BlockSpec((1, nh * t, HEAD_DIM), lambda bi, p, pt: (bi, 0, 0))
    return _paged_call(
        functools.partial(_sb_sample_kernel, g_pages=g, t=t), (b, n_pages // g),
        [pl.BlockSpec((1, HEAD_DIM, nh * t), lambda bi, p, pt: (bi, 0, 0)), new, new]
        + pools + pools,
        out, jax.ShapeDtypeStruct((b, nh * t, HEAD_DIM), F32),
        [pltpu.VMEM((1, nh * t), F32), pltpu.VMEM((nh * t, HEAD_DIM), F32),
         pltpu.VMEM((g + 1, PAGE_SIZE, nh * t), F32)],
        "sb_sample", page_table,
        (qt, _pad_rows(kd_new, PAGE_SIZE), _pad_rows(vd_new, PAGE_SIZE)) + (kpool,) * g + (vpool,) * g)


def odd_mixer_sample(h, b, t, w, cwk, cwv, o, page_table, c_kc, c_vc, c_ks, c_vs, c_kw, c_vw,
                     c_kd, c_vd):
    m, d = h.shape
    n_pages = page_table.shape[1]
    past = n_pages * PAGE_SIZE
    assert past % CMP_BLOCK == 0 and t < CMP_BLOCK
    tabs = rope_tables(past + jnp.arange(t), b, _tile(m, 1024, 16))
    qc, (kc, vc, ks, vs, kw, vw), gc, qd, kd, vd = _odd_project(h, w, tabs)
    nh = qc.shape[1] // HEAD_DIM
    n_pool = c_kc.shape[1]
    pool1 = lambda a: a.reshape(-1, n_pool, PAGE_SIZE, HEAD_DIM)
    seq = lambda a: a.reshape(b, t, -1)
    kcmp, vcmp = nsa_compress_pages(pool1(c_kc), pool1(c_vc), cwk, cwv, o, page_table)
    wb = c_kw.shape[2]
    kw_all = jnp.concatenate([c_kw[o].reshape(b, wb, HEAD_DIM), seq(kw)], axis=1)
    vw_all = jnp.concatenate([c_vw[o].reshape(b, wb, HEAD_DIM), seq(vw)], axis=1)
    nwp = -(-(wb + t) // LANES) * LANES
    q_r = _rows_ht(qc, b, t, nh)
    o_cmp, o_win, sel = nsa_sample_a(q_r, kcmp, vcmp, _pad_rows(kw_all, nwp), _pad_rows(vw_all, nwp),
                                     past, t, wb + t)
    n_selblk = -(-(past + t) // SEL_BLOCK)
    tok = jnp.repeat(sel[:, :, 0:2 * n_selblk:2], SEL_BLOCK, axis=-1)
    tok = jnp.pad(tok, ((0, 0), (0, 0), (0, (n_pages + 1) * PAGE_SIZE - tok.shape[-1])))
    gate_r = _rows_ht(gc, b, t, nh)
    o_c = nsa_sample_b(q_r, tok, pool1(c_ks), pool1(c_vs), seq(ks), seq(vs), o_cmp, o_win, gate_r,
                       o, page_table)
    nh_d = kd.shape[1] // HEAD_DIM
    heads = lambda a: a.reshape(b, t, nh_d, HEAD_DIM)
    o_d = sb_sample(qd, heads(kd), heads(vd), c_kd, c_vd, o, page_table, b, t)
    y = proj_out(_rows_th(o_c, b, t, nh).astype(BF16), _rows_th(o_d, b, t, nh_d).astype(BF16),
                 w["out_c"], w["out_d"])
    st1 = lambda a: a.reshape(b, t, 1, HEAD_DIM)
    return y, (st1(kc), st1(vc), st1(ks), st1(vs), kw_all[:, t:].reshape(b, wb, 1, HEAD_DIM),
               vw_all[:, t:].reshape(b, wb, 1, HEAD_DIM), kd.reshape(b, t, -1, HEAD_DIM),
               vd.reshape(b, t, -1, HEAD_DIM))


def kernel(x_prompt, x_sample, state_conv, cache_dsa_k, cache_dsa_v, cache_dsa_kidx, cache_nsa_kc, cache_nsa_vc, cache_nsa_ks, cache_nsa_vs, cache_nsa_kw, cache_nsa_vw, cache_sb_k, cache_sb_v, cache_mem_k, cache_mem_v, page_table, mem_prompt, norm_pre, norm_post, norm_mem, w_in_even, conv_w, w_out_even, w_in_odd, cmp_wk, cmp_wv, w_out_odd, w_mq, w_mk, w_mv, w_mo, w_gate, w_up, w_down):
    bp, tp, d = x_prompt.shape
    bs, ts, _ = x_sample.shape
    depth = norm_pre.shape[0]
    n_mem = mem_prompt.shape[1]
    hm = w_mq.shape[2]
    nh_c = nh_d = w_out_odd.shape[1] // 2 // HEAD_DIM
    xp = x_prompt.reshape(bp * tp, d)
    xs = x_sample.reshape(bs * ts, d)
    hp = norm_cast(xp, norm_pre[0, 0])
    hs = norm_cast(xs, norm_pre[0, 0])
    ev_p, ev_s, od_p, od_s, mem_p = [], [], [], [], []
    wg, wu, wd = w_gate.astype(BF16), w_up.astype(BF16), w_down.astype(BF16)
    for li in range(depth):
        g_pre, g_post = norm_pre[li], norm_post[li]
        if li % 2 == 0:
            e = li // 2
            w = _even_weights(w_in_even[e], conv_w[e], w_out_even[e])
            mp, stp = even_mixer_prompt(hp, bp, tp, w)
            ms, sts = even_mixer_sample(hs, bs, ts, w, e, page_table, state_conv,
                                        cache_dsa_k, cache_dsa_v, cache_dsa_kidx)
            ev_p.append(stp)
            ev_s.append(sts)
        else:
            o = li // 2
            w = _odd_weights(w_in_odd[o], w_out_odd[o], nh_c, nh_d)
            mp, stp = odd_mixer_prompt(hp, bp, tp, w, cmp_wk[o], cmp_wv[o])
            ms, sts = odd_mixer_sample(hs, bs, ts, w, cmp_wk[o], cmp_wv[o], o, page_table,
                                       cache_nsa_kc, cache_nsa_vc, cache_nsa_ks, cache_nsa_vs,
                                       cache_nsa_kw, cache_nsa_vw, cache_sb_k, cache_sb_v)
            od_p.append(stp)
            od_s.append(sts)
        xp, hp = resid_norm(xp, mp, g_post[0], g_pre[1])
        xs, hs = resid_norm(xs, ms, g_post[0], g_pre[1])
        wq, wo = w_mq[li].astype(BF16), w_mo[li].astype(BF16)
        hmem = norm_cast(mem_prompt.reshape(bp * n_mem, d), norm_mem[li])
        mkp = proj(hmem, w_mk[li].astype(BF16), name="proj_mk")
        mvp = proj(hmem, w_mv[li].astype(BF16), name="proj_mv")
        mem_p.append((mkp.reshape(bp, n_mem, H_MEM, HEAD_DIM), mvp.reshape(bp, n_mem, H_MEM, HEAD_DIM)))
        xp, hp = mem_sublayer(hp.reshape(bp, tp, d), xp.reshape(bp, tp, d), wq, mkp.reshape(bp, n_mem, hm),
                              mvp.reshape(bp, n_mem, hm), wo, g_post[1], g_pre[2])
        xs, hs = mem_sublayer(hs.reshape(bs, ts, d), xs.reshape(bs, ts, d), wq,
                              cache_mem_k[li].reshape(bs, n_mem, hm),
                              cache_mem_v[li].reshape(bs, n_mem, hm), wo, g_post[1], g_pre[2])
        xp, hp = xp.reshape(bp * tp, d), hp.reshape(bp * tp, d)
        xs, hs = xs.reshape(bs * ts, d), hs.reshape(bs * ts, d)
        g_next = norm_pre[li + 1, 0] if li + 1 < depth else None
        xp, hp = resid_norm(xp, ffn(hp, wg, wu, wd, li), g_post[2], g_next)
        xs, hs = resid_norm(xs, ffn(hs, wg, wu, wd, li), g_post[2], g_next)
    stack = lambda lst, j: jnp.stack([s[j] for s in lst])
    return ((xp.reshape(bp, tp, d), xs.reshape(bs, ts, d))
            + tuple(stack(ev_p, j) for j in range(4)) + tuple(stack(od_p, j) for j in range(8))
            + (stack(mem_p, 0), stack(mem_p, 1))
            + tuple(stack(ev_s, j) for j in range(4)) + tuple(stack(od_s, j) for j in range(8)))
```

```python
import functools

import numpy as np
import jax
import jax.numpy as jnp
from jax import lax
from jax.experimental import pallas as pl
from jax.experimental.pallas import tpu as pltpu

F32 = jnp.float32
BF16 = jnp.bfloat16

HEAD_DIM = 128
PAGE_SIZE = 128
CONV_K = 3
KV_B = 4
H_IDX = 16
D_IDX = 128
TOPK_MAX = 256
CMP_BLOCK = 32
SEL_BLOCK = 64
N_SEL = 16
WINDOW = 512
FORCE_BONUS = 1.0e6
H_MEM = 4
ROPE_THETA = 10000.0
EPS = 1e-6
NEG = -1e30
LANES = 128
VMEM_LIMIT = 56 * 1024 * 1024


def _cp(*sem):
    return pltpu.CompilerParams(dimension_semantics=sem, vmem_limit_bytes=VMEM_LIMIT)


def _tile(n, pref, mult):
    t = (min(pref, n) // mult) * mult
    while t >= mult:
        if n % t == 0:
            return t
        t -= mult
    return n


def _dot(a, b):
    return jnp.dot(a, b, preferred_element_type=F32)


def _dot_nt(a, b):
    return lax.dot_general(a, b, (((1,), (1,)), ((), ())), preferred_element_type=F32)


def _rms(x, g):
    return x * lax.rsqrt(jnp.mean(x * x, axis=-1, keepdims=True) + EPS) * g


def _rope(y, cos, sin):
    return y * cos + pltpu.roll(y, HEAD_DIM // 2, 1) * sin


def _norm_cast_kernel(x_ref, g_ref, o_ref):
    o_ref[...] = _rms(x_ref[...], g_ref[...]).astype(o_ref.dtype)


def norm_cast(x, g):
    m, d = x.shape
    tm = _tile(m, 256, 16)
    row = pl.BlockSpec((tm, d), lambda i: (i, 0))
    return pl.pallas_call(
        _norm_cast_kernel, grid=(m // tm,),
        in_specs=[row, pl.BlockSpec((1, d), lambda i: (0, 0))],
        out_specs=row, out_shape=jax.ShapeDtypeStruct((m, d), BF16),
        compiler_params=_cp("arbitrary"), name="norm_cast")(x, g.reshape(1, d))


def _resid_norm_kernel(x_ref, y_ref, gp_ref, gn_ref, xo_ref, h_ref):
    xn = x_ref[...] + _rms(y_ref[...], gp_ref[...])
    xo_ref[...] = xn
    h_ref[...] = _rms(xn, gn_ref[...]).astype(h_ref.dtype)


def _resid_kernel(x_ref, y_ref, gp_ref, xo_ref):
    xo_ref[...] = x_ref[...] + _rms(y_ref[...], gp_ref[...])


def resid_norm(x, y, g_post, g_next):
    m, d = x.shape
    tm = _tile(m, 256, 16)
    row = pl.BlockSpec((tm, d), lambda i: (i, 0))
    vec = pl.BlockSpec((1, d), lambda i: (0, 0))
    if g_next is None:
        return pl.pallas_call(
            _resid_kernel, grid=(m // tm,), in_specs=[row, row, vec], out_specs=row,
            out_shape=jax.ShapeDtypeStruct((m, d), F32),
            compiler_params=_cp("arbitrary"), name="resid")(x, y, g_post.reshape(1, d)), None
    return pl.pallas_call(
        _resid_norm_kernel, grid=(m // tm,), in_specs=[row, row, vec, vec],
        out_specs=[row, row],
        out_shape=[jax.ShapeDtypeStruct((m, d), F32), jax.ShapeDtypeStruct((m, d), BF16)],
        compiler_params=_cp("arbitrary"), name="resid_norm")(
            x, y, g_post.reshape(1, d), g_next.reshape(1, d))


def _mm_kernel(*refs, nx, nw, ne, groups, epilogue):
    xr, wr = refs[:nx], refs[nx:nx + nw]
    er, orf = refs[nx + nw:nx + nw + ne], refs[nx + nw + ne:]
    accs = []
    for grp in groups:
        acc = None
        for xi, wi in grp:
            w = wr[wi][...]
            d = _dot(xr[xi][...], w if w.dtype == BF16 else w.astype(BF16))
            acc = d if acc is None else acc + d
        accs.append(acc)
    epilogue(accs, er, orf)


def _ep_plain(accs, er, orf):
    orf[0][...] = accs[0].astype(orf[0].dtype)


def _ep_rope(accs, er, orf):
    cos, sin = er[0][...], er[1][...]
    y = accs[0]
    for c in range(y.shape[1] // HEAD_DIM):
        sl = slice(c * HEAD_DIM, (c + 1) * HEAD_DIM)
        orf[0][:, sl] = _rope(y[:, sl], cos, sin).astype(orf[0].dtype)


def _ep_split(accs, er, orf, *, rope_chunks):
    cos, sin = er[0][...], er[1][...]
    y = accs[0]
    for c in range(len(orf)):
        yc = y[:, c * HEAD_DIM:(c + 1) * HEAD_DIM]
        if c in rope_chunks:
            yc = _rope(yc, cos, sin)
        orf[c][...] = yc.astype(orf[c].dtype)


def matmul(xs, ws, groups, epilogue, out_dtypes, *, tables=None, n_tab_blocks=1,
           tm_pref=1024, tn_pref=512, split_out=0, layer=None, single_buffer_x=False, name="matmul"):
    m = xs[0].shape[0]
    n = ws[0].shape[-1]
    if tables is not None:
        tm = tables[0].shape[0] // n_tab_blocks
    else:
        tm = _tile(m, tm_pref, 16)
    tn = n if (split_out or n % LANES) else _tile(n, tn_pref, LANES)
    xmode = dict(pipeline_mode=pl.Buffered(1)) if single_buffer_x else {}
    in_specs = [pl.BlockSpec((tm, x.shape[1]), lambda i, j: (i, 0), **xmode) for x in xs]
    if layer is None:
        in_specs += [pl.BlockSpec((w.shape[0], tn), lambda i, j: (0, j)) for w in ws]
    else:
        in_specs += [pl.BlockSpec((None, w.shape[1], tn), lambda i, j: (layer, 0, j)) for w in ws]
    extras = []
    if tables is not None:
        nb = n_tab_blocks
        in_specs += [pl.BlockSpec((tm, HEAD_DIM), lambda i, j: (i % nb, 0))] * 2
        extras = list(tables)
    if split_out:
        out_specs = [pl.BlockSpec((tm, HEAD_DIM), lambda i, j: (i, 0))] * split_out
        out_shape = [jax.ShapeDtypeStruct((m, HEAD_DIM), dt) for dt in out_dtypes]
    else:
        out_specs = [pl.BlockSpec((tm, tn), lambda i, j: (i, j))]
        out_shape = [jax.ShapeDtypeStruct((m, n), out_dtypes[0])]
    kern = functools.partial(_mm_kernel, nx=len(xs), nw=len(ws), ne=len(extras),
                             groups=groups, epilogue=epilogue)
    out = pl.pallas_call(
        kern, grid=(m // tm, n // tn), in_specs=in_specs, out_specs=out_specs,
        out_shape=out_shape, compiler_params=_cp("arbitrary", "arbitrary"), name=name)(
            *xs, *ws, *extras)
    return out if split_out else out[0]


def proj(h, w, dtype=F32, name="proj"):
    return matmul([h], [w], [[(0, 0)]], _ep_plain, [dtype], name=name)


def proj_rope(h, w, tabs, dtype, name="proj_rope"):
    cos, sin, nb = tabs
    return matmul([h], [w], [[(0, 0)]], _ep_rope, [dtype], tables=(cos, sin),
                  n_tab_blocks=nb, name=name)


def proj_out(xa, xb, wa, wb, name="proj_out"):
    return matmul([xa, xb], [wa, wb], [[(0, 0), (1, 1)]], _ep_plain, [F32], name=name)


def rope_tables(pos, reps, tm):
    half = HEAD_DIM // 2
    inv = ROPE_THETA ** (-jnp.arange(half, dtype=F32) / half)
    ang = pos.astype(F32)[:, None] * inv[None, :]
    cos, sin = jnp.cos(ang), jnp.sin(ang)
    cos = jnp.concatenate([cos, cos], axis=-1)
    sin = jnp.concatenate([-sin, sin], axis=-1)
    t = pos.shape[0]
    if tm > t:
        cos, sin = jnp.tile(cos, (tm // t, 1)), jnp.tile(sin, (tm // t, 1))
        return cos, sin, 1
    return cos, sin, t // tm


def _conv_kernel(x_ref, wb_ref, wc_ref, wx_ref, cw_ref, init_ref, ya_ref, st_ref, carry_ref):
    i = pl.program_id(2)

    @pl.when(i == 0)
    def _():
        carry_ref[...] = init_ref[0]

    x = x_ref[0]
    bg = _dot(x, wb_ref[...])
    u = _dot(x, wc_ref[...]) * _dot(x, wx_ref[...])
    tm = u.shape[0]
    c = carry_ref[...]
    rows = lax.broadcasted_iota(jnp.int32, u.shape, 0)
    u1 = jnp.where(rows == 0, c[1:2], pltpu.roll(u, 1, 0))
    u2 = jnp.where(rows == 0, c[0:1], jnp.where(rows == 1, c[1:2], pltpu.roll(u, 2, 0)))
    cw = cw_ref[...]
    conv = cw[0:1] * u2 + cw[1:2] * u1 + cw[2:3] * u
    ya_ref[0] = (bg * conv).astype(ya_ref.dtype)
    new = u[tm - (CONV_K - 1):tm]
    carry_ref[...] = new
    st_ref[0] = new


def conv_mixer(h3, wb, wc, wx, cw, init):
    b, t, d = h3.shape
    c = wb.shape[1]
    tm = _tile(t, 1024, 16)
    tn = _tile(c, 512, LANES)
    wspec = pl.BlockSpec((d, tn), lambda j, bi, i: (0, j))
    return pl.pallas_call(
        _conv_kernel, grid=(c // tn, b, t // tm),
        in_specs=[pl.BlockSpec((1, tm, d), lambda j, bi, i: (bi, i, 0)), wspec, wspec, wspec,
                  pl.BlockSpec((CONV_K, tn), lambda j, bi, i: (0, j)),
                  pl.BlockSpec((1, CONV_K - 1, tn), lambda j, bi, i: (bi, 0, j))],
        out_specs=[pl.BlockSpec((1, tm, tn), lambda j, bi, i: (bi, i, j)),
                   pl.BlockSpec((1, CONV_K - 1, tn), lambda j, bi, i: (bi, 0, j))],
        out_shape=[jax.ShapeDtypeStruct((b, t, c), BF16),
                   jax.ShapeDtypeStruct((b, CONV_K - 1, c), F32)],
        scratch_shapes=[pltpu.VMEM((CONV_K - 1, tn), F32)],
        compiler_params=_cp("arbitrary", "arbitrary", "arbitrary"), name="conv_mixer")(
            h3, wb, wc, wx, cw, init)


def _stack_heads(q_ref, heads):
    return jnp.concatenate([q_ref[:, h * HEAD_DIM:(h + 1) * HEAD_DIM] for h in heads], axis=0)


LOG2E = 1.4426950408889634


def _with_ones_column(v):
    ones = jnp.where(lax.broadcasted_iota(jnp.int32, v.shape, 1) == 0, 1.0, 0.0).astype(v.dtype)
    return jnp.concatenate([v, ones], axis=1)


def _softmax_av(qk, scale, bias, v_ones):
    s = qk * (scale * LOG2E) + bias[None]
    e = jnp.exp2(s - jnp.max(s, axis=-1, keepdims=True))
    r, tq, n = e.shape
    o = _dot(e.reshape(r * tq, n).astype(BF16), v_ones)
    return o[:, :HEAD_DIM] / o[:, HEAD_DIM:HEAD_DIM + 1]


def _sort_key(x):
    bits = lax.bitcast_convert_type(x + 0.0, jnp.int32)
    return jnp.where(bits < 0, bits ^ jnp.int32(0x7FFFFFFF), bits)


INT_MIN = -2 ** 31


def _kth_largest_key(key, k):
    def body(it, othr):
        bit = lax.shift_left(jnp.int32(1), jnp.int32(31) - it)
        cand = othr | bit
        cnt = jnp.sum(jnp.where(key >= (cand ^ jnp.int32(INT_MIN)), 1.0, 0.0), axis=1, keepdims=True)
        return jnp.where(cnt >= k, cand, othr)
    othr = lax.fori_loop(0, 32, body, jnp.zeros((key.shape[0], 1), jnp.int32))
    return othr ^ jnp.int32(INT_MIN)


def _topk_mask(key, k, scratch_ref):
    thr = _kth_largest_key(key, k)
    live = key > jnp.int32(INT_MIN)
    ge = (key >= thr) & live
    scratch_ref[...] = jnp.where(ge, 1.0, 0.0)
    n_ge = jnp.sum(jnp.where(ge, 1.0, 0.0), axis=1, keepdims=True)

    @pl.when(jnp.max(n_ge) > k)
    def _():
        gt = key > thr
        eq = (key == thr) & live
        need = k - jnp.sum(jnp.where(gt, 1.0, 0.0), axis=1, keepdims=True)
        r = lax.broadcasted_iota(jnp.int32, (LANES, LANES), 0)
        c = lax.broadcasted_iota(jnp.int32, (LANES, LANES), 1)
        before = jnp.where(r < c, 1.0, 0.0).astype(BF16)
        run = jnp.zeros_like(need)
        for ch in range(key.shape[1] // LANES):
            sl = slice(ch * LANES, (ch + 1) * LANES)
            e = jnp.where(eq[:, sl], 1.0, 0.0)
            pre = _dot(e.astype(BF16), before) + run
            scratch_ref[:, sl] = jnp.where(gt[:, sl], 1.0, e * jnp.where(pre < need, 1.0, 0.0))
            run = run + jnp.sum(e, axis=1, keepdims=True)


CAUSAL_BUCKETS = 4


def _by_key_extent(i, tq, t, fn):
    nb = CAUSAL_BUCKETS if t % (CAUSAL_BUCKETS * tq) == 0 else 1
    size = t // nb
    for bkt in range(nb):
        @pl.when((i * tq) // size == bkt)
        def _():
            fn((bkt + 1) * size)


def _dsa_prompt_kernel(qi_ref, wi_ref, ki_ref, q_ref, k_ref, v_ref, o_ref,
                       kib_ref, kb_ref, vb_ref, mask_ref, *, topk):
    i = pl.program_id(1)

    @pl.when(i == 0)
    def _():
        kib_ref[...] = ki_ref[...].astype(BF16)
        kb_ref[...] = k_ref[...].astype(BF16)
        for g in range(KV_B):
            vb_ref[g] = _with_ones_column(v_ref[:, g * HEAD_DIM:(g + 1) * HEAD_DIM].astype(BF16))

    tq = q_ref.shape[0]
    t = ki_ref.shape[0]
    wi = wi_ref[...]
    rep = q_ref.shape[1] // HEAD_DIM // KV_B

    def attend(n):
        kib = kib_ref[0:n]
        score = jnp.zeros((tq, n), F32)
        for h in range(H_IDX):
            s = _dot_nt(qi_ref[:, h * D_IDX:(h + 1) * D_IDX], kib) * D_IDX ** -0.5
            score = score + jnp.maximum(s, 0.0) * wi[:, h:h + 1]
        score = score * H_IDX ** -0.5
        qpos = i * tq + lax.broadcasted_iota(jnp.int32, (tq, n), 0)
        kpos = lax.broadcasted_iota(jnp.int32, (tq, n), 1)
        key = jnp.where(kpos <= qpos, _sort_key(score), jnp.int32(INT_MIN))
        sel_ref = mask_ref.at[:, 0:n]
        _topk_mask(key, topk, sel_ref)
        bias = jnp.where(sel_ref[...] > 0.0, 0.0, NEG)
        for g in range(KV_B):
            qs = _stack_heads(q_ref, range(g * rep, (g + 1) * rep))
            sl = slice(g * HEAD_DIM, (g + 1) * HEAD_DIM)
            qk = _dot_nt(qs, kb_ref[0:n, sl]).reshape(rep, tq, n)
            o = _softmax_av(qk, HEAD_DIM ** -0.5, bias, vb_ref[g, 0:n])
            for r in range(rep):
                h = g * rep + r
                o_ref[:, h * HEAD_DIM:(h + 1) * HEAD_DIM] = o[r * tq:(r + 1) * tq].astype(o_ref.dtype)

    _by_key_extent(i, tq, t, attend)


def dsa_prompt(qi, wi, ki, q, k, v, b, t):
    m = q.shape[0]
    tq = _tile(t, 128, 16)
    nq = t // tq
    topk = min(TOPK_MAX, t // 4)
    qrow = lambda w: pl.BlockSpec((tq, w), lambda bi, i: (bi * nq + i, 0))
    full = lambda w: pl.BlockSpec((t, w), lambda bi, i: (bi, 0))
    return pl.pallas_call(
        functools.partial(_dsa_prompt_kernel, topk=topk), grid=(b, nq),
        in_specs=[qrow(qi.shape[1]), qrow(wi.shape[1]), full(ki.shape[1]),
                  qrow(q.shape[1]), full(k.shape[1]), full(v.shape[1])],
        out_specs=qrow(q.shape[1]),
        out_shape=jax.ShapeDtypeStruct((m, q.shape[1]), BF16),
        scratch_shapes=[pltpu.VMEM((t, ki.shape[1]), BF16), pltpu.VMEM((t, k.shape[1]), BF16),
                        pltpu.VMEM((KV_B, t, 2 * HEAD_DIM), BF16), pltpu.VMEM((tq, t), F32)],
        compiler_params=_cp("arbitrary", "arbitrary"), name="dsa_prompt")(qi, wi, ki, q, k, v)


def _masked_softmax(s, mask):
    m = jnp.max(jnp.where(mask, s, NEG), axis=-1, keepdims=True)
    m = jnp.where(m > 0.5 * NEG, m, 0.0)
    e = jnp.where(mask, jnp.exp(s - m), 0.0)
    return e / jnp.maximum(jnp.sum(e, axis=-1, keepdims=True), 1e-30)


def _pair_sums(imp, col):
    n = imp.shape[1]
    return imp + jnp.where(col % 2 == 0, pltpu.roll(imp, n - 1, 1), pltpu.roll(imp, 1, 1))


def _select_blocks(bs, col, qpos, n_selblk):
    blk = col // 2
    cur = qpos // SEL_BLOCK
    forced = (blk == 0) | (blk == cur) | (blk == cur - 1)
    admiss = (blk * SEL_BLOCK <= qpos) & (blk < n_selblk)
    work = jnp.where(admiss, jnp.where(forced, bs + FORCE_BONUS, bs), NEG)
    sel = jnp.zeros(bs.shape, jnp.bool_)
    big = jnp.int32(2 ** 30)
    for _ in range(min(N_SEL, n_selblk)):
        mx = jnp.max(work, axis=1, keepdims=True)
        idx = jnp.min(jnp.where(work == mx, col, big), axis=1, keepdims=True)
        pick = blk == idx // 2
        sel = sel | pick
        work = jnp.where(pick, -3e38, work)
    return jnp.where(sel & admiss, 1.0, 0.0)


def _nsa_prompt_kernel(q_ref, gc_ref, kc_ref, vc_ref, ks_ref, vs_ref, kw_ref, vw_ref,
                       cwk_ref, cwv_ref, o_ref,
                       kcmp_ref, vcmp_ref, ksb_ref, vsb_ref, kwb_ref, vwb_ref, *, win):
    i = pl.program_id(1)
    tq = q_ref.shape[0]
    t = kc_ref.shape[0]
    nb = t // CMP_BLOCK
    ncp = kcmp_ref.shape[0]
    n_selblk = -(-t // SEL_BLOCK)
    nh = q_ref.shape[1] // HEAD_DIM
    scale = HEAD_DIM ** -0.5

    @pl.when(i == 0)
    def _():
        kcmp_ref[...] = jnp.zeros(kcmp_ref.shape, kcmp_ref.dtype)
        vcmp_ref[...] = jnp.zeros(vcmp_ref.shape, vcmp_ref.dtype)
        kc = kc_ref[...].reshape(nb, CMP_BLOCK, HEAD_DIM)
        vc = vc_ref[...].reshape(nb, CMP_BLOCK, HEAD_DIM)
        kcmp_ref[0:nb] = jnp.sum(kc * cwk_ref[...][None], axis=1).astype(BF16)
        vcmp_ref[0:nb] = jnp.sum(vc * cwv_ref[...][None], axis=1).astype(BF16)
        ksb_ref[...] = ks_ref[...].astype(BF16)
        vsb_ref[...] = _with_ones_column(vs_ref[...].astype(BF16))
        kwb_ref[...] = kw_ref[...].astype(BF16)
        vwb_ref[...] = _with_ones_column(vw_ref[...].astype(BF16))

    qs = _stack_heads(q_ref, range(nh))
    col = lax.broadcasted_iota(jnp.int32, (tq, ncp), 1)
    qpos_c = i * tq + lax.broadcasted_iota(jnp.int32, (tq, ncp), 0)
    cmask = ((col + 1) * CMP_BLOCK - 1 <= qpos_c) & (col < nb)
    s = (_dot_nt(qs, kcmp_ref[...]) * scale).reshape(nh, tq, ncp)
    p = _masked_softmax(s, cmask[None])
    o_cmp = _dot(p.reshape(nh * tq, ncp).astype(BF16), vcmp_ref[...])
    imp = jnp.sum(p, axis=0)
    sel = _select_blocks(_pair_sums(imp, col), col, qpos_c, n_selblk).astype(BF16)
    start = pl.multiple_of(jnp.clip(i * tq - WINDOW, 0, t - win), 16)
    qpos_w = i * tq + lax.broadcasted_iota(jnp.int32, (tq, win), 0)
    kpos_w = start + lax.broadcasted_iota(jnp.int32, (tq, win), 1)
    bias_win = jnp.where((kpos_w <= qpos_w) & (kpos_w > qpos_w - WINDOW), 0.0, NEG)
    kwin = kwb_ref[pl.ds(start, win), :]
    vwin = vwb_ref[pl.ds(start, win), :]
    gate = jax.nn.sigmoid(gc_ref[...])
    grp = 4
    o_wins = []
    for hg in range(nh // grp):
        q4 = qs[hg * grp * tq:(hg + 1) * grp * tq]
        qk_win = _dot_nt(q4, kwin).reshape(grp, tq, win)
        o_wins.append(_softmax_av(qk_win, scale, bias_win, vwin))

    def attend(n):
        er = lax.broadcasted_iota(jnp.int32, (ncp, n), 0)
        ec = lax.broadcasted_iota(jnp.int32, (ncp, n), 1)
        expand = jnp.where(er == 2 * (ec // SEL_BLOCK), 1.0, 0.0).astype(BF16)
        tok = _dot(sel, expand)
        qpos = i * tq + lax.broadcasted_iota(jnp.int32, (tq, n), 0)
        kpos = lax.broadcasted_iota(jnp.int32, (tq, n), 1)
        bias_sel = jnp.where((tok > 0.5) & (kpos <= qpos), 0.0, NEG)
        for hg in range(nh // grp):
            q4 = qs[hg * grp * tq:(hg + 1) * grp * tq]
            qk_sel = _dot_nt(q4, ksb_ref[0:n]).reshape(grp, tq, n)
            o_sel = _softmax_av(qk_sel, scale, bias_sel, vsb_ref[0:n])
            o_win = o_wins[hg]
            for r in range(grp):
                h = hg * grp + r
                rows = slice(r * tq, (r + 1) * tq)
                o = (gate[:, 3 * h:3 * h + 1] * o_cmp[h * tq:(h + 1) * tq]
                     + gate[:, 3 * h + 1:3 * h + 2] * o_sel[rows] + gate[:, 3 * h + 2:3 * h + 3] * o_win[rows])
                o_ref[:, h * HEAD_DIM:(h + 1) * HEAD_DIM] = o.astype(o_ref.dtype)

    _by_key_extent(i, tq, t, attend)


def nsa_prompt(q, gc, kc, vc, ks, vs, kw, vw, cwk, cwv, b, t):
    m, hq = q.shape
    tq = _tile(t, 128, 16)
    nq = t // tq
    win = min(WINDOW + tq, t)
    ncp = LANES
    assert t // CMP_BLOCK <= ncp and t % SEL_BLOCK == 0
    qrow = lambda w: pl.BlockSpec((tq, w), lambda bi, i: (bi * nq + i, 0))
    full = pl.BlockSpec((t, HEAD_DIM), lambda bi, i: (bi, 0))
    cw = pl.BlockSpec((CMP_BLOCK, HEAD_DIM), lambda bi, i: (0, 0))
    kv = pltpu.VMEM((t, HEAD_DIM), BF16)
    kv_ones = pltpu.VMEM((t, 2 * HEAD_DIM), BF16)
    return pl.pallas_call(
        functools.partial(_nsa_prompt_kernel, win=win), grid=(b, nq),
        in_specs=[qrow(hq), qrow(gc.shape[1])] + [full] * 6 + [cw, cw],
        out_specs=qrow(hq), out_shape=jax.ShapeDtypeStruct((m, hq), BF16),
        scratch_shapes=[pltpu.VMEM((ncp, HEAD_DIM), BF16), pltpu.VMEM((ncp, HEAD_DIM), BF16),
                        kv, kv_ones, kv, kv_ones],
        compiler_params=_cp("arbitrary", "arbitrary"), name="nsa_prompt")(
            q, gc, kc, vc, ks, vs, kw, vw, cwk, cwv)


def _log_sigmoids(z):
    ls = jnp.minimum(z, 0.0) - jnp.log(1.0 + jnp.exp(-jnp.abs(z)))
    return ls, ls - z


def _split_bf16(x):
    hi = x.astype(BF16)
    return hi, (x - hi.astype(F32)).astype(BF16)


SB_HEADS_PER_STEP = 2


def _sb_prompt_kernel(q_ref, k_ref, v_ref, o_ref):
    i = pl.program_id(2)
    tq = q_ref.shape[0]
    nhs = q_ref.shape[1] // HEAD_DIM
    r = lax.broadcasted_iota(jnp.int32, (tq, tq), 0)
    c = lax.broadcasted_iota(jnp.int32, (tq, tq), 1)
    later = jnp.where(r > c, 1.0, 0.0).astype(BF16)
    before = c < r
    qs = [q_ref[:, h * HEAD_DIM:(h + 1) * HEAD_DIM] for h in range(nhs)]
    cols = lambda h: slice(h * HEAD_DIM, (h + 1) * HEAD_DIM)
    block_off = lambda j: pl.multiple_of(jnp.maximum(j, 0) * tq, tq)

    def scores(h, off):
        return _dot_nt(qs[h], k_ref[pl.ds(off, tq), cols(h)].astype(BF16)) * HEAD_DIM ** -0.5

    def weights(z, carry, m):
        ls, lneg = _log_sigmoids(z)
        if m is not None:
            lneg = jnp.where(m, lneg, 0.0)
        hi, lo = _split_bf16(lneg)
        after = _dot(hi, later) + _dot(lo, later)
        a = jnp.exp(ls + after + carry)
        if m is not None:
            a = jnp.where(m, a, 0.0)
        return a.astype(BF16), carry + after[:, 0:1] + lneg[:, 0:1]

    def weighted_values(h, a, off, acc):
        return acc + _dot(a, v_ref[pl.ds(off, tq), cols(h)].astype(BF16))

    st = []
    for h in range(nhs):
        a, carry = weights(scores(h, block_off(i)), jnp.zeros((tq, 1), F32), before)
        st.append((scores(h, block_off(i - 1)), a, carry, jnp.zeros((tq, HEAD_DIM), F32)))

    def body(jj, st):
        out = []
        for h in range(nhs):
            z, a_prev, carry, acc = st[h]
            z_next = scores(h, block_off(i - 2 - jj))
            acc = weighted_values(h, a_prev, block_off(i - jj), acc)
            a, carry = weights(z, carry, None)
            out.append((z_next, a, carry, acc))
        return tuple(out)

    st = lax.fori_loop(0, i, body, tuple(st))
    for h in range(nhs):
        acc = weighted_values(h, st[h][1], 0, st[h][3])
        o_ref[:, h * HEAD_DIM:(h + 1) * HEAD_DIM] = acc.astype(o_ref.dtype)


def sb_prompt(q, k, v, b, t):
    m, hq = q.shape
    nhs = SB_HEADS_PER_STEP
    tq = _tile(t, 256, 16)
    nq = t // tq
    w = nhs * HEAD_DIM
    qrow = pl.BlockSpec((tq, w), lambda bi, h, i: (bi * nq + i, h))
    full = pl.BlockSpec((t, w), lambda bi, h, i: (bi, h))
    return pl.pallas_call(
        _sb_prompt_kernel, grid=(b, hq // w, nq),
        in_specs=[qrow, full, full], out_specs=qrow,
        out_shape=jax.ShapeDtypeStruct((m, hq), BF16),
        compiler_params=_cp("arbitrary", "arbitrary", "arbitrary"), name="sb_prompt")(q, k, v)


def _mem_kernel(h_ref, x_ref, wq_ref, mk_ref, mv_ref, wo_ref, gp_ref, gn_ref, xo_ref, ho_ref):
    q = _dot(h_ref[0], wq_ref[...])
    outs = []
    for hh in range(H_MEM):
        sl = slice(hh * HEAD_DIM, (hh + 1) * HEAD_DIM)
        s = _dot_nt(q[:, sl].astype(BF16), mk_ref[0, :, sl].astype(BF16)) * HEAD_DIM ** -0.5
        e = jnp.exp(s - jnp.max(s, axis=-1, keepdims=True))
        o = _dot(e.astype(BF16), mv_ref[0, :, sl].astype(BF16))
        outs.append(o / jnp.sum(e, axis=-1, keepdims=True))
    y = _dot(jnp.concatenate(outs, axis=1).astype(BF16), wo_ref[...])
    xn = x_ref[0] + _rms(y, gp_ref[...])
    xo_ref[0] = xn
    ho_ref[0] = _rms(xn, gn_ref[...]).astype(ho_ref.dtype)


def mem_sublayer(h3, x3, wq, mk, mv, wo, g_post, g_next):
    b, t, d = h3.shape
    tq = _tile(t, 256, 16)
    nm, hm = mk.shape[1], mk.shape[2]
    row = pl.BlockSpec((1, tq, d), lambda bi, i: (bi, i, 0))
    mem = pl.BlockSpec((1, nm, hm), lambda bi, i: (bi, 0, 0))
    vec = pl.BlockSpec((1, d), lambda bi, i: (0, 0))
    return pl.pallas_call(
        _mem_kernel, grid=(b, t // tq),
        in_specs=[row, row, pl.BlockSpec((d, hm), lambda bi, i: (0, 0)), mem, mem,
                  pl.BlockSpec((hm, d), lambda bi, i: (0, 0)), vec, vec],
        out_specs=[row, row],
        out_shape=[jax.ShapeDtypeStruct((b, t, d), F32), jax.ShapeDtypeStruct((b, t, d), BF16)],
        compiler_params=_cp("arbitrary", "arbitrary"), name="mem_sublayer")(
            h3, x3, wq, mk, mv, wo, g_post.reshape(1, d), g_next.reshape(1, d))


def _ep_swiglu(accs, er, orf):
    g, u = accs
    orf[0][...] = ((g * jax.nn.sigmoid(g)) * u).astype(orf[0].dtype)


def ffn(h, wg, wu, wd, li):
    act = matmul([h], [wg, wu], [[(0, 0)], [(0, 1)]], _ep_swiglu, [BF16], tm_pref=2048,
                 single_buffer_x=True, layer=li, name="ffn_gate_up")
    return matmul([act], [wd], [[(0, 0)]], _ep_plain, [F32], tm_pref=512, layer=li, name="ffn_down")


def _odd_weights(w_in, w_out, nh_c, nh_d):
    sizes = [nh_c * HEAD_DIM] + [HEAD_DIM] * 6 + [nh_c * 3] + [nh_d * HEAD_DIM] * 3
    offs = np.cumsum([0] + sizes)
    cut = lambda a, b_: w_in[:, offs[a]:offs[b_]].astype(BF16)
    w = {"qc": cut(0, 1), "kv6": cut(1, 7), "gc": cut(7, 8), "qd": cut(8, 9), "kd": cut(9, 10),
         "vd": cut(10, 11)}
    w["out_c"] = w_out[:nh_c * HEAD_DIM].astype(BF16)
    w["out_d"] = w_out[nh_c * HEAD_DIM:].astype(BF16)
    return w


def _odd_project(h, w, tabs):
    cos, sin, nb = tabs
    qc = proj_rope(h, w["qc"], tabs, BF16, name="proj_qc")
    kv6 = matmul([h], [w["kv6"]], [[(0, 0)]], functools.partial(_ep_split, rope_chunks=(0, 2, 4)),
                 [F32] * 6, tables=(cos, sin), n_tab_blocks=nb, split_out=6, name="proj_kv6")
    gc = proj(h, w["gc"], name="proj_gc")
    qd = proj(h, w["qd"], BF16, name="proj_qd")
    kd = proj(h, w["kd"], name="proj_kd")
    vd = proj(h, w["vd"], name="proj_vd")
    return qc, kv6, gc, qd, kd, vd


def odd_mixer_prompt(h, b, t, w, cwk, cwv):
    m, d = h.shape
    tabs = rope_tables(jnp.arange(t), b, _tile(m, 1024, 16))
    qc, (kc, vc, ks, vs, kw, vw), gc, qd, kd, vd = _odd_project(h, w, tabs)
    o_c = nsa_prompt(qc, gc, kc, vc, ks, vs, kw, vw, cwk, cwv, b, t)
    o_d = sb_prompt(qd, kd, vd, b, t)
    y = proj_out(o_c, o_d, w["out_c"], w["out_d"])
    nw = min(WINDOW, t)
    st1 = lambda a: a.reshape(b, t, 1, HEAD_DIM)
    sth = lambda a: a.reshape(b, t, -1, HEAD_DIM)
    return y, (st1(kc), st1(vc), st1(ks), st1(vs), st1(kw)[:, t - nw:], st1(vw)[:, t - nw:],
               sth(kd), sth(vd))


def _even_weights(w_in, conv_w, w_out):
    c = conv_w.shape[1]
    hq = w_out.shape[0] - c
    sizes = [c, c, c, hq, KV_B * HEAD_DIM, KV_B * HEAD_DIM, H_IDX * D_IDX, D_IDX, H_IDX]
    offs = np.cumsum([0] + sizes)
    names = ["bg", "cg", "xa", "q", "k", "v", "qi", "ki", "wi"]
    w = {n: w_in[:, offs[j]:offs[j + 1]].astype(BF16) for j, n in enumerate(names)}
    w["conv_w"] = conv_w
    w["out_a"] = w_out[:c].astype(BF16)
    w["out_b"] = w_out[c:].astype(BF16)
    return w


def _even_project(h, w, tabs):
    q = proj_rope(h, w["q"], tabs, BF16, name="proj_q")
    k = proj_rope(h, w["k"], tabs, F32, name="proj_k")
    v = proj(h, w["v"], name="proj_v")
    qi = proj_rope(h, w["qi"], tabs, BF16, name="proj_qi")
    ki = proj_rope(h, w["ki"], tabs, F32, name="proj_ki")
    wi = proj(h, w["wi"], name="proj_wi")
    return q, k, v, qi, ki, wi


def even_mixer_prompt(h, b, t, w):
    m, d = h.shape
    tabs = rope_tables(jnp.arange(t), b, _tile(m, 1024, 16))
    init = jnp.zeros((b, CONV_K - 1, w["bg"].shape[1]), F32)
    ya, conv_state = conv_mixer(h.reshape(b, t, d), w["bg"], w["cg"], w["xa"], w["conv_w"], init)
    q, k, v, qi, ki, wi = _even_project(h, w, tabs)
    ob = dsa_prompt(qi, wi, ki, q, k, v, b, t)
    y = proj_out(ya.reshape(m, -1), ob, w["out_a"], w["out_b"])
    return y, (conv_state, k.reshape(b, t, KV_B, HEAD_DIM), v.reshape(b, t, KV_B, HEAD_DIM),
               ki.reshape(b, t, D_IDX))


def _paged_call(kern, grid, in_specs, out_specs, out_shape, scratch, name, page_table, args):
    gs = pltpu.PrefetchScalarGridSpec(num_scalar_prefetch=1, grid=grid, in_specs=in_specs,
                                      out_specs=out_specs, scratch_shapes=scratch)
    return pl.pallas_call(kern, grid_spec=gs, out_shape=out_shape,
                          compiler_params=_cp("arbitrary", "arbitrary"), name=name)(page_table, *args)


def _pad_rows(a, rows):
    return jnp.pad(a, ((0, 0), (0, rows - a.shape[1])) + ((0, 0),) * (a.ndim - 2))


def _page_specs(block_tail, layer, g_pages, page_of):
    zeros = (0,) * len(block_tail)

    def spec(j):
        return pl.BlockSpec((1, 1) + block_tail,
                            lambda bi, p, pt: (layer, page_of(bi, p, pt, j)) + zeros)
    return [spec(j) for j in range(g_pages)]


def _forward_pages(g_pages):
    return lambda bi, p, pt, j: pt[bi, p * g_pages + j]


def _rows_ht(a, b, t, nh):
    w = a.shape[1] // nh
    return a.reshape(b, t, nh, w).transpose(0, 2, 1, 3).reshape(b, nh * t, w)


def _rows_th(a, b, t, nh):
    w = a.shape[2]
    return a.reshape(b, nh, t, w).transpose(0, 2, 1, 3).reshape(b * t, nh * w)


def _dsa_scores_kernel(pt_ref, qi_ref, wi_ref, new_ref, *rest, g_pages, t):
    pools, (o_ref, onew_ref) = rest[:g_pages], rest[g_pages:]
    qi, wi = qi_ref[0], wi_ref[0]

    def scores(kb):
        s = _dot_nt(qi, kb) * D_IDX ** -0.5
        s = jnp.maximum(s, 0.0) * wi
        return jnp.sum(s.reshape(H_IDX, t, kb.shape[0]), axis=0) * H_IDX ** -0.5

    o_ref[0] = scores(jnp.concatenate([r[0, 0] for r in pools], axis=0).astype(BF16))

    @pl.when(pl.program_id(1) == 0)
    def _():
        onew_ref[0] = scores(new_ref[0].astype(BF16))


def dsa_sample_scores(qi_r, wi_r, ki_new, pool, e, page_table, t):
    b, n_pages = page_table.shape
    g = _tile(n_pages, 8, 1)
    kern = functools.partial(_dsa_scores_kernel, g_pages=g, t=t)
    per_b = lambda r, w: pl.BlockSpec((1, r, w), lambda bi, p, pt: (bi, 0, 0))
    past, new = _paged_call(
        kern, (b, n_pages // g),
        [per_b(H_IDX * t, D_IDX), per_b(H_IDX * t, 1), per_b(PAGE_SIZE, D_IDX)]
        + _page_specs((PAGE_SIZE, D_IDX), e, g, _forward_pages(g)),
        [pl.BlockSpec((1, t, g * PAGE_SIZE), lambda bi, p, pt: (bi, 0, p)), per_b(t, PAGE_SIZE)],
        [jax.ShapeDtypeStruct((b, t, n_pages * PAGE_SIZE), F32),
         jax.ShapeDtypeStruct((b, t, PAGE_SIZE), F32)], [], "dsa_sample_scores",
        page_table, (qi_r, wi_r, _pad_rows(ki_new, PAGE_SIZE)) + (pool,) * g)
    return jnp.concatenate([past, new], axis=-1)


def _dsa_topk_kernel(s_ref, o_ref, mask_ref, *, past, topk):
    score = s_ref[0]
    kpos = lax.broadcasted_iota(jnp.int32, score.shape, 1)
    qpos = past + lax.broadcasted_iota(jnp.int32, score.shape, 0)
    key = jnp.where(kpos <= qpos, _sort_key(score), jnp.int32(INT_MIN))
    _topk_mask(key, topk, mask_ref)
    o_ref[0] = mask_ref[...]


def dsa_sample_topk(score, past, topk):
    b, t, nk = score.shape
    blk = pl.BlockSpec((1, t, nk), lambda bi: (bi, 0, 0))
    return pl.pallas_call(
        functools.partial(_dsa_topk_kernel, past=past, topk=topk), grid=(b,),
        in_specs=[blk], out_specs=blk, out_shape=jax.ShapeDtypeStruct((b, t, nk), F32),
        scratch_shapes=[pltpu.VMEM((t, nk), F32)],
        compiler_params=_cp("arbitrary"), name="dsa_sample_topk")(score)


def _online_softmax_step(s, valid, v, m_ref, l_ref, acc_ref):
    m_old = m_ref[...]
    m_new = jnp.maximum(m_old, jnp.max(jnp.where(valid, s, NEG), axis=-1, keepdims=True))
    alpha = jnp.exp(m_old - m_new)
    e = jnp.where(valid, jnp.exp(s - m_new), 0.0)
    l_ref[...] = alpha * l_ref[...] + jnp.sum(e, axis=-1, keepdims=True)
    acc_ref[...] = alpha * acc_ref[...] + _dot(e.astype(BF16), v)
    m_ref[...] = m_new


def _init_softmax_state(m_ref, l_ref, acc_ref):
    m_ref[...] = jnp.full(m_ref.shape, NEG, F32)
    l_ref[...] = jnp.zeros(l_ref.shape, F32)
    acc_ref[...] = jnp.zeros(acc_ref.shape, F32)


def _dsa_sample_attn_kernel(pt_ref, qt_ref, mask_ref, masknew_ref, knew_ref, vnew_ref, *rest,
                            g_pages, t):
    kps, vps = rest[:g_pages], rest[g_pages:2 * g_pages]
    o_ref, m_ref, l_ref, acc_ref, z_ref = rest[2 * g_pages:]
    p = pl.program_id(1)
    n, ng = PAGE_SIZE, KV_B
    cols = qt_ref.shape[2]
    per_group = cols // ng

    @pl.when(p == 0)
    def _():
        _init_softmax_state(m_ref, l_ref, acc_ref)

    iota = lambda shape, d: lax.broadcasted_iota(jnp.int32, shape, d)
    own_group = iota((n * ng, cols), 0) % ng == iota((n * ng, cols), 1) // per_group
    spread = jnp.where(iota((n, n * ng), 1) // ng == iota((n, n * ng), 0), 1.0, 0.0).astype(BF16)
    own_rows = iota((cols, n * ng), 0) // per_group == iota((cols, n * ng), 1) % ng

    def scores(k2, slot):
        z_ref[slot] = jnp.where(own_group, _dot(k2.astype(BF16), qt_ref[0]), 0.0)
        z = z_ref[slot, pl.ds(0, n, stride=ng), :]
        for g in range(1, ng):
            z = z + z_ref[slot, pl.ds(g, n, stride=ng), :]
        return z.T

    def weighted_values(e, v2):
        a2 = jnp.where(own_rows, _dot(e.astype(BF16), spread), 0.0)
        return _dot(a2.astype(BF16), v2.astype(BF16))

    def update(k2s, v2s, valid_t, first_slot):
        s = jnp.concatenate([scores(k2, first_slot + j) for j, k2 in enumerate(k2s)], axis=1)
        s = s * HEAD_DIM ** -0.5
        valid = jnp.concatenate([valid_t] * (cols // t), axis=0)
        m_old = m_ref[...]
        m_new = jnp.maximum(m_old, jnp.max(jnp.where(valid, s, NEG), axis=-1, keepdims=True))
        alpha = jnp.exp(m_old - m_new)
        e = jnp.where(valid, jnp.exp(s - m_new), 0.0)
        l_ref[...] = alpha * l_ref[...] + jnp.sum(e, axis=-1, keepdims=True)
        pv = None
        for j, v2 in enumerate(v2s):
            d = weighted_values(e[:, j * n:(j + 1) * n], v2)
            pv = d if pv is None else pv + d
        acc_ref[...] = alpha * acc_ref[...] + pv
        m_ref[...] = m_new

    update([r[...] for r in kps], [r[...] for r in vps], mask_ref[0] > 0.5, 0)

    @pl.when(p == pl.num_programs(1) - 1)
    def _():
        update([knew_ref[0]], [vnew_ref[0]], masknew_ref[0] > 0.5, g_pages)
        o_ref[0] = acc_ref[...] / l_ref[...]


def dsa_sample_attn(q_r, mask, kpool, vpool, k_new, v_new, e, page_table):
    b, n_pages = page_table.shape
    t = mask.shape[1]
    cols = q_r.shape[1]
    g = _tile(n_pages, 8, 1)
    rows = PAGE_SIZE * KV_B
    flat = lambda pool: pool.reshape(pool.shape[0], -1, HEAD_DIM)
    new_rows = lambda a: _pad_rows(a, PAGE_SIZE).reshape(b, rows, HEAD_DIM)
    per_b = lambda r, w: pl.BlockSpec((1, r, w), lambda bi, p, pt: (bi, 0, 0))
    pools = [pl.BlockSpec((None, rows, HEAD_DIM), functools.partial(
        lambda bi, p, pt, j: (e, pt[bi, p * g + j], 0), j=j)) for j in range(g)]
    return _paged_call(
        functools.partial(_dsa_sample_attn_kernel, g_pages=g, t=t), (b, n_pages // g),
        [per_b(HEAD_DIM, cols), pl.BlockSpec((1, t, g * PAGE_SIZE), lambda bi, p, pt: (bi, 0, p)),
         pl.BlockSpec((1, t, PAGE_SIZE), lambda bi, p, pt: (bi, 0, n_pages)),
         per_b(rows, HEAD_DIM), per_b(rows, HEAD_DIM)] + pools + pools,
        per_b(cols, HEAD_DIM), jax.ShapeDtypeStruct((b, cols, HEAD_DIM), F32),
        [pltpu.VMEM((cols, 1), F32), pltpu.VMEM((cols, 1), F32), pltpu.VMEM((cols, HEAD_DIM), F32),
         pltpu.VMEM((g + 1, rows, cols), F32)],
        "dsa_sample_attn", page_table,
        (q_r.transpose(0, 2, 1), mask, mask, new_rows(k_new), new_rows(v_new))
        + (flat(kpool),) * g + (flat(vpool),) * g)


def even_mixer_sample(h, b, t, w, e, page_table, state_conv, c_k, c_v, c_kidx):
    m, d = h.shape
    n_pages = page_table.shape[1]
    past = n_pages * PAGE_SIZE
    tabs = rope_tables(past + jnp.arange(t), b, _tile(m, 1024, 16))
    ya, conv_state = conv_mixer(h.reshape(b, t, d), w["bg"], w["cg"], w["xa"], w["conv_w"], state_conv[e])
    q, k, v, qi, ki, wi = _even_project(h, w, tabs)
    score = dsa_sample_scores(_rows_ht(qi, b, t, H_IDX), _rows_ht(wi, b, t, H_IDX),
                              ki.reshape(b, t, D_IDX), c_kidx, e, page_table, t)
    mask = dsa_sample_topk(score, past, min(TOPK_MAX, (past + t) // 4))
    nh = q.shape[1] // HEAD_DIM
    o = dsa_sample_attn(_rows_ht(q, b, t, nh), mask, c_k, c_v, k.reshape(b, t, KV_B, HEAD_DIM),
                        v.reshape(b, t, KV_B, HEAD_DIM), e, page_table)
    ob = _rows_th(o, b, t, nh).astype(BF16)
    y = proj_out(ya.reshape(m, -1), ob, w["out_a"], w["out_b"])
    return y, (conv_state, k.reshape(b, t, KV_B, HEAD_DIM), v.reshape(b, t, KV_B, HEAD_DIM),
               ki.reshape(b, t, D_IDX))


def _compress_pages_kernel(pt_ref, cwk_ref, cwv_ref, *rest, g_pages):
    kcs, vcs = rest[:g_pages], rest[g_pages:2 * g_pages]
    ko_ref, vo_ref = rest[2 * g_pages:]
    nb = PAGE_SIZE // CMP_BLOCK
    cwk, cwv = cwk_ref[...][None], cwv_ref[...][None]
    for j in range(g_pages):
        ko_ref[0, j] = jnp.sum(kcs[j][0, 0].reshape(nb, CMP_BLOCK, HEAD_DIM) * cwk, axis=1)
        vo_ref[0, j] = jnp.sum(vcs[j][0, 0].reshape(nb, CMP_BLOCK, HEAD_DIM) * cwv, axis=1)


def nsa_compress_pages(kpool, vpool, cwk, cwv, o, page_table):
    b, n_pages = page_table.shape
    nb = PAGE_SIZE // CMP_BLOCK
    g = _tile(n_pages, 8, 1)
    pools = _page_specs((PAGE_SIZE, HEAD_DIM), o, g, _forward_pages(g))
    cw = pl.BlockSpec((CMP_BLOCK, HEAD_DIM), lambda bi, p, pt: (0, 0))
    out = pl.BlockSpec((1, g, nb, HEAD_DIM), lambda bi, p, pt: (bi, p, 0, 0))
    shp = jax.ShapeDtypeStruct((b, n_pages, nb, HEAD_DIM), F32)
    kc, vc = _paged_call(functools.partial(_compress_pages_kernel, g_pages=g), (b, n_pages // g),
                         [cw, cw] + pools + pools, [out, out], [shp, shp], [], "nsa_compress_pages",
                         page_table, (cwk, cwv) + (kpool,) * g + (vpool,) * g)
    return kc.reshape(b, n_pages * nb, HEAD_DIM), vc.reshape(b, n_pages * nb, HEAD_DIM)


def _nsa_sample_a_kernel(q_ref, kcmp_ref, vcmp_ref, kw_ref, vw_ref, ocmp_ref, owin_ref, sel_ref,
                         *, past, t, nwin, n_selblk):
    q = q_ref[0]
    rows = q.shape[0]
    nh = rows // t
    scale = HEAD_DIM ** -0.5
    nb = kcmp_ref.shape[1]
    col = lax.broadcasted_iota(jnp.int32, (rows, nb), 1)
    pos = past + lax.broadcasted_iota(jnp.int32, (rows, nb), 0) % t
    s = _dot_nt(q, kcmp_ref[0].astype(BF16)) * scale
    p = _masked_softmax(s, (col + 1) * CMP_BLOCK - 1 <= pos)
    ocmp_ref[0] = _dot(p.astype(BF16), vcmp_ref[0].astype(BF16))
    imp = jnp.sum(p.reshape(nh, t, nb), axis=0)
    ncol = sel_ref.shape[2]
    imp = jnp.concatenate([imp, jnp.zeros((t, ncol - nb), F32)], axis=1)
    col_s = lax.broadcasted_iota(jnp.int32, (t, ncol), 1)
    pos_s = past + lax.broadcasted_iota(jnp.int32, (t, ncol), 0)
    sel_ref[0] = _select_blocks(_pair_sums(imp, col_s), col_s, pos_s, n_selblk)
    nwp = kw_ref.shape[1]
    colw = lax.broadcasted_iota(jnp.int32, (rows, nwp), 1)
    posw = past + lax.broadcasted_iota(jnp.int32, (rows, nwp), 0) % t
    kwpos = past + t - nwin + colw
    valid = (kwpos <= posw) & (kwpos > posw - WINDOW) & (colw < nwin)
    sw = _dot_nt(q, kw_ref[0].astype(BF16)) * scale
    pw = _masked_softmax(sw, valid)
    owin_ref[0] = _dot(pw.astype(BF16), vw_ref[0].astype(BF16))


def nsa_sample_a(q_r, kcmp, vcmp, kw_pad, vw_pad, past, t, nwin):
    b, rows, _ = q_r.shape
    nb = kcmp.shape[1]
    n_selblk = -(-(past + t) // SEL_BLOCK)
    ncol = -(-2 * n_selblk // LANES) * LANES
    assert ncol > nb >= 2 * n_selblk - 2 and nb % LANES == 0
    per_b = lambda r, w: pl.BlockSpec((1, r, w), lambda bi: (bi, 0, 0))
    kern = functools.partial(_nsa_sample_a_kernel, past=past, t=t, nwin=nwin, n_selblk=n_selblk)
    return pl.pallas_call(
        kern, grid=(b,),
        in_specs=[per_b(rows, HEAD_DIM), per_b(nb, HEAD_DIM), per_b(nb, HEAD_DIM),
                  per_b(kw_pad.shape[1], HEAD_DIM), per_b(kw_pad.shape[1], HEAD_DIM)],
        out_specs=[per_b(rows, HEAD_DIM), per_b(rows, HEAD_DIM), per_b(t, ncol)],
        out_shape=[jax.ShapeDtypeStruct((b, rows, HEAD_DIM), F32)] * 2
        + [jax.ShapeDtypeStruct((b, t, ncol), F32)],
        compiler_params=_cp("arbitrary"), name="nsa_sample_a")(q_r, kcmp, vcmp, kw_pad, vw_pad)


def _nsa_sample_b_kernel(pt_ref, q_ref, tok_ref, toknew_ref, knew_ref, vnew_ref,
                         ocmp_ref, owin_ref, gate_ref, *rest, g_pages, n_pages, t):
    kps, vps = rest[:g_pages], rest[g_pages:2 * g_pages]
    o_ref, m_ref, l_ref, acc_ref = rest[2 * g_pages:]
    p = pl.program_id(1)

    @pl.when(p == 0)
    def _():
        _init_softmax_state(m_ref, l_ref, acc_ref)

    q = q_ref[0]
    rows = q.shape[0]

    def update(keys, vals, tok, first_key):
        n = keys.shape[0]
        kpos = first_key + lax.broadcasted_iota(jnp.int32, (rows, n), 1)
        qpos = n_pages * PAGE_SIZE + lax.broadcasted_iota(jnp.int32, (rows, n), 0) % t
        valid = jnp.concatenate([tok > 0.5] * (rows // t), axis=0) & (kpos <= qpos)
        s = _dot_nt(q, keys.astype(BF16)) * HEAD_DIM ** -0.5
        _online_softmax_step(s, valid, vals.astype(BF16), m_ref, l_ref, acc_ref)

    update(jnp.concatenate([r[0, 0] for r in kps], axis=0),
           jnp.concatenate([r[0, 0] for r in vps], axis=0), tok_ref[0], p * (g_pages * PAGE_SIZE))

    @pl.when(p == pl.num_programs(1) - 1)
    def _():
        update(knew_ref[0], vnew_ref[0], toknew_ref[0], n_pages * PAGE_SIZE)
        g = jax.nn.sigmoid(gate_ref[0])
        o_ref[0] = (g[:, 0:1] * ocmp_ref[0] + g[:, 1:2] * (acc_ref[...] / l_ref[...])
                    + g[:, 2:3] * owin_ref[0])


def nsa_sample_b(q_r, tok, kpool, vpool, ks_new, vs_new, o_cmp, o_win, gate_r, o, page_table):
    b, n_pages = page_table.shape
    rows = q_r.shape[1]
    t = tok.shape[1]
    g = _tile(n_pages, 8, 1)
    per_b = lambda r, w: pl.BlockSpec((1, r, w), lambda bi, p, pt: (bi, 0, 0))
    pools = _page_specs((PAGE_SIZE, HEAD_DIM), o, g, _forward_pages(g))
    kern = functools.partial(_nsa_sample_b_kernel, g_pages=g, n_pages=n_pages, t=t)
    return _paged_call(
        kern, (b, n_pages // g),
        [per_b(rows, HEAD_DIM), pl.BlockSpec((1, t, g * PAGE_SIZE), lambda bi, p, pt: (bi, 0, p)),
         pl.BlockSpec((1, t, PAGE_SIZE), lambda bi, p, pt: (bi, 0, n_pages)),
         per_b(PAGE_SIZE, HEAD_DIM), per_b(PAGE_SIZE, HEAD_DIM),
         per_b(rows, HEAD_DIM), per_b(rows, HEAD_DIM), per_b(rows, 3)] + pools + pools,
        per_b(rows, HEAD_DIM), jax.ShapeDtypeStruct((b, rows, HEAD_DIM), F32),
        [pltpu.VMEM((rows, 1), F32), pltpu.VMEM((rows, 1), F32), pltpu.VMEM((rows, HEAD_DIM), F32)],
        "nsa_sample_b", page_table,
        (q_r, tok, tok, _pad_rows(ks_new, PAGE_SIZE), _pad_rows(vs_new, PAGE_SIZE),
         o_cmp, o_win, gate_r) + (kpool,) * g + (vpool,) * g)


def _sb_sample_kernel(pt_ref, qt2_ref, knew_ref, vnew_ref, *rest, g_pages, t):
    kps, vps = rest[:g_pages], rest[g_pages:2 * g_pages]
    o_ref, carry_ref, acc_ref, z_ref = rest[2 * g_pages:]
    p = pl.program_id(1)
    n, nh = knew_ref.shape[1], knew_ref.shape[2]
    cols = nh * t

    @pl.when(p == 0)
    def _():
        carry_ref[...] = jnp.zeros(carry_ref.shape, F32)
        acc_ref[...] = jnp.zeros(acc_ref.shape, F32)

    iota = lambda shape, d: lax.broadcasted_iota(jnp.int32, shape, d)
    later = jnp.where(iota((n, n), 1) > iota((n, n), 0), 1.0, 0.0).astype(BF16)
    own_head = iota((nh, cols), 1) // t == iota((nh, cols), 0)
    spread = jnp.where(iota((n, n * nh), 1) // nh == iota((n, n * nh), 0), 1.0, 0.0).astype(BF16)
    own_rows = iota((cols, n * nh), 0) // t == iota((cols, n * nh), 1) % nh

    def all_scores(k3s):
        flat = [k3.reshape(n * nh, HEAD_DIM).astype(BF16) for k3 in k3s]
        out = []
        for j in range(0, len(flat) - 1, 2):
            both = _dot(jnp.concatenate(flat[j:j + 2], axis=1), qt2_ref[0])
            out += [both[:, :cols], both[:, cols:]]
        if len(flat) % 2:
            out.append(_dot(flat[-1], qt2_ref[0, :HEAD_DIM, :cols]))
        return out

    def log_weights(z_all, slot, m):
        z_all = z_all.reshape(n, nh, cols)
        z_ref[slot] = jnp.sum(jnp.where(own_head[None], z_all, 0.0), axis=1)
        ls, lneg = _log_sigmoids(z_ref[slot] * HEAD_DIM ** -0.5)
        if m is not None:
            lneg = jnp.where(m, lneg, 0.0)
        hi, lo = _split_bf16(lneg)
        after = _dot(later, hi) + _dot(later, lo)
        return ls + after, after[0:1] + lneg[0:1]

    def weighted_values(v3, log_a, m):
        a = jnp.exp(log_a)
        if m is not None:
            a = jnp.where(m, a, 0.0)
        a2 = jnp.where(own_rows, _dot(a.T.astype(BF16), spread), 0.0)
        return _dot(a2.astype(BF16), v3.reshape(n * nh, HEAD_DIM).astype(BF16))

    @pl.when(p == 0)
    def _():
        m = iota((n, cols), 0) < iota((n, cols), 1) % t
        log_a, total = log_weights(all_scores([knew_ref[0]])[0], g_pages, m)
        acc_ref[...] += weighted_values(vnew_ref[0], log_a, m)
        carry_ref[...] += total

    z_alls = all_scores([kps[j][0, 0] for j in range(g_pages)])
    parts = [log_weights(z_alls[j], j, None) for j in range(g_pages)]
    carry = carry_ref[...]
    acc = acc_ref[...]
    for j in range(g_pages):
        acc = acc + weighted_values(vps[j][0, 0], parts[j][0] + carry, None)
        carry = carry + parts[j][1]
    carry_ref[...] = carry
    acc_ref[...] = acc

    @pl.when(p == pl.num_programs(1) - 1)
    def _():
        o_ref[0] = acc_ref[...]


def sb_sample(qd, kd_new, vd_new, kpool, vpool, o, page_table, b, t):
    n_pages = page_table.shape[1]
    nh = qd.shape[1] // HEAD_DIM
    g = _tile(n_pages, 4, 1)
    qt = qd.reshape(b, t, nh, HEAD_DIM).transpose(0, 3, 2, 1).reshape(b, HEAD_DIM, nh * t)
    zq = jnp.zeros_like(qt)
    qt2 = jnp.concatenate([jnp.concatenate([qt, zq], axis=2), jnp.concatenate([zq, qt], axis=2)], axis=1)
    new = pl.BlockSpec((1, PAGE_SIZE, nh, HEAD_DIM), lambda bi, p, pt: (bi, 0, 0, 0))
    pools = _page_specs((PAGE_SIZE, nh, HEAD_DIM), o, g,
                        lambda bi, p, pt, j: pt[bi, n_pages - 1 - (p * g + j)])
    out = pl.BlockSpec((1, nh * t, HEAD_DIM), lambda bi, p, pt: (bi, 0, 0))
    return _paged_call(
        functools.partial(_sb_sample_kernel, g_pages=g, t=t), (b, n_pages // g),
        [pl.BlockSpec((1, 2 * HEAD_DIM, 2 * nh * t), lambda bi, p, pt: (bi, 0, 0)), new, new]
        + pools + pools,
        out, jax.ShapeDtypeStruct((b, nh * t, HEAD_DIM), F32),
        [pltpu.VMEM((1, nh * t), F32), pltpu.VMEM((nh * t, HEAD_DIM), F32),
         pltpu.VMEM((g + 1, PAGE_SIZE, nh * t), F32)],
        "sb_sample", page_table,
        (qt2, _pad_rows(kd_new, PAGE_SIZE), _pad_rows(vd_new, PAGE_SIZE)) + (kpool,) * g + (vpool,) * g)


def odd_mixer_sample(h, b, t, w, cwk, cwv, o, page_table, c_kc, c_vc, c_ks, c_vs, c_kw, c_vw,
                     c_kd, c_vd):
    m, d = h.shape
    n_pages = page_table.shape[1]
    past = n_pages * PAGE_SIZE
    assert past % CMP_BLOCK == 0 and t < CMP_BLOCK
    tabs = rope_tables(past + jnp.arange(t), b, _tile(m, 1024, 16))
    qc, (kc, vc, ks, vs, kw, vw), gc, qd, kd, vd = _odd_project(h, w, tabs)
    nh = qc.shape[1] // HEAD_DIM
    n_pool = c_kc.shape[1]
    pool1 = lambda a: a.reshape(-1, n_pool, PAGE_SIZE, HEAD_DIM)
    seq = lambda a: a.reshape(b, t, -1)
    kcmp, vcmp = nsa_compress_pages(pool1(c_kc), pool1(c_vc), cwk, cwv, o, page_table)
    wb = c_kw.shape[2]
    kw_all = jnp.concatenate([c_kw[o].reshape(b, wb, HEAD_DIM), seq(kw)], axis=1)
    vw_all = jnp.concatenate([c_vw[o].reshape(b, wb, HEAD_DIM), seq(vw)], axis=1)
    nwp = -(-(wb + t) // LANES) * LANES
    q_r = _rows_ht(qc, b, t, nh)
    o_cmp, o_win, sel = nsa_sample_a(q_r, kcmp, vcmp, _pad_rows(kw_all, nwp), _pad_rows(vw_all, nwp),
                                     past, t, wb + t)
    n_selblk = -(-(past + t) // SEL_BLOCK)
    tok = jnp.repeat(sel[:, :, 0:2 * n_selblk:2], SEL_BLOCK, axis=-1)
    tok = jnp.pad(tok, ((0, 0), (0, 0), (0, (n_pages + 1) * PAGE_SIZE - tok.shape[-1])))
    gate_r = _rows_ht(gc, b, t, nh)
    o_c = nsa_sample_b(q_r, tok, pool1(c_ks), pool1(c_vs), seq(ks), seq(vs), o_cmp, o_win, gate_r,
                       o, page_table)
    nh_d = kd.shape[1] // HEAD_DIM
    heads = lambda a: a.reshape(b, t, nh_d, HEAD_DIM)
    o_d = sb_sample(qd, heads(kd), heads(vd), c_kd, c_vd, o, page_table, b, t)
    y = proj_out(_rows_th(o_c, b, t, nh).astype(BF16), _rows_th(o_d, b, t, nh_d).astype(BF16),
                 w["out_c"], w["out_d"])
    st1 = lambda a: a.reshape(b, t, 1, HEAD_DIM)
    return y, (st1(kc), st1(vc), st1(ks), st1(vs), kw_all[:, t:].reshape(b, wb, 1, HEAD_DIM),
               vw_all[:, t:].reshape(b, wb, 1, HEAD_DIM), kd.reshape(b, t, -1, HEAD_DIM),
               vd.reshape(b, t, -1, HEAD_DIM))


def kernel(x_prompt, x_sample, state_conv, cache_dsa_k, cache_dsa_v, cache_dsa_kidx, cache_nsa_kc, cache_nsa_vc, cache_nsa_ks, cache_nsa_vs, cache_nsa_kw, cache_nsa_vw, cache_sb_k, cache_sb_v, cache_mem_k, cache_mem_v, page_table, mem_prompt, norm_pre, norm_post, norm_mem, w_in_even, conv_w, w_out_even, w_in_odd, cmp_wk, cmp_wv, w_out_odd, w_mq, w_mk, w_mv, w_mo, w_gate, w_up, w_down):
    bp, tp, d = x_prompt.shape
    bs, ts, _ = x_sample.shape
    depth = norm_pre.shape[0]
    n_mem = mem_prompt.shape[1]
    hm = w_mq.shape[2]
    nh_c = nh_d = w_out_odd.shape[1] // 2 // HEAD_DIM
    xp = x_prompt.reshape(bp * tp, d)
    xs = x_sample.reshape(bs * ts, d)
    hp = norm_cast(xp, norm_pre[0, 0])
    hs = norm_cast(xs, norm_pre[0, 0])
    ev_p, ev_s, od_p, od_s, mem_p = [], [], [], [], []
    wg, wu, wd = w_gate, w_up, w_down.astype(BF16)
    for li in range(depth):
        g_pre, g_post = norm_pre[li], norm_post[li]
        if li % 2 == 0:
            e = li // 2
            w = _even_weights(w_in_even[e], conv_w[e], w_out_even[e])
            mp, stp = even_mixer_prompt(hp, bp, tp, w)
            ms, sts = even_mixer_sample(hs, bs, ts, w, e, page_table, state_conv,
                                        cache_dsa_k, cache_dsa_v, cache_dsa_kidx)
            ev_p.append(stp)
            ev_s.append(sts)
        else:
            o = li // 2
            w = _odd_weights(w_in_odd[o], w_out_odd[o], nh_c, nh_d)
            mp, stp = odd_mixer_prompt(hp, bp, tp, w, cmp_wk[o], cmp_wv[o])
            ms, sts = odd_mixer_sample(hs, bs, ts, w, cmp_wk[o], cmp_wv[o], o, page_table,
                                       cache_nsa_kc, cache_nsa_vc, cache_nsa_ks, cache_nsa_vs,
                                       cache_nsa_kw, cache_nsa_vw, cache_sb_k, cache_sb_v)
            od_p.append(stp)
            od_s.append(sts)
        xp, hp = resid_norm(xp, mp, g_post[0], g_pre[1])
        xs, hs = resid_norm(xs, ms, g_post[0], g_pre[1])
        wq, wo = w_mq[li].astype(BF16), w_mo[li].astype(BF16)
        hmem = norm_cast(mem_prompt.reshape(bp * n_mem, d), norm_mem[li])
        mkp = proj(hmem, w_mk[li].astype(BF16), name="proj_mk")
        mvp = proj(hmem, w_mv[li].astype(BF16), name="proj_mv")
        mem_p.append((mkp.reshape(bp, n_mem, H_MEM, HEAD_DIM), mvp.reshape(bp, n_mem, H_MEM, HEAD_DIM)))
        xp, hp = mem_sublayer(hp.reshape(bp, tp, d), xp.reshape(bp, tp, d), wq, mkp.reshape(bp, n_mem, hm),
                              mvp.reshape(bp, n_mem, hm), wo, g_post[1], g_pre[2])
        xs, hs = mem_sublayer(hs.reshape(bs, ts, d), xs.reshape(bs, ts, d), wq,
                              cache_mem_k[li].reshape(bs, n_mem, hm),
                              cache_mem_v[li].reshape(bs, n_mem, hm), wo, g_post[1], g_pre[2])
        xp, hp = xp.reshape(bp * tp, d), hp.reshape(bp * tp, d)
        xs, hs = xs.reshape(bs * ts, d), hs.reshape(bs * ts, d)
        g_next = norm_pre[li + 1, 0] if li + 1 < depth else None
        xp, hp = resid_norm(xp, ffn(hp, wg, wu, wd, li), g_post[2], g_next)
        xs, hs = resid_norm(xs, ffn(hs, wg, wu, wd, li), g_post[2], g_next)
    stack = lambda lst, j: jnp.stack([s[j] for s in lst])
    return ((xp.reshape(bp, tp, d), xs.reshape(bs, ts, d))
            + tuple(stack(ev_p, j) for j in range(4)) + tuple(stack(od_p, j) for j in range(8))
            + (stack(mem_p, 0), stack(mem_p, 1))
            + tuple(stack(ev_s, j) for j in range(4)) + tuple(stack(od_s, j) for j in range(8)))
```

```python
import functools

import numpy as np
import jax
import jax.numpy as jnp
from jax import lax
from jax.experimental import pallas as pl
from jax.experimental.pallas import tpu as pltpu

F32 = jnp.float32
BF16 = jnp.bfloat16

HEAD_DIM = 128
PAGE_SIZE = 128
CONV_K = 3
KV_B = 4
H_IDX = 16
D_IDX = 128
TOPK_MAX = 256
CMP_BLOCK = 32
SEL_BLOCK = 64
N_SEL = 16
WINDOW = 512
FORCE_BONUS = 1.0e6
H_MEM = 4
ROPE_THETA = 10000.0
EPS = 1e-6
NEG = -1e30
LANES = 128
VMEM_LIMIT = 56 * 1024 * 1024


def _cp(*sem):
    return pltpu.CompilerParams(dimension_semantics=sem, vmem_limit_bytes=VMEM_LIMIT)


def _tile(n, pref, mult):
    t = (min(pref, n) // mult) * mult
    while t >= mult:
        if n % t == 0:
            return t
        t -= mult
    return n


def _dot(a, b):
    return jnp.dot(a, b, preferred_element_type=F32)


def _dot_nt(a, b):
    return lax.dot_general(a, b, (((1,), (1,)), ((), ())), preferred_element_type=F32)


def _rms(x, g):
    return x * lax.rsqrt(jnp.mean(x * x, axis=-1, keepdims=True) + EPS) * g


def _rope(y, cos, sin):
    return y * cos + pltpu.roll(y, HEAD_DIM // 2, 1) * sin


def _norm_cast_kernel(x_ref, g_ref, o_ref):
    o_ref[...] = _rms(x_ref[...], g_ref[...]).astype(o_ref.dtype)


def norm_cast(x, g):
    m, d = x.shape
    tm = _tile(m, 256, 16)
    row = pl.BlockSpec((tm, d), lambda i: (i, 0))
    return pl.pallas_call(
        _norm_cast_kernel, grid=(m // tm,),
        in_specs=[row, pl.BlockSpec((1, d), lambda i: (0, 0))],
        out_specs=row, out_shape=jax.ShapeDtypeStruct((m, d), BF16),
        compiler_params=_cp("arbitrary"), name="norm_cast")(x, g.reshape(1, d))


def _resid_norm_kernel(x_ref, y_ref, gp_ref, gn_ref, xo_ref, h_ref):
    xn = x_ref[...] + _rms(y_ref[...], gp_ref[...])
    xo_ref[...] = xn
    h_ref[...] = _rms(xn, gn_ref[...]).astype(h_ref.dtype)


def _resid_kernel(x_ref, y_ref, gp_ref, xo_ref):
    xo_ref[...] = x_ref[...] + _rms(y_ref[...], gp_ref[...])


def resid_norm(x, y, g_post, g_next):
    m, d = x.shape
    tm = _tile(m, 256, 16)
    row = pl.BlockSpec((tm, d), lambda i: (i, 0))
    vec = pl.BlockSpec((1, d), lambda i: (0, 0))
    if g_next is None:
        return pl.pallas_call(
            _resid_kernel, grid=(m // tm,), in_specs=[row, row, vec], out_specs=row,
            out_shape=jax.ShapeDtypeStruct((m, d), F32),
            compiler_params=_cp("arbitrary"), name="resid")(x, y, g_post.reshape(1, d)), None
    return pl.pallas_call(
        _resid_norm_kernel, grid=(m // tm,), in_specs=[row, row, vec, vec],
        out_specs=[row, row],
        out_shape=[jax.ShapeDtypeStruct((m, d), F32), jax.ShapeDtypeStruct((m, d), BF16)],
        compiler_params=_cp("arbitrary"), name="resid_norm")(
            x, y, g_post.reshape(1, d), g_next.reshape(1, d))


def _mm_kernel(*refs, nx, nw, ne, groups, epilogue):
    xr, wr = refs[:nx], refs[nx:nx + nw]
    er, orf = refs[nx + nw:nx + nw + ne], refs[nx + nw + ne:]
    accs = []
    for grp in groups:
        acc = None
        for xi, wi in grp:
            w = wr[wi][...]
            d = _dot(xr[xi][...], w if w.dtype == BF16 else w.astype(BF16))
            acc = d if acc is None else acc + d
        accs.append(acc)
    epilogue(accs, er, orf)


def _ep_plain(accs, er, orf):
    orf[0][...] = accs[0].astype(orf[0].dtype)


def _ep_rope(accs, er, orf):
    cos, sin = er[0][...], er[1][...]
    y = accs[0]
    for c in range(y.shape[1] // HEAD_DIM):
        sl = slice(c * HEAD_DIM, (c + 1) * HEAD_DIM)
        orf[0][:, sl] = _rope(y[:, sl], cos, sin).astype(orf[0].dtype)


def _ep_split(accs, er, orf, *, rope_chunks):
    cos, sin = er[0][...], er[1][...]
    y = accs[0]
    for c in range(len(orf)):
        yc = y[:, c * HEAD_DIM:(c + 1) * HEAD_DIM]
        if c in rope_chunks:
            yc = _rope(yc, cos, sin)
        orf[c][...] = yc.astype(orf[c].dtype)


def matmul(xs, ws, groups, epilogue, out_dtypes, *, tables=None, n_tab_blocks=1,
           tm_pref=1024, tn_pref=512, split_out=0, layer=None, col_off=0, n_cols=None,
           single_buffer_x=False, name="matmul"):
    m = xs[0].shape[0]
    n = n_cols or ws[0].shape[-1]
    if tables is not None:
        tm = tables[0].shape[0] // n_tab_blocks
    else:
        tm = _tile(m, tm_pref, 16)
    tn = n if (split_out or n % LANES) else _tile(n, tn_pref, LANES)
    xmode = dict(pipeline_mode=pl.Buffered(1)) if single_buffer_x else {}
    in_specs = [pl.BlockSpec((tm, x.shape[1]), lambda i, j: (i, 0), **xmode) for x in xs]
    if layer is None:
        in_specs += [pl.BlockSpec((w.shape[0], tn), lambda i, j: (0, j)) for w in ws]
    else:
        assert col_off % tn == 0
        joff = col_off // tn
        in_specs += [pl.BlockSpec((None, w.shape[1], tn), lambda i, j: (layer, 0, j + joff)) for w in ws]
    extras = []
    if tables is not None:
        nb = n_tab_blocks
        in_specs += [pl.BlockSpec((tm, HEAD_DIM), lambda i, j: (i % nb, 0))] * 2
        extras = list(tables)
    if split_out:
        out_specs = [pl.BlockSpec((tm, HEAD_DIM), lambda i, j: (i, 0))] * split_out
        out_shape = [jax.ShapeDtypeStruct((m, HEAD_DIM), dt) for dt in out_dtypes]
    else:
        out_specs = [pl.BlockSpec((tm, tn), lambda i, j: (i, j))]
        out_shape = [jax.ShapeDtypeStruct((m, n), out_dtypes[0])]
    kern = functools.partial(_mm_kernel, nx=len(xs), nw=len(ws), ne=len(extras),
                             groups=groups, epilogue=epilogue)
    out = pl.pallas_call(
        kern, grid=(m // tm, n // tn), in_specs=in_specs, out_specs=out_specs,
        out_shape=out_shape, compiler_params=_cp("arbitrary", "arbitrary"), name=name)(
            *xs, *ws, *extras)
    return out if split_out else out[0]


def proj(h, w, dtype=F32, name="proj"):
    return matmul([h], [w], [[(0, 0)]], _ep_plain, [dtype], name=name)


def proj_rope(h, w, tabs, dtype, name="proj_rope", **kw):
    cos, sin, nb = tabs
    return matmul([h], [w], [[(0, 0)]], _ep_rope, [dtype], tables=(cos, sin),
                  n_tab_blocks=nb, name=name, **kw)


def proj_out(xa, xb, wa, wb, name="proj_out"):
    return matmul([xa, xb], [wa, wb], [[(0, 0), (1, 1)]], _ep_plain, [F32], name=name)


def rope_tables(pos, reps, tm):
    half = HEAD_DIM // 2
    inv = ROPE_THETA ** (-jnp.arange(half, dtype=F32) / half)
    ang = pos.astype(F32)[:, None] * inv[None, :]
    cos, sin = jnp.cos(ang), jnp.sin(ang)
    cos = jnp.concatenate([cos, cos], axis=-1)
    sin = jnp.concatenate([-sin, sin], axis=-1)
    t = pos.shape[0]
    if tm > t:
        cos, sin = jnp.tile(cos, (tm // t, 1)), jnp.tile(sin, (tm // t, 1))
        return cos, sin, 1
    return cos, sin, t // tm


def _conv_kernel(x_ref, wb_ref, wc_ref, wx_ref, cw_ref, init_ref, ya_ref, st_ref, carry_ref):
    i = pl.program_id(2)

    @pl.when(i == 0)
    def _():
        carry_ref[...] = init_ref[0]

    x = x_ref[0]
    bg = _dot(x, wb_ref[...].astype(BF16))
    u = _dot(x, wc_ref[...].astype(BF16)) * _dot(x, wx_ref[...].astype(BF16))
    tm = u.shape[0]
    c = carry_ref[...]
    rows = lax.broadcasted_iota(jnp.int32, u.shape, 0)
    u1 = jnp.where(rows == 0, c[1:2], pltpu.roll(u, 1, 0))
    u2 = jnp.where(rows == 0, c[0:1], jnp.where(rows == 1, c[1:2], pltpu.roll(u, 2, 0)))
    cw = cw_ref[...]
    conv = cw[0:1] * u2 + cw[1:2] * u1 + cw[2:3] * u
    ya_ref[0] = (bg * conv).astype(ya_ref.dtype)
    new = u[tm - (CONV_K - 1):tm]
    carry_ref[...] = new
    st_ref[0] = new


def conv_mixer(h3, w_in, layer, cw, init):
    b, t, d = h3.shape
    c = cw.shape[1]
    tm = _tile(t, 1024, 16)
    tn = _tile(c, 256, LANES)
    wspec = lambda seg: pl.BlockSpec((None, d, tn), lambda j, bi, i: (layer, 0, seg * (c // tn) + j))
    return pl.pallas_call(
        _conv_kernel, grid=(c // tn, b, t // tm),
        in_specs=[pl.BlockSpec((1, tm, d), lambda j, bi, i: (bi, i, 0)), wspec(0), wspec(1), wspec(2),
                  pl.BlockSpec((CONV_K, tn), lambda j, bi, i: (0, j)),
                  pl.BlockSpec((1, CONV_K - 1, tn), lambda j, bi, i: (bi, 0, j))],
        out_specs=[pl.BlockSpec((1, tm, tn), lambda j, bi, i: (bi, i, j)),
                   pl.BlockSpec((1, CONV_K - 1, tn), lambda j, bi, i: (bi, 0, j))],
        out_shape=[jax.ShapeDtypeStruct((b, t, c), BF16),
                   jax.ShapeDtypeStruct((b, CONV_K - 1, c), F32)],
        scratch_shapes=[pltpu.VMEM((CONV_K - 1, tn), F32)],
        compiler_params=_cp("arbitrary", "arbitrary", "arbitrary"), name="conv_mixer")(
            h3, w_in, w_in, w_in, cw, init)


def _stack_heads(q_ref, heads):
    return jnp.concatenate([q_ref[:, h * HEAD_DIM:(h + 1) * HEAD_DIM] for h in heads], axis=0)


LOG2E = 1.4426950408889634


def _with_ones_column(v):
    ones = jnp.where(lax.broadcasted_iota(jnp.int32, v.shape, 1) == 0, 1.0, 0.0).astype(v.dtype)
    return jnp.concatenate([v, ones], axis=1)


def _softmax_av(qk, scale, bias, v_ones):
    s = qk * (scale * LOG2E) + bias[None]
    e = jnp.exp2(s - jnp.max(s, axis=-1, keepdims=True))
    r, tq, n = e.shape
    o = _dot(e.reshape(r * tq, n).astype(BF16), v_ones)
    return o[:, :HEAD_DIM] / o[:, HEAD_DIM:HEAD_DIM + 1]


def _sort_key(x):
    bits = lax.bitcast_convert_type(x + 0.0, jnp.int32)
    return jnp.where(bits < 0, bits ^ jnp.int32(0x7FFFFFFF), bits)


INT_MIN = -2 ** 31


def _kth_largest_key(key, k):
    def body(it, othr):
        bit = lax.shift_left(jnp.int32(1), jnp.int32(31) - it)
        cand = othr | bit
        cnt = jnp.sum(jnp.where(key >= (cand ^ jnp.int32(INT_MIN)), 1.0, 0.0), axis=1, keepdims=True)
        return jnp.where(cnt >= k, cand, othr)
    othr = lax.fori_loop(0, 32, body, jnp.zeros((key.shape[0], 1), jnp.int32))
    return othr ^ jnp.int32(INT_MIN)


def _topk_mask(key, k, scratch_ref):
    thr = _kth_largest_key(key, k)
    live = key > jnp.int32(INT_MIN)
    ge = (key >= thr) & live
    scratch_ref[...] = jnp.where(ge, 1.0, 0.0)
    n_ge = jnp.sum(jnp.where(ge, 1.0, 0.0), axis=1, keepdims=True)

    @pl.when(jnp.max(n_ge) > k)
    def _():
        gt = key > thr
        eq = (key == thr) & live
        need = k - jnp.sum(jnp.where(gt, 1.0, 0.0), axis=1, keepdims=True)
        r = lax.broadcasted_iota(jnp.int32, (LANES, LANES), 0)
        c = lax.broadcasted_iota(jnp.int32, (LANES, LANES), 1)
        before = jnp.where(r < c, 1.0, 0.0).astype(BF16)
        run = jnp.zeros_like(need)
        for ch in range(key.shape[1] // LANES):
            sl = slice(ch * LANES, (ch + 1) * LANES)
            e = jnp.where(eq[:, sl], 1.0, 0.0)
            pre = _dot(e.astype(BF16), before) + run
            scratch_ref[:, sl] = jnp.where(gt[:, sl], 1.0, e * jnp.where(pre < need, 1.0, 0.0))
            run = run + jnp.sum(e, axis=1, keepdims=True)


CAUSAL_BUCKETS = 4


def _by_key_extent(i, tq, t, fn):
    nb = CAUSAL_BUCKETS if t % (CAUSAL_BUCKETS * tq) == 0 else 1
    size = t // nb
    for bkt in range(nb):
        @pl.when((i * tq) // size == bkt)
        def _():
            fn((bkt + 1) * size)


def _dsa_prompt_kernel(qi_ref, wi_ref, ki_ref, q_ref, k_ref, v_ref, o_ref,
                       kib_ref, kb_ref, vb_ref, mask_ref, *, topk):
    i = pl.program_id(1)

    @pl.when(i == 0)
    def _():
        kib_ref[...] = ki_ref[...].astype(BF16)
        kb_ref[...] = k_ref[...].astype(BF16)
        for g in range(KV_B):
            vb_ref[g] = _with_ones_column(v_ref[:, g * HEAD_DIM:(g + 1) * HEAD_DIM].astype(BF16))

    tq = q_ref.shape[0]
    t = ki_ref.shape[0]
    wi = wi_ref[...]
    rep = q_ref.shape[1] // HEAD_DIM // KV_B

    def attend(n):
        kib = kib_ref[0:n]
        score = jnp.zeros((tq, n), F32)
        for h in range(H_IDX):
            s = _dot_nt(qi_ref[:, h * D_IDX:(h + 1) * D_IDX], kib) * D_IDX ** -0.5
            score = score + jnp.maximum(s, 0.0) * wi[:, h:h + 1]
        score = score * H_IDX ** -0.5
        qpos = i * tq + lax.broadcasted_iota(jnp.int32, (tq, n), 0)
        kpos = lax.broadcasted_iota(jnp.int32, (tq, n), 1)
        key = jnp.where(kpos <= qpos, _sort_key(score), jnp.int32(INT_MIN))
        sel_ref = mask_ref.at[:, 0:n]
        _topk_mask(key, topk, sel_ref)
        bias = jnp.where(sel_ref[...] > 0.0, 0.0, NEG)
        for g in range(KV_B):
            qs = _stack_heads(q_ref, range(g * rep, (g + 1) * rep))
            sl = slice(g * HEAD_DIM, (g + 1) * HEAD_DIM)
            qk = _dot_nt(qs, kb_ref[0:n, sl]).reshape(rep, tq, n)
            o = _softmax_av(qk, HEAD_DIM ** -0.5, bias, vb_ref[g, 0:n])
            for r in range(rep):
                h = g * rep + r
                o_ref[:, h * HEAD_DIM:(h + 1) * HEAD_DIM] = o[r * tq:(r + 1) * tq].astype(o_ref.dtype)

    _by_key_extent(i, tq, t, attend)


def dsa_prompt(qi, wi, ki, q, k, v, b, t):
    m = q.shape[0]
    tq = _tile(t, 128, 16)
    nq = t // tq
    topk = min(TOPK_MAX, t // 4)
    qrow = lambda w: pl.BlockSpec((tq, w), lambda bi, i: (bi * nq + i, 0))
    full = lambda w: pl.BlockSpec((t, w), lambda bi, i: (bi, 0))
    return pl.pallas_call(
        functools.partial(_dsa_prompt_kernel, topk=topk), grid=(b, nq),
        in_specs=[qrow(qi.shape[1]), qrow(wi.shape[1]), full(ki.shape[1]),
                  qrow(q.shape[1]), full(k.shape[1]), full(v.shape[1])],
        out_specs=qrow(q.shape[1]),
        out_shape=jax.ShapeDtypeStruct((m, q.shape[1]), BF16),
        scratch_shapes=[pltpu.VMEM((t, ki.shape[1]), BF16), pltpu.VMEM((t, k.shape[1]), BF16),
                        pltpu.VMEM((KV_B, t, 2 * HEAD_DIM), BF16), pltpu.VMEM((tq, t), F32)],
        compiler_params=_cp("arbitrary", "arbitrary"), name="dsa_prompt")(qi, wi, ki, q, k, v)


def _masked_softmax(s, mask):
    m = jnp.max(jnp.where(mask, s, NEG), axis=-1, keepdims=True)
    m = jnp.where(m > 0.5 * NEG, m, 0.0)
    e = jnp.where(mask, jnp.exp(s - m), 0.0)
    return e / jnp.maximum(jnp.sum(e, axis=-1, keepdims=True), 1e-30)


def _pair_sums(imp, col):
    n = imp.shape[1]
    return imp + jnp.where(col % 2 == 0, pltpu.roll(imp, n - 1, 1), pltpu.roll(imp, 1, 1))


def _select_blocks(bs, col, qpos, n_selblk):
    blk = col // 2
    cur = qpos // SEL_BLOCK
    forced = (blk == 0) | (blk == cur) | (blk == cur - 1)
    admiss = (blk * SEL_BLOCK <= qpos) & (blk < n_selblk)
    work = jnp.where(admiss, jnp.where(forced, bs + FORCE_BONUS, bs), NEG)
    sel = jnp.zeros(bs.shape, jnp.bool_)
    big = jnp.int32(2 ** 30)
    for _ in range(min(N_SEL, n_selblk)):
        mx = jnp.max(work, axis=1, keepdims=True)
        idx = jnp.min(jnp.where(work == mx, col, big), axis=1, keepdims=True)
        pick = blk == idx // 2
        sel = sel | pick
        work = jnp.where(pick, -3e38, work)
    return jnp.where(sel & admiss, 1.0, 0.0)


def _nsa_prompt_kernel(q_ref, gc_ref, kc_ref, vc_ref, ks_ref, vs_ref, kw_ref, vw_ref,
                       cwk_ref, cwv_ref, o_ref,
                       kcmp_ref, vcmp_ref, ksb_ref, vsb_ref, kwb_ref, vwb_ref, *, win):
    i = pl.program_id(1)
    tq = q_ref.shape[0]
    t = kc_ref.shape[0]
    nb = t // CMP_BLOCK
    ncp = kcmp_ref.shape[0]
    n_selblk = -(-t // SEL_BLOCK)
    nh = q_ref.shape[1] // HEAD_DIM
    scale = HEAD_DIM ** -0.5

    @pl.when(i == 0)
    def _():
        kcmp_ref[...] = jnp.zeros(kcmp_ref.shape, kcmp_ref.dtype)
        vcmp_ref[...] = jnp.zeros(vcmp_ref.shape, vcmp_ref.dtype)
        kc = kc_ref[...].reshape(nb, CMP_BLOCK, HEAD_DIM)
        vc = vc_ref[...].reshape(nb, CMP_BLOCK, HEAD_DIM)
        kcmp_ref[0:nb] = jnp.sum(kc * cwk_ref[...][None], axis=1).astype(BF16)
        vcmp_ref[0:nb] = jnp.sum(vc * cwv_ref[...][None], axis=1).astype(BF16)
        ksb_ref[...] = ks_ref[...].astype(BF16)
        vsb_ref[...] = _with_ones_column(vs_ref[...].astype(BF16))
        kwb_ref[...] = kw_ref[...].astype(BF16)
        vwb_ref[...] = _with_ones_column(vw_ref[...].astype(BF16))

    qs = _stack_heads(q_ref, range(nh))
    col = lax.broadcasted_iota(jnp.int32, (tq, ncp), 1)
    qpos_c = i * tq + lax.broadcasted_iota(jnp.int32, (tq, ncp), 0)
    cmask = ((col + 1) * CMP_BLOCK - 1 <= qpos_c) & (col < nb)
    s = (_dot_nt(qs, kcmp_ref[...]) * scale).reshape(nh, tq, ncp)
    p = _masked_softmax(s, cmask[None])
    o_cmp = _dot(p.reshape(nh * tq, ncp).astype(BF16), vcmp_ref[...])
    imp = jnp.sum(p, axis=0)
    sel = _select_blocks(_pair_sums(imp, col), col, qpos_c, n_selblk).astype(BF16)
    start = pl.multiple_of(jnp.clip(i * tq - WINDOW, 0, t - win), 16)
    qpos_w = i * tq + lax.broadcasted_iota(jnp.int32, (tq, win), 0)
    kpos_w = start + lax.broadcasted_iota(jnp.int32, (tq, win), 1)
    bias_win = jnp.where((kpos_w <= qpos_w) & (kpos_w > qpos_w - WINDOW), 0.0, NEG)
    kwin = kwb_ref[pl.ds(start, win), :]
    vwin = vwb_ref[pl.ds(start, win), :]
    gate = jax.nn.sigmoid(gc_ref[...])
    grp = 4
    o_wins = []
    for hg in range(nh // grp):
        q4 = qs[hg * grp * tq:(hg + 1) * grp * tq]
        qk_win = _dot_nt(q4, kwin).reshape(grp, tq, win)
        o_wins.append(_softmax_av(qk_win, scale, bias_win, vwin))

    def attend(n):
        er = lax.broadcasted_iota(jnp.int32, (ncp, n), 0)
        ec = lax.broadcasted_iota(jnp.int32, (ncp, n), 1)
        expand = jnp.where(er == 2 * (ec // SEL_BLOCK), 1.0, 0.0).astype(BF16)
        tok = _dot(sel, expand)
        qpos = i * tq + lax.broadcasted_iota(jnp.int32, (tq, n), 0)
        kpos = lax.broadcasted_iota(jnp.int32, (tq, n), 1)
        bias_sel = jnp.where((tok > 0.5) & (kpos <= qpos), 0.0, NEG)
        for hg in range(nh // grp):
            q4 = qs[hg * grp * tq:(hg + 1) * grp * tq]
            qk_sel = _dot_nt(q4, ksb_ref[0:n]).reshape(grp, tq, n)
            o_sel = _softmax_av(qk_sel, scale, bias_sel, vsb_ref[0:n])
            o_win = o_wins[hg]
            for r in range(grp):
                h = hg * grp + r
                rows = slice(r * tq, (r + 1) * tq)
                o = (gate[:, 3 * h:3 * h + 1] * o_cmp[h * tq:(h + 1) * tq]
                     + gate[:, 3 * h + 1:3 * h + 2] * o_sel[rows] + gate[:, 3 * h + 2:3 * h + 3] * o_win[rows])
                o_ref[:, h * HEAD_DIM:(h + 1) * HEAD_DIM] = o.astype(o_ref.dtype)

    _by_key_extent(i, tq, t, attend)


def nsa_prompt(q, gc, kc, vc, ks, vs, kw, vw, cwk, cwv, b, t):
    m, hq = q.shape
    tq = _tile(t, 128, 16)
    nq = t // tq
    win = min(WINDOW + tq, t)
    ncp = LANES
    assert t // CMP_BLOCK <= ncp and t % SEL_BLOCK == 0
    qrow = lambda w: pl.BlockSpec((tq, w), lambda bi, i: (bi * nq + i, 0))
    full = pl.BlockSpec((t, HEAD_DIM), lambda bi, i: (bi, 0))
    cw = pl.BlockSpec((CMP_BLOCK, HEAD_DIM), lambda bi, i: (0, 0))
    kv = pltpu.VMEM((t, HEAD_DIM), BF16)
    kv_ones = pltpu.VMEM((t, 2 * HEAD_DIM), BF16)
    return pl.pallas_call(
        functools.partial(_nsa_prompt_kernel, win=win), grid=(b, nq),
        in_specs=[qrow(hq), qrow(gc.shape[1])] + [full] * 6 + [cw, cw],
        out_specs=qrow(hq), out_shape=jax.ShapeDtypeStruct((m, hq), BF16),
        scratch_shapes=[pltpu.VMEM((ncp, HEAD_DIM), BF16), pltpu.VMEM((ncp, HEAD_DIM), BF16),
                        kv, kv_ones, kv, kv_ones],
        compiler_params=_cp("arbitrary", "arbitrary"), name="nsa_prompt")(
            q, gc, kc, vc, ks, vs, kw, vw, cwk, cwv)


def _log_sigmoids(z):
    ls = jnp.minimum(z, 0.0) - jnp.log(1.0 + jnp.exp(-jnp.abs(z)))
    return ls, ls - z


def _split_bf16(x):
    hi = x.astype(BF16)
    return hi, (x - hi.astype(F32)).astype(BF16)


SB_HEADS_PER_STEP = 2


def _sb_prompt_kernel(q_ref, k_ref, v_ref, o_ref):
    i = pl.program_id(2)
    tq = q_ref.shape[0]
    nhs = q_ref.shape[1] // HEAD_DIM
    r = lax.broadcasted_iota(jnp.int32, (tq, tq), 0)
    c = lax.broadcasted_iota(jnp.int32, (tq, tq), 1)
    later = jnp.where(r > c, 1.0, 0.0).astype(BF16)
    before = c < r
    qs = [q_ref[:, h * HEAD_DIM:(h + 1) * HEAD_DIM] for h in range(nhs)]
    cols = lambda h: slice(h * HEAD_DIM, (h + 1) * HEAD_DIM)
    block_off = lambda j: pl.multiple_of(jnp.maximum(j, 0) * tq, tq)

    def scores(h, off):
        return _dot_nt(qs[h], k_ref[pl.ds(off, tq), cols(h)].astype(BF16)) * HEAD_DIM ** -0.5

    def weights(z, carry, m):
        ls, lneg = _log_sigmoids(z)
        if m is not None:
            lneg = jnp.where(m, lneg, 0.0)
        hi, lo = _split_bf16(lneg)
        after = _dot(hi, later) + _dot(lo, later)
        a = jnp.exp(ls + after + carry)
        if m is not None:
            a = jnp.where(m, a, 0.0)
        return a.astype(BF16), carry + after[:, 0:1] + lneg[:, 0:1]

    def weighted_values(h, a, off, acc):
        return acc + _dot(a, v_ref[pl.ds(off, tq), cols(h)].astype(BF16))

    st = []
    for h in range(nhs):
        a, carry = weights(scores(h, block_off(i)), jnp.zeros((tq, 1), F32), before)
        st.append((scores(h, block_off(i - 1)), a, carry, jnp.zeros((tq, HEAD_DIM), F32)))

    def body(jj, st):
        out = []
        for h in range(nhs):
            z, a_prev, carry, acc = st[h]
            z_next = scores(h, block_off(i - 2 - jj))
            acc = weighted_values(h, a_prev, block_off(i - jj), acc)
            a, carry = weights(z, carry, None)
            out.append((z_next, a, carry, acc))
        return tuple(out)

    st = lax.fori_loop(0, i, body, tuple(st))
    for h in range(nhs):
        acc = weighted_values(h, st[h][1], 0, st[h][3])
        o_ref[:, h * HEAD_DIM:(h + 1) * HEAD_DIM] = acc.astype(o_ref.dtype)


def sb_prompt(q, k, v, b, t):
    m, hq = q.shape
    nhs = SB_HEADS_PER_STEP
    tq = _tile(t, 256, 16)
    nq = t // tq
    w = nhs * HEAD_DIM
    qrow = pl.BlockSpec((tq, w), lambda bi, h, i: (bi * nq + i, h))
    full = pl.BlockSpec((t, w), lambda bi, h, i: (bi, h))
    return pl.pallas_call(
        _sb_prompt_kernel, grid=(b, hq // w, nq),
        in_specs=[qrow, full, full], out_specs=qrow,
        out_shape=jax.ShapeDtypeStruct((m, hq), BF16),
        compiler_params=_cp("arbitrary", "arbitrary", "arbitrary"), name="sb_prompt")(q, k, v)


def _mem_kernel(h_ref, x_ref, wq_ref, mk_ref, mv_ref, wo_ref, gp_ref, gn_ref, xo_ref, ho_ref):
    q = _dot(h_ref[0], wq_ref[...])
    outs = []
    for hh in range(H_MEM):
        sl = slice(hh * HEAD_DIM, (hh + 1) * HEAD_DIM)
        s = _dot_nt(q[:, sl].astype(BF16), mk_ref[0, :, sl].astype(BF16)) * HEAD_DIM ** -0.5
        e = jnp.exp(s - jnp.max(s, axis=-1, keepdims=True))
        o = _dot(e.astype(BF16), mv_ref[0, :, sl].astype(BF16))
        outs.append(o / jnp.sum(e, axis=-1, keepdims=True))
    y = _dot(jnp.concatenate(outs, axis=1).astype(BF16), wo_ref[...])
    xn = x_ref[0] + _rms(y, gp_ref[...])
    xo_ref[0] = xn
    ho_ref[0] = _rms(xn, gn_ref[...]).astype(ho_ref.dtype)


def mem_sublayer(h3, x3, wq, mk, mv, wo, g_post, g_next):
    b, t, d = h3.shape
    tq = _tile(t, 256, 16)
    nm, hm = mk.shape[1], mk.shape[2]
    row = pl.BlockSpec((1, tq, d), lambda bi, i: (bi, i, 0))
    mem = pl.BlockSpec((1, nm, hm), lambda bi, i: (bi, 0, 0))
    vec = pl.BlockSpec((1, d), lambda bi, i: (0, 0))
    return pl.pallas_call(
        _mem_kernel, grid=(b, t // tq),
        in_specs=[row, row, pl.BlockSpec((d, hm), lambda bi, i: (0, 0)), mem, mem,
                  pl.BlockSpec((hm, d), lambda bi, i: (0, 0)), vec, vec],
        out_specs=[row, row],
        out_shape=[jax.ShapeDtypeStruct((b, t, d), F32), jax.ShapeDtypeStruct((b, t, d), BF16)],
        compiler_params=_cp("arbitrary", "arbitrary"), name="mem_sublayer")(
            h3, x3, wq, mk, mv, wo, g_post.reshape(1, d), g_next.reshape(1, d))


def _ep_swiglu(accs, er, orf):
    g, u = accs
    orf[0][...] = ((g * jax.nn.sigmoid(g)) * u).astype(orf[0].dtype)


def ffn(h, wg, wu, wd, li):
    act = matmul([h], [wg, wu], [[(0, 0)], [(0, 1)]], _ep_swiglu, [BF16], tm_pref=2048,
                 single_buffer_x=True, layer=li, name="ffn_gate_up")
    return matmul([act], [wd], [[(0, 0)]], _ep_plain, [F32], tm_pref=512, layer=li, name="ffn_down")


def _odd_weights(w_in, w_out, nh_c, nh_d):
    sizes = [nh_c * HEAD_DIM] + [HEAD_DIM] * 6 + [nh_c * 3] + [nh_d * HEAD_DIM] * 3
    offs = np.cumsum([0] + sizes)
    cut = lambda a, b_: w_in[:, offs[a]:offs[b_]].astype(BF16)
    w = {"qc": cut(0, 1), "kv6": cut(1, 7), "gc": cut(7, 8), "qd": cut(8, 9), "kd": cut(9, 10),
         "vd": cut(10, 11)}
    w["out_c"] = w_out[:nh_c * HEAD_DIM].astype(BF16)
    w["out_d"] = w_out[nh_c * HEAD_DIM:].astype(BF16)
    return w


def _odd_project(h, w, tabs):
    cos, sin, nb = tabs
    qc = proj_rope(h, w["qc"], tabs, BF16, name="proj_qc")
    kv6 = matmul([h], [w["kv6"]], [[(0, 0)]], functools.partial(_ep_split, rope_chunks=(0, 2, 4)),
                 [F32] * 6, tables=(cos, sin), n_tab_blocks=nb, split_out=6, name="proj_kv6")
    gc = proj(h, w["gc"], name="proj_gc")
    qd = proj(h, w["qd"], BF16, name="proj_qd")
    kd = proj(h, w["kd"], name="proj_kd")
    vd = proj(h, w["vd"], name="proj_vd")
    return qc, kv6, gc, qd, kd, vd


def odd_mixer_prompt(h, b, t, w, cwk, cwv):
    m, d = h.shape
    tabs = rope_tables(jnp.arange(t), b, _tile(m, 1024, 16))
    qc, (kc, vc, ks, vs, kw, vw), gc, qd, kd, vd = _odd_project(h, w, tabs)
    o_c = nsa_prompt(qc, gc, kc, vc, ks, vs, kw, vw, cwk, cwv, b, t)
    o_d = sb_prompt(qd, kd, vd, b, t)
    y = proj_out(o_c, o_d, w["out_c"], w["out_d"])
    nw = min(WINDOW, t)
    st1 = lambda a: a.reshape(b, t, 1, HEAD_DIM)
    sth = lambda a: a.reshape(b, t, -1, HEAD_DIM)
    return y, (st1(kc), st1(vc), st1(ks), st1(vs), st1(kw)[:, t - nw:], st1(vw)[:, t - nw:],
               sth(kd), sth(vd))


def _even_weights(w_in, e, conv_w, w_out):
    c = conv_w.shape[1]
    hq = w_out.shape[0] - c
    sizes = [c, c, c, hq, KV_B * HEAD_DIM, KV_B * HEAD_DIM, H_IDX * D_IDX, D_IDX, H_IDX]
    offs = np.cumsum([0] + sizes)
    names = ["bg", "cg", "xa", "q", "k", "v", "qi", "ki", "wi"]
    w = {n: w_in[e, :, offs[j]:offs[j + 1]].astype(BF16) for j, n in enumerate(names) if n in ("k", "v", "ki", "wi")}
    w.update(w_in=w_in, layer=e, q_off=int(offs[3]), qi_off=int(offs[6]), n_q=hq, n_qi=H_IDX * D_IDX)
    w["conv_w"] = conv_w
    w["out_a"] = w_out[:c].astype(BF16)
    w["out_b"] = w_out[c:].astype(BF16)
    return w


TM_WIDE = 2048


def _even_project(h, w, pos, reps):
    m = h.shape[0]
    tabs = rope_tables(pos, reps, _tile(m, 1024, 16))
    tabs_wide = rope_tables(pos, reps, _tile(m, TM_WIDE, 16))
    wide = dict(layer=w["layer"], single_buffer_x=True)
    q = proj_rope(h, w["w_in"], tabs_wide, BF16, name="proj_q", col_off=w["q_off"], n_cols=w["n_q"], **wide)
    k = proj_rope(h, w["k"], tabs, F32, name="proj_k")
    v = proj(h, w["v"], name="proj_v")
    qi = proj_rope(h, w["w_in"], tabs_wide, BF16, name="proj_qi", col_off=w["qi_off"], n_cols=w["n_qi"], **wide)
    ki = proj_rope(h, w["ki"], tabs, F32, name="proj_ki")
    wi = proj(h, w["wi"], name="proj_wi")
    return q, k, v, qi, ki, wi


def even_mixer_prompt(h, b, t, w):
    m, d = h.shape
    init = jnp.zeros((b, CONV_K - 1, w["conv_w"].shape[1]), F32)
    ya, conv_state = conv_mixer(h.reshape(b, t, d), w["w_in"], w["layer"], w["conv_w"], init)
    q, k, v, qi, ki, wi = _even_project(h, w, jnp.arange(t), b)
    ob = dsa_prompt(qi, wi, ki, q, k, v, b, t)
    y = proj_out(ya.reshape(m, -1), ob, w["out_a"], w["out_b"])
    return y, (conv_state, k.reshape(b, t, KV_B, HEAD_DIM), v.reshape(b, t, KV_B, HEAD_DIM),
               ki.reshape(b, t, D_IDX))


def _paged_call(kern, grid, in_specs, out_specs, out_shape, scratch, name, page_table, args):
    gs = pltpu.PrefetchScalarGridSpec(num_scalar_prefetch=1, grid=grid, in_specs=in_specs,
                                      out_specs=out_specs, scratch_shapes=scratch)
    return pl.pallas_call(kern, grid_spec=gs, out_shape=out_shape,
                          compiler_params=_cp("arbitrary", "arbitrary"), name=name)(page_table, *args)


def _pad_rows(a, rows):
    return jnp.pad(a, ((0, 0), (0, rows - a.shape[1])) + ((0, 0),) * (a.ndim - 2))


def _page_specs(block_tail, layer, g_pages, page_of):
    zeros = (0,) * len(block_tail)

    def spec(j):
        return pl.BlockSpec((1, 1) + block_tail,
                            lambda bi, p, pt: (layer, page_of(bi, p, pt, j)) + zeros)
    return [spec(j) for j in range(g_pages)]


def _forward_pages(g_pages):
    return lambda bi, p, pt, j: pt[bi, p * g_pages + j]


def _rows_ht(a, b, t, nh):
    w = a.shape[1] // nh
    return a.reshape(b, t, nh, w).transpose(0, 2, 1, 3).reshape(b, nh * t, w)


def _rows_th(a, b, t, nh):
    w = a.shape[2]
    return a.reshape(b, nh, t, w).transpose(0, 2, 1, 3).reshape(b * t, nh * w)


def _dsa_scores_kernel(pt_ref, qi_ref, wi_ref, new_ref, *rest, g_pages, t):
    pools, (o_ref, onew_ref) = rest[:g_pages], rest[g_pages:]
    qi, wi = qi_ref[0], wi_ref[0]

    def scores(kb):
        s = _dot_nt(qi, kb) * D_IDX ** -0.5
        s = jnp.maximum(s, 0.0) * wi
        return jnp.sum(s.reshape(H_IDX, t, kb.shape[0]), axis=0) * H_IDX ** -0.5

    o_ref[0] = scores(jnp.concatenate([r[0, 0] for r in pools], axis=0).astype(BF16))

    @pl.when(pl.program_id(1) == 0)
    def _():
        onew_ref[0] = scores(new_ref[0].astype(BF16))


def dsa_sample_scores(qi_r, wi_r, ki_new, pool, e, page_table, t):
    b, n_pages = page_table.shape
    g = _tile(n_pages, 8, 1)
    kern = functools.partial(_dsa_scores_kernel, g_pages=g, t=t)
    per_b = lambda r, w: pl.BlockSpec((1, r, w), lambda bi, p, pt: (bi, 0, 0))
    past, new = _paged_call(
        kern, (b, n_pages // g),
        [per_b(H_IDX * t, D_IDX), per_b(H_IDX * t, 1), per_b(PAGE_SIZE, D_IDX)]
        + _page_specs((PAGE_SIZE, D_IDX), e, g, _forward_pages(g)),
        [pl.BlockSpec((1, t, g * PAGE_SIZE), lambda bi, p, pt: (bi, 0, p)), per_b(t, PAGE_SIZE)],
        [jax.ShapeDtypeStruct((b, t, n_pages * PAGE_SIZE), F32),
         jax.ShapeDtypeStruct((b, t, PAGE_SIZE), F32)], [], "dsa_sample_scores",
        page_table, (qi_r, wi_r, _pad_rows(ki_new, PAGE_SIZE)) + (pool,) * g)
    return jnp.concatenate([past, new], axis=-1)


def _dsa_topk_kernel(s_ref, o_ref, mask_ref, *, past, topk):
    score = s_ref[0]
    kpos = lax.broadcasted_iota(jnp.int32, score.shape, 1)
    qpos = past + lax.broadcasted_iota(jnp.int32, score.shape, 0)
    key = jnp.where(kpos <= qpos, _sort_key(score), jnp.int32(INT_MIN))
    _topk_mask(key, topk, mask_ref)
    o_ref[0] = mask_ref[...]


def dsa_sample_topk(score, past, topk):
    b, t, nk = score.shape
    blk = pl.BlockSpec((1, t, nk), lambda bi: (bi, 0, 0))
    return pl.pallas_call(
        functools.partial(_dsa_topk_kernel, past=past, topk=topk), grid=(b,),
        in_specs=[blk], out_specs=blk, out_shape=jax.ShapeDtypeStruct((b, t, nk), F32),
        scratch_shapes=[pltpu.VMEM((t, nk), F32)],
        compiler_params=_cp("arbitrary"), name="dsa_sample_topk")(score)


def _online_softmax_step(s, valid, v, m_ref, l_ref, acc_ref):
    m_old = m_ref[...]
    m_new = jnp.maximum(m_old, jnp.max(jnp.where(valid, s, NEG), axis=-1, keepdims=True))
    alpha = jnp.exp(m_old - m_new)
    e = jnp.where(valid, jnp.exp(s - m_new), 0.0)
    l_ref[...] = alpha * l_ref[...] + jnp.sum(e, axis=-1, keepdims=True)
    acc_ref[...] = alpha * acc_ref[...] + _dot(e.astype(BF16), v)
    m_ref[...] = m_new


def _init_softmax_state(m_ref, l_ref, acc_ref):
    m_ref[...] = jnp.full(m_ref.shape, NEG, F32)
    l_ref[...] = jnp.zeros(l_ref.shape, F32)
    acc_ref[...] = jnp.zeros(acc_ref.shape, F32)


def _dsa_sample_attn_kernel(pt_ref, qt_ref, mask_ref, masknew_ref, knew_ref, vnew_ref, *rest,
                            g_pages, t):
    kps, vps = rest[:g_pages], rest[g_pages:2 * g_pages]
    o_ref, m_ref, l_ref, acc_ref, z_ref = rest[2 * g_pages:]
    p = pl.program_id(1)
    n, ng = PAGE_SIZE, KV_B
    cols = qt_ref.shape[2]
    per_group = cols // ng

    @pl.when(p == 0)
    def _():
        _init_softmax_state(m_ref, l_ref, acc_ref)

    iota = lambda shape, d: lax.broadcasted_iota(jnp.int32, shape, d)
    own_group = iota((n * ng, cols), 0) % ng == iota((n * ng, cols), 1) // per_group
    spread = jnp.where(iota((n, n * ng), 1) // ng == iota((n, n * ng), 0), 1.0, 0.0).astype(BF16)
    own_rows = iota((cols, n * ng), 0) // per_group == iota((cols, n * ng), 1) % ng

    def scores(k2, slot):
        z_ref[slot] = jnp.where(own_group, _dot(k2.astype(BF16), qt_ref[0]), 0.0)
        z = z_ref[slot, pl.ds(0, n, stride=ng), :]
        for g in range(1, ng):
            z = z + z_ref[slot, pl.ds(g, n, stride=ng), :]
        return z.T

    def weighted_values(e, v2):
        a2 = jnp.where(own_rows, _dot(e.astype(BF16), spread), 0.0)
        return _dot(a2.astype(BF16), v2.astype(BF16))

    def update(k2s, v2s, valid_t, first_slot):
        s = jnp.concatenate([scores(k2, first_slot + j) for j, k2 in enumerate(k2s)], axis=1)
        s = s * HEAD_DIM ** -0.5
        valid = jnp.concatenate([valid_t] * (cols // t), axis=0)
        m_old = m_ref[...]
        m_new = jnp.maximum(m_old, jnp.max(jnp.where(valid, s, NEG), axis=-1, keepdims=True))
        alpha = jnp.exp(m_old - m_new)
        e = jnp.where(valid, jnp.exp(s - m_new), 0.0)
        l_ref[...] = alpha * l_ref[...] + jnp.sum(e, axis=-1, keepdims=True)
        pv = None
        for j, v2 in enumerate(v2s):
            d = weighted_values(e[:, j * n:(j + 1) * n], v2)
            pv = d if pv is None else pv + d
        acc_ref[...] = alpha * acc_ref[...] + pv
        m_ref[...] = m_new

    update([r[...] for r in kps], [r[...] for r in vps], mask_ref[0] > 0.5, 0)

    @pl.when(p == pl.num_programs(1) - 1)
    def _():
        update([knew_ref[0]], [vnew_ref[0]], masknew_ref[0] > 0.5, g_pages)
        o_ref[0] = acc_ref[...] / l_ref[...]


def dsa_sample_attn(q_r, mask, kpool, vpool, k_new, v_new, e, page_table):
    b, n_pages = page_table.shape
    t = mask.shape[1]
    cols = q_r.shape[1]
    g = _tile(n_pages, 8, 1)
    rows = PAGE_SIZE * KV_B
    flat = lambda pool: pool.reshape(pool.shape[0], -1, HEAD_DIM)
    new_rows = lambda a: _pad_rows(a, PAGE_SIZE).reshape(b, rows, HEAD_DIM)
    per_b = lambda r, w: pl.BlockSpec((1, r, w), lambda bi, p, pt: (bi, 0, 0))
    pools = [pl.BlockSpec((None, rows, HEAD_DIM), functools.partial(
        lambda bi, p, pt, j: (e, pt[bi, p * g + j], 0), j=j)) for j in range(g)]
    return _paged_call(
        functools.partial(_dsa_sample_attn_kernel, g_pages=g, t=t), (b, n_pages // g),
        [per_b(HEAD_DIM, cols), pl.BlockSpec((1, t, g * PAGE_SIZE), lambda bi, p, pt: (bi, 0, p)),
         pl.BlockSpec((1, t, PAGE_SIZE), lambda bi, p, pt: (bi, 0, n_pages)),
         per_b(rows, HEAD_DIM), per_b(rows, HEAD_DIM)] + pools + pools,
        per_b(cols, HEAD_DIM), jax.ShapeDtypeStruct((b, cols, HEAD_DIM), F32),
        [pltpu.VMEM((cols, 1), F32), pltpu.VMEM((cols, 1), F32), pltpu.VMEM((cols, HEAD_DIM), F32),
         pltpu.VMEM((g + 1, rows, cols), F32)],
        "dsa_sample_attn", page_table,
        (q_r.transpose(0, 2, 1), mask, mask, new_rows(k_new), new_rows(v_new))
        + (flat(kpool),) * g + (flat(vpool),) * g)


def even_mixer_sample(h, b, t, w, e, page_table, state_conv, c_k, c_v, c_kidx):
    m, d = h.shape
    n_pages = page_table.shape[1]
    past = n_pages * PAGE_SIZE
    ya, conv_state = conv_mixer(h.reshape(b, t, d), w["w_in"], w["layer"], w["conv_w"], state_conv[e])
    q, k, v, qi, ki, wi = _even_project(h, w, past + jnp.arange(t), b)
    score = dsa_sample_scores(_rows_ht(qi, b, t, H_IDX), _rows_ht(wi, b, t, H_IDX),
                              ki.reshape(b, t, D_IDX), c_kidx, e, page_table, t)
    mask = dsa_sample_topk(score, past, min(TOPK_MAX, (past + t) // 4))
    nh = q.shape[1] // HEAD_DIM
    o = dsa_sample_attn(_rows_ht(q, b, t, nh), mask, c_k, c_v, k.reshape(b, t, KV_B, HEAD_DIM),
                        v.reshape(b, t, KV_B, HEAD_DIM), e, page_table)
    ob = _rows_th(o, b, t, nh).astype(BF16)
    y = proj_out(ya.reshape(m, -1), ob, w["out_a"], w["out_b"])
    return y, (conv_state, k.reshape(b, t, KV_B, HEAD_DIM), v.reshape(b, t, KV_B, HEAD_DIM),
               ki.reshape(b, t, D_IDX))


def _compress_pages_kernel(pt_ref, cwk_ref, cwv_ref, *rest, g_pages):
    kcs, vcs = rest[:g_pages], rest[g_pages:2 * g_pages]
    ko_ref, vo_ref = rest[2 * g_pages:]
    nb = PAGE_SIZE // CMP_BLOCK
    cwk, cwv = cwk_ref[...][None], cwv_ref[...][None]
    for j in range(g_pages):
        ko_ref[0, j] = jnp.sum(kcs[j][0, 0].reshape(nb, CMP_BLOCK, HEAD_DIM) * cwk, axis=1)
        vo_ref[0, j] = jnp.sum(vcs[j][0, 0].reshape(nb, CMP_BLOCK, HEAD_DIM) * cwv, axis=1)


def nsa_compress_pages(kpool, vpool, cwk, cwv, o, page_table):
    b, n_pages = page_table.shape
    nb = PAGE_SIZE // CMP_BLOCK
    g = _tile(n_pages, 8, 1)
    pools = _page_specs((PAGE_SIZE, HEAD_DIM), o, g, _forward_pages(g))
    cw = pl.BlockSpec((CMP_BLOCK, HEAD_DIM), lambda bi, p, pt: (0, 0))
    out = pl.BlockSpec((1, g, nb, HEAD_DIM), lambda bi, p, pt: (bi, p, 0, 0))
    shp = jax.ShapeDtypeStruct((b, n_pages, nb, HEAD_DIM), F32)
    kc, vc = _paged_call(functools.partial(_compress_pages_kernel, g_pages=g), (b, n_pages // g),
                         [cw, cw] + pools + pools, [out, out], [shp, shp], [], "nsa_compress_pages",
                         page_table, (cwk, cwv) + (kpool,) * g + (vpool,) * g)
    return kc.reshape(b, n_pages * nb, HEAD_DIM), vc.reshape(b, n_pages * nb, HEAD_DIM)


def _nsa_sample_a_kernel(q_ref, kcmp_ref, vcmp_ref, kw_ref, vw_ref, ocmp_ref, owin_ref, sel_ref,
                         *, past, t, nwin, n_selblk):
    q = q_ref[0]
    rows = q.shape[0]
    nh = rows // t
    scale = HEAD_DIM ** -0.5
    nb = kcmp_ref.shape[1]
    col = lax.broadcasted_iota(jnp.int32, (rows, nb), 1)
    pos = past + lax.broadcasted_iota(jnp.int32, (rows, nb), 0) % t
    s = _dot_nt(q, kcmp_ref[0].astype(BF16)) * scale
    p = _masked_softmax(s, (col + 1) * CMP_BLOCK - 1 <= pos)
    ocmp_ref[0] = _dot(p.astype(BF16), vcmp_ref[0].astype(BF16))
    imp = jnp.sum(p.reshape(nh, t, nb), axis=0)
    ncol = sel_ref.shape[2]
    imp = jnp.concatenate([imp, jnp.zeros((t, ncol - nb), F32)], axis=1)
    col_s = lax.broadcasted_iota(jnp.int32, (t, ncol), 1)
    pos_s = past + lax.broadcasted_iota(jnp.int32, (t, ncol), 0)
    sel_ref[0] = _select_blocks(_pair_sums(imp, col_s), col_s, pos_s, n_selblk)
    nwp = kw_ref.shape[1]
    colw = lax.broadcasted_iota(jnp.int32, (rows, nwp), 1)
    posw = past + lax.broadcasted_iota(jnp.int32, (rows, nwp), 0) % t
    kwpos = past + t - nwin + colw
    valid = (kwpos <= posw) & (kwpos > posw - WINDOW) & (colw < nwin)
    sw = _dot_nt(q, kw_ref[0].astype(BF16)) * scale
    pw = _masked_softmax(sw, valid)
    owin_ref[0] = _dot(pw.astype(BF16), vw_ref[0].astype(BF16))


def nsa_sample_a(q_r, kcmp, vcmp, kw_pad, vw_pad, past, t, nwin):
    b, rows, _ = q_r.shape
    nb = kcmp.shape[1]
    n_selblk = -(-(past + t) // SEL_BLOCK)
    ncol = -(-2 * n_selblk // LANES) * LANES
    assert ncol > nb >= 2 * n_selblk - 2 and nb % LANES == 0
    per_b = lambda r, w: pl.BlockSpec((1, r, w), lambda bi: (bi, 0, 0))
    kern = functools.partial(_nsa_sample_a_kernel, past=past, t=t, nwin=nwin, n_selblk=n_selblk)
    return pl.pallas_call(
        kern, grid=(b,),
        in_specs=[per_b(rows, HEAD_DIM), per_b(nb, HEAD_DIM), per_b(nb, HEAD_DIM),
                  per_b(kw_pad.shape[1], HEAD_DIM), per_b(kw_pad.shape[1], HEAD_DIM)],
        out_specs=[per_b(rows, HEAD_DIM), per_b(rows, HEAD_DIM), per_b(t, ncol)],
        out_shape=[jax.ShapeDtypeStruct((b, rows, HEAD_DIM), F32)] * 2
        + [jax.ShapeDtypeStruct((b, t, ncol), F32)],
        compiler_params=_cp("arbitrary"), name="nsa_sample_a")(q_r, kcmp, vcmp, kw_pad, vw_pad)


def _nsa_sample_b_kernel(pt_ref, q_ref, tok_ref, toknew_ref, knew_ref, vnew_ref,
                         ocmp_ref, owin_ref, gate_ref, *rest, g_pages, n_pages, t):
    kps, vps = rest[:g_pages], rest[g_pages:2 * g_pages]
    o_ref, m_ref, l_ref, acc_ref = rest[2 * g_pages:]
    p = pl.program_id(1)

    @pl.when(p == 0)
    def _():
        _init_softmax_state(m_ref, l_ref, acc_ref)

    q = q_ref[0]
    rows = q.shape[0]

    def update(keys, vals, tok, first_key):
        n = keys.shape[0]
        kpos = first_key + lax.broadcasted_iota(jnp.int32, (rows, n), 1)
        qpos = n_pages * PAGE_SIZE + lax.broadcasted_iota(jnp.int32, (rows, n), 0) % t
        valid = jnp.concatenate([tok > 0.5] * (rows // t), axis=0) & (kpos <= qpos)
        s = _dot_nt(q, keys.astype(BF16)) * HEAD_DIM ** -0.5
        _online_softmax_step(s, valid, vals.astype(BF16), m_ref, l_ref, acc_ref)

    update(jnp.concatenate([r[0, 0] for r in kps], axis=0),
           jnp.concatenate([r[0, 0] for r in vps], axis=0), tok_ref[0], p * (g_pages * PAGE_SIZE))

    @pl.when(p == pl.num_programs(1) - 1)
    def _():
        update(knew_ref[0], vnew_ref[0], toknew_ref[0], n_pages * PAGE_SIZE)
        g = jax.nn.sigmoid(gate_ref[0])
        o_ref[0] = (g[:, 0:1] * ocmp_ref[0] + g[:, 1:2] * (acc_ref[...] / l_ref[...])
                    + g[:, 2:3] * owin_ref[0])


def nsa_sample_b(q_r, tok, kpool, vpool, ks_new, vs_new, o_cmp, o_win, gate_r, o, page_table):
    b, n_pages = page_table.shape
    rows = q_r.shape[1]
    t = tok.shape[1]
    g = _tile(n_pages, 8, 1)
    per_b = lambda r, w: pl.BlockSpec((1, r, w), lambda bi, p, pt: (bi, 0, 0))
    pools = _page_specs((PAGE_SIZE, HEAD_DIM), o, g, _forward_pages(g))
    kern = functools.partial(_nsa_sample_b_kernel, g_pages=g, n_pages=n_pages, t=t)
    return _paged_call(
        kern, (b, n_pages // g),
        [per_b(rows, HEAD_DIM), pl.BlockSpec((1, t, g * PAGE_SIZE), lambda bi, p, pt: (bi, 0, p)),
         pl.BlockSpec((1, t, PAGE_SIZE), lambda bi, p, pt: (bi, 0, n_pages)),
         per_b(PAGE_SIZE, HEAD_DIM), per_b(PAGE_SIZE, HEAD_DIM),
         per_b(rows, HEAD_DIM), per_b(rows, HEAD_DIM), per_b(rows, 3)] + pools + pools,
        per_b(rows, HEAD_DIM), jax.ShapeDtypeStruct((b, rows, HEAD_DIM), F32),
        [pltpu.VMEM((rows, 1), F32), pltpu.VMEM((rows, 1), F32), pltpu.VMEM((rows, HEAD_DIM), F32)],
        "nsa_sample_b", page_table,
        (q_r, tok, tok, _pad_rows(ks_new, PAGE_SIZE), _pad_rows(vs_new, PAGE_SIZE),
         o_cmp, o_win, gate_r) + (kpool,) * g + (vpool,) * g)


def _sb_sample_kernel(pt_ref, qt2_ref, knew_ref, vnew_ref, *rest, g_pages, t):
    kps, vps = rest[:g_pages], rest[g_pages:2 * g_pages]
    o_ref, carry_ref, acc_ref, z_ref = rest[2 * g_pages:]
    p = pl.program_id(1)
    n, nh = knew_ref.shape[1], knew_ref.shape[2]
    cols = nh * t

    @pl.when(p == 0)
    def _():
        carry_ref[...] = jnp.zeros(carry_ref.shape, F32)
        acc_ref[...] = jnp.zeros(acc_ref.shape, F32)

    iota = lambda shape, d: lax.broadcasted_iota(jnp.int32, shape, d)
    later = jnp.where(iota((n, n), 1) > iota((n, n), 0), 1.0, 0.0).astype(BF16)
    own_head = iota((nh, cols), 1) // t == iota((nh, cols), 0)
    spread = jnp.where(iota((n, n * nh), 1) // nh == iota((n, n * nh), 0), 1.0, 0.0).astype(BF16)
    own_rows = iota((cols, n * nh), 0) // t == iota((cols, n * nh), 1) % nh

    def all_scores(k3s):
        flat = [k3.reshape(n * nh, HEAD_DIM).astype(BF16) for k3 in k3s]
        out = []
        for j in range(0, len(flat) - 1, 2):
            both = _dot(jnp.concatenate(flat[j:j + 2], axis=1), qt2_ref[0])
            out += [both[:, :cols], both[:, cols:]]
        if len(flat) % 2:
            out.append(_dot(flat[-1], qt2_ref[0, :HEAD_DIM, :cols]))
        return out

    def log_weights(z_all, slot, m):
        z_all = z_all.reshape(n, nh, cols)
        z_ref[slot] = jnp.sum(jnp.where(own_head[None], z_all, 0.0), axis=1)
        ls, lneg = _log_sigmoids(z_ref[slot] * HEAD_DIM ** -0.5)
        if m is not None:
            lneg = jnp.where(m, lneg, 0.0)
        hi, lo = _split_bf16(lneg)
        after = _dot(later, hi) + _dot(later, lo)
        return ls + after, after[0:1] + lneg[0:1]

    def weighted_values(v3, log_a, m):
        a = jnp.exp(log_a)
        if m is not None:
            a = jnp.where(m, a, 0.0)
        a2 = jnp.where(own_rows, _dot(a.T.astype(BF16), spread), 0.0)
        return _dot(a2.astype(BF16), v3.reshape(n * nh, HEAD_DIM).astype(BF16))

    @pl.when(p == 0)
    def _():
        m = iota((n, cols), 0) < iota((n, cols), 1) % t
        log_a, total = log_weights(all_scores([knew_ref[0]])[0], g_pages, m)
        acc_ref[...] += weighted_values(vnew_ref[0], log_a, m)
        carry_ref[...] += total

    z_alls = all_scores([kps[j][0, 0] for j in range(g_pages)])
    parts = [log_weights(z_alls[j], j, None) for j in range(g_pages)]
    carry = carry_ref[...]
    acc = acc_ref[...]
    for j in range(g_pages):
        acc = acc + weighted_values(vps[j][0, 0], parts[j][0] + carry, None)
        carry = carry + parts[j][1]
    carry_ref[...] = carry
    acc_ref[...] = acc

    @pl.when(p == pl.num_programs(1) - 1)
    def _():
        o_ref[0] = acc_ref[...]


def sb_sample(qd, kd_new, vd_new, kpool, vpool, o, page_table, b, t):
    n_pages = page_table.shape[1]
    nh = qd.shape[1] // HEAD_DIM
    g = _tile(n_pages, 4, 1)
    qt = qd.reshape(b, t, nh, HEAD_DIM).transpose(0, 3, 2, 1).reshape(b, HEAD_DIM, nh * t)
    zq = jnp.zeros_like(qt)
    qt2 = jnp.concatenate([jnp.concatenate([qt, zq], axis=2), jnp.concatenate([zq, qt], axis=2)], axis=1)
    new = pl.BlockSpec((1, PAGE_SIZE, nh, HEAD_DIM), lambda bi, p, pt: (bi, 0, 0, 0))
    pools = _page_specs((PAGE_SIZE, nh, HEAD_DIM), o, g,
                        lambda bi, p, pt, j: pt[bi, n_pages - 1 - (p * g + j)])
    out = pl.BlockSpec((1, nh * t, HEAD_DIM), lambda bi, p, pt: (bi, 0, 0))
    return _paged_call(
        functools.partial(_sb_sample_kernel, g_pages=g, t=t), (b, n_pages // g),
        [pl.BlockSpec((1, 2 * HEAD_DIM, 2 * nh * t), lambda bi, p, pt: (bi, 0, 0)), new, new]
        + pools + pools,
        out, jax.ShapeDtypeStruct((b, nh * t, HEAD_DIM), F32),
        [pltpu.VMEM((1, nh * t), F32), pltpu.VMEM((nh * t, HEAD_DIM), F32),
         pltpu.VMEM((g + 1, PAGE_SIZE, nh * t), F32)],
        "sb_sample", page_table,
        (qt2, _pad_rows(kd_new, PAGE_SIZE), _pad_rows(vd_new, PAGE_SIZE)) + (kpool,) * g + (vpool,) * g)


def odd_mixer_sample(h, b, t, w, cwk, cwv, o, page_table, c_kc, c_vc, c_ks, c_vs, c_kw, c_vw,
                     c_kd, c_vd):
    m, d = h.shape
    n_pages = page_table.shape[1]
    past = n_pages * PAGE_SIZE
    assert past % CMP_BLOCK == 0 and t < CMP_BLOCK
    tabs = rope_tables(past + jnp.arange(t), b, _tile(m, 1024, 16))
    qc, (kc, vc, ks, vs, kw, vw), gc, qd, kd, vd = _odd_project(h, w, tabs)
    nh = qc.shape[1] // HEAD_DIM
    n_pool = c_kc.shape[1]
    pool1 = lambda a: a.reshape(-1, n_pool, PAGE_SIZE, HEAD_DIM)
    seq = lambda a: a.reshape(b, t, -1)
    kcmp, vcmp = nsa_compress_pages(pool1(c_kc), pool1(c_vc), cwk, cwv, o, page_table)
    wb = c_kw.shape[2]
    kw_all = jnp.concatenate([c_kw[o].reshape(b, wb, HEAD_DIM), seq(kw)], axis=1)
    vw_all = jnp.concatenate([c_vw[o].reshape(b, wb, HEAD_DIM), seq(vw)], axis=1)
    nwp = -(-(wb + t) // LANES) * LANES
    q_r = _rows_ht(qc, b, t, nh)
    o_cmp, o_win, sel = nsa_sample_a(q_r, kcmp, vcmp, _pad_rows(kw_all, nwp), _pad_rows(vw_all, nwp),
                                     past, t, wb + t)
    n_selblk = -(-(past + t) // SEL_BLOCK)
    tok = jnp.repeat(sel[:, :, 0:2 * n_selblk:2], SEL_BLOCK, axis=-1)
    tok = jnp.pad(tok, ((0, 0), (0, 0), (0, (n_pages + 1) * PAGE_SIZE - tok.shape[-1])))
    gate_r = _rows_ht(gc, b, t, nh)
    o_c = nsa_sample_b(q_r, tok, pool1(c_ks), pool1(c_vs), seq(ks), seq(vs), o_cmp, o_win, gate_r,
                       o, page_table)
    nh_d = kd.shape[1] // HEAD_DIM
    heads = lambda a: a.reshape(b, t, nh_d, HEAD_DIM)
    o_d = sb_sample(qd, heads(kd), heads(vd), c_kd, c_vd, o, page_table, b, t)
    y = proj_out(_rows_th(o_c, b, t, nh).astype(BF16), _rows_th(o_d, b, t, nh_d).astype(BF16),
                 w["out_c"], w["out_d"])
    st1 = lambda a: a.reshape(b, t, 1, HEAD_DIM)
    return y, (st1(kc), st1(vc), st1(ks), st1(vs), kw_all[:, t:].reshape(b, wb, 1, HEAD_DIM),
               vw_all[:, t:].reshape(b, wb, 1, HEAD_DIM), kd.reshape(b, t, -1, HEAD_DIM),
               vd.reshape(b, t, -1, HEAD_DIM))


def kernel(x_prompt, x_sample, state_conv, cache_dsa_k, cache_dsa_v, cache_dsa_kidx, cache_nsa_kc, cache_nsa_vc, cache_nsa_ks, cache_nsa_vs, cache_nsa_kw, cache_nsa_vw, cache_sb_k, cache_sb_v, cache_mem_k, cache_mem_v, page_table, mem_prompt, norm_pre, norm_post, norm_mem, w_in_even, conv_w, w_out_even, w_in_odd, cmp_wk, cmp_wv, w_out_odd, w_mq, w_mk, w_mv, w_mo, w_gate, w_up, w_down):
    bp, tp, d = x_prompt.shape
    bs, ts, _ = x_sample.shape
    depth = norm_pre.shape[0]
    n_mem = mem_prompt.shape[1]
    hm = w_mq.shape[2]
    nh_c = nh_d = w_out_odd.shape[1] // 2 // HEAD_DIM
    xp = x_prompt.reshape(bp * tp, d)
    xs = x_sample.reshape(bs * ts, d)
    hp = norm_cast(xp, norm_pre[0, 0])
    hs = norm_cast(xs, norm_pre[0, 0])
    ev_p, ev_s, od_p, od_s, mem_p = [], [], [], [], []
    wg, wu, wd = w_gate, w_up, w_down.astype(BF16)
    for li in range(depth):
        g_pre, g_post = norm_pre[li], norm_post[li]
        if li % 2 == 0:
            e = li // 2
            w = _even_weights(w_in_even, e, conv_w[e], w_out_even[e])
            mp, stp = even_mixer_prompt(hp, bp, tp, w)
            ms, sts = even_mixer_sample(hs, bs, ts, w, e, page_table, state_conv,
                                        cache_dsa_k, cache_dsa_v, cache_dsa_kidx)
            ev_p.append(stp)
            ev_s.append(sts)
        else:
            o = li // 2
            w = _odd_weights(w_in_odd[o], w_out_odd[o], nh_c, nh_d)
            mp, stp = odd_mixer_prompt(hp, bp, tp, w, cmp_wk[o], cmp_wv[o])
            ms, sts = odd_mixer_sample(hs, bs, ts, w, cmp_wk[o], cmp_wv[o], o, page_table,
                                       cache_nsa_kc, cache_nsa_vc, cache_nsa_ks, cache_nsa_vs,
                                       cache_nsa_kw, cache_nsa_vw, cache_sb_k, cache_sb_v)
            od_p.append(stp)
            od_s.append(sts)
        xp, hp = resid_norm(xp, mp, g_post[0], g_pre[1])
        xs, hs = resid_norm(xs, ms, g_post[0], g_pre[1])
        wq, wo = w_mq[li].astype(BF16), w_mo[li].astype(BF16)
        hmem = norm_cast(mem_prompt.reshape(bp * n_mem, d), norm_mem[li])
        mkp = proj(hmem, w_mk[li].astype(BF16), name="proj_mk")
        mvp = proj(hmem, w_mv[li].astype(BF16), name="proj_mv")
        mem_p.append((mkp.reshape(bp, n_mem, H_MEM, HEAD_DIM), mvp.reshape(bp, n_mem, H_MEM, HEAD_DIM)))
        xp, hp = mem_sublayer(hp.reshape(bp, tp, d), xp.reshape(bp, tp, d), wq, mkp.reshape(bp, n_mem, hm),
                              mvp.reshape(bp, n_mem, hm), wo, g_post[1], g_pre[2])
        xs, hs = mem_sublayer(hs.reshape(bs, ts, d), xs.reshape(bs, ts, d), wq,
                              cache_mem_k[li].reshape(bs, n_mem, hm),
                              cache_mem_v[li].reshape(bs, n_mem, hm), wo, g_post[1], g_pre[2])
        xp, hp = xp.reshape(bp * tp, d), hp.reshape(bp * tp, d)
        xs, hs = xs.reshape(bs * ts, d), hs.reshape(bs * ts, d)
        g_next = norm_pre[li + 1, 0] if li + 1 < depth else None
        xp, hp = resid_norm(xp, ffn(hp, wg, wu, wd, li), g_post[2], g_next)
        xs, hs = resid_norm(xs, ffn(hs, wg, wu, wd, li), g_post[2], g_next)
    stack = lambda lst, j: jnp.stack([s[j] for s in lst])
    return ((xp.reshape(bp, tp, d), xs.reshape(bs, ts, d))
            + tuple(stack(ev_p, j) for j in range(4)) + tuple(stack(od_p, j) for j in range(8))
            + (stack(mem_p, 0), stack(mem_p, 1))
            + tuple(stack(ev_s, j) for j in range(4)) + tuple(stack(od_s, j) for j in range(8)))
```

```python
import functools

import numpy as np
import jax
import jax.numpy as jnp
from jax import lax
from jax.experimental import pallas as pl
from jax.experimental.pallas import tpu as pltpu

F32 = jnp.float32
BF16 = jnp.bfloat16

HEAD_DIM = 128
PAGE_SIZE = 128
CONV_K = 3
KV_B = 4
H_IDX = 16
D_IDX = 128
TOPK_MAX = 256
CMP_BLOCK = 32
SEL_BLOCK = 64
N_SEL = 16
WINDOW = 512
FORCE_BONUS = 1.0e6
H_MEM = 4
ROPE_THETA = 10000.0
EPS = 1e-6
NEG = -1e30
LANES = 128
VMEM_LIMIT = 56 * 1024 * 1024


def _cp(*sem):
    return pltpu.CompilerParams(dimension_semantics=sem, vmem_limit_bytes=VMEM_LIMIT)


def _tile(n, pref, mult):
    t = (min(pref, n) // mult) * mult
    while t >= mult:
        if n % t == 0:
            return t
        t -= mult
    return n


def _dot(a, b):
    return jnp.dot(a, b, preferred_element_type=F32)


def _dot_nt(a, b):
    return lax.dot_general(a, b, (((1,), (1,)), ((), ())), preferred_element_type=F32)


def _rms(x, g):
    return x * lax.rsqrt(jnp.mean(x * x, axis=-1, keepdims=True) + EPS) * g


def _rope(y, cos, sin):
    return y * cos + pltpu.roll(y, HEAD_DIM // 2, 1) * sin


def _norm_cast_kernel(x_ref, g_ref, o_ref):
    o_ref[...] = _rms(x_ref[...], g_ref[...]).astype(o_ref.dtype)


def norm_cast(x, g):
    m, d = x.shape
    tm = _tile(m, 256, 16)
    row = pl.BlockSpec((tm, d), lambda i: (i, 0))
    return pl.pallas_call(
        _norm_cast_kernel, grid=(m // tm,),
        in_specs=[row, pl.BlockSpec((1, d), lambda i: (0, 0))],
        out_specs=row, out_shape=jax.ShapeDtypeStruct((m, d), BF16),
        compiler_params=_cp("arbitrary"), name="norm_cast")(x, g.reshape(1, d))


def _resid_norm_kernel(x_ref, y_ref, gp_ref, gn_ref, xo_ref, h_ref):
    xn = x_ref[...] + _rms(y_ref[...], gp_ref[...])
    xo_ref[...] = xn
    h_ref[...] = _rms(xn, gn_ref[...]).astype(h_ref.dtype)


def _resid_kernel(x_ref, y_ref, gp_ref, xo_ref):
    xo_ref[...] = x_ref[...] + _rms(y_ref[...], gp_ref[...])


def resid_norm(x, y, g_post, g_next):
    m, d = x.shape
    tm = _tile(m, 256, 16)
    row = pl.BlockSpec((tm, d), lambda i: (i, 0))
    vec = pl.BlockSpec((1, d), lambda i: (0, 0))
    if g_next is None:
        return pl.pallas_call(
            _resid_kernel, grid=(m // tm,), in_specs=[row, row, vec], out_specs=row,
            out_shape=jax.ShapeDtypeStruct((m, d), F32),
            compiler_params=_cp("arbitrary"), name="resid")(x, y, g_post.reshape(1, d)), None
    return pl.pallas_call(
        _resid_norm_kernel, grid=(m // tm,), in_specs=[row, row, vec, vec],
        out_specs=[row, row],
        out_shape=[jax.ShapeDtypeStruct((m, d), F32), jax.ShapeDtypeStruct((m, d), BF16)],
        compiler_params=_cp("arbitrary"), name="resid_norm")(
            x, y, g_post.reshape(1, d), g_next.reshape(1, d))


def _mm_kernel(*refs, nx, nw, ne, groups, epilogue):
    xr, wr = refs[:nx], refs[nx:nx + nw]
    er, orf = refs[nx + nw:nx + nw + ne], refs[nx + nw + ne:]
    accs = []
    for grp in groups:
        acc = None
        for xi, wi in grp:
            w = wr[wi][...]
            d = _dot(xr[xi][...], w if w.dtype == BF16 else w.astype(BF16))
            acc = d if acc is None else acc + d
        accs.append(acc)
    epilogue(accs, er, orf)


def _ep_plain(accs, er, orf):
    orf[0][...] = accs[0].astype(orf[0].dtype)


def _ep_rope(accs, er, orf):
    cos, sin = er[0][...], er[1][...]
    y = accs[0]
    for c in range(y.shape[1] // HEAD_DIM):
        sl = slice(c * HEAD_DIM, (c + 1) * HEAD_DIM)
        orf[0][:, sl] = _rope(y[:, sl], cos, sin).astype(orf[0].dtype)


def _ep_split(accs, er, orf, *, rope_chunks):
    cos, sin = er[0][...], er[1][...]
    y = accs[0]
    for c in range(len(orf)):
        yc = y[:, c * HEAD_DIM:(c + 1) * HEAD_DIM]
        if c in rope_chunks:
            yc = _rope(yc, cos, sin)
        orf[c][...] = yc.astype(orf[c].dtype)


def matmul(xs, ws, groups, epilogue, out_dtypes, *, tables=None, n_tab_blocks=1,
           tm_pref=1024, tn_pref=512, split_out=0, layer=None, single_buffer_x=False, name="matmul"):
    m = xs[0].shape[0]
    n = ws[0].shape[-1]
    if tables is not None:
        tm = tables[0].shape[0] // n_tab_blocks
    else:
        tm = _tile(m, tm_pref, 16)
    tn = n if (split_out or n % LANES) else _tile(n, tn_pref, LANES)
    xmode = dict(pipeline_mode=pl.Buffered(1)) if single_buffer_x else {}
    in_specs = [pl.BlockSpec((tm, x.shape[1]), lambda i, j: (i, 0), **xmode) for x in xs]
    if layer is None:
        in_specs += [pl.BlockSpec((w.shape[0], tn), lambda i, j: (0, j)) for w in ws]
    else:
        in_specs += [pl.BlockSpec((None, w.shape[1], tn), lambda i, j: (layer, 0, j)) for w in ws]
    extras = []
    if tables is not None:
        nb = n_tab_blocks
        in_specs += [pl.BlockSpec((tm, HEAD_DIM), lambda i, j: (i % nb, 0))] * 2
        extras = list(tables)
    if split_out:
        out_specs = [pl.BlockSpec((tm, HEAD_DIM), lambda i, j: (i, 0))] * split_out
        out_shape = [jax.ShapeDtypeStruct((m, HEAD_DIM), dt) for dt in out_dtypes]
    else:
        out_specs = [pl.BlockSpec((tm, tn), lambda i, j: (i, j))]
        out_shape = [jax.ShapeDtypeStruct((m, n), out_dtypes[0])]
    kern = functools.partial(_mm_kernel, nx=len(xs), nw=len(ws), ne=len(extras),
                             groups=groups, epilogue=epilogue)
    out = pl.pallas_call(
        kern, grid=(m // tm, n // tn), in_specs=in_specs, out_specs=out_specs,
        out_shape=out_shape, compiler_params=_cp("arbitrary", "arbitrary"), name=name)(
            *xs, *ws, *extras)
    return out if split_out else out[0]


def proj(h, w, dtype=F32, name="proj"):
    return matmul([h], [w], [[(0, 0)]], _ep_plain, [dtype], name=name)


def proj_rope(h, w, tabs, dtype, name="proj_rope"):
    cos, sin, nb = tabs
    return matmul([h], [w], [[(0, 0)]], _ep_rope, [dtype], tables=(cos, sin),
                  n_tab_blocks=nb, name=name)


def proj_out(xa, xb, wa, wb, name="proj_out"):
    return matmul([xa, xb], [wa, wb], [[(0, 0), (1, 1)]], _ep_plain, [F32], name=name)


def rope_tables(pos, reps, tm):
    half = HEAD_DIM // 2
    inv = ROPE_THETA ** (-jnp.arange(half, dtype=F32) / half)
    ang = pos.astype(F32)[:, None] * inv[None, :]
    cos, sin = jnp.cos(ang), jnp.sin(ang)
    cos = jnp.concatenate([cos, cos], axis=-1)
    sin = jnp.concatenate([-sin, sin], axis=-1)
    t = pos.shape[0]
    if tm > t:
        cos, sin = jnp.tile(cos, (tm // t, 1)), jnp.tile(sin, (tm // t, 1))
        return cos, sin, 1
    return cos, sin, t // tm


def _conv_kernel(x_ref, wb_ref, wc_ref, wx_ref, cw_ref, init_ref, ya_ref, st_ref, carry_ref):
    i = pl.program_id(2)

    @pl.when(i == 0)
    def _():
        carry_ref[...] = init_ref[0]

    x = x_ref[0]
    bg = _dot(x, wb_ref[...])
    u = _dot(x, wc_ref[...]) * _dot(x, wx_ref[...])
    tm = u.shape[0]
    c = carry_ref[...]
    rows = lax.broadcasted_iota(jnp.int32, u.shape, 0)
    u1 = jnp.where(rows == 0, c[1:2], pltpu.roll(u, 1, 0))
    u2 = jnp.where(rows == 0, c[0:1], jnp.where(rows == 1, c[1:2], pltpu.roll(u, 2, 0)))
    cw = cw_ref[...]
    conv = cw[0:1] * u2 + cw[1:2] * u1 + cw[2:3] * u
    ya_ref[0] = (bg * conv).astype(ya_ref.dtype)
    new = u[tm - (CONV_K - 1):tm]
    carry_ref[...] = new
    st_ref[0] = new


def conv_mixer(h3, wb, wc, wx, cw, init):
    b, t, d = h3.shape
    c = wb.shape[1]
    tm = _tile(t, 1024, 16)
    tn = _tile(c, 512, LANES)
    wspec = pl.BlockSpec((d, tn), lambda j, bi, i: (0, j))
    return pl.pallas_call(
        _conv_kernel, grid=(c // tn, b, t // tm),
        in_specs=[pl.BlockSpec((1, tm, d), lambda j, bi, i: (bi, i, 0)), wspec, wspec, wspec,
                  pl.BlockSpec((CONV_K, tn), lambda j, bi, i: (0, j)),
                  pl.BlockSpec((1, CONV_K - 1, tn), lambda j, bi, i: (bi, 0, j))],
        out_specs=[pl.BlockSpec((1, tm, tn), lambda j, bi, i: (bi, i, j)),
                   pl.BlockSpec((1, CONV_K - 1, tn), lambda j, bi, i: (bi, 0, j))],
        out_shape=[jax.ShapeDtypeStruct((b, t, c), BF16),
                   jax.ShapeDtypeStruct((b, CONV_K - 1, c), F32)],
        scratch_shapes=[pltpu.VMEM((CONV_K - 1, tn), F32)],
        compiler_params=_cp("arbitrary", "arbitrary", "arbitrary"), name="conv_mixer")(
            h3, wb, wc, wx, cw, init)


def _stack_heads(q_ref, heads):
    return jnp.concatenate([q_ref[:, h * HEAD_DIM:(h + 1) * HEAD_DIM] for h in heads], axis=0)


LOG2E = 1.4426950408889634


def _with_ones_column(v):
    ones = jnp.where(lax.broadcasted_iota(jnp.int32, v.shape, 1) == 0, 1.0, 0.0).astype(v.dtype)
    return jnp.concatenate([v, ones], axis=1)


def _softmax_av(qk, scale, bias, v_ones):
    s = qk * (scale * LOG2E) + bias[None]
    e = jnp.exp2(s - jnp.max(s, axis=-1, keepdims=True))
    r, tq, n = e.shape
    o = _dot(e.reshape(r * tq, n).astype(BF16), v_ones)
    return o[:, :HEAD_DIM] / o[:, HEAD_DIM:HEAD_DIM + 1]


def _sort_key(x):
    bits = lax.bitcast_convert_type(x + 0.0, jnp.int32)
    return jnp.where(bits < 0, bits ^ jnp.int32(0x7FFFFFFF), bits)


INT_MIN = -2 ** 31


def _kth_largest_key(key, k):
    def body(it, othr):
        bit = lax.shift_left(jnp.int32(1), jnp.int32(31) - it)
        cand = othr | bit
        cnt = jnp.sum(jnp.where(key >= (cand ^ jnp.int32(INT_MIN)), 1.0, 0.0), axis=1, keepdims=True)
        return jnp.where(cnt >= k, cand, othr)
    othr = lax.fori_loop(0, 32, body, jnp.zeros((key.shape[0], 1), jnp.int32))
    return othr ^ jnp.int32(INT_MIN)


def _topk_mask(key, k, scratch_ref):
    thr = _kth_largest_key(key, k)
    live = key > jnp.int32(INT_MIN)
    ge = (key >= thr) & live
    scratch_ref[...] = jnp.where(ge, 1.0, 0.0)
    n_ge = jnp.sum(jnp.where(ge, 1.0, 0.0), axis=1, keepdims=True)

    @pl.when(jnp.max(n_ge) > k)
    def _():
        gt = key > thr
        eq = (key == thr) & live
        need = k - jnp.sum(jnp.where(gt, 1.0, 0.0), axis=1, keepdims=True)
        r = lax.broadcasted_iota(jnp.int32, (LANES, LANES), 0)
        c = lax.broadcasted_iota(jnp.int32, (LANES, LANES), 1)
        before = jnp.where(r < c, 1.0, 0.0).astype(BF16)
        run = jnp.zeros_like(need)
        for ch in range(key.shape[1] // LANES):
            sl = slice(ch * LANES, (ch + 1) * LANES)
            e = jnp.where(eq[:, sl], 1.0, 0.0)
            pre = _dot(e.astype(BF16), before) + run
            scratch_ref[:, sl] = jnp.where(gt[:, sl], 1.0, e * jnp.where(pre < need, 1.0, 0.0))
            run = run + jnp.sum(e, axis=1, keepdims=True)


CAUSAL_BUCKETS = 4


def _by_key_extent(i, tq, t, fn):
    nb = CAUSAL_BUCKETS if t % (CAUSAL_BUCKETS * tq) == 0 else 1
    size = t // nb
    for bkt in range(nb):
        @pl.when((i * tq) // size == bkt)
        def _():
            fn((bkt + 1) * size)


def _dsa_prompt_kernel(qi_ref, wi_ref, ki_ref, q_ref, k_ref, v_ref, o_ref,
                       kib_ref, kb_ref, vb_ref, mask_ref, *, topk):
    i = pl.program_id(1)

    @pl.when(i == 0)
    def _():
        kib_ref[...] = ki_ref[...].astype(BF16)
        kb_ref[...] = k_ref[...].astype(BF16)
        for g in range(KV_B):
            vb_ref[g] = _with_ones_column(v_ref[:, g * HEAD_DIM:(g + 1) * HEAD_DIM].astype(BF16))

    tq = q_ref.shape[0]
    t = ki_ref.shape[0]
    wi = wi_ref[...]
    rep = q_ref.shape[1] // HEAD_DIM // KV_B

    def attend(n):
        kib = kib_ref[0:n]
        score = jnp.zeros((tq, n), F32)
        for h in range(H_IDX):
            s = _dot_nt(qi_ref[:, h * D_IDX:(h + 1) * D_IDX], kib) * D_IDX ** -0.5
            score = score + jnp.maximum(s, 0.0) * wi[:, h:h + 1]
        score = score * H_IDX ** -0.5
        qpos = i * tq + lax.broadcasted_iota(jnp.int32, (tq, n), 0)
        kpos = lax.broadcasted_iota(jnp.int32, (tq, n), 1)
        key = jnp.where(kpos <= qpos, _sort_key(score), jnp.int32(INT_MIN))
        sel_ref = mask_ref.at[:, 0:n]
        _topk_mask(key, topk, sel_ref)
        bias = jnp.where(sel_ref[...] > 0.0, 0.0, NEG)
        for g in range(KV_B):
            qs = _stack_heads(q_ref, range(g * rep, (g + 1) * rep))
            sl = slice(g * HEAD_DIM, (g + 1) * HEAD_DIM)
            qk = _dot_nt(qs, kb_ref[0:n, sl]).reshape(rep, tq, n)
            o = _softmax_av(qk, HEAD_DIM ** -0.5, bias, vb_ref[g, 0:n])
            for r in range(rep):
                h = g * rep + r
                o_ref[:, h * HEAD_DIM:(h + 1) * HEAD_DIM] = o[r * tq:(r + 1) * tq].astype(o_ref.dtype)

    _by_key_extent(i, tq, t, attend)


def dsa_prompt(qi, wi, ki, q, k, v, b, t):
    m = q.shape[0]
    tq = _tile(t, 128, 16)
    nq = t // tq
    topk = min(TOPK_MAX, t // 4)
    qrow = lambda w: pl.BlockSpec((tq, w), lambda bi, i: (bi * nq + i, 0))
    full = lambda w: pl.BlockSpec((t, w), lambda bi, i: (bi, 0))
    return pl.pallas_call(
        functools.partial(_dsa_prompt_kernel, topk=topk), grid=(b, nq),
        in_specs=[qrow(qi.shape[1]), qrow(wi.shape[1]), full(ki.shape[1]),
                  qrow(q.shape[1]), full(k.shape[1]), full(v.shape[1])],
        out_specs=qrow(q.shape[1]),
        out_shape=jax.ShapeDtypeStruct((m, q.shape[1]), BF16),
        scratch_shapes=[pltpu.VMEM((t, ki.shape[1]), BF16), pltpu.VMEM((t, k.shape[1]), BF16),
                        pltpu.VMEM((KV_B, t, 2 * HEAD_DIM), BF16), pltpu.VMEM((tq, t), F32)],
        compiler_params=_cp("arbitrary", "arbitrary"), name="dsa_prompt")(qi, wi, ki, q, k, v)


def _masked_softmax(s, mask):
    m = jnp.max(jnp.where(mask, s, NEG), axis=-1, keepdims=True)
    m = jnp.where(m > 0.5 * NEG, m, 0.0)
    e = jnp.where(mask, jnp.exp(s - m), 0.0)
    return e / jnp.maximum(jnp.sum(e, axis=-1, keepdims=True), 1e-30)


def _pair_sums(imp, col):
    n = imp.shape[1]
    return imp + jnp.where(col % 2 == 0, pltpu.roll(imp, n - 1, 1), pltpu.roll(imp, 1, 1))


def _select_blocks(bs, col, qpos, n_selblk):
    blk = col // 2
    cur = qpos // SEL_BLOCK
    forced = (blk == 0) | (blk == cur) | (blk == cur - 1)
    admiss = (blk * SEL_BLOCK <= qpos) & (blk < n_selblk)
    work = jnp.where(admiss, jnp.where(forced, bs + FORCE_BONUS, bs), NEG)
    sel = jnp.zeros(bs.shape, jnp.bool_)
    big = jnp.int32(2 ** 30)
    for _ in range(min(N_SEL, n_selblk)):
        mx = jnp.max(work, axis=1, keepdims=True)
        idx = jnp.min(jnp.where(work == mx, col, big), axis=1, keepdims=True)
        pick = blk == idx // 2
        sel = sel | pick
        work = jnp.where(pick, -3e38, work)
    return jnp.where(sel & admiss, 1.0, 0.0)


def _nsa_prompt_kernel(q_ref, gc_ref, kc_ref, vc_ref, ks_ref, vs_ref, kw_ref, vw_ref,
                       cwk_ref, cwv_ref, o_ref,
                       kcmp_ref, vcmp_ref, ksb_ref, vsb_ref, kwb_ref, vwb_ref, *, win):
    i = pl.program_id(1)
    tq = q_ref.shape[0]
    t = kc_ref.shape[0]
    nb = t // CMP_BLOCK
    ncp = kcmp_ref.shape[0]
    n_selblk = -(-t // SEL_BLOCK)
    nh = q_ref.shape[1] // HEAD_DIM
    scale = HEAD_DIM ** -0.5

    @pl.when(i == 0)
    def _():
        kcmp_ref[...] = jnp.zeros(kcmp_ref.shape, kcmp_ref.dtype)
        vcmp_ref[...] = jnp.zeros(vcmp_ref.shape, vcmp_ref.dtype)
        kc = kc_ref[...].reshape(nb, CMP_BLOCK, HEAD_DIM)
        vc = vc_ref[...].reshape(nb, CMP_BLOCK, HEAD_DIM)
        kcmp_ref[0:nb] = jnp.sum(kc * cwk_ref[...][None], axis=1).astype(BF16)
        vcmp_ref[0:nb] = jnp.sum(vc * cwv_ref[...][None], axis=1).astype(BF16)
        ksb_ref[...] = ks_ref[...].astype(BF16)
        vsb_ref[...] = _with_ones_column(vs_ref[...].astype(BF16))
        kwb_ref[...] = kw_ref[...].astype(BF16)
        vwb_ref[...] = _with_ones_column(vw_ref[...].astype(BF16))

    qs = _stack_heads(q_ref, range(nh))
    col = lax.broadcasted_iota(jnp.int32, (tq, ncp), 1)
    qpos_c = i * tq + lax.broadcasted_iota(jnp.int32, (tq, ncp), 0)
    cmask = ((col + 1) * CMP_BLOCK - 1 <= qpos_c) & (col < nb)
    s = (_dot_nt(qs, kcmp_ref[...]) * scale).reshape(nh, tq, ncp)
    p = _masked_softmax(s, cmask[None])
    o_cmp = _dot(p.reshape(nh * tq, ncp).astype(BF16), vcmp_ref[...])
    imp = jnp.sum(p, axis=0)
    sel = _select_blocks(_pair_sums(imp, col), col, qpos_c, n_selblk).astype(BF16)
    start = pl.multiple_of(jnp.clip(i * tq - WINDOW, 0, t - win), 16)
    qpos_w = i * tq + lax.broadcasted_iota(jnp.int32, (tq, win), 0)
    kpos_w = start + lax.broadcasted_iota(jnp.int32, (tq, win), 1)
    bias_win = jnp.where((kpos_w <= qpos_w) & (kpos_w > qpos_w - WINDOW), 0.0, NEG)
    kwin = kwb_ref[pl.ds(start, win), :]
    vwin = vwb_ref[pl.ds(start, win), :]
    gate = jax.nn.sigmoid(gc_ref[...])
    grp = 4
    o_wins = []
    for hg in range(nh // grp):
        q4 = qs[hg * grp * tq:(hg + 1) * grp * tq]
        qk_win = _dot_nt(q4, kwin).reshape(grp, tq, win)
        o_wins.append(_softmax_av(qk_win, scale, bias_win, vwin))

    def attend(n):
        er = lax.broadcasted_iota(jnp.int32, (ncp, n), 0)
        ec = lax.broadcasted_iota(jnp.int32, (ncp, n), 1)
        expand = jnp.where(er == 2 * (ec // SEL_BLOCK), 1.0, 0.0).astype(BF16)
        tok = _dot(sel, expand)
        qpos = i * tq + lax.broadcasted_iota(jnp.int32, (tq, n), 0)
        kpos = lax.broadcasted_iota(jnp.int32, (tq, n), 1)
        bias_sel = jnp.where((tok > 0.5) & (kpos <= qpos), 0.0, NEG)
        for hg in range(nh // grp):
            q4 = qs[hg * grp * tq:(hg + 1) * grp * tq]
            qk_sel = _dot_nt(q4, ksb_ref[0:n]).reshape(grp, tq, n)
            o_sel = _softmax_av(qk_sel, scale, bias_sel, vsb_ref[0:n])
            o_win = o_wins[hg]
            for r in range(grp):
                h = hg * grp + r
                rows = slice(r * tq, (r + 1) * tq)
                o = (gate[:, 3 * h:3 * h + 1] * o_cmp[h * tq:(h + 1) * tq]
                     + gate[:, 3 * h + 1:3 * h + 2] * o_sel[rows] + gate[:, 3 * h + 2:3 * h + 3] * o_win[rows])
                o_ref[:, h * HEAD_DIM:(h + 1) * HEAD_DIM] = o.astype(o_ref.dtype)

    _by_key_extent(i, tq, t, attend)


def nsa_prompt(q, gc, kc, vc, ks, vs, kw, vw, cwk, cwv, b, t):
    m, hq = q.shape
    tq = _tile(t, 128, 16)
    nq = t // tq
    win = min(WINDOW + tq, t)
    ncp = LANES
    assert t // CMP_BLOCK <= ncp and t % SEL_BLOCK == 0
    qrow = lambda w: pl.BlockSpec((tq, w), lambda bi, i: (bi * nq + i, 0))
    full = pl.BlockSpec((t, HEAD_DIM), lambda bi, i: (bi, 0))
    cw = pl.BlockSpec((CMP_BLOCK, HEAD_DIM), lambda bi, i: (0, 0))
    kv = pltpu.VMEM((t, HEAD_DIM), BF16)
    kv_ones = pltpu.VMEM((t, 2 * HEAD_DIM), BF16)
    return pl.pallas_call(
        functools.partial(_nsa_prompt_kernel, win=win), grid=(b, nq),
        in_specs=[qrow(hq), qrow(gc.shape[1])] + [full] * 6 + [cw, cw],
        out_specs=qrow(hq), out_shape=jax.ShapeDtypeStruct((m, hq), BF16),
        scratch_shapes=[pltpu.VMEM((ncp, HEAD_DIM), BF16), pltpu.VMEM((ncp, HEAD_DIM), BF16),
                        kv, kv_ones, kv, kv_ones],
        compiler_params=_cp("arbitrary", "arbitrary"), name="nsa_prompt")(
            q, gc, kc, vc, ks, vs, kw, vw, cwk, cwv)


def _log_sigmoids(z):
    ls = jnp.minimum(z, 0.0) - jnp.log(1.0 + jnp.exp(-jnp.abs(z)))
    return ls, ls - z


def _split_bf16(x):
    hi = x.astype(BF16)
    return hi, (x - hi.astype(F32)).astype(BF16)


SB_HEADS_PER_STEP = 2


def _sb_prompt_kernel(q_ref, k_ref, v_ref, o_ref):
    i = pl.program_id(2)
    tq = q_ref.shape[0]
    nhs = q_ref.shape[1] // HEAD_DIM
    r = lax.broadcasted_iota(jnp.int32, (tq, tq), 0)
    c = lax.broadcasted_iota(jnp.int32, (tq, tq), 1)
    later = jnp.where(r > c, 1.0, 0.0).astype(BF16)
    before = c < r
    qs = [q_ref[:, h * HEAD_DIM:(h + 1) * HEAD_DIM] for h in range(nhs)]
    cols = lambda h: slice(h * HEAD_DIM, (h + 1) * HEAD_DIM)
    block_off = lambda j: pl.multiple_of(jnp.maximum(j, 0) * tq, tq)

    def scores(h, off):
        return _dot_nt(qs[h], k_ref[pl.ds(off, tq), cols(h)].astype(BF16)) * HEAD_DIM ** -0.5

    def weights(z, carry, m):
        ls, lneg = _log_sigmoids(z)
        if m is not None:
            lneg = jnp.where(m, lneg, 0.0)
        hi, lo = _split_bf16(lneg)
        after = _dot(hi, later) + _dot(lo, later)
        a = jnp.exp(ls + after + carry)
        if m is not None:
            a = jnp.where(m, a, 0.0)
        return a.astype(BF16), carry + after[:, 0:1] + lneg[:, 0:1]

    def weighted_values(h, a, off, acc):
        return acc + _dot(a, v_ref[pl.ds(off, tq), cols(h)].astype(BF16))

    st = []
    for h in range(nhs):
        a, carry = weights(scores(h, block_off(i)), jnp.zeros((tq, 1), F32), before)
        st.append((scores(h, block_off(i - 1)), a, carry, jnp.zeros((tq, HEAD_DIM), F32)))

    def body(jj, st):
        out = []
        for h in range(nhs):
            z, a_prev, carry, acc = st[h]
            z_next = scores(h, block_off(i - 2 - jj))
            acc = weighted_values(h, a_prev, block_off(i - jj), acc)
            a, carry = weights(z, carry, None)
            out.append((z_next, a, carry, acc))
        return tuple(out)

    st = lax.fori_loop(0, i, body, tuple(st))
    for h in range(nhs):
        acc = weighted_values(h, st[h][1], 0, st[h][3])
        o_ref[:, h * HEAD_DIM:(h + 1) * HEAD_DIM] = acc.astype(o_ref.dtype)


def sb_prompt(q, k, v, b, t):
    m, hq = q.shape
    nhs = SB_HEADS_PER_STEP
    tq = _tile(t, 256, 16)
    nq = t // tq
    w = nhs * HEAD_DIM
    qrow = pl.BlockSpec((tq, w), lambda bi, h, i: (bi * nq + i, h))
    full = pl.BlockSpec((t, w), lambda bi, h, i: (bi, h))
    return pl.pallas_call(
        _sb_prompt_kernel, grid=(b, hq // w, nq),
        in_specs=[qrow, full, full], out_specs=qrow,
        out_shape=jax.ShapeDtypeStruct((m, hq), BF16),
        compiler_params=_cp("arbitrary", "arbitrary", "arbitrary"), name="sb_prompt")(q, k, v)


def _mem_kernel(h_ref, x_ref, wq_ref, mk_ref, mv_ref, wo_ref, gp_ref, gn_ref, xo_ref, ho_ref):
    q = _dot(h_ref[0], wq_ref[...])
    outs = []
    for hh in range(H_MEM):
        sl = slice(hh * HEAD_DIM, (hh + 1) * HEAD_DIM)
        s = _dot_nt(q[:, sl].astype(BF16), mk_ref[0, :, sl].astype(BF16)) * HEAD_DIM ** -0.5
        e = jnp.exp(s - jnp.max(s, axis=-1, keepdims=True))
        o = _dot(e.astype(BF16), mv_ref[0, :, sl].astype(BF16))
        outs.append(o / jnp.sum(e, axis=-1, keepdims=True))
    y = _dot(jnp.concatenate(outs, axis=1).astype(BF16), wo_ref[...])
    xn = x_ref[0] + _rms(y, gp_ref[...])
    xo_ref[0] = xn
    ho_ref[0] = _rms(xn, gn_ref[...]).astype(ho_ref.dtype)


def mem_sublayer(h3, x3, wq, mk, mv, wo, g_post, g_next):
    b, t, d = h3.shape
    tq = _tile(t, 256, 16)
    nm, hm = mk.shape[1], mk.shape[2]
    row = pl.BlockSpec((1, tq, d), lambda bi, i: (bi, i, 0))
    mem = pl.BlockSpec((1, nm, hm), lambda bi, i: (bi, 0, 0))
    vec = pl.BlockSpec((1, d), lambda bi, i: (0, 0))
    return pl.pallas_call(
        _mem_kernel, grid=(b, t // tq),
        in_specs=[row, row, pl.BlockSpec((d, hm), lambda bi, i: (0, 0)), mem, mem,
                  pl.BlockSpec((hm, d), lambda bi, i: (0, 0)), vec, vec],
        out_specs=[row, row],
        out_shape=[jax.ShapeDtypeStruct((b, t, d), F32), jax.ShapeDtypeStruct((b, t, d), BF16)],
        compiler_params=_cp("arbitrary", "arbitrary"), name="mem_sublayer")(
            h3, x3, wq, mk, mv, wo, g_post.reshape(1, d), g_next.reshape(1, d))


def _ep_swiglu(accs, er, orf):
    g, u = accs
    orf[0][...] = ((g * jax.nn.sigmoid(g)) * u).astype(orf[0].dtype)


def ffn(h, wg, wu, wd, li):
    act = matmul([h], [wg, wu], [[(0, 0)], [(0, 1)]], _ep_swiglu, [BF16], tm_pref=2048,
                 single_buffer_x=True, layer=li, name="ffn_gate_up")
    return matmul([act], [wd], [[(0, 0)]], _ep_plain, [F32], tm_pref=512, layer=li, name="ffn_down")


def _odd_weights(w_in, w_out, nh_c, nh_d):
    sizes = [nh_c * HEAD_DIM] + [HEAD_DIM] * 6 + [nh_c * 3] + [nh_d * HEAD_DIM] * 3
    offs = np.cumsum([0] + sizes)
    cut = lambda a, b_: w_in[:, offs[a]:offs[b_]].astype(BF16)
    w = {"qc": cut(0, 1), "kv6": cut(1, 7), "gc": cut(7, 8), "qd": cut(8, 9), "kd": cut(9, 10),
         "vd": cut(10, 11)}
    w["out_c"] = w_out[:nh_c * HEAD_DIM].astype(BF16)
    w["out_d"] = w_out[nh_c * HEAD_DIM:].astype(BF16)
    return w


def _odd_project(h, w, tabs):
    cos, sin, nb = tabs
    qc = proj_rope(h, w["qc"], tabs, BF16, name="proj_qc")
    kv6 = matmul([h], [w["kv6"]], [[(0, 0)]], functools.partial(_ep_split, rope_chunks=(0, 2, 4)),
                 [F32] * 6, tables=(cos, sin), n_tab_blocks=nb, split_out=6, name="proj_kv6")
    gc = proj(h, w["gc"], name="proj_gc")
    qd = proj(h, w["qd"], BF16, name="proj_qd")
    kd = proj(h, w["kd"], name="proj_kd")
    vd = proj(h, w["vd"], name="proj_vd")
    return qc, kv6, gc, qd, kd, vd


def odd_mixer_prompt(h, b, t, w, cwk, cwv):
    m, d = h.shape
    tabs = rope_tables(jnp.arange(t), b, _tile(m, 1024, 16))
    qc, (kc, vc, ks, vs, kw, vw), gc, qd, kd, vd = _odd_project(h, w, tabs)
    o_c = nsa_prompt(qc, gc, kc, vc, ks, vs, kw, vw, cwk, cwv, b, t)
    o_d = sb_prompt(qd, kd, vd, b, t)
    y = proj_out(o_c, o_d, w["out_c"], w["out_d"])
    nw = min(WINDOW, t)
    st1 = lambda a: a.reshape(b, t, 1, HEAD_DIM)
    sth = lambda a: a.reshape(b, t, -1, HEAD_DIM)
    return y, (st1(kc), st1(vc), st1(ks), st1(vs), st1(kw)[:, t - nw:], st1(vw)[:, t - nw:],
               sth(kd), sth(vd))


def _even_weights(w_in, conv_w, w_out):
    c = conv_w.shape[1]
    hq = w_out.shape[0] - c
    sizes = [c, c, c, hq, KV_B * HEAD_DIM, KV_B * HEAD_DIM, H_IDX * D_IDX, D_IDX, H_IDX]
    offs = np.cumsum([0] + sizes)
    names = ["bg", "cg", "xa", "q", "k", "v", "qi", "ki", "wi"]
    w = {n: w_in[:, offs[j]:offs[j + 1]].astype(BF16) for j, n in enumerate(names)}
    w["conv_w"] = conv_w
    w["out_a"] = w_out[:c].astype(BF16)
    w["out_b"] = w_out[c:].astype(BF16)
    return w


def _even_project(h, w, tabs):
    q = proj_rope(h, w["q"], tabs, BF16, name="proj_q")
    k = proj_rope(h, w["k"], tabs, F32, name="proj_k")
    v = proj(h, w["v"], name="proj_v")
    qi = proj_rope(h, w["qi"], tabs, BF16, name="proj_qi")
    ki = proj_rope(h, w["ki"], tabs, F32, name="proj_ki")
    wi = proj(h, w["wi"], name="proj_wi")
    return q, k, v, qi, ki, wi


def even_mixer_prompt(h, b, t, w):
    m, d = h.shape
    tabs = rope_tables(jnp.arange(t), b, _tile(m, 1024, 16))
    init = jnp.zeros((b, CONV_K - 1, w["bg"].shape[1]), F32)
    ya, conv_state = conv_mixer(h.reshape(b, t, d), w["bg"], w["cg"], w["xa"], w["conv_w"], init)
    q, k, v, qi, ki, wi = _even_project(h, w, tabs)
    ob = dsa_prompt(qi, wi, ki, q, k, v, b, t)
    y = proj_out(ya.reshape(m, -1), ob, w["out_a"], w["out_b"])
    return y, (conv_state, k.reshape(b, t, KV_B, HEAD_DIM), v.reshape(b, t, KV_B, HEAD_DIM),
               ki.reshape(b, t, D_IDX))


PAGES_PER_STEP = 16
PAGES_PER_STEP_SB = 8


def _paged_call(kern, grid, in_specs, out_specs, out_shape, scratch, name, page_table, args):
    gs = pltpu.PrefetchScalarGridSpec(num_scalar_prefetch=1, grid=grid, in_specs=in_specs,
                                      out_specs=out_specs, scratch_shapes=scratch)
    return pl.pallas_call(kern, grid_spec=gs, out_shape=out_shape,
                          compiler_params=_cp("arbitrary", "arbitrary"), name=name)(page_table, *args)


def _pad_rows(a, rows):
    return jnp.pad(a, ((0, 0), (0, rows - a.shape[1])) + ((0, 0),) * (a.ndim - 2))


def _page_specs(block_tail, layer, g_pages, page_of):
    zeros = (0,) * len(block_tail)

    def spec(j):
        return pl.BlockSpec((1, 1) + block_tail,
                            lambda bi, p, pt: (layer, page_of(bi, p, pt, j)) + zeros)
    return [spec(j) for j in range(g_pages)]


def _forward_pages(g_pages):
    return lambda bi, p, pt, j: pt[bi, p * g_pages + j]


def _rows_ht(a, b, t, nh):
    w = a.shape[1] // nh
    return a.reshape(b, t, nh, w).transpose(0, 2, 1, 3).reshape(b, nh * t, w)


def _rows_th(a, b, t, nh):
    w = a.shape[2]
    return a.reshape(b, nh, t, w).transpose(0, 2, 1, 3).reshape(b * t, nh * w)


def _dsa_scores_kernel(pt_ref, qi_ref, wi_ref, new_ref, *rest, g_pages, t):
    pools, (o_ref, onew_ref) = rest[:g_pages], rest[g_pages:]
    qi, wi = qi_ref[0], wi_ref[0]

    def scores(kb):
        s = _dot_nt(qi, kb) * D_IDX ** -0.5
        s = jnp.maximum(s, 0.0) * wi
        return jnp.sum(s.reshape(H_IDX, t, kb.shape[0]), axis=0) * H_IDX ** -0.5

    o_ref[0] = scores(jnp.concatenate([r[0, 0] for r in pools], axis=0).astype(BF16))

    @pl.when(pl.program_id(1) == 0)
    def _():
        onew_ref[0] = scores(new_ref[0].astype(BF16))


def dsa_sample_scores(qi_r, wi_r, ki_new, pool, e, page_table, t):
    b, n_pages = page_table.shape
    g = _tile(n_pages, PAGES_PER_STEP, 1)
    kern = functools.partial(_dsa_scores_kernel, g_pages=g, t=t)
    per_b = lambda r, w: pl.BlockSpec((1, r, w), lambda bi, p, pt: (bi, 0, 0))
    past, new = _paged_call(
        kern, (b, n_pages // g),
        [per_b(H_IDX * t, D_IDX), per_b(H_IDX * t, 1), per_b(PAGE_SIZE, D_IDX)]
        + _page_specs((PAGE_SIZE, D_IDX), e, g, _forward_pages(g)),
        [pl.BlockSpec((1, t, g * PAGE_SIZE), lambda bi, p, pt: (bi, 0, p)), per_b(t, PAGE_SIZE)],
        [jax.ShapeDtypeStruct((b, t, n_pages * PAGE_SIZE), F32),
         jax.ShapeDtypeStruct((b, t, PAGE_SIZE), F32)], [], "dsa_sample_scores",
        page_table, (qi_r, wi_r, _pad_rows(ki_new, PAGE_SIZE)) + (pool,) * g)
    return jnp.concatenate([past, new], axis=-1)


def _dsa_topk_kernel(s_ref, o_ref, mask_ref, *, past, topk):
    score = s_ref[0]
    kpos = lax.broadcasted_iota(jnp.int32, score.shape, 1)
    qpos = past + lax.broadcasted_iota(jnp.int32, score.shape, 0)
    key = jnp.where(kpos <= qpos, _sort_key(score), jnp.int32(INT_MIN))
    _topk_mask(key, topk, mask_ref)
    o_ref[0] = mask_ref[...]


def dsa_sample_topk(score, past, topk):
    b, t, nk = score.shape
    blk = pl.BlockSpec((1, t, nk), lambda bi: (bi, 0, 0))
    return pl.pallas_call(
        functools.partial(_dsa_topk_kernel, past=past, topk=topk), grid=(b,),
        in_specs=[blk], out_specs=blk, out_shape=jax.ShapeDtypeStruct((b, t, nk), F32),
        scratch_shapes=[pltpu.VMEM((t, nk), F32)],
        compiler_params=_cp("arbitrary"), name="dsa_sample_topk")(score)


def _online_softmax_step(s, valid, v, m_ref, l_ref, acc_ref):
    m_old = m_ref[...]
    m_new = jnp.maximum(m_old, jnp.max(jnp.where(valid, s, NEG), axis=-1, keepdims=True))
    alpha = jnp.exp(m_old - m_new)
    e = jnp.where(valid, jnp.exp(s - m_new), 0.0)
    l_ref[...] = alpha * l_ref[...] + jnp.sum(e, axis=-1, keepdims=True)
    acc_ref[...] = alpha * acc_ref[...] + _dot(e.astype(BF16), v)
    m_ref[...] = m_new


def _init_softmax_state(m_ref, l_ref, acc_ref):
    m_ref[...] = jnp.full(m_ref.shape, NEG, F32)
    l_ref[...] = jnp.zeros(l_ref.shape, F32)
    acc_ref[...] = jnp.zeros(acc_ref.shape, F32)


def _dsa_sample_attn_kernel(pt_ref, qt_ref, mask_ref, masknew_ref, knew_ref, vnew_ref, *rest,
                            g_pages, t):
    kps, vps = rest[:g_pages], rest[g_pages:2 * g_pages]
    o_ref, m_ref, l_ref, acc_ref, z_ref = rest[2 * g_pages:]
    p = pl.program_id(1)
    n, ng = PAGE_SIZE, KV_B
    cols = qt_ref.shape[2]
    per_group = cols // ng

    @pl.when(p == 0)
    def _():
        _init_softmax_state(m_ref, l_ref, acc_ref)

    iota = lambda shape, d: lax.broadcasted_iota(jnp.int32, shape, d)
    own_group = iota((n * ng, cols), 0) % ng == iota((n * ng, cols), 1) // per_group
    spread = jnp.where(iota((n, n * ng), 1) // ng == iota((n, n * ng), 0), 1.0, 0.0).astype(BF16)
    own_rows = iota((cols, n * ng), 0) // per_group == iota((cols, n * ng), 1) % ng

    def scores(k2, slot):
        z_ref[slot] = jnp.where(own_group, _dot(k2.astype(BF16), qt_ref[0]), 0.0)
        z = z_ref[slot, pl.ds(0, n, stride=ng), :]
        for g in range(1, ng):
            z = z + z_ref[slot, pl.ds(g, n, stride=ng), :]
        return z.T

    def weighted_values(e, v2):
        a2 = jnp.where(own_rows, _dot(e.astype(BF16), spread), 0.0)
        return _dot(a2.astype(BF16), v2.astype(BF16))

    def update(k2s, v2s, valid_t, first_slot):
        s = jnp.concatenate([scores(k2, first_slot + j) for j, k2 in enumerate(k2s)], axis=1)
        s = s * HEAD_DIM ** -0.5
        valid = jnp.concatenate([valid_t] * (cols // t), axis=0)
        m_old = m_ref[...]
        m_new = jnp.maximum(m_old, jnp.max(jnp.where(valid, s, NEG), axis=-1, keepdims=True))
        alpha = jnp.exp(m_old - m_new)
        e = jnp.where(valid, jnp.exp(s - m_new), 0.0)
        l_ref[...] = alpha * l_ref[...] + jnp.sum(e, axis=-1, keepdims=True)
        pv = None
        for j, v2 in enumerate(v2s):
            d = weighted_values(e[:, j * n:(j + 1) * n], v2)
            pv = d if pv is None else pv + d
        acc_ref[...] = alpha * acc_ref[...] + pv
        m_ref[...] = m_new

    update([r[...] for r in kps], [r[...] for r in vps], mask_ref[0] > 0.5, 0)

    @pl.when(p == pl.num_programs(1) - 1)
    def _():
        update([knew_ref[0]], [vnew_ref[0]], masknew_ref[0] > 0.5, g_pages)
        o_ref[0] = acc_ref[...] / l_ref[...]


def dsa_sample_attn(q_r, mask, kpool, vpool, k_new, v_new, e, page_table):
    b, n_pages = page_table.shape
    t = mask.shape[1]
    cols = q_r.shape[1]
    g = _tile(n_pages, PAGES_PER_STEP, 1)
    rows = PAGE_SIZE * KV_B
    flat = lambda pool: pool.reshape(pool.shape[0], -1, HEAD_DIM)
    new_rows = lambda a: _pad_rows(a, PAGE_SIZE).reshape(b, rows, HEAD_DIM)
    per_b = lambda r, w: pl.BlockSpec((1, r, w), lambda bi, p, pt: (bi, 0, 0))
    pools = [pl.BlockSpec((None, rows, HEAD_DIM), functools.partial(
        lambda bi, p, pt, j: (e, pt[bi, p * g + j], 0), j=j)) for j in range(g)]
    return _paged_call(
        functools.partial(_dsa_sample_attn_kernel, g_pages=g, t=t), (b, n_pages // g),
        [per_b(HEAD_DIM, cols), pl.BlockSpec((1, t, g * PAGE_SIZE), lambda bi, p, pt: (bi, 0, p)),
         pl.BlockSpec((1, t, PAGE_SIZE), lambda bi, p, pt: (bi, 0, n_pages)),
         per_b(rows, HEAD_DIM), per_b(rows, HEAD_DIM)] + pools + pools,
        per_b(cols, HEAD_DIM), jax.ShapeDtypeStruct((b, cols, HEAD_DIM), F32),
        [pltpu.VMEM((cols, 1), F32), pltpu.VMEM((cols, 1), F32), pltpu.VMEM((cols, HEAD_DIM), F32),
         pltpu.VMEM((g + 1, rows, cols), F32)],
        "dsa_sample_attn", page_table,
        (q_r.transpose(0, 2, 1), mask, mask, new_rows(k_new), new_rows(v_new))
        + (flat(kpool),) * g + (flat(vpool),) * g)


def even_mixer_sample(h, b, t, w, e, page_table, state_conv, c_k, c_v, c_kidx):
    m, d = h.shape
    n_pages = page_table.shape[1]
    past = n_pages * PAGE_SIZE
    tabs = rope_tables(past + jnp.arange(t), b, _tile(m, 1024, 16))
    ya, conv_state = conv_mixer(h.reshape(b, t, d), w["bg"], w["cg"], w["xa"], w["conv_w"], state_conv[e])
    q, k, v, qi, ki, wi = _even_project(h, w, tabs)
    score = dsa_sample_scores(_rows_ht(qi, b, t, H_IDX), _rows_ht(wi, b, t, H_IDX),
                              ki.reshape(b, t, D_IDX), c_kidx, e, page_table, t)
    mask = dsa_sample_topk(score, past, min(TOPK_MAX, (past + t) // 4))
    nh = q.shape[1] // HEAD_DIM
    o = dsa_sample_attn(_rows_ht(q, b, t, nh), mask, c_k, c_v, k.reshape(b, t, KV_B, HEAD_DIM),
                        v.reshape(b, t, KV_B, HEAD_DIM), e, page_table)
    ob = _rows_th(o, b, t, nh).astype(BF16)
    y = proj_out(ya.reshape(m, -1), ob, w["out_a"], w["out_b"])
    return y, (conv_state, k.reshape(b, t, KV_B, HEAD_DIM), v.reshape(b, t, KV_B, HEAD_DIM),
               ki.reshape(b, t, D_IDX))


def _compress_pages_kernel(pt_ref, cwk_ref, cwv_ref, *rest, g_pages):
    kcs, vcs = rest[:g_pages], rest[g_pages:2 * g_pages]
    ko_ref, vo_ref = rest[2 * g_pages:]
    nb = PAGE_SIZE // CMP_BLOCK
    cwk, cwv = cwk_ref[...][None], cwv_ref[...][None]
    for j in range(g_pages):
        ko_ref[0, j] = jnp.sum(kcs[j][0, 0].reshape(nb, CMP_BLOCK, HEAD_DIM) * cwk, axis=1)
        vo_ref[0, j] = jnp.sum(vcs[j][0, 0].reshape(nb, CMP_BLOCK, HEAD_DIM) * cwv, axis=1)


def nsa_compress_pages(kpool, vpool, cwk, cwv, o, page_table):
    b, n_pages = page_table.shape
    nb = PAGE_SIZE // CMP_BLOCK
    g = _tile(n_pages, PAGES_PER_STEP, 1)
    pools = _page_specs((PAGE_SIZE, HEAD_DIM), o, g, _forward_pages(g))
    cw = pl.BlockSpec((CMP_BLOCK, HEAD_DIM), lambda bi, p, pt: (0, 0))
    out = pl.BlockSpec((1, g, nb, HEAD_DIM), lambda bi, p, pt: (bi, p, 0, 0))
    shp = jax.ShapeDtypeStruct((b, n_pages, nb, HEAD_DIM), F32)
    kc, vc = _paged_call(functools.partial(_compress_pages_kernel, g_pages=g), (b, n_pages // g),
                         [cw, cw] + pools + pools, [out, out], [shp, shp], [], "nsa_compress_pages",
                         page_table, (cwk, cwv) + (kpool,) * g + (vpool,) * g)
    return kc.reshape(b, n_pages * nb, HEAD_DIM), vc.reshape(b, n_pages * nb, HEAD_DIM)


def _nsa_sample_a_kernel(q_ref, kcmp_ref, vcmp_ref, kw_ref, vw_ref, ocmp_ref, owin_ref, sel_ref,
                         *, past, t, nwin, n_selblk):
    q = q_ref[0]
    rows = q.shape[0]
    nh = rows // t
    scale = HEAD_DIM ** -0.5
    nb = kcmp_ref.shape[1]
    col = lax.broadcasted_iota(jnp.int32, (rows, nb), 1)
    pos = past + lax.broadcasted_iota(jnp.int32, (rows, nb), 0) % t
    s = _dot_nt(q, kcmp_ref[0].astype(BF16)) * scale
    p = _masked_softmax(s, (col + 1) * CMP_BLOCK - 1 <= pos)
    ocmp_ref[0] = _dot(p.astype(BF16), vcmp_ref[0].astype(BF16))
    imp = jnp.sum(p.reshape(nh, t, nb), axis=0)
    ncol = sel_ref.shape[2]
    imp = jnp.concatenate([imp, jnp.zeros((t, ncol - nb), F32)], axis=1)
    col_s = lax.broadcasted_iota(jnp.int32, (t, ncol), 1)
    pos_s = past + lax.broadcasted_iota(jnp.int32, (t, ncol), 0)
    sel_ref[0] = _select_blocks(_pair_sums(imp, col_s), col_s, pos_s, n_selblk)
    nwp = kw_ref.shape[1]
    colw = lax.broadcasted_iota(jnp.int32, (rows, nwp), 1)
    posw = past + lax.broadcasted_iota(jnp.int32, (rows, nwp), 0) % t
    kwpos = past + t - nwin + colw
    valid = (kwpos <= posw) & (kwpos > posw - WINDOW) & (colw < nwin)
    sw = _dot_nt(q, kw_ref[0].astype(BF16)) * scale
    pw = _masked_softmax(sw, valid)
    owin_ref[0] = _dot(pw.astype(BF16), vw_ref[0].astype(BF16))


def nsa_sample_a(q_r, kcmp, vcmp, kw_pad, vw_pad, past, t, nwin):
    b, rows, _ = q_r.shape
    nb = kcmp.shape[1]
    n_selblk = -(-(past + t) // SEL_BLOCK)
    ncol = -(-2 * n_selblk // LANES) * LANES
    assert ncol > nb >= 2 * n_selblk - 2 and nb % LANES == 0
    per_b = lambda r, w: pl.BlockSpec((1, r, w), lambda bi: (bi, 0, 0))
    kern = functools.partial(_nsa_sample_a_kernel, past=past, t=t, nwin=nwin, n_selblk=n_selblk)
    return pl.pallas_call(
        kern, grid=(b,),
        in_specs=[per_b(rows, HEAD_DIM), per_b(nb, HEAD_DIM), per_b(nb, HEAD_DIM),
                  per_b(kw_pad.shape[1], HEAD_DIM), per_b(kw_pad.shape[1], HEAD_DIM)],
        out_specs=[per_b(rows, HEAD_DIM), per_b(rows, HEAD_DIM), per_b(t, ncol)],
        out_shape=[jax.ShapeDtypeStruct((b, rows, HEAD_DIM), F32)] * 2
        + [jax.ShapeDtypeStruct((b, t, ncol), F32)],
        compiler_params=_cp("arbitrary"), name="nsa_sample_a")(q_r, kcmp, vcmp, kw_pad, vw_pad)


def _nsa_sample_b_kernel(pt_ref, q_ref, tok_ref, toknew_ref, knew_ref, vnew_ref,
                         ocmp_ref, owin_ref, gate_ref, *rest, g_pages, n_pages, t):
    kps, vps = rest[:g_pages], rest[g_pages:2 * g_pages]
    o_ref, m_ref, l_ref, acc_ref = rest[2 * g_pages:]
    p = pl.program_id(1)

    @pl.when(p == 0)
    def _():
        _init_softmax_state(m_ref, l_ref, acc_ref)

    q = q_ref[0]
    rows = q.shape[0]

    def update(keys, vals, tok, first_key):
        n = keys.shape[0]
        kpos = first_key + lax.broadcasted_iota(jnp.int32, (rows, n), 1)
        qpos = n_pages * PAGE_SIZE + lax.broadcasted_iota(jnp.int32, (rows, n), 0) % t
        valid = jnp.concatenate([tok > 0.5] * (rows // t), axis=0) & (kpos <= qpos)
        s = _dot_nt(q, keys.astype(BF16)) * HEAD_DIM ** -0.5
        _online_softmax_step(s, valid, vals.astype(BF16), m_ref, l_ref, acc_ref)

    update(jnp.concatenate([r[0, 0] for r in kps], axis=0),
           jnp.concatenate([r[0, 0] for r in vps], axis=0), tok_ref[0], p * (g_pages * PAGE_SIZE))

    @pl.when(p == pl.num_programs(1) - 1)
    def _():
        update(knew_ref[0], vnew_ref[0], toknew_ref[0], n_pages * PAGE_SIZE)
        g = jax.nn.sigmoid(gate_ref[0])
        o_ref[0] = (g[:, 0:1] * ocmp_ref[0] + g[:, 1:2] * (acc_ref[...] / l_ref[...])
                    + g[:, 2:3] * owin_ref[0])


def nsa_sample_b(q_r, tok, kpool, vpool, ks_new, vs_new, o_cmp, o_win, gate_r, o, page_table):
    b, n_pages = page_table.shape
    rows = q_r.shape[1]
    t = tok.shape[1]
    g = _tile(n_pages, PAGES_PER_STEP, 1)
    per_b = lambda r, w: pl.BlockSpec((1, r, w), lambda bi, p, pt: (bi, 0, 0))
    pools = _page_specs((PAGE_SIZE, HEAD_DIM), o, g, _forward_pages(g))
    kern = functools.partial(_nsa_sample_b_kernel, g_pages=g, n_pages=n_pages, t=t)
    return _paged_call(
        kern, (b, n_pages // g),
        [per_b(rows, HEAD_DIM), pl.BlockSpec((1, t, g * PAGE_SIZE), lambda bi, p, pt: (bi, 0, p)),
         pl.BlockSpec((1, t, PAGE_SIZE), lambda bi, p, pt: (bi, 0, n_pages)),
         per_b(PAGE_SIZE, HEAD_DIM), per_b(PAGE_SIZE, HEAD_DIM),
         per_b(rows, HEAD_DIM), per_b(rows, HEAD_DIM), per_b(rows, 3)] + pools + pools,
        per_b(rows, HEAD_DIM), jax.ShapeDtypeStruct((b, rows, HEAD_DIM), F32),
        [pltpu.VMEM((rows, 1), F32), pltpu.VMEM((rows, 1), F32), pltpu.VMEM((rows, HEAD_DIM), F32)],
        "nsa_sample_b", page_table,
        (q_r, tok, tok, _pad_rows(ks_new, PAGE_SIZE), _pad_rows(vs_new, PAGE_SIZE),
         o_cmp, o_win, gate_r) + (kpool,) * g + (vpool,) * g)


def _sb_sample_kernel(pt_ref, qt2_ref, knew_ref, vnew_ref, *rest, g_pages, t):
    kps, vps = rest[:g_pages], rest[g_pages:2 * g_pages]
    o_ref, carry_ref, acc_ref, z_ref = rest[2 * g_pages:]
    p = pl.program_id(1)
    n, nh = knew_ref.shape[1], knew_ref.shape[2]
    cols = nh * t

    @pl.when(p == 0)
    def _():
        carry_ref[...] = jnp.zeros(carry_ref.shape, F32)
        acc_ref[...] = jnp.zeros(acc_ref.shape, F32)

    iota = lambda shape, d: lax.broadcasted_iota(jnp.int32, shape, d)
    later = jnp.where(iota((n, n), 1) > iota((n, n), 0), 1.0, 0.0).astype(BF16)
    own_head = iota((nh, cols), 1) // t == iota((nh, cols), 0)
    spread = jnp.where(iota((n, n * nh), 1) // nh == iota((n, n * nh), 0), 1.0, 0.0).astype(BF16)
    own_rows = iota((cols, n * nh), 0) // t == iota((cols, n * nh), 1) % nh

    def all_scores(k3s):
        flat = [k3.reshape(n * nh, HEAD_DIM).astype(BF16) for k3 in k3s]
        out = []
        for j in range(0, len(flat) - 1, 2):
            both = _dot(jnp.concatenate(flat[j:j + 2], axis=1), qt2_ref[0])
            out += [both[:, :cols], both[:, cols:]]
        if len(flat) % 2:
            out.append(_dot(flat[-1], qt2_ref[0, :HEAD_DIM, :cols]))
        return out

    def log_weights(z_all, slot, m):
        z_all = z_all.reshape(n, nh, cols)
        z_ref[slot] = jnp.sum(jnp.where(own_head[None], z_all, 0.0), axis=1)
        ls, lneg = _log_sigmoids(z_ref[slot] * HEAD_DIM ** -0.5)
        if m is not None:
            lneg = jnp.where(m, lneg, 0.0)
        hi, lo = _split_bf16(lneg)
        after = _dot(later, hi) + _dot(later, lo)
        return ls + after, after[0:1] + lneg[0:1]

    def weighted_values(v3, log_a, m):
        a = jnp.exp(log_a)
        if m is not None:
            a = jnp.where(m, a, 0.0)
        a2 = jnp.where(own_rows, _dot(a.T.astype(BF16), spread), 0.0)
        return _dot(a2.astype(BF16), v3.reshape(n * nh, HEAD_DIM).astype(BF16))

    @pl.when(p == 0)
    def _():
        m = iota((n, cols), 0) < iota((n, cols), 1) % t
        log_a, total = log_weights(all_scores([knew_ref[0]])[0], g_pages, m)
        acc_ref[...] += weighted_values(vnew_ref[0], log_a, m)
        carry_ref[...] += total

    z_alls = all_scores([kps[j][0, 0] for j in range(g_pages)])
    parts = [log_weights(z_alls[j], j, None) for j in range(g_pages)]
    carry = carry_ref[...]
    acc = acc_ref[...]
    for j in range(g_pages):
        acc = acc + weighted_values(vps[j][0, 0], parts[j][0] + carry, None)
        carry = carry + parts[j][1]
    carry_ref[...] = carry
    acc_ref[...] = acc

    @pl.when(p == pl.num_programs(1) - 1)
    def _():
        o_ref[0] = acc_ref[...]


def sb_sample(qd, kd_new, vd_new, kpool, vpool, o, page_table, b, t):
    n_pages = page_table.shape[1]
    nh = qd.shape[1] // HEAD_DIM
    g = _tile(n_pages, PAGES_PER_STEP_SB, 1)
    qt = qd.reshape(b, t, nh, HEAD_DIM).transpose(0, 3, 2, 1).reshape(b, HEAD_DIM, nh * t)
    zq = jnp.zeros_like(qt)
    qt2 = jnp.concatenate([jnp.concatenate([qt, zq], axis=2), jnp.concatenate([zq, qt], axis=2)], axis=1)
    new = pl.BlockSpec((1, PAGE_SIZE, nh, HEAD_DIM), lambda bi, p, pt: (bi, 0, 0, 0))
    pools = _page_specs((PAGE_SIZE, nh, HEAD_DIM), o, g,
                        lambda bi, p, pt, j: pt[bi, n_pages - 1 - (p * g + j)])
    out = pl.BlockSpec((1, nh * t, HEAD_DIM), lambda bi, p, pt: (bi, 0, 0))
    return _paged_call(
        functools.partial(_sb_sample_kernel, g_pages=g, t=t), (b, n_pages // g),
        [pl.BlockSpec((1, 2 * HEAD_DIM, 2 * nh * t), lambda bi, p, pt: (bi, 0, 0)), new, new]
        + pools + pools,
        out, jax.ShapeDtypeStruct((b, nh * t, HEAD_DIM), F32),
        [pltpu.VMEM((1, nh * t), F32), pltpu.VMEM((nh * t, HEAD_DIM), F32),
         pltpu.VMEM((g + 1, PAGE_SIZE, nh * t), F32)],
        "sb_sample", page_table,
        (qt2, _pad_rows(kd_new, PAGE_SIZE), _pad_rows(vd_new, PAGE_SIZE)) + (kpool,) * g + (vpool,) * g)


def odd_mixer_sample(h, b, t, w, cwk, cwv, o, page_table, c_kc, c_vc, c_ks, c_vs, c_kw, c_vw,
                     c_kd, c_vd):
    m, d = h.shape
    n_pages = page_table.shape[1]
    past = n_pages * PAGE_SIZE
    assert past % CMP_BLOCK == 0 and t < CMP_BLOCK
    tabs = rope_tables(past + jnp.arange(t), b, _tile(m, 1024, 16))
    qc, (kc, vc, ks, vs, kw, vw), gc, qd, kd, vd = _odd_project(h, w, tabs)
    nh = qc.shape[1] // HEAD_DIM
    n_pool = c_kc.shape[1]
    pool1 = lambda a: a.reshape(-1, n_pool, PAGE_SIZE, HEAD_DIM)
    seq = lambda a: a.reshape(b, t, -1)
    kcmp, vcmp = nsa_compress_pages(pool1(c_kc), pool1(c_vc), cwk, cwv, o, page_table)
    wb = c_kw.shape[2]
    kw_all = jnp.concatenate([c_kw[o].reshape(b, wb, HEAD_DIM), seq(kw)], axis=1)
    vw_all = jnp.concatenate([c_vw[o].reshape(b, wb, HEAD_DIM), seq(vw)], axis=1)
    nwp = -(-(wb + t) // LANES) * LANES
    q_r = _rows_ht(qc, b, t, nh)
    o_cmp, o_win, sel = nsa_sample_a(q_r, kcmp, vcmp, _pad_rows(kw_all, nwp), _pad_rows(vw_all, nwp),
                                     past, t, wb + t)
    n_selblk = -(-(past + t) // SEL_BLOCK)
    tok = jnp.repeat(sel[:, :, 0:2 * n_selblk:2], SEL_BLOCK, axis=-1)
    tok = jnp.pad(tok, ((0, 0), (0, 0), (0, (n_pages + 1) * PAGE_SIZE - tok.shape[-1])))
    gate_r = _rows_ht(gc, b, t, nh)
    o_c = nsa_sample_b(q_r, tok, pool1(c_ks), pool1(c_vs), seq(ks), seq(vs), o_cmp, o_win, gate_r,
                       o, page_table)
    nh_d = kd.shape[1] // HEAD_DIM
    heads = lambda a: a.reshape(b, t, nh_d, HEAD_DIM)
    o_d = sb_sample(qd, heads(kd), heads(vd), c_kd, c_vd, o, page_table, b, t)
    y = proj_out(_rows_th(o_c, b, t, nh).astype(BF16), _rows_th(o_d, b, t, nh_d).astype(BF16),
                 w["out_c"], w["out_d"])
    st1 = lambda a: a.reshape(b, t, 1, HEAD_DIM)
    return y, (st1(kc), st1(vc), st1(ks), st1(vs), kw_all[:, t:].reshape(b, wb, 1, HEAD_DIM),
               vw_all[:, t:].reshape(b, wb, 1, HEAD_DIM), kd.reshape(b, t, -1, HEAD_DIM),
               vd.reshape(b, t, -1, HEAD_DIM))


def kernel(x_prompt, x_sample, state_conv, cache_dsa_k, cache_dsa_v, cache_dsa_kidx, cache_nsa_kc, cache_nsa_vc, cache_nsa_ks, cache_nsa_vs, cache_nsa_kw, cache_nsa_vw, cache_sb_k, cache_sb_v, cache_mem_k, cache_mem_v, page_table, mem_prompt, norm_pre, norm_post, norm_mem, w_in_even, conv_w, w_out_even, w_in_odd, cmp_wk, cmp_wv, w_out_odd, w_mq, w_mk, w_mv, w_mo, w_gate, w_up, w_down):
    bp, tp, d = x_prompt.shape
    bs, ts, _ = x_sample.shape
    depth = norm_pre.shape[0]
    n_mem = mem_prompt.shape[1]
    hm = w_mq.shape[2]
    nh_c = nh_d = w_out_odd.shape[1] // 2 // HEAD_DIM
    xp = x_prompt.reshape(bp * tp, d)
    xs = x_sample.reshape(bs * ts, d)
    hp = norm_cast(xp, norm_pre[0, 0])
    hs = norm_cast(xs, norm_pre[0, 0])
    ev_p, ev_s, od_p, od_s, mem_p = [], [], [], [], []
    wg, wu, wd = w_gate, w_up, w_down.astype(BF16)
    for li in range(depth):
        g_pre, g_post = norm_pre[li], norm_post[li]
        if li % 2 == 0:
            e = li // 2
            w = _even_weights(w_in_even[e], conv_w[e], w_out_even[e])
            mp, stp = even_mixer_prompt(hp, bp, tp, w)
            ms, sts = even_mixer_sample(hs, bs, ts, w, e, page_table, state_conv,
                                        cache_dsa_k, cache_dsa_v, cache_dsa_kidx)
            ev_p.append(stp)
            ev_s.append(sts)
        else:
            o = li // 2
            w = _odd_weights(w_in_odd[o], w_out_odd[o], nh_c, nh_d)
            mp, stp = odd_mixer_prompt(hp, bp, tp, w, cmp_wk[o], cmp_wv[o])
            ms, sts = odd_mixer_sample(hs, bs, ts, w, cmp_wk[o], cmp_wv[o], o, page_table,
                                       cache_nsa_kc, cache_nsa_vc, cache_nsa_ks, cache_nsa_vs,
                                       cache_nsa_kw, cache_nsa_vw, cache_sb_k, cache_sb_v)
            od_p.append(stp)
            od_s.append(sts)
        xp, hp = resid_norm(xp, mp, g_post[0], g_pre[1])
        xs, hs = resid_norm(xs, ms, g_post[0], g_pre[1])
        wq, wo = w_mq[li].astype(BF16), w_mo[li].astype(BF16)
        hmem = norm_cast(mem_prompt.reshape(bp * n_mem, d), norm_mem[li])
        mkp = proj(hmem, w_mk[li].astype(BF16), name="proj_mk")
        mvp = proj(hmem, w_mv[li].astype(BF16), name="proj_mv")
        mem_p.append((mkp.reshape(bp, n_mem, H_MEM, HEAD_DIM), mvp.reshape(bp, n_mem, H_MEM, HEAD_DIM)))
        xp, hp = mem_sublayer(hp.reshape(bp, tp, d), xp.reshape(bp, tp, d), wq, mkp.reshape(bp, n_mem, hm),
                              mvp.reshape(bp, n_mem, hm), wo, g_post[1], g_pre[2])
        xs, hs = mem_sublayer(hs.reshape(bs, ts, d), xs.reshape(bs, ts, d), wq,
                              cache_mem_k[li].reshape(bs, n_mem, hm),
                              cache_mem_v[li].reshape(bs, n_mem, hm), wo, g_post[1], g_pre[2])
        xp, hp = xp.reshape(bp * tp, d), hp.reshape(bp * tp, d)
        xs, hs = xs.reshape(bs * ts, d), hs.reshape(bs * ts, d)
        g_next = norm_pre[li + 1, 0] if li + 1 < depth else None
        xp, hp = resid_norm(xp, ffn(hp, wg, wu, wd, li), g_post[2], g_next)
        xs, hs = resid_norm(xs, ffn(hs, wg, wu, wd, li), g_post[2], g_next)
    stack = lambda lst, j: jnp.stack([s[j] for s in lst])
    return ((xp.reshape(bp, tp, d), xs.reshape(bs, ts, d))
            + tuple(stack(ev_p, j) for j in range(4)) + tuple(stack(od_p, j) for j in range(8))
            + (stack(mem_p, 0), stack(mem_p, 1))
            + tuple(stack(ev_s, j) for j in range(4)) + tuple(stack(od_s, j) for j in range(8)))
```

```python
import functools

import numpy as np
import jax
import jax.numpy as jnp
from jax import lax
from jax.experimental import pallas as pl
from jax.experimental.pallas import tpu as pltpu

F32 = jnp.float32
BF16 = jnp.bfloat16

HEAD_DIM = 128
PAGE_SIZE = 128
CONV_K = 3
KV_B = 4
H_IDX = 16
D_IDX = 128
TOPK_MAX = 256
CMP_BLOCK = 32
SEL_BLOCK = 64
N_SEL = 16
WINDOW = 512
FORCE_BONUS = 1.0e6
H_MEM = 4
ROPE_THETA = 10000.0
EPS = 1e-6
NEG = -1e30
LANES = 128
VMEM_LIMIT = 56 * 1024 * 1024


def _cp(*sem):
    return pltpu.CompilerParams(dimension_semantics=sem, vmem_limit_bytes=VMEM_LIMIT)


def _tile(n, pref, mult):
    t = (min(pref, n) // mult) * mult
    while t >= mult:
        if n % t == 0:
            return t
        t -= mult
    return n


def _dot(a, b):
    return jnp.dot(a, b, preferred_element_type=F32)


def _dot_nt(a, b):
    return lax.dot_general(a, b, (((1,), (1,)), ((), ())), preferred_element_type=F32)


def _rms(x, g):
    return x * lax.rsqrt(jnp.mean(x * x, axis=-1, keepdims=True) + EPS) * g


def _rope(y, cos, sin):
    return y * cos + pltpu.roll(y, HEAD_DIM // 2, 1) * sin


def _norm_cast_kernel(x_ref, g_ref, o_ref):
    o_ref[...] = _rms(x_ref[...], g_ref[...]).astype(o_ref.dtype)


def norm_cast(x, g):
    m, d = x.shape
    tm = _tile(m, 256, 16)
    row = pl.BlockSpec((tm, d), lambda i: (i, 0))
    return pl.pallas_call(
        _norm_cast_kernel, grid=(m // tm,),
        in_specs=[row, pl.BlockSpec((1, d), lambda i: (0, 0))],
        out_specs=row, out_shape=jax.ShapeDtypeStruct((m, d), BF16),
        compiler_params=_cp("arbitrary"), name="norm_cast")(x, g.reshape(1, d))


def _resid_norm_kernel(x_ref, y_ref, gp_ref, gn_ref, xo_ref, h_ref):
    xn = x_ref[...] + _rms(y_ref[...], gp_ref[...])
    xo_ref[...] = xn
    h_ref[...] = _rms(xn, gn_ref[...]).astype(h_ref.dtype)


def _resid_kernel(x_ref, y_ref, gp_ref, xo_ref):
    xo_ref[...] = x_ref[...] + _rms(y_ref[...], gp_ref[...])


def resid_norm(x, y, g_post, g_next):
    m, d = x.shape
    tm = _tile(m, 256, 16)
    row = pl.BlockSpec((tm, d), lambda i: (i, 0))
    vec = pl.BlockSpec((1, d), lambda i: (0, 0))
    if g_next is None:
        return pl.pallas_call(
            _resid_kernel, grid=(m // tm,), in_specs=[row, row, vec], out_specs=row,
            out_shape=jax.ShapeDtypeStruct((m, d), F32),
            compiler_params=_cp("arbitrary"), name="resid")(x, y, g_post.reshape(1, d)), None
    return pl.pallas_call(
        _resid_norm_kernel, grid=(m // tm,), in_specs=[row, row, vec, vec],
        out_specs=[row, row],
        out_shape=[jax.ShapeDtypeStruct((m, d), F32), jax.ShapeDtypeStruct((m, d), BF16)],
        compiler_params=_cp("arbitrary"), name="resid_norm")(
            x, y, g_post.reshape(1, d), g_next.reshape(1, d))


def _mm_kernel(*refs, nx, nw, ne, groups, epilogue):
    xr, wr = refs[:nx], refs[nx:nx + nw]
    er, orf = refs[nx + nw:nx + nw + ne], refs[nx + nw + ne:]
    accs = []
    for grp in groups:
        acc = None
        for xi, wi in grp:
            w = wr[wi][...]
            d = _dot(xr[xi][...], w if w.dtype == BF16 else w.astype(BF16))
            acc = d if acc is None else acc + d
        accs.append(acc)
    epilogue(accs, er, orf)


def _ep_plain(accs, er, orf):
    orf[0][...] = accs[0].astype(orf[0].dtype)


def _ep_rope(accs, er, orf):
    cos, sin = er[0][...], er[1][...]
    y = accs[0]
    for c in range(y.shape[1] // HEAD_DIM):
        sl = slice(c * HEAD_DIM, (c + 1) * HEAD_DIM)
        orf[0][:, sl] = _rope(y[:, sl], cos, sin).astype(orf[0].dtype)


def _ep_split(accs, er, orf, *, rope_chunks):
    cos, sin = er[0][...], er[1][...]
    y = accs[0]
    for c in range(len(orf)):
        yc = y[:, c * HEAD_DIM:(c + 1) * HEAD_DIM]
        if c in rope_chunks:
            yc = _rope(yc, cos, sin)
        orf[c][...] = yc.astype(orf[c].dtype)


def matmul(xs, ws, groups, epilogue, out_dtypes, *, tables=None, n_tab_blocks=1,
           tm_pref=1024, tn_pref=512, split_out=0, layer=None, single_buffer_x=False, name="matmul"):
    m = xs[0].shape[0]
    n = ws[0].shape[-1]
    if tables is not None:
        tm = tables[0].shape[0] // n_tab_blocks
    else:
        tm = _tile(m, tm_pref, 16)
    tn = n if (split_out or n % LANES) else _tile(n, tn_pref, LANES)
    xmode = dict(pipeline_mode=pl.Buffered(1)) if single_buffer_x else {}
    in_specs = [pl.BlockSpec((tm, x.shape[1]), lambda i, j: (i, 0), **xmode) for x in xs]
    if layer is None:
        in_specs += [pl.BlockSpec((w.shape[0], tn), lambda i, j: (0, j)) for w in ws]
    else:
        in_specs += [pl.BlockSpec((None, w.shape[1], tn), lambda i, j: (layer, 0, j)) for w in ws]
    extras = []
    if tables is not None:
        nb = n_tab_blocks
        in_specs += [pl.BlockSpec((tm, HEAD_DIM), lambda i, j: (i % nb, 0))] * 2
        extras = list(tables)
    if split_out:
        out_specs = [pl.BlockSpec((tm, HEAD_DIM), lambda i, j: (i, 0))] * split_out
        out_shape = [jax.ShapeDtypeStruct((m, HEAD_DIM), dt) for dt in out_dtypes]
    else:
        out_specs = [pl.BlockSpec((tm, tn), lambda i, j: (i, j))]
        out_shape = [jax.ShapeDtypeStruct((m, n), out_dtypes[0])]
    kern = functools.partial(_mm_kernel, nx=len(xs), nw=len(ws), ne=len(extras),
                             groups=groups, epilogue=epilogue)
    out = pl.pallas_call(
        kern, grid=(m // tm, n // tn), in_specs=in_specs, out_specs=out_specs,
        out_shape=out_shape, compiler_params=_cp("arbitrary", "arbitrary"), name=name)(
            *xs, *ws, *extras)
    return out if split_out else out[0]


def proj(h, w, dtype=F32, name="proj"):
    return matmul([h], [w], [[(0, 0)]], _ep_plain, [dtype], name=name)


def proj_rope(h, w, tabs, dtype, name="proj_rope"):
    cos, sin, nb = tabs
    return matmul([h], [w], [[(0, 0)]], _ep_rope, [dtype], tables=(cos, sin),
                  n_tab_blocks=nb, name=name)


def proj_out(xa, xb, wa, wb, name="proj_out"):
    return matmul([xa, xb], [wa, wb], [[(0, 0), (1, 1)]], _ep_plain, [F32], name=name)


def rope_tables(pos, reps, tm):
    half = HEAD_DIM // 2
    inv = ROPE_THETA ** (-jnp.arange(half, dtype=F32) / half)
    ang = pos.astype(F32)[:, None] * inv[None, :]
    cos, sin = jnp.cos(ang), jnp.sin(ang)
    cos = jnp.concatenate([cos, cos], axis=-1)
    sin = jnp.concatenate([-sin, sin], axis=-1)
    t = pos.shape[0]
    if tm > t:
        cos, sin = jnp.tile(cos, (tm // t, 1)), jnp.tile(sin, (tm // t, 1))
        return cos, sin, 1
    return cos, sin, t // tm


def _conv_kernel(x_ref, wb_ref, wc_ref, wx_ref, cw_ref, init_ref, ya_ref, st_ref, carry_ref):
    i = pl.program_id(2)

    @pl.when(i == 0)
    def _():
        carry_ref[...] = init_ref[0]

    x = x_ref[0]
    bg = _dot(x, wb_ref[...])
    u = _dot(x, wc_ref[...]) * _dot(x, wx_ref[...])
    tm = u.shape[0]
    c = carry_ref[...]
    rows = lax.broadcasted_iota(jnp.int32, u.shape, 0)
    u1 = jnp.where(rows == 0, c[1:2], pltpu.roll(u, 1, 0))
    u2 = jnp.where(rows == 0, c[0:1], jnp.where(rows == 1, c[1:2], pltpu.roll(u, 2, 0)))
    cw = cw_ref[...]
    conv = cw[0:1] * u2 + cw[1:2] * u1 + cw[2:3] * u
    ya_ref[0] = (bg * conv).astype(ya_ref.dtype)
    new = u[tm - (CONV_K - 1):tm]
    carry_ref[...] = new
    st_ref[0] = new


def conv_mixer(h3, wb, wc, wx, cw, init):
    b, t, d = h3.shape
    c = wb.shape[1]
    tm = _tile(t, 1024, 16)
    tn = _tile(c, 512, LANES)
    wspec = pl.BlockSpec((d, tn), lambda j, bi, i: (0, j))
    return pl.pallas_call(
        _conv_kernel, grid=(c // tn, b, t // tm),
        in_specs=[pl.BlockSpec((1, tm, d), lambda j, bi, i: (bi, i, 0)), wspec, wspec, wspec,
                  pl.BlockSpec((CONV_K, tn), lambda j, bi, i: (0, j)),
                  pl.BlockSpec((1, CONV_K - 1, tn), lambda j, bi, i: (bi, 0, j))],
        out_specs=[pl.BlockSpec((1, tm, tn), lambda j, bi, i: (bi, i, j)),
                   pl.BlockSpec((1, CONV_K - 1, tn), lambda j, bi, i: (bi, 0, j))],
        out_shape=[jax.ShapeDtypeStruct((b, t, c), BF16),
                   jax.ShapeDtypeStruct((b, CONV_K - 1, c), F32)],
        scratch_shapes=[pltpu.VMEM((CONV_K - 1, tn), F32)],
        compiler_params=_cp("arbitrary", "arbitrary", "arbitrary"), name="conv_mixer")(
            h3, wb, wc, wx, cw, init)


def _stack_heads(q_ref, heads):
    return jnp.concatenate([q_ref[:, h * HEAD_DIM:(h + 1) * HEAD_DIM] for h in heads], axis=0)


LOG2E = 1.4426950408889634


def _with_ones_column(v):
    ones = jnp.where(lax.broadcasted_iota(jnp.int32, v.shape, 1) == 0, 1.0, 0.0).astype(v.dtype)
    return jnp.concatenate([v, ones], axis=1)


def _softmax_av(qk, scale, bias, v_ones):
    s = qk * (scale * LOG2E) + bias[None]
    e = jnp.exp2(s - jnp.max(s, axis=-1, keepdims=True))
    r, tq, n = e.shape
    o = _dot(e.reshape(r * tq, n).astype(BF16), v_ones)
    return o[:, :HEAD_DIM] / o[:, HEAD_DIM:HEAD_DIM + 1]


def _sort_key(x):
    bits = lax.bitcast_convert_type(x + 0.0, jnp.int32)
    return jnp.where(bits < 0, bits ^ jnp.int32(0x7FFFFFFF), bits)


INT_MIN = -2 ** 31


def _kth_largest_key(key, k):
    def body(it, othr):
        bit = lax.shift_left(jnp.int32(1), jnp.int32(31) - it)
        cand = othr | bit
        cnt = jnp.sum(jnp.where(key >= (cand ^ jnp.int32(INT_MIN)), 1.0, 0.0), axis=1, keepdims=True)
        return jnp.where(cnt >= k, cand, othr)
    othr = lax.fori_loop(0, 32, body, jnp.zeros((key.shape[0], 1), jnp.int32))
    return othr ^ jnp.int32(INT_MIN)


def _topk_mask(key, k, scratch_ref):
    thr = _kth_largest_key(key, k)
    live = key > jnp.int32(INT_MIN)
    ge = (key >= thr) & live
    scratch_ref[...] = jnp.where(ge, 1.0, 0.0)
    n_ge = jnp.sum(jnp.where(ge, 1.0, 0.0), axis=1, keepdims=True)

    @pl.when(jnp.max(n_ge) > k)
    def _():
        gt = key > thr
        eq = (key == thr) & live
        need = k - jnp.sum(jnp.where(gt, 1.0, 0.0), axis=1, keepdims=True)
        r = lax.broadcasted_iota(jnp.int32, (LANES, LANES), 0)
        c = lax.broadcasted_iota(jnp.int32, (LANES, LANES), 1)
        before = jnp.where(r < c, 1.0, 0.0).astype(BF16)
        run = jnp.zeros_like(need)
        for ch in range(key.shape[1] // LANES):
            sl = slice(ch * LANES, (ch + 1) * LANES)
            e = jnp.where(eq[:, sl], 1.0, 0.0)
            pre = _dot(e.astype(BF16), before) + run
            scratch_ref[:, sl] = jnp.where(gt[:, sl], 1.0, e * jnp.where(pre < need, 1.0, 0.0))
            run = run + jnp.sum(e, axis=1, keepdims=True)


CAUSAL_BUCKETS = 4


def _by_key_extent(i, tq, t, fn):
    nb = CAUSAL_BUCKETS if t % (CAUSAL_BUCKETS * tq) == 0 else 1
    size = t // nb
    for bkt in range(nb):
        @pl.when((i * tq) // size == bkt)
        def _():
            fn((bkt + 1) * size)


def _dsa_prompt_kernel(qi_ref, wi_ref, ki_ref, q_ref, k_ref, v_ref, o_ref,
                       kib_ref, kb_ref, vb_ref, mask_ref, *, topk):
    i = pl.program_id(1)

    @pl.when(i == 0)
    def _():
        kib_ref[...] = ki_ref[...].astype(BF16)
        kb_ref[...] = k_ref[...].astype(BF16)
        for g in range(KV_B):
            vb_ref[g] = _with_ones_column(v_ref[:, g * HEAD_DIM:(g + 1) * HEAD_DIM].astype(BF16))

    tq = q_ref.shape[0]
    t = ki_ref.shape[0]
    wi = wi_ref[...]
    rep = q_ref.shape[1] // HEAD_DIM // KV_B

    def attend(n):
        kib = kib_ref[0:n]
        score = jnp.zeros((tq, n), F32)
        for h in range(H_IDX):
            s = _dot_nt(qi_ref[:, h * D_IDX:(h + 1) * D_IDX], kib) * D_IDX ** -0.5
            score = score + jnp.maximum(s, 0.0) * wi[:, h:h + 1]
        score = score * H_IDX ** -0.5
        qpos = i * tq + lax.broadcasted_iota(jnp.int32, (tq, n), 0)
        kpos = lax.broadcasted_iota(jnp.int32, (tq, n), 1)
        key = jnp.where(kpos <= qpos, _sort_key(score), jnp.int32(INT_MIN))
        sel_ref = mask_ref.at[:, 0:n]
        _topk_mask(key, topk, sel_ref)
        bias = jnp.where(sel_ref[...] > 0.0, 0.0, NEG)
        for g in range(KV_B):
            qs = _stack_heads(q_ref, range(g * rep, (g + 1) * rep))
            sl = slice(g * HEAD_DIM, (g + 1) * HEAD_DIM)
            qk = _dot_nt(qs, kb_ref[0:n, sl]).reshape(rep, tq, n)
            o = _softmax_av(qk, HEAD_DIM ** -0.5, bias, vb_ref[g, 0:n])
            for r in range(rep):
                h = g * rep + r
                o_ref[:, h * HEAD_DIM:(h + 1) * HEAD_DIM] = o[r * tq:(r + 1) * tq].astype(o_ref.dtype)

    _by_key_extent(i, tq, t, attend)


def dsa_prompt(qi, wi, ki, q, k, v, b, t):
    m = q.shape[0]
    tq = _tile(t, 128, 16)
    nq = t // tq
    topk = min(TOPK_MAX, t // 4)
    qrow = lambda w: pl.BlockSpec((tq, w), lambda bi, i: (bi * nq + i, 0))
    full = lambda w: pl.BlockSpec((t, w), lambda bi, i: (bi, 0))
    return pl.pallas_call(
        functools.partial(_dsa_prompt_kernel, topk=topk), grid=(b, nq),
        in_specs=[qrow(qi.shape[1]), qrow(wi.shape[1]), full(ki.shape[1]),
                  qrow(q.shape[1]), full(k.shape[1]), full(v.shape[1])],
        out_specs=qrow(q.shape[1]),
        out_shape=jax.ShapeDtypeStruct((m, q.shape[1]), BF16),
        scratch_shapes=[pltpu.VMEM((t, ki.shape[1]), BF16), pltpu.VMEM((t, k.shape[1]), BF16),
                        pltpu.VMEM((KV_B, t, 2 * HEAD_DIM), BF16), pltpu.VMEM((tq, t), F32)],
        compiler_params=_cp("arbitrary", "arbitrary"), name="dsa_prompt")(qi, wi, ki, q, k, v)


def _masked_softmax(s, mask):
    m = jnp.max(jnp.where(mask, s, NEG), axis=-1, keepdims=True)
    m = jnp.where(m > 0.5 * NEG, m, 0.0)
    e = jnp.where(mask, jnp.exp(s - m), 0.0)
    return e / jnp.maximum(jnp.sum(e, axis=-1, keepdims=True), 1e-30)


def _pair_sums(imp, col):
    n = imp.shape[1]
    return imp + jnp.where(col % 2 == 0, pltpu.roll(imp, n - 1, 1), pltpu.roll(imp, 1, 1))


def _select_blocks(bs, col, qpos, n_selblk):
    blk = col // 2
    cur = qpos // SEL_BLOCK
    forced = (blk == 0) | (blk == cur) | (blk == cur - 1)
    admiss = (blk * SEL_BLOCK <= qpos) & (blk < n_selblk)
    work = jnp.where(admiss, jnp.where(forced, bs + FORCE_BONUS, bs), NEG)
    sel = jnp.zeros(bs.shape, jnp.bool_)
    big = jnp.int32(2 ** 30)
    for _ in range(min(N_SEL, n_selblk)):
        mx = jnp.max(work, axis=1, keepdims=True)
        idx = jnp.min(jnp.where(work == mx, col, big), axis=1, keepdims=True)
        pick = blk == idx // 2
        sel = sel | pick
        work = jnp.where(pick, -3e38, work)
    return jnp.where(sel & admiss, 1.0, 0.0)


def _nsa_prompt_kernel(q_ref, gc_ref, kc_ref, vc_ref, ks_ref, vs_ref, kw_ref, vw_ref,
                       cwk_ref, cwv_ref, o_ref,
                       kcmp_ref, vcmp_ref, ksb_ref, vsb_ref, kwb_ref, vwb_ref, *, win):
    i = pl.program_id(1)
    tq = q_ref.shape[0]
    t = kc_ref.shape[0]
    nb = t // CMP_BLOCK
    ncp = kcmp_ref.shape[0]
    n_selblk = -(-t // SEL_BLOCK)
    nh = q_ref.shape[1] // HEAD_DIM
    scale = HEAD_DIM ** -0.5

    @pl.when(i == 0)
    def _():
        kcmp_ref[...] = jnp.zeros(kcmp_ref.shape, kcmp_ref.dtype)
        vcmp_ref[...] = jnp.zeros(vcmp_ref.shape, vcmp_ref.dtype)
        kc = kc_ref[...].reshape(nb, CMP_BLOCK, HEAD_DIM)
        vc = vc_ref[...].reshape(nb, CMP_BLOCK, HEAD_DIM)
        kcmp_ref[0:nb] = jnp.sum(kc * cwk_ref[...][None], axis=1).astype(BF16)
        vcmp_ref[0:nb] = jnp.sum(vc * cwv_ref[...][None], axis=1).astype(BF16)
        ksb_ref[...] = ks_ref[...].astype(BF16)
        vsb_ref[...] = _with_ones_column(vs_ref[...].astype(BF16))
        kwb_ref[...] = kw_ref[...].astype(BF16)
        vwb_ref[...] = _with_ones_column(vw_ref[...].astype(BF16))

    qs = _stack_heads(q_ref, range(nh))
    col = lax.broadcasted_iota(jnp.int32, (tq, ncp), 1)
    qpos_c = i * tq + lax.broadcasted_iota(jnp.int32, (tq, ncp), 0)
    cmask = ((col + 1) * CMP_BLOCK - 1 <= qpos_c) & (col < nb)
    s = (_dot_nt(qs, kcmp_ref[...]) * scale).reshape(nh, tq, ncp)
    p = _masked_softmax(s, cmask[None])
    o_cmp = _dot(p.reshape(nh * tq, ncp).astype(BF16), vcmp_ref[...])
    imp = jnp.sum(p, axis=0)
    sel = _select_blocks(_pair_sums(imp, col), col, qpos_c, n_selblk).astype(BF16)
    start = pl.multiple_of(jnp.clip(i * tq - WINDOW, 0, t - win), 16)
    qpos_w = i * tq + lax.broadcasted_iota(jnp.int32, (tq, win), 0)
    kpos_w = start + lax.broadcasted_iota(jnp.int32, (tq, win), 1)
    bias_win = jnp.where((kpos_w <= qpos_w) & (kpos_w > qpos_w - WINDOW), 0.0, NEG)
    kwin = kwb_ref[pl.ds(start, win), :]
    vwin = vwb_ref[pl.ds(start, win), :]
    gate = jax.nn.sigmoid(gc_ref[...])
    grp = 4
    o_wins = []
    for hg in range(nh // grp):
        q4 = qs[hg * grp * tq:(hg + 1) * grp * tq]
        qk_win = _dot_nt(q4, kwin).reshape(grp, tq, win)
        o_wins.append(_softmax_av(qk_win, scale, bias_win, vwin))

    def attend(n):
        er = lax.broadcasted_iota(jnp.int32, (ncp, n), 0)
        ec = lax.broadcasted_iota(jnp.int32, (ncp, n), 1)
        expand = jnp.where(er == 2 * (ec // SEL_BLOCK), 1.0, 0.0).astype(BF16)
        tok = _dot(sel, expand)
        qpos = i * tq + lax.broadcasted_iota(jnp.int32, (tq, n), 0)
        kpos = lax.broadcasted_iota(jnp.int32, (tq, n), 1)
        bias_sel = jnp.where((tok > 0.5) & (kpos <= qpos), 0.0, NEG)
        for hg in range(nh // grp):
            q4 = qs[hg * grp * tq:(hg + 1) * grp * tq]
            qk_sel = _dot_nt(q4, ksb_ref[0:n]).reshape(grp, tq, n)
            o_sel = _softmax_av(qk_sel, scale, bias_sel, vsb_ref[0:n])
            o_win = o_wins[hg]
            for r in range(grp):
                h = hg * grp + r
                rows = slice(r * tq, (r + 1) * tq)
                o = (gate[:, 3 * h:3 * h + 1] * o_cmp[h * tq:(h + 1) * tq]
                     + gate[:, 3 * h + 1:3 * h + 2] * o_sel[rows] + gate[:, 3 * h + 2:3 * h + 3] * o_win[rows])
                o_ref[:, h * HEAD_DIM:(h + 1) * HEAD_DIM] = o.astype(o_ref.dtype)

    _by_key_extent(i, tq, t, attend)


def nsa_prompt(q, gc, kc, vc, ks, vs, kw, vw, cwk, cwv, b, t):
    m, hq = q.shape
    tq = _tile(t, 128, 16)
    nq = t // tq
    win = min(WINDOW + tq, t)
    ncp = LANES
    assert t // CMP_BLOCK <= ncp and t % SEL_BLOCK == 0
    qrow = lambda w: pl.BlockSpec((tq, w), lambda bi, i: (bi * nq + i, 0))
    full = pl.BlockSpec((t, HEAD_DIM), lambda bi, i: (bi, 0))
    cw = pl.BlockSpec((CMP_BLOCK, HEAD_DIM), lambda bi, i: (0, 0))
    kv = pltpu.VMEM((t, HEAD_DIM), BF16)
    kv_ones = pltpu.VMEM((t, 2 * HEAD_DIM), BF16)
    return pl.pallas_call(
        functools.partial(_nsa_prompt_kernel, win=win), grid=(b, nq),
        in_specs=[qrow(hq), qrow(gc.shape[1])] + [full] * 6 + [cw, cw],
        out_specs=qrow(hq), out_shape=jax.ShapeDtypeStruct((m, hq), BF16),
        scratch_shapes=[pltpu.VMEM((ncp, HEAD_DIM), BF16), pltpu.VMEM((ncp, HEAD_DIM), BF16),
                        kv, kv_ones, kv, kv_ones],
        compiler_params=_cp("arbitrary", "arbitrary"), name="nsa_prompt")(
            q, gc, kc, vc, ks, vs, kw, vw, cwk, cwv)


def _log_sigmoids(z):
    ls = jnp.minimum(z, 0.0) - jnp.log(1.0 + jnp.exp(-jnp.abs(z)))
    return ls, ls - z


def _split_bf16(x):
    hi = x.astype(BF16)
    return hi, (x - hi.astype(F32)).astype(BF16)


SB_HEADS_PER_STEP = 4


def _sb_prompt_kernel(q_ref, k_ref, v_ref, o_ref):
    i = pl.program_id(2)
    tq = q_ref.shape[0]
    nhs = q_ref.shape[1] // HEAD_DIM
    r = lax.broadcasted_iota(jnp.int32, (tq, tq), 0)
    c = lax.broadcasted_iota(jnp.int32, (tq, tq), 1)
    later = jnp.where(r > c, 1.0, 0.0).astype(BF16)
    before = c < r
    qs = [q_ref[:, h * HEAD_DIM:(h + 1) * HEAD_DIM] for h in range(nhs)]
    cols = lambda h: slice(h * HEAD_DIM, (h + 1) * HEAD_DIM)
    block_off = lambda j: pl.multiple_of(jnp.maximum(j, 0) * tq, tq)

    def scores(h, off):
        return _dot_nt(qs[h], k_ref[pl.ds(off, tq), cols(h)].astype(BF16)) * HEAD_DIM ** -0.5

    def weights(z, carry, m):
        ls, lneg = _log_sigmoids(z)
        if m is not None:
            lneg = jnp.where(m, lneg, 0.0)
        hi, lo = _split_bf16(lneg)
        after = _dot(hi, later) + _dot(lo, later)
        a = jnp.exp(ls + after + carry)
        if m is not None:
            a = jnp.where(m, a, 0.0)
        return a.astype(BF16), carry + after[:, 0:1] + lneg[:, 0:1]

    def weighted_values(h, a, off, acc):
        return acc + _dot(a, v_ref[pl.ds(off, tq), cols(h)].astype(BF16))

    st = []
    for h in range(nhs):
        a, carry = weights(scores(h, block_off(i)), jnp.zeros((tq, 1), F32), before)
        st.append((scores(h, block_off(i - 1)), a, carry, jnp.zeros((tq, HEAD_DIM), F32)))

    def body(jj, st):
        out = []
        for h in range(nhs):
            z, a_prev, carry, acc = st[h]
            z_next = scores(h, block_off(i - 2 - jj))
            acc = weighted_values(h, a_prev, block_off(i - jj), acc)
            a, carry = weights(z, carry, None)
            out.append((z_next, a, carry, acc))
        return tuple(out)

    st = lax.fori_loop(0, i, body, tuple(st))
    for h in range(nhs):
        acc = weighted_values(h, st[h][1], 0, st[h][3])
        o_ref[:, h * HEAD_DIM:(h + 1) * HEAD_DIM] = acc.astype(o_ref.dtype)


def sb_prompt(q, k, v, b, t):
    m, hq = q.shape
    nhs = SB_HEADS_PER_STEP
    tq = _tile(t, 256, 16)
    nq = t // tq
    w = nhs * HEAD_DIM
    qrow = pl.BlockSpec((tq, w), lambda bi, h, i: (bi * nq + i, h))
    full = pl.BlockSpec((t, w), lambda bi, h, i: (bi, h))
    return pl.pallas_call(
        _sb_prompt_kernel, grid=(b, hq // w, nq),
        in_specs=[qrow, full, full], out_specs=qrow,
        out_shape=jax.ShapeDtypeStruct((m, hq), BF16),
        compiler_params=_cp("arbitrary", "arbitrary", "arbitrary"), name="sb_prompt")(q, k, v)


def _mem_kernel(h_ref, x_ref, wq_ref, mk_ref, mv_ref, wo_ref, gp_ref, gn_ref, xo_ref, ho_ref):
    q = _dot(h_ref[0], wq_ref[...])
    outs = []
    for hh in range(H_MEM):
        sl = slice(hh * HEAD_DIM, (hh + 1) * HEAD_DIM)
        s = _dot_nt(q[:, sl].astype(BF16), mk_ref[0, :, sl].astype(BF16)) * HEAD_DIM ** -0.5
        e = jnp.exp(s - jnp.max(s, axis=-1, keepdims=True))
        o = _dot(e.astype(BF16), mv_ref[0, :, sl].astype(BF16))
        outs.append(o / jnp.sum(e, axis=-1, keepdims=True))
    y = _dot(jnp.concatenate(outs, axis=1).astype(BF16), wo_ref[...])
    xn = x_ref[0] + _rms(y, gp_ref[...])
    xo_ref[0] = xn
    ho_ref[0] = _rms(xn, gn_ref[...]).astype(ho_ref.dtype)


def mem_sublayer(h3, x3, wq, mk, mv, wo, g_post, g_next):
    b, t, d = h3.shape
    tq = _tile(t, 256, 16)
    nm, hm = mk.shape[1], mk.shape[2]
    row = pl.BlockSpec((1, tq, d), lambda bi, i: (bi, i, 0))
    mem = pl.BlockSpec((1, nm, hm), lambda bi, i: (bi, 0, 0))
    vec = pl.BlockSpec((1, d), lambda bi, i: (0, 0))
    return pl.pallas_call(
        _mem_kernel, grid=(b, t // tq),
        in_specs=[row, row, pl.BlockSpec((d, hm), lambda bi, i: (0, 0)), mem, mem,
                  pl.BlockSpec((hm, d), lambda bi, i: (0, 0)), vec, vec],
        out_specs=[row, row],
        out_shape=[jax.ShapeDtypeStruct((b, t, d), F32), jax.ShapeDtypeStruct((b, t, d), BF16)],
        compiler_params=_cp("arbitrary", "arbitrary"), name="mem_sublayer")(
            h3, x3, wq, mk, mv, wo, g_post.reshape(1, d), g_next.reshape(1, d))


def _ep_swiglu(accs, er, orf):
    g, u = accs
    orf[0][...] = ((g * jax.nn.sigmoid(g)) * u).astype(orf[0].dtype)


def ffn(h, wg, wu, wd, li):
    act = matmul([h], [wg, wu], [[(0, 0)], [(0, 1)]], _ep_swiglu, [BF16], tm_pref=2048,
                 single_buffer_x=True, layer=li, name="ffn_gate_up")
    return matmul([act], [wd], [[(0, 0)]], _ep_plain, [F32], tm_pref=512, layer=li, name="ffn_down")


def _odd_weights(w_in, w_out, nh_c, nh_d):
    sizes = [nh_c * HEAD_DIM] + [HEAD_DIM] * 6 + [nh_c * 3] + [nh_d * HEAD_DIM] * 3
    offs = np.cumsum([0] + sizes)
    cut = lambda a, b_: w_in[:, offs[a]:offs[b_]].astype(BF16)
    w = {"qc": cut(0, 1), "kv6": cut(1, 7), "gc": cut(7, 8), "qd": cut(8, 9), "kd": cut(9, 10),
         "vd": cut(10, 11)}
    w["out_c"] = w_out[:nh_c * HEAD_DIM].astype(BF16)
    w["out_d"] = w_out[nh_c * HEAD_DIM:].astype(BF16)
    return w


def _odd_project(h, w, tabs):
    cos, sin, nb = tabs
    qc = proj_rope(h, w["qc"], tabs, BF16, name="proj_qc")
    kv6 = matmul([h], [w["kv6"]], [[(0, 0)]], functools.partial(_ep_split, rope_chunks=(0, 2, 4)),
                 [F32] * 6, tables=(cos, sin), n_tab_blocks=nb, split_out=6, name="proj_kv6")
    gc = proj(h, w["gc"], name="proj_gc")
    qd = proj(h, w["qd"], BF16, name="proj_qd")
    kd = proj(h, w["kd"], name="proj_kd")
    vd = proj(h, w["vd"], name="proj_vd")
    return qc, kv6, gc, qd, kd, vd


def odd_mixer_prompt(h, b, t, w, cwk, cwv):
    m, d = h.shape
    tabs = rope_tables(jnp.arange(t), b, _tile(m, 1024, 16))
    qc, (kc, vc, ks, vs, kw, vw), gc, qd, kd, vd = _odd_project(h, w, tabs)
    o_c = nsa_prompt(qc, gc, kc, vc, ks, vs, kw, vw, cwk, cwv, b, t)
    o_d = sb_prompt(qd, kd, vd, b, t)
    y = proj_out(o_c, o_d, w["out_c"], w["out_d"])
    nw = min(WINDOW, t)
    st1 = lambda a: a.reshape(b, t, 1, HEAD_DIM)
    sth = lambda a: a.reshape(b, t, -1, HEAD_DIM)
    return y, (st1(kc), st1(vc), st1(ks), st1(vs), st1(kw)[:, t - nw:], st1(vw)[:, t - nw:],
               sth(kd), sth(vd))


def _even_weights(w_in, conv_w, w_out):
    c = conv_w.shape[1]
    hq = w_out.shape[0] - c
    sizes = [c, c, c, hq, KV_B * HEAD_DIM, KV_B * HEAD_DIM, H_IDX * D_IDX, D_IDX, H_IDX]
    offs = np.cumsum([0] + sizes)
    names = ["bg", "cg", "xa", "q", "k", "v", "qi", "ki", "wi"]
    w = {n: w_in[:, offs[j]:offs[j + 1]].astype(BF16) for j, n in enumerate(names)}
    w["conv_w"] = conv_w
    w["out_a"] = w_out[:c].astype(BF16)
    w["out_b"] = w_out[c:].astype(BF16)
    return w


def _even_project(h, w, tabs):
    q = proj_rope(h, w["q"], tabs, BF16, name="proj_q")
    k = proj_rope(h, w["k"], tabs, F32, name="proj_k")
    v = proj(h, w["v"], name="proj_v")
    qi = proj_rope(h, w["qi"], tabs, BF16, name="proj_qi")
    ki = proj_rope(h, w["ki"], tabs, F32, name="proj_ki")
    wi = proj(h, w["wi"], name="proj_wi")
    return q, k, v, qi, ki, wi


def even_mixer_prompt(h, b, t, w):
    m, d = h.shape
    tabs = rope_tables(jnp.arange(t), b, _tile(m, 1024, 16))
    init = jnp.zeros((b, CONV_K - 1, w["bg"].shape[1]), F32)
    ya, conv_state = conv_mixer(h.reshape(b, t, d), w["bg"], w["cg"], w["xa"], w["conv_w"], init)
    q, k, v, qi, ki, wi = _even_project(h, w, tabs)
    ob = dsa_prompt(qi, wi, ki, q, k, v, b, t)
    y = proj_out(ya.reshape(m, -1), ob, w["out_a"], w["out_b"])
    return y, (conv_state, k.reshape(b, t, KV_B, HEAD_DIM), v.reshape(b, t, KV_B, HEAD_DIM),
               ki.reshape(b, t, D_IDX))


PAGES_PER_STEP_SMALL = 32
PAGES_PER_STEP = 16
PAGES_PER_STEP_SB = 8


def _paged_call(kern, grid, in_specs, out_specs, out_shape, scratch, name, page_table, args):
    gs = pltpu.PrefetchScalarGridSpec(num_scalar_prefetch=1, grid=grid, in_specs=in_specs,
                                      out_specs=out_specs, scratch_shapes=scratch)
    return pl.pallas_call(kern, grid_spec=gs, out_shape=out_shape,
                          compiler_params=_cp("arbitrary", "arbitrary"), name=name)(page_table, *args)


def _pad_rows(a, rows):
    return jnp.pad(a, ((0, 0), (0, rows - a.shape[1])) + ((0, 0),) * (a.ndim - 2))


def _page_specs(block_tail, layer, g_pages, page_of):
    zeros = (0,) * len(block_tail)

    def spec(j):
        return pl.BlockSpec((1, 1) + block_tail,
                            lambda bi, p, pt: (layer, page_of(bi, p, pt, j)) + zeros)
    return [spec(j) for j in range(g_pages)]


def _forward_pages(g_pages):
    return lambda bi, p, pt, j: pt[bi, p * g_pages + j]


def _rows_ht(a, b, t, nh):
    w = a.shape[1] // nh
    return a.reshape(b, t, nh, w).transpose(0, 2, 1, 3).reshape(b, nh * t, w)


def _rows_th(a, b, t, nh):
    w = a.shape[2]
    return a.reshape(b, nh, t, w).transpose(0, 2, 1, 3).reshape(b * t, nh * w)


def _dsa_scores_kernel(pt_ref, qi_ref, wi_ref, new_ref, *rest, g_pages, t):
    pools, (o_ref, onew_ref) = rest[:g_pages], rest[g_pages:]
    qi, wi = qi_ref[0], wi_ref[0]

    def scores(kb):
        s = _dot_nt(qi, kb) * D_IDX ** -0.5
        s = jnp.maximum(s, 0.0) * wi
        return jnp.sum(s.reshape(H_IDX, t, kb.shape[0]), axis=0) * H_IDX ** -0.5

    o_ref[0] = scores(jnp.concatenate([r[0, 0] for r in pools], axis=0).astype(BF16))

    @pl.when(pl.program_id(1) == 0)
    def _():
        onew_ref[0] = scores(new_ref[0].astype(BF16))


def dsa_sample_scores(qi_r, wi_r, ki_new, pool, e, page_table, t):
    b, n_pages = page_table.shape
    g = _tile(n_pages, PAGES_PER_STEP_SMALL, 1)
    kern = functools.partial(_dsa_scores_kernel, g_pages=g, t=t)
    per_b = lambda r, w: pl.BlockSpec((1, r, w), lambda bi, p, pt: (bi, 0, 0))
    past, new = _paged_call(
        kern, (b, n_pages // g),
        [per_b(H_IDX * t, D_IDX), per_b(H_IDX * t, 1), per_b(PAGE_SIZE, D_IDX)]
        + _page_specs((PAGE_SIZE, D_IDX), e, g, _forward_pages(g)),
        [pl.BlockSpec((1, t, g * PAGE_SIZE), lambda bi, p, pt: (bi, 0, p)), per_b(t, PAGE_SIZE)],
        [jax.ShapeDtypeStruct((b, t, n_pages * PAGE_SIZE), F32),
         jax.ShapeDtypeStruct((b, t, PAGE_SIZE), F32)], [], "dsa_sample_scores",
        page_table, (qi_r, wi_r, _pad_rows(ki_new, PAGE_SIZE)) + (pool,) * g)
    return jnp.concatenate([past, new], axis=-1)


def _dsa_topk_kernel(s_ref, o_ref, mask_ref, *, past, topk):
    score = s_ref[0]
    kpos = lax.broadcasted_iota(jnp.int32, score.shape, 1)
    qpos = past + lax.broadcasted_iota(jnp.int32, score.shape, 0)
    key = jnp.where(kpos <= qpos, _sort_key(score), jnp.int32(INT_MIN))
    _topk_mask(key, topk, mask_ref)
    o_ref[0] = mask_ref[...]


def dsa_sample_topk(score, past, topk):
    b, t, nk = score.shape
    blk = pl.BlockSpec((1, t, nk), lambda bi: (bi, 0, 0))
    return pl.pallas_call(
        functools.partial(_dsa_topk_kernel, past=past, topk=topk), grid=(b,),
        in_specs=[blk], out_specs=blk, out_shape=jax.ShapeDtypeStruct((b, t, nk), F32),
        scratch_shapes=[pltpu.VMEM((t, nk), F32)],
        compiler_params=_cp("arbitrary"), name="dsa_sample_topk")(score)


def _online_softmax_step(s, valid, v, m_ref, l_ref, acc_ref):
    m_old = m_ref[...]
    m_new = jnp.maximum(m_old, jnp.max(jnp.where(valid, s, NEG), axis=-1, keepdims=True))
    alpha = jnp.exp(m_old - m_new)
    e = jnp.where(valid, jnp.exp(s - m_new), 0.0)
    l_ref[...] = alpha * l_ref[...] + jnp.sum(e, axis=-1, keepdims=True)
    acc_ref[...] = alpha * acc_ref[...] + _dot(e.astype(BF16), v)
    m_ref[...] = m_new


def _init_softmax_state(m_ref, l_ref, acc_ref):
    m_ref[...] = jnp.full(m_ref.shape, NEG, F32)
    l_ref[...] = jnp.zeros(l_ref.shape, F32)
    acc_ref[...] = jnp.zeros(acc_ref.shape, F32)


def _dsa_sample_attn_kernel(pt_ref, qt_ref, mask_ref, masknew_ref, knew_ref, vnew_ref, *rest,
                            g_pages, t):
    kps, vps = rest[:g_pages], rest[g_pages:2 * g_pages]
    o_ref, m_ref, l_ref, acc_ref, z_ref = rest[2 * g_pages:]
    p = pl.program_id(1)
    n, ng = PAGE_SIZE, KV_B
    cols = qt_ref.shape[2]
    per_group = cols // ng

    @pl.when(p == 0)
    def _():
        _init_softmax_state(m_ref, l_ref, acc_ref)

    iota = lambda shape, d: lax.broadcasted_iota(jnp.int32, shape, d)
    own_group = iota((n * ng, cols), 0) % ng == iota((n * ng, cols), 1) // per_group
    spread = jnp.where(iota((n, n * ng), 1) // ng == iota((n, n * ng), 0), 1.0, 0.0).astype(BF16)
    own_rows = iota((cols, n * ng), 0) // per_group == iota((cols, n * ng), 1) % ng

    def scores(k2, slot):
        z_ref[slot] = jnp.where(own_group, _dot(k2.astype(BF16), qt_ref[0]), 0.0)
        z = z_ref[slot, pl.ds(0, n, stride=ng), :]
        for g in range(1, ng):
            z = z + z_ref[slot, pl.ds(g, n, stride=ng), :]
        return z.T

    def weighted_values(e, v2):
        a2 = jnp.where(own_rows, _dot(e.astype(BF16), spread), 0.0)
        return _dot(a2.astype(BF16), v2.astype(BF16))

    def update(k2s, v2s, valid_t, first_slot):
        s = jnp.concatenate([scores(k2, first_slot + j) for j, k2 in enumerate(k2s)], axis=1)
        s = s * HEAD_DIM ** -0.5
        valid = jnp.concatenate([valid_t] * (cols // t), axis=0)
        m_old = m_ref[...]
        m_new = jnp.maximum(m_old, jnp.max(jnp.where(valid, s, NEG), axis=-1, keepdims=True))
        alpha = jnp.exp(m_old - m_new)
        e = jnp.where(valid, jnp.exp(s - m_new), 0.0)
        l_ref[...] = alpha * l_ref[...] + jnp.sum(e, axis=-1, keepdims=True)
        pv = None
        for j, v2 in enumerate(v2s):
            d = weighted_values(e[:, j * n:(j + 1) * n], v2)
            pv = d if pv is None else pv + d
        acc_ref[...] = alpha * acc_ref[...] + pv
        m_ref[...] = m_new

    update([r[...] for r in kps], [r[...] for r in vps], mask_ref[0] > 0.5, 0)

    @pl.when(p == pl.num_programs(1) - 1)
    def _():
        update([knew_ref[0]], [vnew_ref[0]], masknew_ref[0] > 0.5, g_pages)
        o_ref[0] = acc_ref[...] / l_ref[...]


def dsa_sample_attn(q_r, mask, kpool, vpool, k_new, v_new, e, page_table):
    b, n_pages = page_table.shape
    t = mask.shape[1]
    cols = q_r.shape[1]
    g = _tile(n_pages, PAGES_PER_STEP, 1)
    rows = PAGE_SIZE * KV_B
    flat = lambda pool: pool.reshape(pool.shape[0], -1, HEAD_DIM)
    new_rows = lambda a: _pad_rows(a, PAGE_SIZE).reshape(b, rows, HEAD_DIM)
    per_b = lambda r, w: pl.BlockSpec((1, r, w), lambda bi, p, pt: (bi, 0, 0))
    pools = [pl.BlockSpec((None, rows, HEAD_DIM), functools.partial(
        lambda bi, p, pt, j: (e, pt[bi, p * g + j], 0), j=j)) for j in range(g)]
    return _paged_call(
        functools.partial(_dsa_sample_attn_kernel, g_pages=g, t=t), (b, n_pages // g),
        [per_b(HEAD_DIM, cols), pl.BlockSpec((1, t, g * PAGE_SIZE), lambda bi, p, pt: (bi, 0, p)),
         pl.BlockSpec((1, t, PAGE_SIZE), lambda bi, p, pt: (bi, 0, n_pages)),
         per_b(rows, HEAD_DIM), per_b(rows, HEAD_DIM)] + pools + pools,
        per_b(cols, HEAD_DIM), jax.ShapeDtypeStruct((b, cols, HEAD_DIM), F32),
        [pltpu.VMEM((cols, 1), F32), pltpu.VMEM((cols, 1), F32), pltpu.VMEM((cols, HEAD_DIM), F32),
         pltpu.VMEM((g + 1, rows, cols), F32)],
        "dsa_sample_attn", page_table,
        (q_r.transpose(0, 2, 1), mask, mask, new_rows(k_new), new_rows(v_new))
        + (flat(kpool),) * g + (flat(vpool),) * g)


def even_mixer_sample(h, b, t, w, e, page_table, state_conv, c_k, c_v, c_kidx):
    m, d = h.shape
    n_pages = page_table.shape[1]
    past = n_pages * PAGE_SIZE
    tabs = rope_tables(past + jnp.arange(t), b, _tile(m, 1024, 16))
    ya, conv_state = conv_mixer(h.reshape(b, t, d), w["bg"], w["cg"], w["xa"], w["conv_w"], state_conv[e])
    q, k, v, qi, ki, wi = _even_project(h, w, tabs)
    score = dsa_sample_scores(_rows_ht(qi, b, t, H_IDX), _rows_ht(wi, b, t, H_IDX),
                              ki.reshape(b, t, D_IDX), c_kidx, e, page_table, t)
    mask = dsa_sample_topk(score, past, min(TOPK_MAX, (past + t) // 4))
    nh = q.shape[1] // HEAD_DIM
    o = dsa_sample_attn(_rows_ht(q, b, t, nh), mask, c_k, c_v, k.reshape(b, t, KV_B, HEAD_DIM),
                        v.reshape(b, t, KV_B, HEAD_DIM), e, page_table)
    ob = _rows_th(o, b, t, nh).astype(BF16)
    y = proj_out(ya.reshape(m, -1), ob, w["out_a"], w["out_b"])
    return y, (conv_state, k.reshape(b, t, KV_B, HEAD_DIM), v.reshape(b, t, KV_B, HEAD_DIM),
               ki.reshape(b, t, D_IDX))


def _compress_pages_kernel(pt_ref, cwk_ref, cwv_ref, *rest, g_pages):
    kcs, vcs = rest[:g_pages], rest[g_pages:2 * g_pages]
    ko_ref, vo_ref = rest[2 * g_pages:]
    nb = PAGE_SIZE // CMP_BLOCK
    cwk, cwv = cwk_ref[...][None], cwv_ref[...][None]
    for j in range(g_pages):
        ko_ref[0, j] = jnp.sum(kcs[j][0, 0].reshape(nb, CMP_BLOCK, HEAD_DIM) * cwk, axis=1)
        vo_ref[0, j] = jnp.sum(vcs[j][0, 0].reshape(nb, CMP_BLOCK, HEAD_DIM) * cwv, axis=1)


def nsa_compress_pages(kpool, vpool, cwk, cwv, o, page_table):
    b, n_pages = page_table.shape
    nb = PAGE_SIZE // CMP_BLOCK
    g = _tile(n_pages, PAGES_PER_STEP_SMALL, 1)
    pools = _page_specs((PAGE_SIZE, HEAD_DIM), o, g, _forward_pages(g))
    cw = pl.BlockSpec((CMP_BLOCK, HEAD_DIM), lambda bi, p, pt: (0, 0))
    out = pl.BlockSpec((1, g, nb, HEAD_DIM), lambda bi, p, pt: (bi, p, 0, 0))
    shp = jax.ShapeDtypeStruct((b, n_pages, nb, HEAD_DIM), F32)
    kc, vc = _paged_call(functools.partial(_compress_pages_kernel, g_pages=g), (b, n_pages // g),
                         [cw, cw] + pools + pools, [out, out], [shp, shp], [], "nsa_compress_pages",
                         page_table, (cwk, cwv) + (kpool,) * g + (vpool,) * g)
    return kc.reshape(b, n_pages * nb, HEAD_DIM), vc.reshape(b, n_pages * nb, HEAD_DIM)


def _nsa_sample_a_kernel(q_ref, kcmp_ref, vcmp_ref, kw_ref, vw_ref, ocmp_ref, owin_ref, sel_ref,
                         *, past, t, nwin, n_selblk):
    q = q_ref[0]
    rows = q.shape[0]
    nh = rows // t
    scale = HEAD_DIM ** -0.5
    nb = kcmp_ref.shape[1]
    col = lax.broadcasted_iota(jnp.int32, (rows, nb), 1)
    pos = past + lax.broadcasted_iota(jnp.int32, (rows, nb), 0) % t
    s = _dot_nt(q, kcmp_ref[0].astype(BF16)) * scale
    p = _masked_softmax(s, (col + 1) * CMP_BLOCK - 1 <= pos)
    ocmp_ref[0] = _dot(p.astype(BF16), vcmp_ref[0].astype(BF16))
    imp = jnp.sum(p.reshape(nh, t, nb), axis=0)
    ncol = sel_ref.shape[2]
    imp = jnp.concatenate([imp, jnp.zeros((t, ncol - nb), F32)], axis=1)
    col_s = lax.broadcasted_iota(jnp.int32, (t, ncol), 1)
    pos_s = past + lax.broadcasted_iota(jnp.int32, (t, ncol), 0)
    sel_ref[0] = _select_blocks(_pair_sums(imp, col_s), col_s, pos_s, n_selblk)
    nwp = kw_ref.shape[1]
    colw = lax.broadcasted_iota(jnp.int32, (rows, nwp), 1)
    posw = past + lax.broadcasted_iota(jnp.int32, (rows, nwp), 0) % t
    kwpos = past + t - nwin + colw
    valid = (kwpos <= posw) & (kwpos > posw - WINDOW) & (colw < nwin)
    sw = _dot_nt(q, kw_ref[0].astype(BF16)) * scale
    pw = _masked_softmax(sw, valid)
    owin_ref[0] = _dot(pw.astype(BF16), vw_ref[0].astype(BF16))


def nsa_sample_a(q_r, kcmp, vcmp, kw_pad, vw_pad, past, t, nwin):
    b, rows, _ = q_r.shape
    nb = kcmp.shape[1]
    n_selblk = -(-(past + t) // SEL_BLOCK)
    ncol = -(-2 * n_selblk // LANES) * LANES
    assert ncol > nb >= 2 * n_selblk - 2 and nb % LANES == 0
    per_b = lambda r, w: pl.BlockSpec((1, r, w), lambda bi: (bi, 0, 0))
    kern = functools.partial(_nsa_sample_a_kernel, past=past, t=t, nwin=nwin, n_selblk=n_selblk)
    return pl.pallas_call(
        kern, grid=(b,),
        in_specs=[per_b(rows, HEAD_DIM), per_b(nb, HEAD_DIM), per_b(nb, HEAD_DIM),
                  per_b(kw_pad.shape[1], HEAD_DIM), per_b(kw_pad.shape[1], HEAD_DIM)],
        out_specs=[per_b(rows, HEAD_DIM), per_b(rows, HEAD_DIM), per_b(t, ncol)],
        out_shape=[jax.ShapeDtypeStruct((b, rows, HEAD_DIM), F32)] * 2
        + [jax.ShapeDtypeStruct((b, t, ncol), F32)],
        compiler_params=_cp("arbitrary"), name="nsa_sample_a")(q_r, kcmp, vcmp, kw_pad, vw_pad)


def _nsa_sample_b_kernel(pt_ref, q_ref, tok_ref, toknew_ref, knew_ref, vnew_ref,
                         ocmp_ref, owin_ref, gate_ref, *rest, g_pages, n_pages, t):
    kps, vps = rest[:g_pages], rest[g_pages:2 * g_pages]
    o_ref, m_ref, l_ref, acc_ref = rest[2 * g_pages:]
    p = pl.program_id(1)

    @pl.when(p == 0)
    def _():
        _init_softmax_state(m_ref, l_ref, acc_ref)

    q = q_ref[0]
    rows = q.shape[0]

    def update(keys, vals, tok, first_key):
        n = keys.shape[0]
        kpos = first_key + lax.broadcasted_iota(jnp.int32, (rows, n), 1)
        qpos = n_pages * PAGE_SIZE + lax.broadcasted_iota(jnp.int32, (rows, n), 0) % t
        valid = jnp.concatenate([tok > 0.5] * (rows // t), axis=0) & (kpos <= qpos)
        s = _dot_nt(q, keys.astype(BF16)) * HEAD_DIM ** -0.5
        _online_softmax_step(s, valid, vals.astype(BF16), m_ref, l_ref, acc_ref)

    update(jnp.concatenate([r[0, 0] for r in kps], axis=0),
           jnp.concatenate([r[0, 0] for r in vps], axis=0), tok_ref[0], p * (g_pages * PAGE_SIZE))

    @pl.when(p == pl.num_programs(1) - 1)
    def _():
        update(knew_ref[0], vnew_ref[0], toknew_ref[0], n_pages * PAGE_SIZE)
        g = jax.nn.sigmoid(gate_ref[0])
        o_ref[0] = (g[:, 0:1] * ocmp_ref[0] + g[:, 1:2] * (acc_ref[...] / l_ref[...])
                    + g[:, 2:3] * owin_ref[0])


def nsa_sample_b(q_r, tok, kpool, vpool, ks_new, vs_new, o_cmp, o_win, gate_r, o, page_table):
    b, n_pages = page_table.shape
    rows = q_r.shape[1]
    t = tok.shape[1]
    g = _tile(n_pages, PAGES_PER_STEP_SMALL, 1)
    per_b = lambda r, w: pl.BlockSpec((1, r, w), lambda bi, p, pt: (bi, 0, 0))
    pools = _page_specs((PAGE_SIZE, HEAD_DIM), o, g, _forward_pages(g))
    kern = functools.partial(_nsa_sample_b_kernel, g_pages=g, n_pages=n_pages, t=t)
    return _paged_call(
        kern, (b, n_pages // g),
        [per_b(rows, HEAD_DIM), pl.BlockSpec((1, t, g * PAGE_SIZE), lambda bi, p, pt: (bi, 0, p)),
         pl.BlockSpec((1, t, PAGE_SIZE), lambda bi, p, pt: (bi, 0, n_pages)),
         per_b(PAGE_SIZE, HEAD_DIM), per_b(PAGE_SIZE, HEAD_DIM),
         per_b(rows, HEAD_DIM), per_b(rows, HEAD_DIM), per_b(rows, 3)] + pools + pools,
        per_b(rows, HEAD_DIM), jax.ShapeDtypeStruct((b, rows, HEAD_DIM), F32),
        [pltpu.VMEM((rows, 1), F32), pltpu.VMEM((rows, 1), F32), pltpu.VMEM((rows, HEAD_DIM), F32)],
        "nsa_sample_b", page_table,
        (q_r, tok, tok, _pad_rows(ks_new, PAGE_SIZE), _pad_rows(vs_new, PAGE_SIZE),
         o_cmp, o_win, gate_r) + (kpool,) * g + (vpool,) * g)


def _sb_sample_kernel(pt_ref, qt2_ref, knew_ref, vnew_ref, *rest, g_pages, t):
    kps, vps = rest[:g_pages], rest[g_pages:2 * g_pages]
    o_ref, carry_ref, acc_ref, z_ref = rest[2 * g_pages:]
    p = pl.program_id(1)
    n, nh = knew_ref.shape[1], knew_ref.shape[2]
    cols = nh * t

    @pl.when(p == 0)
    def _():
        carry_ref[...] = jnp.zeros(carry_ref.shape, F32)
        acc_ref[...] = jnp.zeros(acc_ref.shape, F32)

    iota = lambda shape, d: lax.broadcasted_iota(jnp.int32, shape, d)
    later = jnp.where(iota((n, n), 1) > iota((n, n), 0), 1.0, 0.0).astype(BF16)
    own_head = iota((nh, cols), 1) // t == iota((nh, cols), 0)
    spread = jnp.where(iota((n, n * nh), 1) // nh == iota((n, n * nh), 0), 1.0, 0.0).astype(BF16)
    own_rows = iota((cols, n * nh), 0) // t == iota((cols, n * nh), 1) % nh

    def all_scores(k3s):
        flat = [k3.reshape(n * nh, HEAD_DIM).astype(BF16) for k3 in k3s]
        out = []
        for j in range(0, len(flat) - 1, 2):
            both = _dot(jnp.concatenate(flat[j:j + 2], axis=1), qt2_ref[0])
            out += [both[:, :cols], both[:, cols:]]
        if len(flat) % 2:
            out.append(_dot(flat[-1], qt2_ref[0, :HEAD_DIM, :cols]))
        return out

    def log_weights(z_all, slot, m):
        z_all = z_all.reshape(n, nh, cols)
        z_ref[slot] = jnp.sum(jnp.where(own_head[None], z_all, 0.0), axis=1)
        ls, lneg = _log_sigmoids(z_ref[slot] * HEAD_DIM ** -0.5)
        if m is not None:
            lneg = jnp.where(m, lneg, 0.0)
        hi, lo = _split_bf16(lneg)
        after = _dot(later, hi) + _dot(later, lo)
        return ls + after, after[0:1] + lneg[0:1]

    def weighted_values(v3, log_a, m):
        a = jnp.exp(log_a)
        if m is not None:
            a = jnp.where(m, a, 0.0)
        a2 = jnp.where(own_rows, _dot(a.T.astype(BF16), spread), 0.0)
        return _dot(a2.astype(BF16), v3.reshape(n * nh, HEAD_DIM).astype(BF16))

    @pl.when(p == 0)
    def _():
        m = iota((n, cols), 0) < iota((n, cols), 1) % t
        log_a, total = log_weights(all_scores([knew_ref[0]])[0], g_pages, m)
        acc_ref[...] += weighted_values(vnew_ref[0], log_a, m)
        carry_ref[...] += total

    z_alls = all_scores([kps[j][0, 0] for j in range(g_pages)])
    parts = [log_weights(z_alls[j], j, None) for j in range(g_pages)]
    carry = carry_ref[...]
    acc = acc_ref[...]
    for j in range(g_pages):
        acc = acc + weighted_values(vps[j][0, 0], parts[j][0] + carry, None)
        carry = carry + parts[j][1]
    carry_ref[...] = carry
    acc_ref[...] = acc

    @pl.when(p == pl.num_programs(1) - 1)
    def _():
        o_ref[0] = acc_ref[...]


def sb_sample(qd, kd_new, vd_new, kpool, vpool, o, page_table, b, t):
    n_pages = page_table.shape[1]
    nh = qd.shape[1] // HEAD_DIM
    g = _tile(n_pages, PAGES_PER_STEP_SB, 1)
    qt = qd.reshape(b, t, nh, HEAD_DIM).transpose(0, 3, 2, 1).reshape(b, HEAD_DIM, nh * t)
    zq = jnp.zeros_like(qt)
    qt2 = jnp.concatenate([jnp.concatenate([qt, zq], axis=2), jnp.concatenate([zq, qt], axis=2)], axis=1)
    new = pl.BlockSpec((1, PAGE_SIZE, nh, HEAD_DIM), lambda bi, p, pt: (bi, 0, 0, 0))
    pools = _page_specs((PAGE_SIZE, nh, HEAD_DIM), o, g,
                        lambda bi, p, pt, j: pt[bi, n_pages - 1 - (p * g + j)])
    out = pl.BlockSpec((1, nh * t, HEAD_DIM), lambda bi, p, pt: (bi, 0, 0))
    return _paged_call(
        functools.partial(_sb_sample_kernel, g_pages=g, t=t), (b, n_pages // g),
        [pl.BlockSpec((1, 2 * HEAD_DIM, 2 * nh * t), lambda bi, p, pt: (bi, 0, 0)), new, new]
        + pools + pools,
        out, jax.ShapeDtypeStruct((b, nh * t, HEAD_DIM), F32),
        [pltpu.VMEM((1, nh * t), F32), pltpu.VMEM((nh * t, HEAD_DIM), F32),
         pltpu.VMEM((g + 1, PAGE_SIZE, nh * t), F32)],
        "sb_sample", page_table,
        (qt2, _pad_rows(kd_new, PAGE_SIZE), _pad_rows(vd_new, PAGE_SIZE)) + (kpool,) * g + (vpool,) * g)


def odd_mixer_sample(h, b, t, w, cwk, cwv, o, page_table, c_kc, c_vc, c_ks, c_vs, c_kw, c_vw,
                     c_kd, c_vd):
    m, d = h.shape
    n_pages = page_table.shape[1]
    past = n_pages * PAGE_SIZE
    assert past % CMP_BLOCK == 0 and t < CMP_BLOCK
    tabs = rope_tables(past + jnp.arange(t), b, _tile(m, 1024, 16))
    qc, (kc, vc, ks, vs, kw, vw), gc, qd, kd, vd = _odd_project(h, w, tabs)
    nh = qc.shape[1] // HEAD_DIM
    n_pool = c_kc.shape[1]
    pool1 = lambda a: a.reshape(-1, n_pool, PAGE_SIZE, HEAD_DIM)
    seq = lambda a: a.reshape(b, t, -1)
    kcmp, vcmp = nsa_compress_pages(pool1(c_kc), pool1(c_vc), cwk, cwv, o, page_table)
    wb = c_kw.shape[2]
    kw_all = jnp.concatenate([c_kw[o].reshape(b, wb, HEAD_DIM), seq(kw)], axis=1)
    vw_all = jnp.concatenate([c_vw[o].reshape(b, wb, HEAD_DIM), seq(vw)], axis=1)
    nwp = -(-(wb + t) // LANES) * LANES
    q_r = _rows_ht(qc, b, t, nh)
    o_cmp, o_win, sel = nsa_sample_a(q_r, kcmp, vcmp, _pad_rows(kw_all, nwp), _pad_rows(vw_all, nwp),
                                     past, t, wb + t)
    n_selblk = -(-(past + t) // SEL_BLOCK)
    tok = jnp.repeat(sel[:, :, 0:2 * n_selblk:2], SEL_BLOCK, axis=-1)
    tok = jnp.pad(tok, ((0, 0), (0, 0), (0, (n_pages + 1) * PAGE_SIZE - tok.shape[-1])))
    gate_r = _rows_ht(gc, b, t, nh)
    o_c = nsa_sample_b(q_r, tok, pool1(c_ks), pool1(c_vs), seq(ks), seq(vs), o_cmp, o_win, gate_r,
                       o, page_table)
    nh_d = kd.shape[1] // HEAD_DIM
    heads = lambda a: a.reshape(b, t, nh_d, HEAD_DIM)
    o_d = sb_sample(qd, heads(kd), heads(vd), c_kd, c_vd, o, page_table, b, t)
    y = proj_out(_rows_th(o_c, b, t, nh).astype(BF16), _rows_th(o_d, b, t, nh_d).astype(BF16),
                 w["out_c"], w["out_d"])
    st1 = lambda a: a.reshape(b, t, 1, HEAD_DIM)
    return y, (st1(kc), st1(vc), st1(ks), st1(vs), kw_all[:, t:].reshape(b, wb, 1, HEAD_DIM),
               vw_all[:, t:].reshape(b, wb, 1, HEAD_DIM), kd.reshape(b, t, -1, HEAD_DIM),
               vd.reshape(b, t, -1, HEAD_DIM))


def kernel(x_prompt, x_sample, state_conv, cache_dsa_k, cache_dsa_v, cache_dsa_kidx, cache_nsa_kc, cache_nsa_vc, cache_nsa_ks, cache_nsa_vs, cache_nsa_kw, cache_nsa_vw, cache_sb_k, cache_sb_v, cache_mem_k, cache_mem_v, page_table, mem_prompt, norm_pre, norm_post, norm_mem, w_in_even, conv_w, w_out_even, w_in_odd, cmp_wk, cmp_wv, w_out_odd, w_mq, w_mk, w_mv, w_mo, w_gate, w_up, w_down):
    bp, tp, d = x_prompt.shape
    bs, ts, _ = x_sample.shape
    depth = norm_pre.shape[0]
    n_mem = mem_prompt.shape[1]
    hm = w_mq.shape[2]
    nh_c = nh_d = w_out_odd.shape[1] // 2 // HEAD_DIM
    xp = x_prompt.reshape(bp * tp, d)
    xs = x_sample.reshape(bs * ts, d)
    hp = norm_cast(xp, norm_pre[0, 0])
    hs = norm_cast(xs, norm_pre[0, 0])
    ev_p, ev_s, od_p, od_s, mem_p = [], [], [], [], []
    wg, wu, wd = w_gate, w_up, w_down.astype(BF16)
    for li in range(depth):
        g_pre, g_post = norm_pre[li], norm_post[li]
        if li % 2 == 0:
            e = li // 2
            w = _even_weights(w_in_even[e], conv_w[e], w_out_even[e])
            mp, stp = even_mixer_prompt(hp, bp, tp, w)
            ms, sts = even_mixer_sample(hs, bs, ts, w, e, page_table, state_conv,
                                        cache_dsa_k, cache_dsa_v, cache_dsa_kidx)
            ev_p.append(stp)
            ev_s.append(sts)
        else:
            o = li // 2
            w = _odd_weights(w_in_odd[o], w_out_odd[o], nh_c, nh_d)
            mp, stp = odd_mixer_prompt(hp, bp, tp, w, cmp_wk[o], cmp_wv[o])
            ms, sts = odd_mixer_sample(hs, bs, ts, w, cmp_wk[o], cmp_wv[o], o, page_table,
                                       cache_nsa_kc, cache_nsa_vc, cache_nsa_ks, cache_nsa_vs,
                                       cache_nsa_kw, cache_nsa_vw, cache_sb_k, cache_sb_v)
            od_p.append(stp)
            od_s.append(sts)
        xp, hp = resid_norm(xp, mp, g_post[0], g_pre[1])
        xs, hs = resid_norm(xs, ms, g_post[0], g_pre[1])
        wq, wo = w_mq[li].astype(BF16), w_mo[li].astype(BF16)
        hmem = norm_cast(mem_prompt.reshape(bp * n_mem, d), norm_mem[li])
        mkp = proj(hmem, w_mk[li].astype(BF16), name="proj_mk")
        mvp = proj(hmem, w_mv[li].astype(BF16), name="proj_mv")
        mem_p.append((mkp.reshape(bp, n_mem, H_MEM, HEAD_DIM), mvp.reshape(bp, n_mem, H_MEM, HEAD_DIM)))
        xp, hp = mem_sublayer(hp.reshape(bp, tp, d), xp.reshape(bp, tp, d), wq, mkp.reshape(bp, n_mem, hm),
                              mvp.reshape(bp, n_mem, hm), wo, g_post[1], g_pre[2])
        xs, hs = mem_sublayer(hs.reshape(bs, ts, d), xs.reshape(bs, ts, d), wq,
                              cache_mem_k[li].reshape(bs, n_mem, hm),
                              cache_mem_v[li].reshape(bs, n_mem, hm), wo, g_post[1], g_pre[2])
        xp, hp = xp.reshape(bp * tp, d), hp.reshape(bp * tp, d)
        xs, hs = xs.reshape(bs * ts, d), hs.reshape(bs * ts, d)
        g_next = norm_pre[li + 1, 0] if li + 1 < depth else None
        xp, hp = resid_norm(xp, ffn(hp, wg, wu, wd, li), g_post[2], g_next)
        xs, hs = resid_norm(xs, ffn(hs, wg, wu, wd, li), g_post[2], g_next)
    stack = lambda lst, j: jnp.stack([s[j] for s in lst])
    return ((xp.reshape(bp, tp, d), xs.reshape(bs, ts, d))
            + tuple(stack(ev_p, j) for j in range(4)) + tuple(stack(od_p, j) for j in range(8))
            + (stack(mem_p, 0), stack(mem_p, 1))
            + tuple(stack(ev_s, j) for j in range(4)) + tuple(stack(od_s, j) for j in range(8)))
```
